```python
import jax, jax.numpy as jnp
from jax import lax
import numpy as np

D_MODEL = 1024
BATCH = 2
SEQ = 8192
DEPTH = 2

GRID_W = 64
CTX_LEN = 256
N_DIR = 2
LRU_WIDTH = 512
LRU_HEADS = 8
LRU_HEAD_DIM = LRU_WIDTH // LRU_HEADS
LRU_CONV = 4
LRU_C = 8.0
CONV_WIDTH = 512
CONV_GROUPS = 8
CONV_K = 31
CONV_PAD = CONV_K // 2
MIX_WIDTH = LRU_WIDTH + CONV_WIDTH
IN_WIDTH = 2 * LRU_WIDTH + 3 * CONV_WIDTH
EPS = 1e-6

kernel_name = "hybrid_rglru_conformer_prefix_block"


def rms_norm(x, g):
    xf = x.astype(jnp.float32)
    y = xf * lax.rsqrt(jnp.mean(xf * xf, axis=-1, keepdims=True) + EPS)
    return (y * g.astype(jnp.float32)).astype(x.dtype)


def layer_norm(x, g, b):
    xf = x.astype(jnp.float32)
    xc = xf - jnp.mean(xf, axis=-1, keepdims=True)
    var = jnp.mean(xc * xc, axis=-1, keepdims=True)
    return (xc * lax.rsqrt(var + EPS) * g.astype(jnp.float32) + b.astype(jnp.float32)).astype(x.dtype)


def depthwise_conv1d(u, taps, pad):
    return lax.conv_general_dilated(
        u, taps[:, None, :].astype(u.dtype), window_strides=(1,), padding=[pad],
        dimension_numbers=('NWC', 'WIO', 'NWC'), feature_group_count=u.shape[-1])


def rglru_coeffs(v, w_r, b_r, w_i, b_i, lam):
    bn, t, _ = v.shape
    vh = v.reshape(bn, t, LRU_HEADS, LRU_HEAD_DIM)
    r = jax.nn.sigmoid((jnp.einsum('bthi,hij->bthj', vh, w_r).reshape(bn, t, LRU_WIDTH) + b_r).astype(jnp.float32))
    i = jax.nn.sigmoid((jnp.einsum('bthi,hij->bthj', vh, w_i).reshape(bn, t, LRU_WIDTH) + b_i).astype(jnp.float32))
    log_a = -LRU_C * r * jax.nn.softplus(-lam.astype(jnp.float32))
    a = jnp.exp(log_a)
    b = jnp.sqrt(-jnp.expm1(2.0 * log_a)) * (i * v.astype(jnp.float32))
    return a, b


def linear_scan(a, b, h0, reverse):
    if reverse:
        a = jnp.flip(a, axis=1)
        b = jnp.flip(b, axis=1)

    def combine(left, right):
        return (left[0] * right[0], right[0] * left[1] + right[1])

    a_cum, h = lax.associative_scan(combine, (a, b), axis=1)
    h = h + a_cum * h0[:, None, :]
    h_last = h[:, -1]
    if reverse:
        h = jnp.flip(h, axis=1)
    return h, h_last


def rglru_branch(vc_in, vx_in, conv_w, conv_b, w_r, b_r, w_i, b_i, lam):
    out_c = jnp.zeros(vc_in.shape, jnp.float32)
    out_x = jnp.zeros(vx_in.shape, jnp.float32)
    for d in range(N_DIR):
        pad = (LRU_CONV - 1, 0) if d == 0 else (0, LRU_CONV - 1)
        vc = depthwise_conv1d(vc_in, conv_w[d], pad) + conv_b[d]
        ac, bc = rglru_coeffs(vc, w_r[d], b_r[d], w_i[d], b_i[d], lam[d])
        h0 = jnp.zeros((vc.shape[0], LRU_WIDTH), jnp.float32)
        hc, hc_last = linear_scan(ac, bc, h0, reverse=(d == 1))
        vx = depthwise_conv1d(vx_in, conv_w[d], pad) + conv_b[d]
        ax, bx = rglru_coeffs(vx, w_r[d], b_r[d], w_i[d], b_i[d], lam[d])
        hx, _ = linear_scan(ax, bx, hc_last, reverse=(d == 1))
        out_c = out_c + hc
        out_x = out_x + hx
    return out_c, out_x


def conformer_latent(val, glu, dw_w, dw_b, ln_g, ln_b, rows):
    v = val * jax.nn.sigmoid(glu)
    bn = v.shape[0]
    half = CONV_WIDTH // 2
    grid = v.reshape(bn, rows, GRID_W, CONV_WIDTH)
    taps_h = dw_w[:, :half][None, :, None, :].astype(v.dtype)
    taps_v = dw_w[:, half:][:, None, None, :].astype(v.dtype)
    yh = lax.conv_general_dilated(grid[..., :half], taps_h, (1, 1), [(0, 0), (CONV_PAD, CONV_PAD)],
                                  dimension_numbers=('NHWC', 'HWIO', 'NHWC'), feature_group_count=half)
    yv = lax.conv_general_dilated(grid[..., half:], taps_v, (1, 1), [(CONV_PAD, CONV_PAD), (0, 0)],
                                  dimension_numbers=('NHWC', 'HWIO', 'NHWC'), feature_group_count=half)
    y = jnp.concatenate([yh, yv], axis=-1).reshape(bn, rows * GRID_W, CONV_WIDTH) + dw_b
    return jax.nn.silu(layer_norm(y, ln_g, ln_b))


def conformer_context(val, glu, dw_w, dw_b, ln_g, ln_b):
    v = val * jax.nn.sigmoid(glu)
    y = depthwise_conv1d(v, dw_w, (CONV_PAD, CONV_PAD)) + dw_b
    return jax.nn.silu(layer_norm(y, ln_g, ln_b))


def hybrid_layer(x, xc, c_act, cctx_act, w_mod, b_mod, g_pre, g_post, w_in, conv_a_w, conv_a_b,
                 w_rgate, b_rgate, w_igate, b_igate, lru_lambda, dw_w, dw_b, ln_g, ln_b, w_out,
                 rows, update_ctx):
    mod_x = (c_act @ w_mod + b_mod)[:, None, :]
    mod_c = (cctx_act @ w_mod + b_mod)[None, None, :]
    shift_x, scale_x, gate_x = jnp.split(mod_x, 3, axis=-1)
    shift_c, scale_c, gate_c = jnp.split(mod_c, 3, axis=-1)

    hx = rms_norm(x, g_pre) * (1.0 + scale_x) + shift_x
    hc = rms_norm(xc, g_pre) * (1.0 + scale_c) + shift_c

    splits = [LRU_WIDTH, 2 * LRU_WIDTH, 2 * LRU_WIDTH + CONV_WIDTH, 2 * LRU_WIDTH + 2 * CONV_WIDTH]
    ux = hx @ w_in
    a_val_x, a_gate_x, b_val_x, b_glu_x, b_gate_x = jnp.split(ux, splits, axis=-1)
    if update_ctx:
        uc = hc @ w_in
        a_val_c, a_gate_c, b_val_c, b_glu_c, b_gate_c = jnp.split(uc, splits, axis=-1)
    else:
        a_val_c = hc @ w_in[:, :LRU_WIDTH]

    rec_c, rec_x = rglru_branch(a_val_c, a_val_x, conv_a_w, conv_a_b, w_rgate, b_rgate,
                                w_igate, b_igate, lru_lambda)
    conv_x = conformer_latent(b_val_x, b_glu_x, dw_w, dw_b, ln_g, ln_b, rows)

    mix_x = jnp.concatenate([rec_x.astype(x.dtype) * jax.nn.silu(a_gate_x),
                             conv_x * jax.nn.silu(b_gate_x)], axis=-1) @ w_out
    x = x + gate_x * rms_norm(mix_x, g_post)

    if update_ctx:
        conv_c = conformer_context(b_val_c, b_glu_c, dw_w, dw_b, ln_g, ln_b)
        mix_c = jnp.concatenate([rec_c.astype(xc.dtype) * jax.nn.silu(a_gate_c),
                                 conv_c * jax.nn.silu(b_gate_c)], axis=-1) @ w_out
        xc = xc + gate_c * rms_norm(mix_c, g_post)
    return x, xc


def setup_inputs(seed: int = 0) -> dict:
    key = jax.random.key(seed)
    ks = jax.random.split(key, 24)
    D = D_MODEL
    f32 = jnp.float32
    nrm = lambda k, shape, s: jax.random.normal(k, shape, f32) * s
    x = nrm(ks[0], (BATCH, SEQ, D), 1.0)
    c = nrm(ks[1], (BATCH, D), 1.0)
    ctx = nrm(ks[2], (BATCH, CTX_LEN, D), 1.0)
    c_ctx = nrm(ks[3], (D,), 1.0)
    w_mod = nrm(ks[4], (DEPTH, D, 3 * D), 0.5 * D ** -0.5)
    b_mod = nrm(ks[5], (DEPTH, 3 * D), 0.02)
    g_pre = 1.0 + nrm(ks[6], (DEPTH, D), 0.02)
    g_post = 1.0 + nrm(ks[7], (DEPTH, D), 0.02)
    w_in = nrm(ks[8], (DEPTH, D, IN_WIDTH), D ** -0.5)
    conv_a_w = nrm(ks[9], (DEPTH, N_DIR, LRU_CONV, LRU_WIDTH), LRU_CONV ** -0.5)
    conv_a_b = nrm(ks[10], (DEPTH, N_DIR, LRU_WIDTH), 0.02)
    w_rgate = nrm(ks[11], (DEPTH, N_DIR, LRU_HEADS, LRU_HEAD_DIM, LRU_HEAD_DIM), LRU_HEAD_DIM ** -0.5)
    b_rgate = nrm(ks[12], (DEPTH, N_DIR, LRU_WIDTH), 0.02)
    w_igate = nrm(ks[13], (DEPTH, N_DIR, LRU_HEADS, LRU_HEAD_DIM, LRU_HEAD_DIM), LRU_HEAD_DIM ** -0.5)
    b_igate = nrm(ks[14], (DEPTH, N_DIR, LRU_WIDTH), 0.02)
    u = jax.random.uniform(ks[15], (DEPTH, N_DIR, LRU_WIDTH), f32, 0.9, 0.999)
    a0 = u ** (1.0 / LRU_C)
    lru_lambda = jnp.log(a0) - jnp.log1p(-a0)
    dw_w = nrm(ks[16], (DEPTH, CONV_K, CONV_WIDTH), CONV_K ** -0.5)
    dw_b = nrm(ks[17], (DEPTH, CONV_WIDTH), 0.02)
    ln_g = 1.0 + nrm(ks[18], (DEPTH, CONV_WIDTH), 0.02)
    ln_b = nrm(ks[19], (DEPTH, CONV_WIDTH), 0.02)
    w_out = nrm(ks[20], (DEPTH, MIX_WIDTH, D), MIX_WIDTH ** -0.5)
    return {"x": x, "c": c, "ctx": ctx, "c_ctx": c_ctx, "w_mod": w_mod, "b_mod": b_mod,
            "g_pre": g_pre, "g_post": g_post, "w_in": w_in, "conv_a_w": conv_a_w,
            "conv_a_b": conv_a_b, "w_rgate": w_rgate, "b_rgate": b_rgate, "w_igate": w_igate,
            "b_igate": b_igate, "lru_lambda": lru_lambda, "dw_w": dw_w, "dw_b": dw_b,
            "ln_g": ln_g, "ln_b": ln_b, "w_out": w_out}


def reference(x, c, ctx, c_ctx, w_mod, b_mod, g_pre, g_post, w_in, conv_a_w, conv_a_b,
              w_rgate, b_rgate, w_igate, b_igate, lru_lambda, dw_w, dw_b, ln_g, ln_b, w_out):
    rows = x.shape[1] // GRID_W
    c_act = jax.nn.silu(c)
    cctx_act = jax.nn.silu(c_ctx)
    xc = ctx
    for l in range(DEPTH):
        x, xc = hybrid_layer(x, xc, c_act, cctx_act, w_mod[l], b_mod[l], g_pre[l], g_post[l], w_in[l],
                             conv_a_w[l], conv_a_b[l], w_rgate[l], b_rgate[l], w_igate[l], b_igate[l],
                             lru_lambda[l], dw_w[l], dw_b[l], ln_g[l], ln_b[l], w_out[l],
                             rows, update_ctx=(l < DEPTH - 1))
    return x
```

```python
import functools

import jax
import jax.numpy as jnp
from jax import lax
from jax.experimental import pallas as pl
from jax.experimental.pallas import tpu as pltpu

F32 = jnp.float32
BF16 = jnp.bfloat16

EPS = 1e-6
LRU_C = 8.0
LRU_CONV = 4
CONV_K = 31
CONV_PAD = CONV_K // 2
GRID_W = 64
HALO = 16
ROWS = 8
CHUNK = 64
SEG_PAD = 16

TILE_PROJ = 1024
TILE_MIX = 512
VMEM_LIMIT = 56 * 1024 * 1024


def _silu(x):
    return x * jax.nn.sigmoid(x)


def _softplus(x):
    return jnp.maximum(x, 0.0) + jnp.log1p(jnp.exp(-jnp.abs(x)))


def _rms_norm(x, g):
    ms = jnp.mean(x * x, axis=-1, keepdims=True)
    return x * lax.rsqrt(ms + EPS) * g


def _layer_norm(x, g, b):
    mu = jnp.mean(x, axis=-1, keepdims=True)
    xc = x - mu
    var = jnp.mean(xc * xc, axis=-1, keepdims=True)
    return xc * lax.rsqrt(var + EPS) * g + b


def _dot(a, b):
    return jnp.dot(a, b, preferred_element_type=F32)


def _short_conv(aext_ref, n, cw, cb, d):
    start = HALO - (LRU_CONV - 1) if d == 0 else HALO
    acc = cb
    for k in range(LRU_CONV):
        acc = acc + cw[k:k + 1, :] * aext_ref[pl.ds(start + k, n), :]
    return acc


def _gate_matmuls(vc, wg_ref, g_ref, d, width):
    half = width // 2
    vcb = vc.astype(BF16)
    for hf in range(2):
        res = _dot(vcb[:, hf * half:(hf + 1) * half], wg_ref[d, hf])
        g_ref[d, :, hf * half:(hf + 1) * half] = res[:, :half]
        g_ref[d, :, width + hf * half:width + (hf + 1) * half] = res[:, half:]


def _rglru_scan(g_ref, vc_ref, br, bi, spl, hf_ref, hb_ref, ac_ref, hf0, n, width):
    nblk = n // ROWS
    row = lax.broadcasted_iota(jnp.int32, (ROWS, width), 0)

    def coeffs(d, r0):
        g = g_ref[d, pl.ds(r0, ROWS), :]
        vc = vc_ref[d, pl.ds(r0, ROWS), :]
        r = jax.nn.sigmoid(g[:, :width] + br[d:d + 1, :])
        ig = jax.nn.sigmoid(g[:, width:] + bi[d:d + 1, :])
        la = spl[d:d + 1, :] * r
        a = jnp.exp(la)
        mult = jnp.sqrt(-jnp.tanh(la) * (a * a + 1.0))
        return a, mult * (ig * vc)

    def body(j, carry):
        hf, hb, ab = carry
        r0 = pl.multiple_of(j * ROWS, ROWS)
        a, b = coeffs(0, r0)
        for s in (1, 2, 4):
            m = row >= s
            a_sh = jnp.where(m, pltpu.roll(a, s, 0), 1.0)
            b_sh = jnp.where(m, pltpu.roll(b, s, 0), 0.0)
            b = b + a * b_sh
            a = a * a_sh
        h = b + a * hf
        hf_ref[pl.ds(r0, ROWS), :] = h
        hf = h[ROWS - 1:ROWS, :]

        r1 = pl.multiple_of((nblk - 1 - j) * ROWS, ROWS)
        a, b = coeffs(1, r1)
        for s in (1, 2, 4):
            m = row < ROWS - s
            a_sh = jnp.where(m, pltpu.roll(a, ROWS - s, 0), 1.0)
            b_sh = jnp.where(m, pltpu.roll(b, ROWS - s, 0), 0.0)
            b = b + a * b_sh
            a = a * a_sh
        h = b + a * hb
        ac = a * ab
        hb_ref[pl.ds(r1, ROWS), :] = h
        ac_ref[pl.ds(r1, ROWS), :] = ac
        return hf, h[0:1, :], ac[0:1, :]

    init = (hf0, jnp.zeros((1, width), F32), jnp.ones((1, width), F32))
    hf, _, _ = lax.fori_loop(0, nblk, body, init, unroll=2)
    return hf


def _shifted_copies(vpad_ref, sh_ref):
    n = sh_ref.shape[1]
    for j in range(ROWS):
        sh_ref[j] = vpad_ref[pl.ds(j, n), :]


def _time_conv(sh_ref, w_ref, col0, ncol, nchunks, base_fn, out_ref):
    def body(ci, carry):
        base = base_fn(ci)
        acc = jnp.zeros((CHUNK, ncol), F32)
        for k in range(CONV_K):
            off = SEG_PAD - CONV_PAD + k
            src = sh_ref[off % ROWS, pl.ds(base + (off // ROWS) * ROWS, CHUNK), :]
            acc = acc + w_ref[k:k + 1, col0:col0 + ncol] * src
        out_ref[pl.ds(pl.multiple_of(ci * CHUNK, CHUNK), CHUNK), 0:ncol] = acc
        return carry

    lax.fori_loop(0, nchunks, body, 0)


def _mods_body(act_ref, w_ref, b_ref, o_ref):
    a = _silu(act_ref[...])
    o_ref[0] = jnp.dot(a, w_ref[0], precision=lax.Precision.HIGHEST,
                       preferred_element_type=F32) + b_ref[0]


def _mods_call(act, w_mod, b_mod):
    depth, d_model, d3 = w_mod.shape
    ncol = d_model
    return pl.pallas_call(
        _mods_body,
        grid=(depth, d3 // ncol),
        in_specs=[
            pl.BlockSpec((ROWS, d_model), lambda l, n: (0, 0)),
            pl.BlockSpec((1, d_model, ncol), lambda l, n: (l, 0, n)),
            pl.BlockSpec((1, 1, ncol), lambda l, n: (l, 0, n)),
        ],
        out_specs=pl.BlockSpec((1, ROWS, ncol), lambda l, n: (l, 0, n)),
        out_shape=jax.ShapeDtypeStruct((depth, ROWS, d3), F32),
        compiler_params=pltpu.CompilerParams(
            dimension_semantics=("arbitrary", "arbitrary"), vmem_limit_bytes=VMEM_LIMIT),
        name="mods",
    )(act, w_mod, b_mod.reshape(depth, 1, d3))


def _modulate(x, g, m, d_model):
    return _rms_norm(x, g) * (1.0 + m[:, d_model:2 * d_model]) + m[:, :d_model]


def _ctx_body(ctx_ref, mod_ref, gpre_ref, gpost_ref, win_ref, wo_ref, wg_ref, cw_ref, cb_ref,
              br_ref, bi_ref, lam_ref, dww_ref, dwb_ref, lng_ref, lnb_ref,
              h0_ref,
              aext_s, vc_s, g_s, hf_s, hb_s, ac_s, vpad_s, sh_s, y_s,
              *, depth, d_model, width, n):
    xc = ctx_ref[0]
    zeros_halo = jnp.zeros((HALO, width), F32)
    zero_state = jnp.zeros((1, width), F32)
    for l in range(depth):
        update = l < depth - 1
        m = mod_ref[l]
        hc = _modulate(xc, gpre_ref[l:l + 1, :], m, d_model).astype(BF16)
        if update:
            u = _dot(hc, win_ref[l])
            a_val = u[:, :width]
        else:
            a_val = _dot(hc, win_ref[l, :, :width])
        aext_s[0:HALO, :] = zeros_halo
        aext_s[HALO:HALO + n, :] = a_val
        aext_s[HALO + n:, :] = zeros_halo
        for d in range(2):
            vc = _short_conv(aext_s, n, cw_ref[l, d], cb_ref[l, d:d + 1, :], d)
            vc_s[d] = vc
            _gate_matmuls(vc, wg_ref.at[l], g_s, d, width)
        spl = -LRU_C * _softplus(-lam_ref[l])
        hf_last = _rglru_scan(g_s, vc_s, br_ref[l], bi_ref[l], spl, hf_s, hb_s, ac_s,
                              zero_state, n, width)
        h0_ref[0, 2 * l:2 * l + 1, :] = hf_last
        h0_ref[0, 2 * l + 1:2 * l + 2, :] = hb_s[0:1, :]
        if update:
            rec = hf_s[...] + hb_s[...]
            v = u[:, 2 * width:3 * width] * jax.nn.sigmoid(u[:, 3 * width:4 * width])
            vpad_s[0:SEG_PAD, :] = zeros_halo
            vpad_s[SEG_PAD:SEG_PAD + n, :] = v
            vpad_s[SEG_PAD + n:, :] = zeros_halo
            _shifted_copies(vpad_s, sh_s)
            _time_conv(sh_s, dww_ref.at[l], 0, width, n // CHUNK,
                       lambda ci: pl.multiple_of(ci * CHUNK, CHUNK), y_s)
            y = y_s[...] + dwb_ref[l:l + 1, :]
            conv = _silu(_layer_norm(y, lng_ref[l:l + 1, :], lnb_ref[l:l + 1, :]))
            mix_a = (rec * _silu(u[:, width:2 * width])).astype(BF16)
            mix_b = (conv * _silu(u[:, 4 * width:5 * width])).astype(BF16)
            mix = _dot(mix_a, wo_ref[l, :width, :]) + _dot(mix_b, wo_ref[l, width:, :])
            xc = xc + m[:, 2 * d_model:] * _rms_norm(mix, gpost_ref[l:l + 1, :])


def _ctx_call(ctx, mod_c, g_pre, g_post, w_in_bf, w_out_bf, wg, conv_a_w, conv_a_b, b_rgate,
              b_igate, lru_lambda, dw_w, dw_b, ln_g, ln_b):
    bsz, n, d_model = ctx.shape
    depth = w_in_bf.shape[0]
    width = conv_a_b.shape[-1]
    npad = n + 2 * SEG_PAD

    def full(a):
        nd = a.ndim
        return pl.BlockSpec(a.shape, lambda b, _nd=nd: (0,) * _nd)

    params = (mod_c, g_pre, g_post, w_in_bf, w_out_bf, wg, conv_a_w, conv_a_b, b_rgate, b_igate,
              lru_lambda, dw_w, dw_b, ln_g, ln_b)
    return pl.pallas_call(
        functools.partial(_ctx_body, depth=depth, d_model=d_model, width=width, n=n),
        grid=(bsz,),
        in_specs=[pl.BlockSpec((1, n, d_model), lambda b: (b, 0, 0))] + [full(a) for a in params],
        out_specs=pl.BlockSpec((1, 2 * depth, width), lambda b: (b, 0, 0)),
        out_shape=jax.ShapeDtypeStruct((bsz, 2 * depth, width), F32),
        scratch_shapes=[
            pltpu.VMEM((n + 2 * HALO, width), F32),
            pltpu.VMEM((2, n, width), F32),
            pltpu.VMEM((2, n, 2 * width), F32),
            pltpu.VMEM((n, width), F32),
            pltpu.VMEM((n, width), F32),
            pltpu.VMEM((n, width), F32),
            pltpu.VMEM((npad, width), F32),
            pltpu.VMEM((ROWS, npad - ROWS, width), F32),
            pltpu.VMEM((n, width), F32),
        ],
        compiler_params=pltpu.CompilerParams(
            dimension_semantics=("arbitrary",), vmem_limit_bytes=VMEM_LIMIT),
        name="context",
    )(ctx, *params)


def _proj_body(x_ref, mod_ref, g_ref, w_ref, hx_ref, v1_ref, v2_ref, *, nt, d_model, width):
    j = pl.program_id(1)
    is_pad = jnp.logical_or(j == 0, j == nt + 1)

    @pl.when(is_pad)
    def _():
        v2_ref[...] = jnp.zeros(v2_ref.shape, v2_ref.dtype)

    @pl.when(jnp.logical_not(is_pad))
    def _():
        hb = _modulate(x_ref[0], g_ref[...], mod_ref[0], d_model).astype(BF16)
        hx_ref[0] = hb
        u = _dot(hb, w_ref[...])
        v = u[:, :width] * jax.nn.sigmoid(u[:, width:])
        half = width // 2
        v1_ref[0] = v[:, :half].astype(BF16)
        v2_ref[0] = v[:, half:].astype(BF16)


def _proj_call(x, mod_l, g_pre_l, w_b):
    bsz, seq, d_model = x.shape
    width = w_b.shape[1] // 2
    half = width // 2
    tt = TILE_PROJ
    nt = seq // tt

    def tok(b, j):
        return (b, jnp.clip(j - 1, 0, nt - 1), 0)

    return pl.pallas_call(
        functools.partial(_proj_body, nt=nt, d_model=d_model, width=width),
        grid=(bsz, nt + 2),
        in_specs=[
            pl.BlockSpec((1, tt, d_model), tok),
            pl.BlockSpec((1, 1, 3 * d_model), lambda b, j: (b, 0, 0)),
            pl.BlockSpec((1, d_model), lambda b, j: (0, 0)),
            pl.BlockSpec(w_b.shape, lambda b, j: (0, 0)),
        ],
        out_specs=[
            pl.BlockSpec((1, tt, d_model), tok),
            pl.BlockSpec((1, tt, half), tok),
            pl.BlockSpec((1, tt, half), lambda b, j: (b, j, 0)),
        ],
        out_shape=[
            jax.ShapeDtypeStruct((bsz, seq, d_model), BF16),
            jax.ShapeDtypeStruct((bsz, seq, half), BF16),
            jax.ShapeDtypeStruct((bsz, seq + 2 * tt, half), BF16),
        ],
        compiler_params=pltpu.CompilerParams(
            dimension_semantics=("arbitrary", "arbitrary"), vmem_limit_bytes=VMEM_LIMIT),
        name="proj",
    )(x, mod_l, g_pre_l.reshape(1, d_model), w_b)


def _mix_body(hx_ref, hxp_ref, hxn_ref, v1_ref, v2_ref, wa_ref, wg_ref, cw_ref, cb_ref, br_ref,
              bi_ref, lam_ref, dww_ref, dwb_ref, lng_ref, lnb_ref, h0f_ref,
              p_ref, q_ref, mb_ref, ab_ref, hbo_ref,
              u_s, aext_s, vc_s, g_s, hf_s, hb_s, ac_s, carry_s, vpad_s, sh_s, v2f_s, y_s,
              *, tt, nt, width, v2_front):
    i = pl.program_id(1)
    half = width // 2
    nseg = tt // GRID_W
    seg_rows = GRID_W + 2 * SEG_PAD

    @pl.when(i == 0)
    def _():
        carry_s[...] = h0f_ref[0]

    u_s[...] = _dot(hx_ref[0], wa_ref[...])
    wa_a = wa_ref[:, :width]
    a_prev = jnp.where(i > 0, _dot(hxp_ref[0], wa_a), 0.0)
    a_next = jnp.where(i < nt - 1, _dot(hxn_ref[0], wa_a), 0.0)
    aext_s[0:HALO, :] = a_prev
    aext_s[HALO:HALO + tt, :] = u_s[:, :width]
    aext_s[HALO + tt:, :] = a_next

    for d in range(2):
        vc = _short_conv(aext_s, tt, cw_ref[d], cb_ref[d:d + 1, :], d)
        vc_s[d] = vc
        _gate_matmuls(vc, wg_ref, g_s, d, width)

    spl = -LRU_C * _softplus(-lam_ref[...])
    carry_s[...] = _rglru_scan(g_s, vc_s, br_ref[...], bi_ref[...], spl, hf_s, hb_s, ac_s,
                               carry_s[...], tt, width)

    sg = _silu(u_s[:, width:2 * width])
    p_ref[0] = ((hf_s[...] + hb_s[...]) * sg).astype(BF16)
    q_ref[0] = (ac_s[...] * sg).astype(BF16)
    ab_ref[0, 0] = ac_s[0:1, :]
    hbo_ref[0, 0] = hb_s[0:1, :]

    zpad = jnp.zeros((SEG_PAD, half), F32)
    for r in range(nseg):
        base = r * seg_rows
        vpad_s[base:base + SEG_PAD, :] = zpad
        vpad_s[base + SEG_PAD:base + SEG_PAD + GRID_W, :] = (
            v1_ref[0, r * GRID_W:(r + 1) * GRID_W, :].astype(F32))
        vpad_s[base + SEG_PAD + GRID_W:base + seg_rows, :] = zpad
    _shifted_copies(vpad_s, sh_s)
    _time_conv(sh_s, dww_ref, 0, half, nseg,
               lambda ci: pl.multiple_of(ci * seg_rows, ROWS), y_s)

    reach = CONV_PAD * GRID_W
    win0 = pl.multiple_of(i * tt + (v2_front - reach), GRID_W)
    v2f_s[...] = v2_ref[0, pl.ds(win0, tt + 2 * reach), :].astype(F32)

    def col_chunk(ci, carry):
        r0 = pl.multiple_of(ci * CHUNK, CHUNK)
        acc = jnp.zeros((CHUNK, half), F32)
        for k in range(CONV_K):
            acc = acc + dww_ref[k:k + 1, half:] * v2f_s[pl.ds(r0 + k * GRID_W, CHUNK), :]
        y_s[pl.ds(r0, CHUNK), half:] = acc
        return carry

    lax.fori_loop(0, tt // CHUNK, col_chunk, 0)

    y = y_s[...] + dwb_ref[...]
    conv = _silu(_layer_norm(y, lng_ref[...], lnb_ref[...]))
    mb_ref[0] = (conv * _silu(u_s[:, 2 * width:])).astype(BF16)


def _mix_call(hx, v1, v2p, w_a, wg_l, cw, cb, br, bi, lam, dww, dwb, lng, lnb, h0f):
    bsz, seq, d_model = hx.shape
    width = cb.shape[-1]
    half = width // 2
    tt = TILE_MIX
    nt = seq // tt
    hb_per_tile = tt // HALO
    n_halo_blocks = seq // HALO
    v2_front = (v2p.shape[1] - seq) // 2
    nseg = tt // GRID_W
    seg_rows = GRID_W + 2 * SEG_PAD
    reach = CONV_PAD * GRID_W

    def full(a):
        nd = a.ndim
        return pl.BlockSpec(a.shape, lambda b, i, _nd=nd: (0,) * _nd)

    small = (w_a, wg_l, cw, cb, br, bi, lam, dww, dwb, lng, lnb)
    tile = lambda b, i: (b, i, 0)
    out_shapes = [jax.ShapeDtypeStruct((bsz, seq, width), BF16)] * 3 + [
        jax.ShapeDtypeStruct((bsz, nt, 1, width), F32)] * 2
    return pl.pallas_call(
        functools.partial(_mix_body, tt=tt, nt=nt, width=width, v2_front=v2_front),
        grid=(bsz, nt),
        in_specs=[
            pl.BlockSpec((1, tt, d_model), tile),
            pl.BlockSpec((1, HALO, d_model),
                         lambda b, i: (b, jnp.maximum(i * hb_per_tile - 1, 0), 0)),
            pl.BlockSpec((1, HALO, d_model),
                         lambda b, i: (b, jnp.minimum((i + 1) * hb_per_tile, n_halo_blocks - 1), 0)),
            pl.BlockSpec((1, tt, half), tile),
            pl.BlockSpec((1, v2p.shape[1], half), lambda b, i: (b, 0, 0)),
        ] + [full(a) for a in small] + [pl.BlockSpec((1, 1, width), lambda b, i: (b, 0, 0))],
        out_specs=[pl.BlockSpec((1, tt, width), tile)] * 3 + [
            pl.BlockSpec((1, 1, 1, width), lambda b, i: (b, i, 0, 0))] * 2,
        out_shape=out_shapes,
        scratch_shapes=[
            pltpu.VMEM((tt, 3 * width), F32),
            pltpu.VMEM((tt + 2 * HALO, width), F32),
            pltpu.VMEM((2, tt, width), F32),
            pltpu.VMEM((2, tt, 2 * width), F32),
            pltpu.VMEM((tt, width), F32),
            pltpu.VMEM((tt, width), F32),
            pltpu.VMEM((tt, width), F32),
            pltpu.VMEM((1, width), F32),
            pltpu.VMEM((nseg * seg_rows, half), F32),
            pltpu.VMEM((ROWS, nseg * seg_rows - ROWS, half), F32),
            pltpu.VMEM((tt + 2 * reach, half), F32),
            pltpu.VMEM((tt, width), F32),
        ],
        compiler_params=pltpu.CompilerParams(
            dimension_semantics=("arbitrary", "arbitrary"), vmem_limit_bytes=VMEM_LIMIT),
        name="mixer",
    )(hx, hx, hx, v1, v2p, *small, h0f)


def _out_body(p_ref, q_ref, mb_ref, x_ref, wo_ref, gpost_ref, mod_ref, ab_ref, hb_ref, h0b_ref,
              o_ref, carry_s, *, d_model, width):
    @pl.when(pl.program_id(1) == 0)
    def _():
        carry_s[...] = h0b_ref[0]

    c = carry_s[...]
    mix_a = (p_ref[0].astype(F32) + q_ref[0].astype(F32) * c).astype(BF16)
    mix = _dot(mix_a, wo_ref[:width, :]) + _dot(mb_ref[0], wo_ref[width:, :])
    gate = mod_ref[0][:, 2 * d_model:]
    o_ref[0] = x_ref[0] + gate * _rms_norm(mix, gpost_ref[...])
    carry_s[...] = hb_ref[0, 0] + ab_ref[0, 0] * c


def _out_call(p, q, mb, x, w_o, g_post_l, mod_l, ab, hbo, h0b):
    bsz, seq, d_model = x.shape
    width = p.shape[-1]
    tt = TILE_MIX
    nt = seq // tt
    rev = lambda b, j: (b, nt - 1 - j, 0)
    return pl.pallas_call(
        functools.partial(_out_body, d_model=d_model, width=width),
        grid=(bsz, nt),
        in_specs=[
            pl.BlockSpec((1, tt, width), rev),
            pl.BlockSpec((1, tt, width), rev),
            pl.BlockSpec((1, tt, width), rev),
            pl.BlockSpec((1, tt, d_model), rev),
            pl.BlockSpec(w_o.shape, lambda b, j: (0, 0)),
            pl.BlockSpec((1, d_model), lambda b, j: (0, 0)),
            pl.BlockSpec((1, 1, 3 * d_model), lambda b, j: (b, 0, 0)),
            pl.BlockSpec((1, 1, 1, width), lambda b, j: (b, nt - 1 - j, 0, 0)),
            pl.BlockSpec((1, 1, 1, width), lambda b, j: (b, nt - 1 - j, 0, 0)),
            pl.BlockSpec((1, 1, width), lambda b, j: (b, 0, 0)),
        ],
        out_specs=pl.BlockSpec((1, tt, d_model), rev),
        out_shape=jax.ShapeDtypeStruct((bsz, seq, d_model), F32),
        scratch_shapes=[pltpu.VMEM((1, width), F32)],
        compiler_params=pltpu.CompilerParams(
            dimension_semantics=("arbitrary", "arbitrary"), vmem_limit_bytes=VMEM_LIMIT),
        name="out",
    )(p, q, mb, x, w_o, g_post_l.reshape(1, d_model), mod_l, ab, hbo, h0b)


def _pack_gate_weights(w_r, w_i):
    depth, ndir, heads, hd, _ = w_r.shape
    hh = heads // 2
    eye = jnp.eye(hh, dtype=w_r.dtype)

    def bd(w):
        w = w.reshape(depth, ndir, 2, hh, hd, hd)
        return jnp.einsum("ldfhij,hg->ldfhigj", w, eye).reshape(depth, ndir, 2, hh * hd, hh * hd)

    return jnp.concatenate([bd(w_r), bd(w_i)], axis=-1).astype(BF16)


def kernel(x, c, ctx, c_ctx, w_mod, b_mod, g_pre, g_post, w_in, conv_a_w, conv_a_b, w_rgate,
           b_rgate, w_igate, b_igate, lru_lambda, dw_w, dw_b, ln_g, ln_b, w_out):
    bsz, seq, d_model = x.shape
    depth = w_mod.shape[0]
    width = conv_a_b.shape[-1]
    assert bsz + 1 <= ROWS and seq % TILE_PROJ == 0 and seq % TILE_MIX == 0
    assert TILE_MIX % GRID_W == 0 and TILE_PROJ >= CONV_PAD * GRID_W
    assert ctx.shape[1] % CHUNK == 0

    act = jnp.concatenate(
        [c, c_ctx[None, :], jnp.zeros((ROWS - bsz - 1, d_model), F32)], axis=0)
    mods = _mods_call(act, w_mod, b_mod)

    w_in_bf = w_in.astype(BF16)
    w_out_bf = w_out.astype(BF16)
    wg = _pack_gate_weights(w_rgate, w_igate)

    h0 = _ctx_call(ctx, mods[:, bsz:bsz + 1, :], g_pre, g_post, w_in_bf, w_out_bf, wg, conv_a_w,
                   conv_a_b, b_rgate, b_igate, lru_lambda, dw_w, dw_b, ln_g, ln_b)

    for l in range(depth):
        mod_l = mods[l].reshape(ROWS, 1, 3 * d_model)
        w_b = w_in_bf[l, :, 2 * width:4 * width]
        w_a = jnp.concatenate([w_in_bf[l, :, :2 * width], w_in_bf[l, :, 4 * width:]], axis=1)
        hx, v1, v2p = _proj_call(x, mod_l, g_pre[l], w_b)
        p, q, mb, ab, hbo = _mix_call(
            hx, v1, v2p, w_a, wg[l], conv_a_w[l], conv_a_b[l], b_rgate[l], b_igate[l],
            lru_lambda[l], dw_w[l], dw_b[l:l + 1], ln_g[l:l + 1], ln_b[l:l + 1],
            h0[:, 2 * l:2 * l + 1, :])
        x = _out_call(p, q, mb, x, w_out_bf[l], g_post[l], mod_l, ab, hbo,
                      h0[:, 2 * l + 1:2 * l + 2, :])
    return x
```

```python
import functools

import jax
import jax.numpy as jnp
from jax import lax
from jax.experimental import pallas as pl
from jax.experimental.pallas import tpu as pltpu

F32 = jnp.float32
BF16 = jnp.bfloat16

EPS = 1e-6
LRU_C = 8.0
LRU_CONV = 4
CONV_K = 31
CONV_PAD = CONV_K // 2
GRID_W = 64
HALO = 16
ROWS = 8
CHUNK = 64
CONV_ROWS = 32
SEG_PAD = 16
MXU_N = 256

TILE_PROJ = 1024
TILE_MIX = 512
VMEM_LIMIT = 56 * 1024 * 1024


def _silu(x):
    return x * jax.nn.sigmoid(x)


def _softplus(x):
    return jnp.maximum(x, 0.0) + jnp.log1p(jnp.exp(-jnp.abs(x)))


def _rms_norm(x, g):
    ms = jnp.mean(x * x, axis=-1, keepdims=True)
    return x * lax.rsqrt(ms + EPS) * g


def _layer_norm(x, g, b):
    mu = jnp.mean(x, axis=-1, keepdims=True)
    xc = x - mu
    var = jnp.mean(xc * xc, axis=-1, keepdims=True)
    return xc * lax.rsqrt(var + EPS) * g + b


def _dot(a, b):
    return jnp.dot(a, b, preferred_element_type=F32)


def _interleave(mxu_pieces, vpu_pieces):
    n = len(mxu_pieces)
    per = -(-len(vpu_pieces) // n) if n else 0
    for j, piece in enumerate(mxu_pieces):
        piece()
        for fill in vpu_pieces[j * per:(j + 1) * per]:
            fill()
    for fill in vpu_pieces[n * per:]:
        fill()


def _perm_matrices(n):
    i = jnp.arange(n)
    src = (n // ROWS) * (i % ROWS) + i // ROWS
    pm = (src[:, None] == i[None, :]).astype(BF16)
    return pm, pm.T


UA_ROW0 = (LRU_CONV - 1) * ROWS


def _short_convs(ua_s, a_prev, a_next, cw, cb, vc_s, n, width):
    sub = n // ROWS
    ext = LRU_CONV - 1
    row = lax.broadcasted_iota(jnp.int32, (ROWS, width), 0)

    def blk(p):
        return slice(p * ROWS, (p + 1) * ROWS)

    for k in range(1, ext + 1):
        tail = pltpu.roll(ua_s[blk(ext + sub - k), :width], 1, 0)
        ua_s[blk(ext - k), :width] = jnp.where(row == 0, a_prev[ext - k:ext - k + 1, :], tail)
    for k in range(ext):
        head = pltpu.roll(ua_s[blk(ext + k), :width], ROWS - 1, 0)
        ua_s[blk(ext + sub + k), :width] = jnp.where(row == ROWS - 1, a_next[k:k + 1, :], head)

    def chunk(ci, carry):
        base = ci * CHUNK
        for d in range(2):
            first = 0 if d == 0 else ext
            acc = jnp.broadcast_to(cb[d:d + 1, :], (CHUNK, width))
            for k in range(LRU_CONV):
                start = pl.multiple_of(base + (first + k) * ROWS, ROWS)
                acc = acc + cw[d, k:k + 1, :] * ua_s[pl.ds(start, CHUNK), :width]
            vc_s[d, pl.ds(pl.multiple_of(base, CHUNK), CHUNK), :] = acc
        return carry

    lax.fori_loop(0, n // CHUNK, chunk, 0)


def _gate_pieces(vc_s, wg_ref, g_s, width):
    half = width // 2

    def piece(d, hf):
        def run():
            res = _dot(vc_s[d, :, hf * half:(hf + 1) * half].astype(BF16), wg_ref[d, hf])
            g_s[d, :, hf * half:(hf + 1) * half] = res[:, :half]
            g_s[d, :, width + hf * half:width + (hf + 1) * half] = res[:, half:]
        return run

    return [piece(d, hf) for d in range(2) for hf in range(2)]


def _sublane_scan(a, h, row, reverse):
    for s in (1, 2, 4):
        if reverse:
            m, shift = row < ROWS - s, ROWS - s
        else:
            m, shift = row >= s, s
        a_sh = jnp.where(m, pltpu.roll(a, shift, 0), 1.0)
        h_sh = jnp.where(m, pltpu.roll(h, shift, 0), 0.0)
        h = h + a * h_sh
        a = a * a_sh
    return a, h


def _rglru_scan(ua_s, br, bi, lam, c0, vc_s, g_s, hl_s, al_s, po_s, q_s, n, width):
    sub = n // ROWS
    row = lax.broadcasted_iota(jnp.int32, (ROWS, width), 0)
    spl = -LRU_C * _softplus(-lam)

    def coeffs(d, r0):
        g = g_s[d, pl.ds(r0, ROWS), :]
        vc = vc_s[d, pl.ds(r0, ROWS), :]
        r = jax.nn.sigmoid(g[:, :width] + br[d:d + 1, :])
        ig = jax.nn.sigmoid(g[:, width:] + bi[d:d + 1, :])
        la = spl[d:d + 1, :] * r
        a = jnp.exp(la)
        x = jnp.tanh(la) * (-1.0 - a * a)
        mult = jnp.where(x > 0.0, x * lax.rsqrt(x), 0.0)
        return a, mult * (ig * vc)

    def local(j, carry):
        hf, af, hb, ab = carry
        r0 = pl.multiple_of(j * ROWS, ROWS)
        a, b = coeffs(0, r0)
        hf = a * hf + b
        af = a * af
        hl_s[0, pl.ds(r0, ROWS), :] = hf
        al_s[0, pl.ds(r0, ROWS), :] = af
        r1 = pl.multiple_of((sub - 1 - j) * ROWS, ROWS)
        a, b = coeffs(1, r1)
        hb = a * hb + b
        ab = a * ab
        hl_s[1, pl.ds(r1, ROWS), :] = hb
        al_s[1, pl.ds(r1, ROWS), :] = ab
        return hf, af, hb, ab

    zero = jnp.zeros((ROWS, width), F32)
    one = jnp.ones((ROWS, width), F32)
    hf, af, hb, ab = lax.fori_loop(0, sub, local, (zero, one, zero, one), unroll=4)

    af, hf = _sublane_scan(af, hf, row, reverse=False)
    end_f = hf + af * c0
    c_f = jnp.where(row == 0, c0, pltpu.roll(end_f, 1, 0))
    ab, hb = _sublane_scan(ab, hb, row, reverse=True)
    c_b = jnp.where(row == ROWS - 1, 0.0, pltpu.roll(hb, ROWS - 1, 0))
    c_a = jnp.where(row == ROWS - 1, 1.0, pltpu.roll(ab, ROWS - 1, 0))

    if po_s is not None:
        def fix(j, carry):
            r0 = pl.ds(pl.multiple_of(j * ROWS, ROWS), ROWS)
            a_b = al_s[1, r0, :]
            h = (hl_s[0, r0, :] + al_s[0, r0, :] * c_f) + (hl_s[1, r0, :] + a_b * c_b)
            sg = _silu(ua_s[pl.ds(pl.multiple_of(UA_ROW0 + j * ROWS, ROWS), ROWS), width:])
            po_s[r0, :] = h * sg
            if q_s is not None:
                q_s[r0, :] = (a_b * c_a) * sg
            return carry

        lax.fori_loop(0, sub, fix, 0, unroll=8)

    return end_f[ROWS - 1:ROWS, :], hb[0:1, :], ab[0:1, :]


def _shifted_copy_pieces(vpad_ref, sh_ref):
    n = sh_ref.shape[1]

    def piece(j):
        def run():
            sh_ref[j] = vpad_ref[pl.ds(j, n), :]
        return run

    return [piece(j) for j in range(ROWS)]


def _time_conv_pieces(sh_ref, w_ref, ncol, bases, out_ref):
    def piece(ci, base):
        def run():
            acc = jnp.zeros((CHUNK, ncol), F32)
            for k in range(CONV_K):
                off = SEG_PAD - CONV_PAD + k
                start = base + (off // ROWS) * ROWS
                acc = acc + w_ref[k:k + 1, 0:ncol] * sh_ref[off % ROWS, start:start + CHUNK, :]
            out_ref[ci * CHUNK:(ci + 1) * CHUNK, 0:ncol] = acc
        return run

    return [piece(ci, base) for ci, base in enumerate(bases)]


def _mods_body(act_ref, w_ref, b_ref, o_ref):
    a = _silu(act_ref[...])
    o_ref[0] = jnp.dot(a, w_ref[0], precision=lax.Precision.HIGHEST,
                       preferred_element_type=F32) + b_ref[0]


def _mods_call(act, w_mod, b_mod):
    depth, d_model, d3 = w_mod.shape
    ncol = d_model
    return pl.pallas_call(
        _mods_body,
        grid=(depth, d3 // ncol),
        in_specs=[
            pl.BlockSpec((ROWS, d_model), lambda l, n: (0, 0)),
            pl.BlockSpec((1, d_model, ncol), lambda l, n: (l, 0, n)),
            pl.BlockSpec((1, 1, ncol), lambda l, n: (l, 0, n)),
        ],
        out_specs=pl.BlockSpec((1, ROWS, ncol), lambda l, n: (l, 0, n)),
        out_shape=jax.ShapeDtypeStruct((depth, ROWS, d3), F32),
        compiler_params=pltpu.CompilerParams(
            dimension_semantics=("arbitrary", "arbitrary"), vmem_limit_bytes=VMEM_LIMIT),
        name="mods",
    )(act, w_mod, b_mod.reshape(depth, 1, d3))


def _modulate(x, g, m, d_model):
    return _rms_norm(x, g) * (1.0 + m[:, d_model:2 * d_model]) + m[:, :d_model]


def _ctx_body(ctx_ref, mod_ref, gpre_ref, gpost_ref, win_ref, wo_ref, wg_ref, cw_ref, cb_ref,
              br_ref, bi_ref, lam_ref, dww_ref, dwb_ref, lng_ref, lnb_ref, pm_ref, pmt_ref,
              h0_ref,
              ua_s, vc_s, g_s, hl_s, al_s, po_s, vpad_s, sh_s, y_s,
              *, depth, d_model, width, n):
    xc = ctx_ref[0]
    zeros_pad = jnp.zeros((SEG_PAD, width), F32)
    no_rows = jnp.zeros((LRU_CONV - 1, width), F32)
    zero_state = jnp.zeros((1, width), F32)
    for l in range(depth):
        update = l < depth - 1
        m = mod_ref[l]
        hc = _modulate(xc, gpre_ref[l:l + 1, :], m, d_model).astype(BF16)
        hcp = _dot(pm_ref[...], hc).astype(BF16)
        if update:
            ua_s[UA_ROW0:UA_ROW0 + n, :] = _dot(hcp, win_ref[l, :, :2 * width])
            ub = _dot(hc, win_ref[l, :, 2 * width:])
        else:
            ua_s[UA_ROW0:UA_ROW0 + n, :width] = _dot(hcp, win_ref[l, :, :width])
        _short_convs(ua_s, no_rows, no_rows, cw_ref[l], cb_ref[l], vc_s, n, width)
        _interleave(_gate_pieces(vc_s, wg_ref.at[l], g_s, width), [])
        hf_last, hb_first, _ = _rglru_scan(
            ua_s, br_ref[l], bi_ref[l], lam_ref[l], zero_state, vc_s, g_s, hl_s, al_s,
            po_s if update else None, None, n, width)
        h0_ref[0, 2 * l:2 * l + 1, :] = hf_last
        h0_ref[0, 2 * l + 1:2 * l + 2, :] = hb_first
        if update:
            v = ub[:, :width] * jax.nn.sigmoid(ub[:, width:2 * width])
            vpad_s[0:SEG_PAD, :] = zeros_pad
            vpad_s[SEG_PAD:SEG_PAD + n, :] = v
            vpad_s[SEG_PAD + n:, :] = zeros_pad
            _interleave([], _shifted_copy_pieces(vpad_s, sh_s))
            _interleave([], _time_conv_pieces(
                sh_s, dww_ref.at[l], width, [ci * CHUNK for ci in range(n // CHUNK)], y_s))
            y = y_s[...] + dwb_ref[l:l + 1, :]
            conv = _silu(_layer_norm(y, lng_ref[l:l + 1, :], lnb_ref[l:l + 1, :]))
            mix_a = _dot(pmt_ref[...], po_s[...].astype(BF16)).astype(BF16)
            mix_b = (conv * _silu(ub[:, 2 * width:])).astype(BF16)
            mix = _dot(mix_a, wo_ref[l, :width, :]) + _dot(mix_b, wo_ref[l, width:, :])
            xc = xc + m[:, 2 * d_model:] * _rms_norm(mix, gpost_ref[l:l + 1, :])


def _ctx_call(ctx, mod_c, g_pre, g_post, w_in_bf, w_out_bf, wg, conv_a_w, conv_a_b, b_rgate,
              b_igate, lru_lambda, dw_w, dw_b, ln_g, ln_b):
    bsz, n, d_model = ctx.shape
    depth = w_in_bf.shape[0]
    width = conv_a_b.shape[-1]
    npad = n + 2 * SEG_PAD
    pm, pmt = _perm_matrices(n)

    def full(a):
        nd = a.ndim
        return pl.BlockSpec(a.shape, lambda b, _nd=nd: (0,) * _nd)

    params = (mod_c, g_pre, g_post, w_in_bf, w_out_bf, wg, conv_a_w, conv_a_b, b_rgate, b_igate,
              lru_lambda, dw_w, dw_b, ln_g, ln_b, pm, pmt)
    return pl.pallas_call(
        functools.partial(_ctx_body, depth=depth, d_model=d_model, width=width, n=n),
        grid=(bsz,),
        in_specs=[pl.BlockSpec((1, n, d_model), lambda b: (b, 0, 0))] + [full(a) for a in params],
        out_specs=pl.BlockSpec((1, 2 * depth, width), lambda b: (b, 0, 0)),
        out_shape=jax.ShapeDtypeStruct((bsz, 2 * depth, width), F32),
        scratch_shapes=[
            pltpu.VMEM((n + 2 * UA_ROW0, 2 * width), F32),
            pltpu.VMEM((2, n, width), F32),
            pltpu.VMEM((2, n, 2 * width), F32),
            pltpu.VMEM((2, n, width), F32),
            pltpu.VMEM((2, n, width), F32),
            pltpu.VMEM((n, width), F32),
            pltpu.VMEM((npad, width), F32),
            pltpu.VMEM((ROWS, npad - ROWS, width), F32),
            pltpu.VMEM((n, width), F32),
        ],
        compiler_params=pltpu.CompilerParams(
            dimension_semantics=("arbitrary",), vmem_limit_bytes=VMEM_LIMIT),
        name="context",
    )(ctx, *params)


def _proj_body(x_ref, mod_ref, g_ref, w_ref, hx_ref, v1_ref, v2_ref, *, nt, d_model, width):
    j = pl.program_id(1)
    is_pad = jnp.logical_or(j == 0, j == nt + 1)

    @pl.when(is_pad)
    def _():
        v2_ref[...] = jnp.zeros(v2_ref.shape, v2_ref.dtype)

    @pl.when(jnp.logical_not(is_pad))
    def _():
        hb = _modulate(x_ref[0], g_ref[...], mod_ref[0], d_model).astype(BF16)
        hx_ref[0] = hb
        u = _dot(hb, w_ref[...])
        v = u[:, :width] * jax.nn.sigmoid(u[:, width:])
        half = width // 2
        v1_ref[0] = v[:, :half].astype(BF16)
        v2_ref[0] = v[:, half:].astype(BF16)


def _proj_call(x, mod_l, g_pre_l, w_b):
    bsz, seq, d_model = x.shape
    width = w_b.shape[1] // 2
    half = width // 2
    tt = TILE_PROJ
    nt = seq // tt

    def tok(b, j):
        return (b, jnp.clip(j - 1, 0, nt - 1), 0)

    return pl.pallas_call(
        functools.partial(_proj_body, nt=nt, d_model=d_model, width=width),
        grid=(bsz, nt + 2),
        in_specs=[
            pl.BlockSpec((1, tt, d_model), tok),
            pl.BlockSpec((1, 1, 3 * d_model), lambda b, j: (b, 0, 0)),
            pl.BlockSpec((1, d_model), lambda b, j: (0, 0)),
            pl.BlockSpec(w_b.shape, lambda b, j: (0, 0)),
        ],
        out_specs=[
            pl.BlockSpec((1, tt, d_model), tok),
            pl.BlockSpec((1, tt, half), tok),
            pl.BlockSpec((1, tt, half), lambda b, j: (b, j, 0)),
        ],
        out_shape=[
            jax.ShapeDtypeStruct((bsz, seq, d_model), BF16),
            jax.ShapeDtypeStruct((bsz, seq, half), BF16),
            jax.ShapeDtypeStruct((bsz, seq + 2 * tt, half), BF16),
        ],
        compiler_params=pltpu.CompilerParams(
            dimension_semantics=("arbitrary", "arbitrary"), vmem_limit_bytes=VMEM_LIMIT),
        name="proj",
    )(x, mod_l, g_pre_l.reshape(1, d_model), w_b)


def _mix_body(hx_ref, hxp_ref, hxn_ref, v1_ref, v2_ref, wa_ref, wg_ref, cw_ref, cb_ref, br_ref,
              bi_ref, lam_ref, dww_ref, dwb_ref, lng_ref, lnb_ref, pm_ref, h0f_ref,
              p_ref, q_ref, mb_ref, ab_ref, hbo_ref,
              hxs_s, ua_s, bg_s, vc_s, g_s, hl_s, al_s, po_s, q_s, carry_s, vpad_s, sh_s,
              v2f_s, y_s,
              *, tt, nt, d_model, width, v2_front):
    i = pl.program_id(1)
    half = width // 2
    nseg = tt // GRID_W
    seg_rows = GRID_W + 2 * SEG_PAD
    ext = LRU_CONV - 1
    reach = CONV_PAD * GRID_W

    @pl.when(i == 0)
    def _():
        carry_s[...] = h0f_ref[0]

    def col_slices(total):
        return [slice(c, c + MXU_N) for c in range(0, total, MXU_N)]

    def permute(cs):
        def run():
            hxs_s[:, cs] = _dot(pm_ref[...], hx_ref[0, :, cs]).astype(BF16)
        return run

    def pad_segment(r):
        def run():
            base = r * seg_rows
            zpad = jnp.zeros((SEG_PAD, half), F32)
            vpad_s[base:base + SEG_PAD, :] = zpad
            vpad_s[base + SEG_PAD:base + SEG_PAD + GRID_W, :] = (
                v1_ref[0, r * GRID_W:(r + 1) * GRID_W, :].astype(F32))
            vpad_s[base + SEG_PAD + GRID_W:base + seg_rows, :] = zpad
        return run

    _interleave([permute(cs) for cs in col_slices(d_model)],
                [pad_segment(r) for r in range(nseg)] + _shifted_copy_pieces(vpad_s, sh_s))

    def conv_rows(src_fn, w_cols, out_row0, out_cols):
        for part in range(CHUNK // CONV_ROWS):
            acc = jnp.zeros((CONV_ROWS, half), F32)
            for k in range(CONV_K):
                acc = acc + dww_ref[k:k + 1, w_cols] * src_fn(k, part * CONV_ROWS)
            y_s[pl.ds(pl.multiple_of(out_row0 + part * CONV_ROWS, CONV_ROWS), CONV_ROWS),
                out_cols] = acc

    def project_a_and_row_conv(j, carry):
        for cc in range(2):
            ci = 2 * j + cc
            base = ci * seg_rows

            def shifted(k, r, base=base):
                off = SEG_PAD - CONV_PAD + k
                start = pl.multiple_of(base + (off // ROWS) * ROWS + r, ROWS)
                return sh_s[off % ROWS, pl.ds(start, CONV_ROWS), :]

            conv_rows(shifted, slice(0, half), ci * CHUNK, slice(0, half))
        cs = pl.ds(pl.multiple_of(j * MXU_N, MXU_N), MXU_N)
        ua_s[UA_ROW0:UA_ROW0 + tt, cs] = _dot(hxs_s[...], wa_ref[:, cs])
        return carry

    lax.fori_loop(0, 2 * width // MXU_N, project_a_and_row_conv, 0)

    wa_a = wa_ref[:, :width]
    a_prev = jnp.where(i > 0, _dot(hxp_ref[0], wa_a)[HALO - ext:, :], 0.0)
    a_next = jnp.where(i < nt - 1, _dot(hxn_ref[0], wa_a)[:ext, :], 0.0)
    bg_s[...] = _dot(hx_ref[0], wa_ref[:, 2 * width:])
    _short_convs(ua_s, a_prev, a_next, cw_ref[...], cb_ref[...], vc_s, tt, width)

    _interleave(_gate_pieces(vc_s, wg_ref, g_s, width), [])
    win0 = pl.multiple_of(i * tt + (v2_front - reach), GRID_W)
    v2f_s[...] = v2_ref[0, pl.ds(win0, tt + 2 * reach), :].astype(F32)

    def col_conv(ci, carry):
        def window(k, r):
            start = pl.multiple_of(ci * CHUNK + k * GRID_W + r, ROWS)
            return v2f_s[pl.ds(start, CONV_ROWS), :]

        conv_rows(window, slice(half, width), ci * CHUNK, slice(half, width))
        return carry

    lax.fori_loop(0, tt // CHUNK, col_conv, 0)

    hf_last, hb_first, a_total = _rglru_scan(
        ua_s, br_ref[...], bi_ref[...], lam_ref[...], carry_s[...], vc_s, g_s, hl_s, al_s,
        po_s, q_s, tt, width)
    carry_s[...] = hf_last
    ab_ref[0, 0] = a_total
    hbo_ref[0, 0] = hb_first
    p_ref[0] = po_s[...].astype(BF16)
    q_ref[0] = q_s[...].astype(BF16)

    y = y_s[...] + dwb_ref[...]
    conv = _silu(_layer_norm(y, lng_ref[...], lnb_ref[...]))
    mb_ref[0] = (conv * _silu(bg_s[...])).astype(BF16)


def _mix_call(hx, v1, v2p, w_a, wg_l, cw, cb, br, bi, lam, dww, dwb, lng, lnb, h0f):
    bsz, seq, d_model = hx.shape
    width = cb.shape[-1]
    half = width // 2
    tt = TILE_MIX
    nt = seq // tt
    hb_per_tile = tt // HALO
    n_halo_blocks = seq // HALO
    v2_front = (v2p.shape[1] - seq) // 2
    nseg = tt // GRID_W
    seg_rows = GRID_W + 2 * SEG_PAD
    reach = CONV_PAD * GRID_W
    pm, _ = _perm_matrices(tt)

    def full(a):
        nd = a.ndim
        return pl.BlockSpec(a.shape, lambda b, i, _nd=nd: (0,) * _nd)

    small = (w_a, wg_l, cw, cb, br, bi, lam, dww, dwb, lng, lnb, pm)
    tile = lambda b, i: (b, i, 0)
    out_shapes = [jax.ShapeDtypeStruct((bsz, seq, width), BF16)] * 3 + [
        jax.ShapeDtypeStruct((bsz, nt, 1, width), F32)] * 2
    return pl.pallas_call(
        functools.partial(_mix_body, tt=tt, nt=nt, d_model=d_model, width=width,
                          v2_front=v2_front),
        grid=(bsz, nt),
        in_specs=[
            pl.BlockSpec((1, tt, d_model), tile),
            pl.BlockSpec((1, HALO, d_model),
                         lambda b, i: (b, jnp.maximum(i * hb_per_tile - 1, 0), 0)),
            pl.BlockSpec((1, HALO, d_model),
                         lambda b, i: (b, jnp.minimum((i + 1) * hb_per_tile, n_halo_blocks - 1), 0)),
            pl.BlockSpec((1, tt, half), tile),
            pl.BlockSpec((1, v2p.shape[1], half), lambda b, i: (b, 0, 0)),
        ] + [full(a) for a in small] + [pl.BlockSpec((1, 1, width), lambda b, i: (b, 0, 0))],
        out_specs=[pl.BlockSpec((1, tt, width), tile)] * 3 + [
            pl.BlockSpec((1, 1, 1, width), lambda b, i: (b, i, 0, 0))] * 2,
        out_shape=out_shapes,
        scratch_shapes=[
            pltpu.VMEM((tt, d_model), BF16),
            pltpu.VMEM((tt + 2 * UA_ROW0, 2 * width), F32),
            pltpu.VMEM((tt, width), F32),
            pltpu.VMEM((2, tt, width), F32),
            pltpu.VMEM((2, tt, 2 * width), F32),
            pltpu.VMEM((2, tt, width), F32),
            pltpu.VMEM((2, tt, width), F32),
            pltpu.VMEM((tt, width), F32),
            pltpu.VMEM((tt, width), F32),
            pltpu.VMEM((1, width), F32),
            pltpu.VMEM((nseg * seg_rows, half), F32),
            pltpu.VMEM((ROWS, nseg * seg_rows - ROWS, half), F32),
            pltpu.VMEM((tt + 2 * reach, half), F32),
            pltpu.VMEM((tt, width), F32),
        ],
        compiler_params=pltpu.CompilerParams(
            dimension_semantics=("arbitrary", "arbitrary"), vmem_limit_bytes=VMEM_LIMIT,
            ),
        name="mixer",
    )(hx, hx, hx, v1, v2p, *small, h0f)


def _out_body(p_ref, q_ref, mb_ref, x_ref, wo_ref, gpost_ref, mod_ref, ab_ref, hb_ref, h0b_ref,
              pmt_ref, o_ref, carry_s, *, d_model, width):
    @pl.when(pl.program_id(1) == 0)
    def _():
        carry_s[...] = h0b_ref[0]

    c = carry_s[...]
    mix_a = (p_ref[0].astype(F32) + q_ref[0].astype(F32) * c).astype(BF16)
    mix_a = _dot(pmt_ref[...], mix_a).astype(BF16)
    mix = _dot(mix_a, wo_ref[:width, :]) + _dot(mb_ref[0], wo_ref[width:, :])
    gate = mod_ref[0][:, 2 * d_model:]
    o_ref[0] = x_ref[0] + gate * _rms_norm(mix, gpost_ref[...])
    carry_s[...] = hb_ref[0, 0] + ab_ref[0, 0] * c


def _out_call(p, q, mb, x, w_o, g_post_l, mod_l, ab, hbo, h0b):
    bsz, seq, d_model = x.shape
    width = p.shape[-1]
    tt = TILE_MIX
    nt = seq // tt
    rev = lambda b, j: (b, nt - 1 - j, 0)
    _, pmt = _perm_matrices(tt)
    return pl.pallas_call(
        functools.partial(_out_body, d_model=d_model, width=width),
        grid=(bsz, nt),
        in_specs=[
            pl.BlockSpec((1, tt, width), rev),
            pl.BlockSpec((1, tt, width), rev),
            pl.BlockSpec((1, tt, width), rev),
            pl.BlockSpec((1, tt, d_model), rev),
            pl.BlockSpec(w_o.shape, lambda b, j: (0, 0)),
            pl.BlockSpec((1, d_model), lambda b, j: (0, 0)),
            pl.BlockSpec((1, 1, 3 * d_model), lambda b, j: (b, 0, 0)),
            pl.BlockSpec((1, 1, 1, width), lambda b, j: (b, nt - 1 - j, 0, 0)),
            pl.BlockSpec((1, 1, 1, width), lambda b, j: (b, nt - 1 - j, 0, 0)),
            pl.BlockSpec((1, 1, width), lambda b, j: (b, 0, 0)),
            pl.BlockSpec((tt, tt), lambda b, j: (0, 0)),
        ],
        out_specs=pl.BlockSpec((1, tt, d_model), rev),
        out_shape=jax.ShapeDtypeStruct((bsz, seq, d_model), F32),
        scratch_shapes=[pltpu.VMEM((1, width), F32)],
        compiler_params=pltpu.CompilerParams(
            dimension_semantics=("arbitrary", "arbitrary"), vmem_limit_bytes=VMEM_LIMIT),
        name="out",
    )(p, q, mb, x, w_o, g_post_l.reshape(1, d_model), mod_l, ab, hbo, h0b, pmt)


def _pack_gate_weights(w_r, w_i):
    depth, ndir, heads, hd, _ = w_r.shape
    hh = heads // 2
    eye = jnp.eye(hh, dtype=w_r.dtype)

    def bd(w):
        w = w.reshape(depth, ndir, 2, hh, hd, hd)
        return jnp.einsum("ldfhij,hg->ldfhigj", w, eye).reshape(depth, ndir, 2, hh * hd, hh * hd)

    return jnp.concatenate([bd(w_r), bd(w_i)], axis=-1).astype(BF16)


def kernel(x, c, ctx, c_ctx, w_mod, b_mod, g_pre, g_post, w_in, conv_a_w, conv_a_b, w_rgate,
           b_rgate, w_igate, b_igate, lru_lambda, dw_w, dw_b, ln_g, ln_b, w_out):
    bsz, seq, d_model = x.shape
    depth = w_mod.shape[0]
    width = conv_a_b.shape[-1]
    assert bsz + 1 <= ROWS and seq % TILE_PROJ == 0 and seq % TILE_MIX == 0
    assert TILE_MIX % GRID_W == 0 and TILE_PROJ >= CONV_PAD * GRID_W
    assert ctx.shape[1] % CHUNK == 0 and width % MXU_N == 0 and d_model % MXU_N == 0

    act = jnp.concatenate(
        [c, c_ctx[None, :], jnp.zeros((ROWS - bsz - 1, d_model), F32)], axis=0)
    mods = _mods_call(act, w_mod, b_mod)

    w_in_bf = w_in.astype(BF16)
    w_out_bf = w_out.astype(BF16)
    wg = _pack_gate_weights(w_rgate, w_igate)

    h0 = _ctx_call(ctx, mods[:, bsz:bsz + 1, :], g_pre, g_post, w_in_bf, w_out_bf, wg, conv_a_w,
                   conv_a_b, b_rgate, b_igate, lru_lambda, dw_w, dw_b, ln_g, ln_b)

    for l in range(depth):
        mod_l = mods[l].reshape(ROWS, 1, 3 * d_model)
        w_b = w_in_bf[l, :, 2 * width:4 * width]
        w_a = jnp.concatenate([w_in_bf[l, :, :2 * width], w_in_bf[l, :, 4 * width:]], axis=1)
        hx, v1, v2p = _proj_call(x, mod_l, g_pre[l], w_b)
        p, q, mb, ab, hbo = _mix_call(
            hx, v1, v2p, w_a, wg[l], conv_a_w[l], conv_a_b[l], b_rgate[l], b_igate[l],
            lru_lambda[l], dw_w[l], dw_b[l:l + 1], ln_g[l:l + 1], ln_b[l:l + 1],
            h0[:, 2 * l:2 * l + 1, :])
        x = _out_call(p, q, mb, x, w_out_bf[l], g_post[l], mod_l, ab, hbo,
                      h0[:, 2 * l + 1:2 * l + 2, :])
    return x
```

```python
import functools

import jax
import jax.numpy as jnp
from jax import lax
from jax.experimental import pallas as pl
from jax.experimental.pallas import tpu as pltpu

F32 = jnp.float32
BF16 = jnp.bfloat16

EPS = 1e-6
LRU_C = 8.0
LRU_CONV = 4
CONV_K = 31
CONV_PAD = CONV_K // 2
GRID_W = 64
HALO = 16
ROWS = 8
CHUNK = 64
CONV_ROWS = 32
SEG_PAD = 16
MXU_N = 256
SPARE_ROWS = 16

TILE_PROJ = 1024
TILE_MIX = 512
VMEM_LIMIT = 56 * 1024 * 1024


def _silu(x):
    return x * jax.nn.sigmoid(x)


def _softplus(x):
    return jnp.maximum(x, 0.0) + jnp.log1p(jnp.exp(-jnp.abs(x)))


def _rms_norm(x, g):
    ms = jnp.mean(x * x, axis=-1, keepdims=True)
    return x * lax.rsqrt(ms + EPS) * g


def _layer_norm(x, g, b):
    mu = jnp.mean(x, axis=-1, keepdims=True)
    xc = x - mu
    var = jnp.mean(xc * xc, axis=-1, keepdims=True)
    return xc * lax.rsqrt(var + EPS) * g + b


def _dot(a, b):
    return jnp.dot(a, b, preferred_element_type=F32)


def _interleave(mxu_pieces, vpu_pieces):
    n = len(mxu_pieces)
    per = -(-len(vpu_pieces) // n) if n else 0
    for j, piece in enumerate(mxu_pieces):
        piece()
        for fill in vpu_pieces[j * per:(j + 1) * per]:
            fill()
    for fill in vpu_pieces[n * per:]:
        fill()


def _mark(spare_ref, token):
    if token is None:
        return
    reps = spare_ref.shape[0] // ROWS
    spare_ref[...] = jnp.concatenate([token] * reps, axis=0).astype(spare_ref.dtype)


def _chain(mxu_pieces, vpu_pieces, after):
    tokens = []
    for piece, dep in zip(mxu_pieces, after):
        while dep is not None and len(tokens) <= dep:
            tokens.append(vpu_pieces[len(tokens)]())
        piece(None if dep is None else tokens[dep])
    for fill in vpu_pieces[len(tokens):]:
        fill()


def _perm_matrices(n):
    i = jnp.arange(n)
    src = (n // ROWS) * (i % ROWS) + i // ROWS
    pm = (src[:, None] == i[None, :]).astype(BF16)
    return pm, pm.T


UA_ROW0 = (LRU_CONV - 1) * ROWS


def _short_convs(ua_s, a_prev, a_next, cw, cb, vc_s, n, width):
    sub = n // ROWS
    ext = LRU_CONV - 1
    row = lax.broadcasted_iota(jnp.int32, (ROWS, width), 0)

    def blk(p):
        return slice(p * ROWS, (p + 1) * ROWS)

    for k in range(1, ext + 1):
        tail = pltpu.roll(ua_s[blk(ext + sub - k), :width], 1, 0)
        ua_s[blk(ext - k), :width] = jnp.where(row == 0, a_prev[ext - k:ext - k + 1, :], tail)
    for k in range(ext):
        head = pltpu.roll(ua_s[blk(ext + k), :width], ROWS - 1, 0)
        ua_s[blk(ext + sub + k), :width] = jnp.where(row == ROWS - 1, a_next[k:k + 1, :], head)

    def chunk(ci, carry):
        base = ci * CHUNK
        for d in range(2):
            first = 0 if d == 0 else ext
            acc = jnp.broadcast_to(cb[d:d + 1, :], (CHUNK, width))
            for k in range(LRU_CONV):
                start = pl.multiple_of(base + (first + k) * ROWS, ROWS)
                acc = acc + cw[d, k:k + 1, :] * ua_s[pl.ds(start, CHUNK), :width]
            vc_s[d, pl.ds(pl.multiple_of(base, CHUNK), CHUNK), :] = acc
        return carry

    lax.fori_loop(0, n // CHUNK, chunk, 0)


def _gate_pieces(vc_s, wg_ref, g_s, width):
    half = width // 2

    def piece(d, hf):
        def run():
            res = _dot(vc_s[d, :, hf * half:(hf + 1) * half].astype(BF16), wg_ref[d, hf])
            g_s[d, :, hf * half:(hf + 1) * half] = res[:, :half]
            g_s[d, :, width + hf * half:width + (hf + 1) * half] = res[:, half:]
        return run

    return [piece(d, hf) for d in range(2) for hf in range(2)]


def _sublane_scan(a, h, row, reverse):
    for s in (1, 2, 4):
        if reverse:
            m, shift = row < ROWS - s, ROWS - s
        else:
            m, shift = row >= s, s
        a_sh = jnp.where(m, pltpu.roll(a, shift, 0), 1.0)
        h_sh = jnp.where(m, pltpu.roll(h, shift, 0), 0.0)
        h = h + a * h_sh
        a = a * a_sh
    return a, h


def _rglru_scan(ua_s, br, bi, lam, c0, vc_s, g_s, hl_s, al_s, po_s, q_s, n, width):
    sub = n // ROWS
    row = lax.broadcasted_iota(jnp.int32, (ROWS, width), 0)
    spl = -LRU_C * _softplus(-lam)

    def coeffs(d, r0):
        g = g_s[d, pl.ds(r0, ROWS), :]
        vc = vc_s[d, pl.ds(r0, ROWS), :]
        r = jax.nn.sigmoid(g[:, :width] + br[d:d + 1, :])
        ig = jax.nn.sigmoid(g[:, width:] + bi[d:d + 1, :])
        la = spl[d:d + 1, :] * r
        a = jnp.exp(la)
        x = jnp.tanh(la) * (-1.0 - a * a)
        mult = jnp.where(x > 0.0, x * lax.rsqrt(x), 0.0)
        return a, mult * (ig * vc)

    def local(j, carry):
        hf, af, hb, ab = carry
        r0 = pl.multiple_of(j * ROWS, ROWS)
        a, b = coeffs(0, r0)
        hf = a * hf + b
        af = a * af
        hl_s[0, pl.ds(r0, ROWS), :] = hf
        al_s[0, pl.ds(r0, ROWS), :] = af
        r1 = pl.multiple_of((sub - 1 - j) * ROWS, ROWS)
        a, b = coeffs(1, r1)
        hb = a * hb + b
        ab = a * ab
        hl_s[1, pl.ds(r1, ROWS), :] = hb
        al_s[1, pl.ds(r1, ROWS), :] = ab
        return hf, af, hb, ab

    zero = jnp.zeros((ROWS, width), F32)
    one = jnp.ones((ROWS, width), F32)
    hf, af, hb, ab = lax.fori_loop(0, sub, local, (zero, one, zero, one), unroll=4)

    af, hf = _sublane_scan(af, hf, row, reverse=False)
    end_f = hf + af * c0
    c_f = jnp.where(row == 0, c0, pltpu.roll(end_f, 1, 0))
    ab, hb = _sublane_scan(ab, hb, row, reverse=True)
    c_b = jnp.where(row == ROWS - 1, 0.0, pltpu.roll(hb, ROWS - 1, 0))
    c_a = jnp.where(row == ROWS - 1, 1.0, pltpu.roll(ab, ROWS - 1, 0))

    if po_s is not None:
        def fix(j, carry):
            r0 = pl.ds(pl.multiple_of(j * ROWS, ROWS), ROWS)
            a_b = al_s[1, r0, :]
            h = (hl_s[0, r0, :] + al_s[0, r0, :] * c_f) + (hl_s[1, r0, :] + a_b * c_b)
            sg = _silu(ua_s[pl.ds(pl.multiple_of(UA_ROW0 + j * ROWS, ROWS), ROWS), width:])
            po_s[r0, :] = h * sg
            if q_s is not None:
                q_s[r0, :] = (a_b * c_a) * sg
            return carry

        lax.fori_loop(0, sub, fix, 0, unroll=8)

    return end_f[ROWS - 1:ROWS, :], hb[0:1, :], ab[0:1, :]


def _shifted_copy_pieces(vpad_ref, sh_ref):
    n = sh_ref.shape[1]

    def piece(j):
        def run():
            sh_ref[j] = vpad_ref[pl.ds(j, n), :]
        return run

    return [piece(j) for j in range(ROWS)]


def _time_conv_pieces(sh_ref, w_ref, ncol, bases, out_ref):
    def piece(ci, base):
        def run():
            acc = jnp.zeros((CHUNK, ncol), F32)
            for k in range(CONV_K):
                off = SEG_PAD - CONV_PAD + k
                start = base + (off // ROWS) * ROWS
                acc = acc + w_ref[k:k + 1, 0:ncol] * sh_ref[off % ROWS, start:start + CHUNK, :]
            out_ref[ci * CHUNK:(ci + 1) * CHUNK, 0:ncol] = acc
        return run

    return [piece(ci, base) for ci, base in enumerate(bases)]


def _mods_body(act_ref, w_ref, b_ref, o_ref):
    a = _silu(act_ref[...])
    o_ref[0] = jnp.dot(a, w_ref[0], precision=lax.Precision.HIGHEST,
                       preferred_element_type=F32) + b_ref[0]


def _mods_call(act, w_mod, b_mod):
    depth, d_model, d3 = w_mod.shape
    ncol = d_model
    return pl.pallas_call(
        _mods_body,
        grid=(depth, d3 // ncol),
        in_specs=[
            pl.BlockSpec((ROWS, d_model), lambda l, n: (0, 0)),
            pl.BlockSpec((1, d_model, ncol), lambda l, n: (l, 0, n)),
            pl.BlockSpec((1, 1, ncol), lambda l, n: (l, 0, n)),
        ],
        out_specs=pl.BlockSpec((1, ROWS, ncol), lambda l, n: (l, 0, n)),
        out_shape=jax.ShapeDtypeStruct((depth, ROWS, d3), F32),
        compiler_params=pltpu.CompilerParams(
            dimension_semantics=("arbitrary", "arbitrary"), vmem_limit_bytes=VMEM_LIMIT),
        name="mods",
    )(act, w_mod, b_mod.reshape(depth, 1, d3))


def _modulate(x, g, m, d_model):
    return _rms_norm(x, g) * (1.0 + m[:, d_model:2 * d_model]) + m[:, :d_model]


def _ctx_body(ctx_ref, mod_ref, gpre_ref, gpost_ref, win_ref, wo_ref, wg_ref, cw_ref, cb_ref,
              br_ref, bi_ref, lam_ref, dww_ref, dwb_ref, lng_ref, lnb_ref, pm_ref, pmt_ref,
              h0_ref,
              ua_s, vc_s, g_s, hl_s, al_s, po_s, vpad_s, sh_s, y_s,
              *, depth, d_model, width, n):
    xc = ctx_ref[0]
    zeros_pad = jnp.zeros((SEG_PAD, width), F32)
    no_rows = jnp.zeros((LRU_CONV - 1, width), F32)
    zero_state = jnp.zeros((1, width), F32)
    for l in range(depth):
        update = l < depth - 1
        m = mod_ref[l]
        hc = _modulate(xc, gpre_ref[l:l + 1, :], m, d_model).astype(BF16)
        hcp = _dot(pm_ref[...], hc).astype(BF16)
        if update:
            ua_s[UA_ROW0:UA_ROW0 + n, :] = _dot(hcp, win_ref[l, :, :2 * width])
            ub = _dot(hc, win_ref[l, :, 2 * width:])
        else:
            ua_s[UA_ROW0:UA_ROW0 + n, :width] = _dot(hcp, win_ref[l, :, :width])
        _short_convs(ua_s, no_rows, no_rows, cw_ref[l], cb_ref[l], vc_s, n, width)
        _interleave(_gate_pieces(vc_s, wg_ref.at[l], g_s, width), [])
        hf_last, hb_first, _ = _rglru_scan(
            ua_s, br_ref[l], bi_ref[l], lam_ref[l], zero_state, vc_s, g_s, hl_s, al_s,
            po_s if update else None, None, n, width)
        h0_ref[0, 2 * l:2 * l + 1, :] = hf_last
        h0_ref[0, 2 * l + 1:2 * l + 2, :] = hb_first
        if update:
            v = ub[:, :width] * jax.nn.sigmoid(ub[:, width:2 * width])
            vpad_s[0:SEG_PAD, :] = zeros_pad
            vpad_s[SEG_PAD:SEG_PAD + n, :] = v
            vpad_s[SEG_PAD + n:, :] = zeros_pad
            _interleave([], _shifted_copy_pieces(vpad_s, sh_s))
            _interleave([], _time_conv_pieces(
                sh_s, dww_ref.at[l], width, [ci * CHUNK for ci in range(n // CHUNK)], y_s))
            y = y_s[...] + dwb_ref[l:l + 1, :]
            conv = _silu(_layer_norm(y, lng_ref[l:l + 1, :], lnb_ref[l:l + 1, :]))
            mix_a = _dot(pmt_ref[...], po_s[...].astype(BF16)).astype(BF16)
            mix_b = (conv * _silu(ub[:, 2 * width:])).astype(BF16)
            mix = _dot(mix_a, wo_ref[l, :width, :]) + _dot(mix_b, wo_ref[l, width:, :])
            xc = xc + m[:, 2 * d_model:] * _rms_norm(mix, gpost_ref[l:l + 1, :])


def _ctx_call(ctx, mod_c, g_pre, g_post, w_in_bf, w_out_bf, wg, conv_a_w, conv_a_b, b_rgate,
              b_igate, lru_lambda, dw_w, dw_b, ln_g, ln_b):
    bsz, n, d_model = ctx.shape
    depth = w_in_bf.shape[0]
    width = conv_a_b.shape[-1]
    npad = n + 2 * SEG_PAD
    pm, pmt = _perm_matrices(n)

    def full(a):
        nd = a.ndim
        return pl.BlockSpec(a.shape, lambda b, _nd=nd: (0,) * _nd)

    params = (mod_c, g_pre, g_post, w_in_bf, w_out_bf, wg, conv_a_w, conv_a_b, b_rgate, b_igate,
              lru_lambda, dw_w, dw_b, ln_g, ln_b, pm, pmt)
    return pl.pallas_call(
        functools.partial(_ctx_body, depth=depth, d_model=d_model, width=width, n=n),
        grid=(bsz,),
        in_specs=[pl.BlockSpec((1, n, d_model), lambda b: (b, 0, 0))] + [full(a) for a in params],
        out_specs=pl.BlockSpec((1, 2 * depth, width), lambda b: (b, 0, 0)),
        out_shape=jax.ShapeDtypeStruct((bsz, 2 * depth, width), F32),
        scratch_shapes=[
            pltpu.VMEM((n + 2 * UA_ROW0, 2 * width), F32),
            pltpu.VMEM((2, n, width), F32),
            pltpu.VMEM((2, n, 2 * width), F32),
            pltpu.VMEM((2, n, width), F32),
            pltpu.VMEM((2, n, width), F32),
            pltpu.VMEM((n, width), F32),
            pltpu.VMEM((npad, width), F32),
            pltpu.VMEM((ROWS, npad - ROWS, width), F32),
            pltpu.VMEM((n, width), F32),
        ],
        compiler_params=pltpu.CompilerParams(
            dimension_semantics=("arbitrary",), vmem_limit_bytes=VMEM_LIMIT),
        name="context",
    )(ctx, *params)


def _proj_body(x_ref, mod_ref, g_ref, w_ref, hx_ref, v1_ref, v2_ref, *, nt, d_model, width):
    j = pl.program_id(1)
    is_pad = jnp.logical_or(j == 0, j == nt + 1)

    @pl.when(is_pad)
    def _():
        v2_ref[...] = jnp.zeros(v2_ref.shape, v2_ref.dtype)

    @pl.when(jnp.logical_not(is_pad))
    def _():
        half = width // 2
        rows = x_ref.shape[1] // 4
        for c in range(4):
            rs = slice(c * rows, (c + 1) * rows)
            hb = _modulate(x_ref[0, rs, :], g_ref[...], mod_ref[0], d_model).astype(BF16)
            hx_ref[0, rs, :] = hb
            u = _dot(hb, w_ref[...])
            v = u[:, :width] * jax.nn.sigmoid(u[:, width:])
            v1_ref[0, rs, :] = v[:, :half].astype(BF16)
            v2_ref[0, rs, :] = v[:, half:].astype(BF16)


def _proj_call(x, mod_l, g_pre_l, w_b):
    bsz, seq, d_model = x.shape
    width = w_b.shape[1] // 2
    half = width // 2
    tt = TILE_PROJ
    nt = seq // tt

    def tok(b, j):
        return (b, jnp.clip(j - 1, 0, nt - 1), 0)

    return pl.pallas_call(
        functools.partial(_proj_body, nt=nt, d_model=d_model, width=width),
        grid=(bsz, nt + 2),
        in_specs=[
            pl.BlockSpec((1, tt, d_model), tok),
            pl.BlockSpec((1, 1, 3 * d_model), lambda b, j: (b, 0, 0)),
            pl.BlockSpec((1, d_model), lambda b, j: (0, 0)),
            pl.BlockSpec(w_b.shape, lambda b, j: (0, 0)),
        ],
        out_specs=[
            pl.BlockSpec((1, tt, d_model), tok),
            pl.BlockSpec((1, tt, half), tok),
            pl.BlockSpec((1, tt, half), lambda b, j: (b, j, 0)),
        ],
        out_shape=[
            jax.ShapeDtypeStruct((bsz, seq, d_model), BF16),
            jax.ShapeDtypeStruct((bsz, seq, half), BF16),
            jax.ShapeDtypeStruct((bsz, seq + 2 * tt, half), BF16),
        ],
        compiler_params=pltpu.CompilerParams(
            dimension_semantics=("arbitrary", "arbitrary"), vmem_limit_bytes=VMEM_LIMIT),
        name="proj",
    )(x, mod_l, g_pre_l.reshape(1, d_model), w_b)


def _mix_body(hx_ref, hxp_ref, hxn_ref, v1_ref, v2_ref, wa_ref, wg_ref, cw_ref, cb_ref, br_ref,
              bi_ref, lam_ref, dww_ref, dwb_ref, lng_ref, lnb_ref, pm_ref, h0f_ref, zero_ref,
              p_ref, q_ref, mb_ref, ab_ref, hbo_ref,
              hxs_s, ua_s, bg_s, vc_s, g_s, hl_s, al_s, po_s, q_s, carry_s, vpad_s, sh_s,
              v2f_s, y_s,
              *, tt, nt, d_model, width, v2_front):
    i = pl.program_id(1)
    half = width // 2
    nseg = tt // GRID_W
    seg_rows = GRID_W + 2 * SEG_PAD
    ext = LRU_CONV - 1
    reach = CONV_PAD * GRID_W
    row_zero = zero_ref[0]

    @pl.when(i == 0)
    def _():
        carry_s[...] = h0f_ref[0]

    def col_slices(total):
        return [slice(c, c + MXU_N) for c in range(0, total, MXU_N)]


    def pad_segment(r):
        def run():
            base = r * seg_rows
            zpad = jnp.zeros((SEG_PAD, half), F32)
            vpad_s[base:base + SEG_PAD, :] = zpad
            vpad_s[base + SEG_PAD:base + SEG_PAD + GRID_W, :] = (
                v1_ref[0, r * GRID_W:(r + 1) * GRID_W, :].astype(F32))
            vpad_s[base + SEG_PAD + GRID_W:base + seg_rows, :] = zpad
        return run

    def load_window():
        win0 = pl.multiple_of(i * tt + (v2_front - reach), GRID_W)
        v2f_s[...] = v2_ref[0, pl.ds(win0, tt + 2 * reach), :].astype(F32)

    def conv_chunk(src_fn, w_cols, ci, out_cols):
        def run():
            token = None
            for part in range(CHUNK // CONV_ROWS):
                acc = jnp.zeros((CONV_ROWS, half), F32)
                for k in range(CONV_K):
                    acc = acc + dww_ref[k:k + 1, w_cols] * src_fn(ci, k, part * CONV_ROWS)
                row0 = ci * CHUNK + part * CONV_ROWS
                y_s[row0:row0 + CONV_ROWS, out_cols] = acc
                token = acc[0:ROWS, 0:128]
            return token
        return run

    def shifted(ci, k, r):
        off = SEG_PAD - CONV_PAD + k
        start = ci * seg_rows + (off // ROWS) * ROWS + r
        return sh_s[off % ROWS, start:start + CONV_ROWS, :]

    def window(ci, k, r):
        start = ci * CHUNK + k * GRID_W + r
        return v2f_s[start:start + CONV_ROWS, :]

    def permute(cs):
        def run(token):
            hxs_s[0:tt, cs] = _dot(pm_ref[...], hx_ref[0, :, cs]).astype(BF16)
        return run

    def project_a(cs):
        def run(token):
            _mark(hxs_s.at[tt:tt + SPARE_ROWS, 0:128], token)
            lhs = hxs_s[pl.ds(pl.multiple_of(row_zero, SPARE_ROWS), tt), :]
            ua_s[UA_ROW0:UA_ROW0 + tt, cs] = _dot(lhs, wa_ref[:, cs])
        return run

    def project_b(cs):
        def run(token):
            cols = slice(2 * width + cs.start, 2 * width + cs.stop)
            bg_s[:, cs] = _dot(hx_ref[0], wa_ref[:, cols])
        return run

    prep = ([pad_segment(r) for r in range(nseg)] + _shifted_copy_pieces(vpad_s, sh_s)
            + [load_window])
    rows_first = len(prep)
    cols_first = rows_first + nseg
    vpu = (prep
           + [conv_chunk(shifted, slice(0, half), ci, slice(0, half)) for ci in range(nseg)]
           + [conv_chunk(window, slice(half, width), ci, slice(half, width))
              for ci in range(tt // CHUNK)])
    mxu = ([permute(cs) for cs in col_slices(d_model)]
           + [project_a(cs) for cs in col_slices(2 * width)]
           + [project_b(cs) for cs in col_slices(width)])
    after = [None, None, rows_first, rows_first + 1,
             rows_first + 2, rows_first + 4, rows_first + 5, rows_first + 7,
             cols_first + 1, cols_first + 2]
    _chain(mxu, vpu, after)

    wa_a = wa_ref[:, :width]
    a_prev = jnp.where(i > 0, _dot(hxp_ref[0], wa_a)[HALO - ext:, :], 0.0)
    a_next = jnp.where(i < nt - 1, _dot(hxn_ref[0], wa_a)[:ext, :], 0.0)
    _short_convs(ua_s, a_prev, a_next, cw_ref[...], cb_ref[...], vc_s, tt, width)

    def finish_conv(c, nrows):
        def run():
            rs = slice(c * nrows, (c + 1) * nrows)
            y = y_s[rs, :] + dwb_ref[...]
            conv = _silu(_layer_norm(y, lng_ref[...], lnb_ref[...]))
            out = conv * _silu(bg_s[rs, :])
            mb_ref[0, rs, :] = out.astype(BF16)
            return out[0:ROWS, 0:128]
        return run

    def gate_piece(d, hf):
        def run(token):
            lo = hf * half
            _mark(vc_s.at[d, tt:tt + SPARE_ROWS, lo:lo + 128], token)
            lhs = vc_s[d, pl.ds(pl.multiple_of(row_zero, SPARE_ROWS), tt), lo:lo + half]
            res = _dot(lhs.astype(BF16), wg_ref[d, hf])
            g_s[d, :, lo:lo + half] = res[:, :half]
            g_s[d, :, width + lo:width + lo + half] = res[:, half:]
        return run

    _chain([gate_piece(d, hf) for d in range(2) for hf in range(2)],
           [finish_conv(c, tt // 4) for c in range(4)], [None, 0, 1, 2])

    hf_last, hb_first, a_total = _rglru_scan(
        ua_s, br_ref[...], bi_ref[...], lam_ref[...], carry_s[...], vc_s, g_s, hl_s, al_s,
        po_s, q_s, tt, width)
    carry_s[...] = hf_last
    ab_ref[0, 0] = a_total
    hbo_ref[0, 0] = hb_first
    p_ref[0] = po_s[...].astype(BF16)
    q_ref[0] = q_s[...].astype(BF16)


def _mix_call(hx, v1, v2p, w_a, wg_l, cw, cb, br, bi, lam, dww, dwb, lng, lnb, h0f):
    bsz, seq, d_model = hx.shape
    width = cb.shape[-1]
    half = width // 2
    tt = TILE_MIX
    nt = seq // tt
    hb_per_tile = tt // HALO
    n_halo_blocks = seq // HALO
    v2_front = (v2p.shape[1] - seq) // 2
    nseg = tt // GRID_W
    seg_rows = GRID_W + 2 * SEG_PAD
    reach = CONV_PAD * GRID_W
    pm, _ = _perm_matrices(tt)

    def full(a):
        nd = a.ndim
        return pl.BlockSpec(a.shape, lambda b, i, _nd=nd: (0,) * _nd)

    small = (w_a, wg_l, cw, cb, br, bi, lam, dww, dwb, lng, lnb, pm)
    tile = lambda b, i: (b, i, 0)
    out_shapes = [jax.ShapeDtypeStruct((bsz, seq, width), BF16)] * 3 + [
        jax.ShapeDtypeStruct((bsz, nt, 1, width), F32)] * 2
    return pl.pallas_call(
        functools.partial(_mix_body, tt=tt, nt=nt, d_model=d_model, width=width,
                          v2_front=v2_front),
        grid=(bsz, nt),
        in_specs=[
            pl.BlockSpec((1, tt, d_model), tile),
            pl.BlockSpec((1, HALO, d_model),
                         lambda b, i: (b, jnp.maximum(i * hb_per_tile - 1, 0), 0)),
            pl.BlockSpec((1, HALO, d_model),
                         lambda b, i: (b, jnp.minimum((i + 1) * hb_per_tile, n_halo_blocks - 1), 0)),
            pl.BlockSpec((1, tt, half), tile),
            pl.BlockSpec((1, v2p.shape[1], half), lambda b, i: (b, 0, 0)),
        ] + [full(a) for a in small] + [
            pl.BlockSpec((1, 1, width), lambda b, i: (b, 0, 0)),
            pl.BlockSpec(memory_space=pltpu.SMEM),
        ],
        out_specs=[pl.BlockSpec((1, tt, width), tile)] * 3 + [
            pl.BlockSpec((1, 1, 1, width), lambda b, i: (b, i, 0, 0))] * 2,
        out_shape=out_shapes,
        scratch_shapes=[
            pltpu.VMEM((tt + SPARE_ROWS, d_model), BF16),
            pltpu.VMEM((tt + 2 * UA_ROW0, 2 * width), F32),
            pltpu.VMEM((tt, width), F32),
            pltpu.VMEM((2, tt + SPARE_ROWS, width), F32),
            pltpu.VMEM((2, tt, 2 * width), F32),
            pltpu.VMEM((2, tt, width), F32),
            pltpu.VMEM((2, tt, width), F32),
            pltpu.VMEM((tt, width), F32),
            pltpu.VMEM((tt, width), F32),
            pltpu.VMEM((1, width), F32),
            pltpu.VMEM((nseg * seg_rows, half), F32),
            pltpu.VMEM((ROWS, nseg * seg_rows - ROWS, half), F32),
            pltpu.VMEM((tt + 2 * reach, half), F32),
            pltpu.VMEM((tt, width), F32),
        ],
        compiler_params=pltpu.CompilerParams(
            dimension_semantics=("arbitrary", "arbitrary"), vmem_limit_bytes=VMEM_LIMIT,
            ),
        name="mixer",
    )(hx, hx, hx, v1, v2p, *small, h0f, jnp.zeros((1,), jnp.int32))


def _out_body(p_ref, q_ref, mb_ref, x_ref, wo_ref, gpost_ref, mod_ref, ab_ref, hb_ref, h0b_ref,
              pmt_ref, o_ref, carry_s, *, d_model, width):
    @pl.when(pl.program_id(1) == 0)
    def _():
        carry_s[...] = h0b_ref[0]

    c = carry_s[...]
    mix_a = (p_ref[0].astype(F32) + q_ref[0].astype(F32) * c).astype(BF16)
    mix_a = _dot(pmt_ref[...], mix_a).astype(BF16)
    mix = _dot(mix_a, wo_ref[:width, :]) + _dot(mb_ref[0], wo_ref[width:, :])
    gate = mod_ref[0][:, 2 * d_model:]
    o_ref[0] = x_ref[0] + gate * _rms_norm(mix, gpost_ref[...])
    carry_s[...] = hb_ref[0, 0] + ab_ref[0, 0] * c


def _out_call(p, q, mb, x, w_o, g_post_l, mod_l, ab, hbo, h0b):
    bsz, seq, d_model = x.shape
    width = p.shape[-1]
    tt = TILE_MIX
    nt = seq // tt
    rev = lambda b, j: (b, nt - 1 - j, 0)
    _, pmt = _perm_matrices(tt)
    return pl.pallas_call(
        functools.partial(_out_body, d_model=d_model, width=width),
        grid=(bsz, nt),
        in_specs=[
            pl.BlockSpec((1, tt, width), rev),
            pl.BlockSpec((1, tt, width), rev),
            pl.BlockSpec((1, tt, width), rev),
            pl.BlockSpec((1, tt, d_model), rev),
            pl.BlockSpec(w_o.shape, lambda b, j: (0, 0)),
            pl.BlockSpec((1, d_model), lambda b, j: (0, 0)),
            pl.BlockSpec((1, 1, 3 * d_model), lambda b, j: (b, 0, 0)),
            pl.BlockSpec((1, 1, 1, width), lambda b, j: (b, nt - 1 - j, 0, 0)),
            pl.BlockSpec((1, 1, 1, width), lambda b, j: (b, nt - 1 - j, 0, 0)),
            pl.BlockSpec((1, 1, width), lambda b, j: (b, 0, 0)),
            pl.BlockSpec((tt, tt), lambda b, j: (0, 0)),
        ],
        out_specs=pl.BlockSpec((1, tt, d_model), rev),
        out_shape=jax.ShapeDtypeStruct((bsz, seq, d_model), F32),
        scratch_shapes=[pltpu.VMEM((1, width), F32)],
        compiler_params=pltpu.CompilerParams(
            dimension_semantics=("arbitrary", "arbitrary"), vmem_limit_bytes=VMEM_LIMIT),
        name="out",
    )(p, q, mb, x, w_o, g_post_l.reshape(1, d_model), mod_l, ab, hbo, h0b, pmt)


def _pack_gate_weights(w_r, w_i):
    depth, ndir, heads, hd, _ = w_r.shape
    hh = heads // 2
    eye = jnp.eye(hh, dtype=w_r.dtype)

    def bd(w):
        w = w.reshape(depth, ndir, 2, hh, hd, hd)
        return jnp.einsum("ldfhij,hg->ldfhigj", w, eye).reshape(depth, ndir, 2, hh * hd, hh * hd)

    return jnp.concatenate([bd(w_r), bd(w_i)], axis=-1).astype(BF16)


def kernel(x, c, ctx, c_ctx, w_mod, b_mod, g_pre, g_post, w_in, conv_a_w, conv_a_b, w_rgate,
           b_rgate, w_igate, b_igate, lru_lambda, dw_w, dw_b, ln_g, ln_b, w_out):
    bsz, seq, d_model = x.shape
    depth = w_mod.shape[0]
    width = conv_a_b.shape[-1]
    assert bsz + 1 <= ROWS and seq % TILE_PROJ == 0 and seq % TILE_MIX == 0
    assert TILE_MIX % GRID_W == 0 and TILE_PROJ >= CONV_PAD * GRID_W
    assert ctx.shape[1] % CHUNK == 0 and width % MXU_N == 0 and d_model % MXU_N == 0

    act = jnp.concatenate(
        [c, c_ctx[None, :], jnp.zeros((ROWS - bsz - 1, d_model), F32)], axis=0)
    mods = _mods_call(act, w_mod, b_mod)

    w_in_bf = w_in.astype(BF16)
    w_out_bf = w_out.astype(BF16)
    wg = _pack_gate_weights(w_rgate, w_igate)

    h0 = _ctx_call(ctx, mods[:, bsz:bsz + 1, :], g_pre, g_post, w_in_bf, w_out_bf, wg, conv_a_w,
                   conv_a_b, b_rgate, b_igate, lru_lambda, dw_w, dw_b, ln_g, ln_b)

    for l in range(depth):
        mod_l = mods[l].reshape(ROWS, 1, 3 * d_model)
        w_b = w_in_bf[l, :, 2 * width:4 * width]
        w_a = jnp.concatenate([w_in_bf[l, :, :2 * width], w_in_bf[l, :, 4 * width:]], axis=1)
        hx, v1, v2p = _proj_call(x, mod_l, g_pre[l], w_b)
        p, q, mb, ab, hbo = _mix_call(
            hx, v1, v2p, w_a, wg[l], conv_a_w[l], conv_a_b[l], b_rgate[l], b_igate[l],
            lru_lambda[l], dw_w[l], dw_b[l:l + 1], ln_g[l:l + 1], ln_b[l:l + 1],
            h0[:, 2 * l:2 * l + 1, :])
        x = _out_call(p, q, mb, x, w_out_bf[l], g_post[l], mod_l, ab, hbo,
                      h0[:, 2 * l + 1:2 * l + 2, :])
    return x
```

```python
import functools

import jax
import jax.numpy as jnp
from jax import lax
from jax.experimental import pallas as pl
from jax.experimental.pallas import tpu as pltpu

F32 = jnp.float32
BF16 = jnp.bfloat16

EPS = 1e-6
LRU_C = 8.0
LRU_CONV = 4
CONV_K = 31
CONV_PAD = CONV_K // 2
GRID_W = 64
HALO = 16
ROWS = 8
CHUNK = 64
CONV_ROWS = 32
SEG_PAD = 16
MXU_N = 256
SPARE_ROWS = 16

TILE_PROJ = 1024
TILE_MIX = 512
VMEM_LIMIT = 56 * 1024 * 1024


def _silu(x):
    return x * jax.nn.sigmoid(x)


def _softplus(x):
    return jnp.maximum(x, 0.0) + jnp.log1p(jnp.exp(-jnp.abs(x)))


def _rms_norm(x, g):
    ms = jnp.mean(x * x, axis=-1, keepdims=True)
    return x * lax.rsqrt(ms + EPS) * g


def _layer_norm(x, g, b):
    mu = jnp.mean(x, axis=-1, keepdims=True)
    xc = x - mu
    var = jnp.mean(xc * xc, axis=-1, keepdims=True)
    return xc * lax.rsqrt(var + EPS) * g + b


def _dot(a, b):
    return jnp.dot(a, b, preferred_element_type=F32)


def _interleave(mxu_pieces, vpu_pieces):
    n = len(mxu_pieces)
    per = -(-len(vpu_pieces) // n) if n else 0
    for j, piece in enumerate(mxu_pieces):
        piece()
        for fill in vpu_pieces[j * per:(j + 1) * per]:
            fill()
    for fill in vpu_pieces[n * per:]:
        fill()


def _mark(spare_ref, token):
    if token is None:
        return
    reps = spare_ref.shape[0] // ROWS
    spare_ref[...] = jnp.concatenate([token] * reps, axis=0).astype(spare_ref.dtype)


def _chain(mxu_pieces, vpu_pieces, after):
    tokens = []
    for piece, dep in zip(mxu_pieces, after):
        while dep is not None and len(tokens) <= dep:
            tokens.append(vpu_pieces[len(tokens)]())
        piece(None if dep is None else tokens[dep])
    for fill in vpu_pieces[len(tokens):]:
        fill()


def _perm_matrices(n):
    i = jnp.arange(n)
    src = (n // ROWS) * (i % ROWS) + i // ROWS
    pm = (src[:, None] == i[None, :]).astype(BF16)
    return pm, pm.T


UA_ROW0 = (LRU_CONV - 1) * ROWS


def _short_convs(ua_s, a_prev, a_next, cw, cb, vc_s, n, width):
    sub = n // ROWS
    ext = LRU_CONV - 1
    row = lax.broadcasted_iota(jnp.int32, (ROWS, width), 0)
    cw = 0.5 * cw
    cb = 0.5 * cb

    def blk(p):
        return slice(p * ROWS, (p + 1) * ROWS)

    for k in range(1, ext + 1):
        tail = pltpu.roll(ua_s[blk(ext + sub - k), :width], 1, 0)
        ua_s[blk(ext - k), :width] = jnp.where(row == 0, a_prev[ext - k:ext - k + 1, :], tail)
    for k in range(ext):
        head = pltpu.roll(ua_s[blk(ext + k), :width], ROWS - 1, 0)
        ua_s[blk(ext + sub + k), :width] = jnp.where(row == ROWS - 1, a_next[k:k + 1, :], head)

    def chunk(ci, carry):
        base = ci * CHUNK
        for d in range(2):
            first = 0 if d == 0 else ext
            acc = jnp.broadcast_to(cb[d:d + 1, :], (CHUNK, width))
            for k in range(LRU_CONV):
                start = pl.multiple_of(base + (first + k) * ROWS, ROWS)
                acc = acc + cw[d, k:k + 1, :] * ua_s[pl.ds(start, CHUNK), :width]
            vc_s[d, pl.ds(pl.multiple_of(base, CHUNK), CHUNK), :] = acc
        return carry

    lax.fori_loop(0, n // CHUNK, chunk, 0)


def _gate_pieces(vc_s, wg_ref, g_s, width):
    half = width // 2

    def piece(d, hf):
        def run():
            res = _dot(vc_s[d, :, hf * half:(hf + 1) * half].astype(BF16), wg_ref[d, hf])
            g_s[d, :, hf * half:(hf + 1) * half] = res[:, :half]
            g_s[d, :, width + hf * half:width + (hf + 1) * half] = res[:, half:]
        return run

    return [piece(d, hf) for d in range(2) for hf in range(2)]


def _sublane_scan(a, h, row, reverse):
    for s in (1, 2, 4):
        if reverse:
            m, shift = row < ROWS - s, ROWS - s
        else:
            m, shift = row >= s, s
        a_sh = jnp.where(m, pltpu.roll(a, shift, 0), 1.0)
        h_sh = jnp.where(m, pltpu.roll(h, shift, 0), 0.0)
        h = h + a * h_sh
        a = a * a_sh
    return a, h


def _rglru_scan(ua_s, br, bi, lam, c0, vc_s, g_s, hl_s, al_s, po_s, q_s, n, width):
    sub = n // ROWS
    row = lax.broadcasted_iota(jnp.int32, (ROWS, width), 0)
    def rows_of(v, d):
        return jnp.broadcast_to(v[d:d + 1, :], (ROWS, width))

    spl_half = (-0.5 * LRU_C) * _softplus(-lam)
    spl_half = [rows_of(spl_half, d) for d in range(2)]
    br_half = [rows_of(0.5 * br, d) for d in range(2)]
    bi_half = [rows_of(0.5 * bi, d) for d in range(2)]

    def coeffs(d, r0):
        g = g_s[d, pl.ds(r0, ROWS), :]
        vh = vc_s[d, pl.ds(r0, ROWS), :]
        t_r = jnp.tanh(g[:, :width] + br_half[d])
        t_i = jnp.tanh(g[:, width:] + bi_half[d])
        la = spl_half[d] + spl_half[d] * t_r
        a = jnp.exp(la)
        x = jnp.tanh(la) * (-1.0 - a * a)
        mult = jnp.where(x > 0.0, x * lax.rsqrt(x), 0.0)
        return a, mult * (vh + vh * t_i)

    def local(j, carry):
        hf, af, hb, ab = carry
        r0 = pl.multiple_of(j * ROWS, ROWS)
        a, b = coeffs(0, r0)
        hf = a * hf + b
        af = a * af
        hl_s[0, pl.ds(r0, ROWS), :] = hf
        al_s[0, pl.ds(r0, ROWS), :] = af
        r1 = pl.multiple_of((sub - 1 - j) * ROWS, ROWS)
        a, b = coeffs(1, r1)
        hb = a * hb + b
        ab = a * ab
        hl_s[1, pl.ds(r1, ROWS), :] = hb
        al_s[1, pl.ds(r1, ROWS), :] = ab
        return hf, af, hb, ab

    zero = jnp.zeros((ROWS, width), F32)
    one = jnp.ones((ROWS, width), F32)
    hf, af, hb, ab = lax.fori_loop(0, sub, local, (zero, one, zero, one), unroll=4)

    af, hf = _sublane_scan(af, hf, row, reverse=False)
    end_f = hf + af * c0
    c_f = jnp.where(row == 0, c0, pltpu.roll(end_f, 1, 0))
    ab, hb = _sublane_scan(ab, hb, row, reverse=True)
    c_b = jnp.where(row == ROWS - 1, 0.0, pltpu.roll(hb, ROWS - 1, 0))
    c_a = jnp.where(row == ROWS - 1, 1.0, pltpu.roll(ab, ROWS - 1, 0))

    if po_s is not None:
        def fix(j, carry):
            r0 = pl.ds(pl.multiple_of(j * ROWS, ROWS), ROWS)
            a_b = al_s[1, r0, :]
            h = (hl_s[0, r0, :] + al_s[0, r0, :] * c_f) + (hl_s[1, r0, :] + a_b * c_b)
            sg = _silu(ua_s[pl.ds(pl.multiple_of(UA_ROW0 + j * ROWS, ROWS), ROWS), width:])
            po_s[r0, :] = h * sg
            if q_s is not None:
                q_s[r0, :] = (a_b * c_a) * sg
            return carry

        lax.fori_loop(0, sub, fix, 0, unroll=8)

    return end_f[ROWS - 1:ROWS, :], hb[0:1, :], ab[0:1, :]


def _shifted_copy_pieces(vpad_ref, sh_ref):
    n = sh_ref.shape[1]

    def piece(j):
        def run():
            sh_ref[j] = vpad_ref[pl.ds(j, n), :]
        return run

    return [piece(j) for j in range(ROWS)]


def _time_conv_pieces(sh_ref, w_ref, ncol, bases, out_ref):
    def piece(ci, base):
        def run():
            acc = jnp.zeros((CHUNK, ncol), F32)
            for k in range(CONV_K):
                off = SEG_PAD - CONV_PAD + k
                start = base + (off // ROWS) * ROWS
                acc = acc + w_ref[k:k + 1, 0:ncol] * sh_ref[off % ROWS, start:start + CHUNK, :]
            out_ref[ci * CHUNK:(ci + 1) * CHUNK, 0:ncol] = acc
        return run

    return [piece(ci, base) for ci, base in enumerate(bases)]


def _mods_body(act_ref, w_ref, b_ref, o_ref):
    a = _silu(act_ref[...])
    o_ref[0] = jnp.dot(a, w_ref[0], precision=lax.Precision.HIGHEST,
                       preferred_element_type=F32) + b_ref[0]


def _mods_call(act, w_mod, b_mod):
    depth, d_model, d3 = w_mod.shape
    ncol = d_model
    return pl.pallas_call(
        _mods_body,
        grid=(depth, d3 // ncol),
        in_specs=[
            pl.BlockSpec((ROWS, d_model), lambda l, n: (0, 0)),
            pl.BlockSpec((1, d_model, ncol), lambda l, n: (l, 0, n)),
            pl.BlockSpec((1, 1, ncol), lambda l, n: (l, 0, n)),
        ],
        out_specs=pl.BlockSpec((1, ROWS, ncol), lambda l, n: (l, 0, n)),
        out_shape=jax.ShapeDtypeStruct((depth, ROWS, d3), F32),
        compiler_params=pltpu.CompilerParams(
            dimension_semantics=("arbitrary", "arbitrary"), vmem_limit_bytes=VMEM_LIMIT),
        name="mods",
    )(act, w_mod, b_mod.reshape(depth, 1, d3))


def _modulate(x, g, m, d_model):
    return _rms_norm(x, g) * (1.0 + m[:, d_model:2 * d_model]) + m[:, :d_model]


def _ctx_body(ctx_ref, mod_ref, gpre_ref, gpost_ref, win_ref, wo_ref, wg_ref, cw_ref, cb_ref,
              br_ref, bi_ref, lam_ref, dww_ref, dwb_ref, lng_ref, lnb_ref, pm_ref, pmt_ref,
              h0_ref,
              ua_s, vc_s, g_s, hl_s, al_s, po_s, vpad_s, sh_s, y_s,
              *, depth, d_model, width, n):
    xc = ctx_ref[0]
    zeros_pad = jnp.zeros((SEG_PAD, width), F32)
    no_rows = jnp.zeros((LRU_CONV - 1, width), F32)
    zero_state = jnp.zeros((1, width), F32)
    for l in range(depth):
        update = l < depth - 1
        m = mod_ref[l]
        hc = _modulate(xc, gpre_ref[l:l + 1, :], m, d_model).astype(BF16)
        hcp = _dot(pm_ref[...], hc).astype(BF16)
        if update:
            ua_s[UA_ROW0:UA_ROW0 + n, :] = _dot(hcp, win_ref[l, :, :2 * width])
            ub = _dot(hc, win_ref[l, :, 2 * width:])
        else:
            ua_s[UA_ROW0:UA_ROW0 + n, :width] = _dot(hcp, win_ref[l, :, :width])
        _short_convs(ua_s, no_rows, no_rows, cw_ref[l], cb_ref[l], vc_s, n, width)
        _interleave(_gate_pieces(vc_s, wg_ref.at[l], g_s, width), [])
        hf_last, hb_first, _ = _rglru_scan(
            ua_s, br_ref[l], bi_ref[l], lam_ref[l], zero_state, vc_s, g_s, hl_s, al_s,
            po_s if update else None, None, n, width)
        h0_ref[0, 2 * l:2 * l + 1, :] = hf_last
        h0_ref[0, 2 * l + 1:2 * l + 2, :] = hb_first
        if update:
            v = ub[:, :width] * jax.nn.sigmoid(ub[:, width:2 * width])
            vpad_s[0:SEG_PAD, :] = zeros_pad
            vpad_s[SEG_PAD:SEG_PAD + n, :] = v
            vpad_s[SEG_PAD + n:, :] = zeros_pad
            _interleave([], _shifted_copy_pieces(vpad_s, sh_s))
            _interleave([], _time_conv_pieces(
                sh_s, dww_ref.at[l], width, [ci * CHUNK for ci in range(n // CHUNK)], y_s))
            y = y_s[...] + dwb_ref[l:l + 1, :]
            conv = _silu(_layer_norm(y, lng_ref[l:l + 1, :], lnb_ref[l:l + 1, :]))
            mix_a = _dot(pmt_ref[...], po_s[...].astype(BF16)).astype(BF16)
            mix_b = (conv * _silu(ub[:, 2 * width:])).astype(BF16)
            mix = _dot(mix_a, wo_ref[l, :width, :]) + _dot(mix_b, wo_ref[l, width:, :])
            xc = xc + m[:, 2 * d_model:] * _rms_norm(mix, gpost_ref[l:l + 1, :])


def _ctx_call(ctx, mod_c, g_pre, g_post, w_in_bf, w_out_bf, wg, conv_a_w, conv_a_b, b_rgate,
              b_igate, lru_lambda, dw_w, dw_b, ln_g, ln_b):
    bsz, n, d_model = ctx.shape
    depth = w_in_bf.shape[0]
    width = conv_a_b.shape[-1]
    npad = n + 2 * SEG_PAD
    pm, pmt = _perm_matrices(n)

    def full(a):
        nd = a.ndim
        return pl.BlockSpec(a.shape, lambda b, _nd=nd: (0,) * _nd)

    params = (mod_c, g_pre, g_post, w_in_bf, w_out_bf, wg, conv_a_w, conv_a_b, b_rgate, b_igate,
              lru_lambda, dw_w, dw_b, ln_g, ln_b, pm, pmt)
    return pl.pallas_call(
        functools.partial(_ctx_body, depth=depth, d_model=d_model, width=width, n=n),
        grid=(bsz,),
        in_specs=[pl.BlockSpec((1, n, d_model), lambda b: (b, 0, 0))] + [full(a) for a in params],
        out_specs=pl.BlockSpec((1, 2 * depth, width), lambda b: (b, 0, 0)),
        out_shape=jax.ShapeDtypeStruct((bsz, 2 * depth, width), F32),
        scratch_shapes=[
            pltpu.VMEM((n + 2 * UA_ROW0, 2 * width), F32),
            pltpu.VMEM((2, n, width), F32),
            pltpu.VMEM((2, n, 2 * width), F32),
            pltpu.VMEM((2, n, width), F32),
            pltpu.VMEM((2, n, width), F32),
            pltpu.VMEM((n, width), F32),
            pltpu.VMEM((npad, width), F32),
            pltpu.VMEM((ROWS, npad - ROWS, width), F32),
            pltpu.VMEM((n, width), F32),
        ],
        compiler_params=pltpu.CompilerParams(
            dimension_semantics=("arbitrary",), vmem_limit_bytes=VMEM_LIMIT),
        name="context",
    )(ctx, *params)


def _proj_body(x_ref, mod_ref, g_ref, w_ref, hx_ref, v1_ref, v2_ref, *, nt, d_model, width):
    j = pl.program_id(1)
    is_pad = jnp.logical_or(j == 0, j == nt + 1)

    @pl.when(is_pad)
    def _():
        v2_ref[...] = jnp.zeros(v2_ref.shape, v2_ref.dtype)

    @pl.when(jnp.logical_not(is_pad))
    def _():
        half = width // 2
        rows = x_ref.shape[1] // 4
        for c in range(4):
            rs = slice(c * rows, (c + 1) * rows)
            hb = _modulate(x_ref[0, rs, :], g_ref[...], mod_ref[0], d_model).astype(BF16)
            hx_ref[0, rs, :] = hb
            u = _dot(hb, w_ref[...])
            v = u[:, :width] * jax.nn.sigmoid(u[:, width:])
            v1_ref[0, rs, :] = v[:, :half].astype(BF16)
            v2_ref[0, rs, :] = v[:, half:].astype(BF16)


def _proj_call(x, mod_l, g_pre_l, w_b):
    bsz, seq, d_model = x.shape
    width = w_b.shape[1] // 2
    half = width // 2
    tt = TILE_PROJ
    nt = seq // tt

    def tok(b, j):
        return (b, jnp.clip(j - 1, 0, nt - 1), 0)

    return pl.pallas_call(
        functools.partial(_proj_body, nt=nt, d_model=d_model, width=width),
        grid=(bsz, nt + 2),
        in_specs=[
            pl.BlockSpec((1, tt, d_model), tok),
            pl.BlockSpec((1, 1, 3 * d_model), lambda b, j: (b, 0, 0)),
            pl.BlockSpec((1, d_model), lambda b, j: (0, 0)),
            pl.BlockSpec(w_b.shape, lambda b, j: (0, 0)),
        ],
        out_specs=[
            pl.BlockSpec((1, tt, d_model), tok),
            pl.BlockSpec((1, tt, half), tok),
            pl.BlockSpec((1, tt, half), lambda b, j: (b, j, 0)),
        ],
        out_shape=[
            jax.ShapeDtypeStruct((bsz, seq, d_model), BF16),
            jax.ShapeDtypeStruct((bsz, seq, half), BF16),
            jax.ShapeDtypeStruct((bsz, seq + 2 * tt, half), BF16),
        ],
        compiler_params=pltpu.CompilerParams(
            dimension_semantics=("arbitrary", "arbitrary"), vmem_limit_bytes=VMEM_LIMIT),
        name="proj",
    )(x, mod_l, g_pre_l.reshape(1, d_model), w_b)


def _mix_body(hx_ref, hxp_ref, hxn_ref, v1_ref, v2_ref, wa_ref, wg_ref, cw_ref, cb_ref, br_ref,
              bi_ref, lam_ref, dww_ref, dwb_ref, lng_ref, lnb_ref, pm_ref, h0f_ref, zero_ref,
              p_ref, q_ref, mb_ref, ab_ref, hbo_ref,
              hxs_s, ua_s, bg_s, vc_s, g_s, hl_s, al_s, po_s, q_s, carry_s, vpad_s, sh_s,
              v2f_s, y_s,
              *, tt, nt, d_model, width, v2_front):
    i = pl.program_id(1)
    half = width // 2
    nseg = tt // GRID_W
    seg_rows = GRID_W + 2 * SEG_PAD
    ext = LRU_CONV - 1
    reach = CONV_PAD * GRID_W
    row_zero = zero_ref[0]

    @pl.when(i == 0)
    def _():
        carry_s[...] = h0f_ref[0]

    def col_slices(total):
        return [slice(c, c + MXU_N) for c in range(0, total, MXU_N)]


    def pad_segment(r):
        def run():
            base = r * seg_rows
            zpad = jnp.zeros((SEG_PAD, half), F32)
            vpad_s[base:base + SEG_PAD, :] = zpad
            vpad_s[base + SEG_PAD:base + SEG_PAD + GRID_W, :] = (
                v1_ref[0, r * GRID_W:(r + 1) * GRID_W, :].astype(F32))
            vpad_s[base + SEG_PAD + GRID_W:base + seg_rows, :] = zpad
        return run

    def load_window():
        win0 = pl.multiple_of(i * tt + (v2_front - reach), GRID_W)
        v2f_s[...] = v2_ref[0, pl.ds(win0, tt + 2 * reach), :].astype(F32)

    def conv_chunk(src_fn, w_cols, ci, out_cols):
        def run():
            token = None
            for part in range(CHUNK // CONV_ROWS):
                acc = jnp.zeros((CONV_ROWS, half), F32)
                for k in range(CONV_K):
                    acc = acc + dww_ref[k:k + 1, w_cols] * src_fn(ci, k, part * CONV_ROWS)
                row0 = ci * CHUNK + part * CONV_ROWS
                y_s[row0:row0 + CONV_ROWS, out_cols] = acc
                token = acc[0:ROWS, 0:128]
            return token
        return run

    def shifted(ci, k, r):
        off = SEG_PAD - CONV_PAD + k
        start = ci * seg_rows + (off // ROWS) * ROWS + r
        return sh_s[off % ROWS, start:start + CONV_ROWS, :]

    def window(ci, k, r):
        start = ci * CHUNK + k * GRID_W + r
        return v2f_s[start:start + CONV_ROWS, :]

    def permute(cs):
        def run(token):
            hxs_s[0:tt, cs] = _dot(pm_ref[...], hx_ref[0, :, cs]).astype(BF16)
        return run

    def project_a(cs):
        def run(token):
            _mark(hxs_s.at[tt:tt + SPARE_ROWS, 0:128], token)
            lhs = hxs_s[pl.ds(pl.multiple_of(row_zero, SPARE_ROWS), tt), :]
            ua_s[UA_ROW0:UA_ROW0 + tt, cs] = _dot(lhs, wa_ref[:, cs])
        return run

    def project_b(cs):
        def run(token):
            cols = slice(2 * width + cs.start, 2 * width + cs.stop)
            bg_s[:, cs] = _dot(hx_ref[0], wa_ref[:, cols])
        return run

    prep = ([pad_segment(r) for r in range(nseg)] + _shifted_copy_pieces(vpad_s, sh_s)
            + [load_window])
    rows_first = len(prep)
    cols_first = rows_first + nseg
    vpu = (prep
           + [conv_chunk(shifted, slice(0, half), ci, slice(0, half)) for ci in range(nseg)]
           + [conv_chunk(window, slice(half, width), ci, slice(half, width))
              for ci in range(tt // CHUNK)])
    mxu = ([permute(cs) for cs in col_slices(d_model)]
           + [project_a(cs) for cs in col_slices(2 * width)]
           + [project_b(cs) for cs in col_slices(width)])
    after = [None, None, rows_first, rows_first + 1,
             rows_first + 2, rows_first + 4, rows_first + 5, rows_first + 7,
             cols_first + 1, cols_first + 2]
    _chain(mxu, vpu, after)

    wa_a = wa_ref[:, :width]
    a_prev = jnp.where(i > 0, _dot(hxp_ref[0], wa_a)[HALO - ext:, :], 0.0)
    a_next = jnp.where(i < nt - 1, _dot(hxn_ref[0], wa_a)[:ext, :], 0.0)
    _short_convs(ua_s, a_prev, a_next, cw_ref[...], cb_ref[...], vc_s, tt, width)

    def finish_conv(c, nrows):
        def run():
            rs = slice(c * nrows, (c + 1) * nrows)
            y = y_s[rs, :] + dwb_ref[...]
            conv = _silu(_layer_norm(y, lng_ref[...], lnb_ref[...]))
            out = conv * _silu(bg_s[rs, :])
            mb_ref[0, rs, :] = out.astype(BF16)
            return out[0:ROWS, 0:128]
        return run

    def gate_piece(d, hf):
        def run(token):
            lo = hf * half
            _mark(vc_s.at[d, tt:tt + SPARE_ROWS, lo:lo + 128], token)
            lhs = vc_s[d, pl.ds(pl.multiple_of(row_zero, SPARE_ROWS), tt), lo:lo + half]
            res = _dot(lhs.astype(BF16), wg_ref[d, hf])
            g_s[d, :, lo:lo + half] = res[:, :half]
            g_s[d, :, width + lo:width + lo + half] = res[:, half:]
        return run

    _chain([gate_piece(d, hf) for d in range(2) for hf in range(2)],
           [finish_conv(c, tt // 4) for c in range(4)], [None, 0, 1, 2])

    hf_last, hb_first, a_total = _rglru_scan(
        ua_s, br_ref[...], bi_ref[...], lam_ref[...], carry_s[...], vc_s, g_s, hl_s, al_s,
        po_s, q_s, tt, width)
    carry_s[...] = hf_last
    ab_ref[0, 0] = a_total
    hbo_ref[0, 0] = hb_first
    p_ref[0] = po_s[...].astype(BF16)
    q_ref[0] = q_s[...].astype(BF16)


def _mix_call(hx, v1, v2p, w_a, wg_l, cw, cb, br, bi, lam, dww, dwb, lng, lnb, h0f):
    bsz, seq, d_model = hx.shape
    width = cb.shape[-1]
    half = width // 2
    tt = TILE_MIX
    nt = seq // tt
    hb_per_tile = tt // HALO
    n_halo_blocks = seq // HALO
    v2_front = (v2p.shape[1] - seq) // 2
    nseg = tt // GRID_W
    seg_rows = GRID_W + 2 * SEG_PAD
    reach = CONV_PAD * GRID_W
    pm, _ = _perm_matrices(tt)

    def full(a):
        nd = a.ndim
        return pl.BlockSpec(a.shape, lambda b, i, _nd=nd: (0,) * _nd)

    small = (w_a, wg_l, cw, cb, br, bi, lam, dww, dwb, lng, lnb, pm)
    tile = lambda b, i: (b, i, 0)
    out_shapes = [jax.ShapeDtypeStruct((bsz, seq, width), BF16)] * 3 + [
        jax.ShapeDtypeStruct((bsz, nt, 1, width), F32)] * 2
    return pl.pallas_call(
        functools.partial(_mix_body, tt=tt, nt=nt, d_model=d_model, width=width,
                          v2_front=v2_front),
        grid=(bsz, nt),
        in_specs=[
            pl.BlockSpec((1, tt, d_model), tile),
            pl.BlockSpec((1, HALO, d_model),
                         lambda b, i: (b, jnp.maximum(i * hb_per_tile - 1, 0), 0)),
            pl.BlockSpec((1, HALO, d_model),
                         lambda b, i: (b, jnp.minimum((i + 1) * hb_per_tile, n_halo_blocks - 1), 0)),
            pl.BlockSpec((1, tt, half), tile),
            pl.BlockSpec((1, v2p.shape[1], half), lambda b, i: (b, 0, 0)),
        ] + [full(a) for a in small] + [
            pl.BlockSpec((1, 1, width), lambda b, i: (b, 0, 0)),
            pl.BlockSpec(memory_space=pltpu.SMEM),
        ],
        out_specs=[pl.BlockSpec((1, tt, width), tile)] * 3 + [
            pl.BlockSpec((1, 1, 1, width), lambda b, i: (b, i, 0, 0))] * 2,
        out_shape=out_shapes,
        scratch_shapes=[
            pltpu.VMEM((tt + SPARE_ROWS, d_model), BF16),
            pltpu.VMEM((tt + 2 * UA_ROW0, 2 * width), F32),
            pltpu.VMEM((tt, width), F32),
            pltpu.VMEM((2, tt + SPARE_ROWS, width), F32),
            pltpu.VMEM((2, tt, 2 * width), F32),
            pltpu.VMEM((2, tt, width), F32),
            pltpu.VMEM((2, tt, width), F32),
            pltpu.VMEM((tt, width), F32),
            pltpu.VMEM((tt, width), F32),
            pltpu.VMEM((1, width), F32),
            pltpu.VMEM((nseg * seg_rows, half), F32),
            pltpu.VMEM((ROWS, nseg * seg_rows - ROWS, half), F32),
            pltpu.VMEM((tt + 2 * reach, half), F32),
            pltpu.VMEM((tt, width), F32),
        ],
        compiler_params=pltpu.CompilerParams(
            dimension_semantics=("arbitrary", "arbitrary"), vmem_limit_bytes=VMEM_LIMIT,
            ),
        name="mixer",
    )(hx, hx, hx, v1, v2p, *small, h0f, jnp.zeros((1,), jnp.int32))


def _out_body(p_ref, q_ref, mb_ref, x_ref, wo_ref, gpost_ref, mod_ref, ab_ref, hb_ref, h0b_ref,
              pmt_ref, o_ref, carry_s, *, d_model, width):
    @pl.when(pl.program_id(1) == 0)
    def _():
        carry_s[...] = h0b_ref[0]

    c = carry_s[...]
    mix_a = (p_ref[0].astype(F32) + q_ref[0].astype(F32) * c).astype(BF16)
    mix_a = _dot(pmt_ref[...], mix_a).astype(BF16)
    mix = _dot(mix_a, wo_ref[:width, :]) + _dot(mb_ref[0], wo_ref[width:, :])
    gate = mod_ref[0][:, 2 * d_model:]
    o_ref[0] = x_ref[0] + gate * _rms_norm(mix, gpost_ref[...])
    carry_s[...] = hb_ref[0, 0] + ab_ref[0, 0] * c


def _out_call(p, q, mb, x, w_o, g_post_l, mod_l, ab, hbo, h0b):
    bsz, seq, d_model = x.shape
    width = p.shape[-1]
    tt = TILE_MIX
    nt = seq // tt
    rev = lambda b, j: (b, nt - 1 - j, 0)
    _, pmt = _perm_matrices(tt)
    return pl.pallas_call(
        functools.partial(_out_body, d_model=d_model, width=width),
        grid=(bsz, nt),
        in_specs=[
            pl.BlockSpec((1, tt, width), rev),
            pl.BlockSpec((1, tt, width), rev),
            pl.BlockSpec((1, tt, width), rev),
            pl.BlockSpec((1, tt, d_model), rev),
            pl.BlockSpec(w_o.shape, lambda b, j: (0, 0)),
            pl.BlockSpec((1, d_model), lambda b, j: (0, 0)),
            pl.BlockSpec((1, 1, 3 * d_model), lambda b, j: (b, 0, 0)),
            pl.BlockSpec((1, 1, 1, width), lambda b, j: (b, nt - 1 - j, 0, 0)),
            pl.BlockSpec((1, 1, 1, width), lambda b, j: (b, nt - 1 - j, 0, 0)),
            pl.BlockSpec((1, 1, width), lambda b, j: (b, 0, 0)),
            pl.BlockSpec((tt, tt), lambda b, j: (0, 0)),
        ],
        out_specs=pl.BlockSpec((1, tt, d_model), rev),
        out_shape=jax.ShapeDtypeStruct((bsz, seq, d_model), F32),
        scratch_shapes=[pltpu.VMEM((1, width), F32)],
        compiler_params=pltpu.CompilerParams(
            dimension_semantics=("arbitrary", "arbitrary"), vmem_limit_bytes=VMEM_LIMIT),
        name="out",
    )(p, q, mb, x, w_o, g_post_l.reshape(1, d_model), mod_l, ab, hbo, h0b, pmt)


def _pack_gate_weights(w_r, w_i):
    depth, ndir, heads, hd, _ = w_r.shape
    hh = heads // 2
    eye = jnp.eye(hh, dtype=w_r.dtype)

    def bd(w):
        w = w.reshape(depth, ndir, 2, hh, hd, hd)
        return jnp.einsum("ldfhij,hg->ldfhigj", w, eye).reshape(depth, ndir, 2, hh * hd, hh * hd)

    return jnp.concatenate([bd(w_r), bd(w_i)], axis=-1).astype(BF16)


def kernel(x, c, ctx, c_ctx, w_mod, b_mod, g_pre, g_post, w_in, conv_a_w, conv_a_b, w_rgate,
           b_rgate, w_igate, b_igate, lru_lambda, dw_w, dw_b, ln_g, ln_b, w_out):
    bsz, seq, d_model = x.shape
    depth = w_mod.shape[0]
    width = conv_a_b.shape[-1]
    assert bsz + 1 <= ROWS and seq % TILE_PROJ == 0 and seq % TILE_MIX == 0
    assert TILE_MIX % GRID_W == 0 and TILE_PROJ >= CONV_PAD * GRID_W
    assert ctx.shape[1] % CHUNK == 0 and width % MXU_N == 0 and d_model % MXU_N == 0

    act = jnp.concatenate(
        [c, c_ctx[None, :], jnp.zeros((ROWS - bsz - 1, d_model), F32)], axis=0)
    mods = _mods_call(act, w_mod, b_mod)

    w_in_bf = w_in.astype(BF16)
    w_out_bf = w_out.astype(BF16)
    wg = _pack_gate_weights(w_rgate, w_igate)

    h0 = _ctx_call(ctx, mods[:, bsz:bsz + 1, :], g_pre, g_post, w_in_bf, w_out_bf, wg, conv_a_w,
                   conv_a_b, b_rgate, b_igate, lru_lambda, dw_w, dw_b, ln_g, ln_b)

    for l in range(depth):
        mod_l = mods[l].reshape(ROWS, 1, 3 * d_model)
        w_b = w_in_bf[l, :, 2 * width:4 * width]
        w_a = jnp.concatenate([w_in_bf[l, :, :2 * width], w_in_bf[l, :, 4 * width:]], axis=1)
        hx, v1, v2p = _proj_call(x, mod_l, g_pre[l], w_b)
        p, q, mb, ab, hbo = _mix_call(
            hx, v1, v2p, w_a, wg[l], conv_a_w[l], conv_a_b[l], b_rgate[l], b_igate[l],
            lru_lambda[l], dw_w[l], dw_b[l:l + 1], ln_g[l:l + 1], ln_b[l:l + 1],
            h0[:, 2 * l:2 * l + 1, :])
        x = _out_call(p, q, mb, x, w_out_bf[l], g_post[l], mod_l, ab, hbo,
                      h0[:, 2 * l + 1:2 * l + 2, :])
    return x
```

```python
import functools

import jax
import jax.numpy as jnp
from jax import lax
from jax.experimental import pallas as pl
from jax.experimental.pallas import tpu as pltpu

F32 = jnp.float32
BF16 = jnp.bfloat16

EPS = 1e-6
LRU_C = 8.0
LRU_CONV = 4
CONV_K = 31
CONV_PAD = CONV_K // 2
GRID_W = 64
HALO = 16
ROWS = 8
CHUNK = 64
CONV_ROWS = 32
SEG_PAD = 16
MXU_N = 256
SPARE_ROWS = 16
TAP_GROUP = 8

TILE_PROJ = 1024
TILE_MIX = 512
VMEM_LIMIT = 56 * 1024 * 1024


def _silu(x):
    return x * jax.nn.sigmoid(x)


def _softplus(x):
    return jnp.maximum(x, 0.0) + jnp.log1p(jnp.exp(-jnp.abs(x)))


def _rms_norm(x, g):
    ms = jnp.mean(x * x, axis=-1, keepdims=True)
    return x * lax.rsqrt(ms + EPS) * g


def _layer_norm(x, g, b):
    mu = jnp.mean(x, axis=-1, keepdims=True)
    xc = x - mu
    var = jnp.mean(xc * xc, axis=-1, keepdims=True)
    return xc * lax.rsqrt(var + EPS) * g + b


def _dot(a, b):
    return jnp.dot(a, b, preferred_element_type=F32)


def _interleave(mxu_pieces, vpu_pieces):
    n = len(mxu_pieces)
    per = -(-len(vpu_pieces) // n) if n else 0
    for j, piece in enumerate(mxu_pieces):
        piece()
        for fill in vpu_pieces[j * per:(j + 1) * per]:
            fill()
    for fill in vpu_pieces[n * per:]:
        fill()


def _mark(spare_ref, token):
    if token is None:
        return
    reps = spare_ref.shape[0] // ROWS
    spare_ref[...] = jnp.concatenate([token] * reps, axis=0).astype(spare_ref.dtype)


def _chain(mxu_pieces, vpu_pieces, after):
    tokens = []
    for piece, dep in zip(mxu_pieces, after):
        while dep is not None and len(tokens) <= dep:
            tokens.append(vpu_pieces[len(tokens)]())
        piece(None if dep is None else tokens[dep])
    for fill in vpu_pieces[len(tokens):]:
        fill()


def _perm_matrices(n):
    i = jnp.arange(n)
    src = (n // ROWS) * (i % ROWS) + i // ROWS
    pm = (src[:, None] == i[None, :]).astype(BF16)
    return pm, pm.T


UA_ROW0 = (LRU_CONV - 1) * ROWS


def _short_convs(ua_s, a_prev, a_next, cw, cb, vc_s, n, width):
    sub = n // ROWS
    ext = LRU_CONV - 1
    row = lax.broadcasted_iota(jnp.int32, (ROWS, width), 0)
    cw = 0.5 * cw
    cb = 0.5 * cb

    def blk(p):
        return slice(p * ROWS, (p + 1) * ROWS)

    for k in range(1, ext + 1):
        tail = pltpu.roll(ua_s[blk(ext + sub - k), :width], 1, 0)
        ua_s[blk(ext - k), :width] = jnp.where(row == 0, a_prev[ext - k:ext - k + 1, :], tail)
    for k in range(ext):
        head = pltpu.roll(ua_s[blk(ext + k), :width], ROWS - 1, 0)
        ua_s[blk(ext + sub + k), :width] = jnp.where(row == ROWS - 1, a_next[k:k + 1, :], head)

    def chunk(ci, carry):
        base = ci * CHUNK
        for d in range(2):
            first = 0 if d == 0 else ext
            acc = jnp.broadcast_to(cb[d:d + 1, :], (CHUNK, width))
            for k in range(LRU_CONV):
                start = pl.multiple_of(base + (first + k) * ROWS, ROWS)
                acc = acc + cw[d, k:k + 1, :] * ua_s[pl.ds(start, CHUNK), :width]
            vc_s[d, pl.ds(pl.multiple_of(base, CHUNK), CHUNK), :] = acc
        return carry

    lax.fori_loop(0, n // CHUNK, chunk, 0)


def _gate_pieces(vc_s, wg_ref, g_s, width):
    half = width // 2

    def piece(d, hf):
        def run():
            res = _dot(vc_s[d, :, hf * half:(hf + 1) * half].astype(BF16), wg_ref[d, hf])
            g_s[d, :, hf * half:(hf + 1) * half] = res[:, :half]
            g_s[d, :, width + hf * half:width + (hf + 1) * half] = res[:, half:]
        return run

    return [piece(d, hf) for d in range(2) for hf in range(2)]


def _sublane_scan(a, h, row, reverse):
    for s in (1, 2, 4):
        if reverse:
            m, shift = row < ROWS - s, ROWS - s
        else:
            m, shift = row >= s, s
        a_sh = jnp.where(m, pltpu.roll(a, shift, 0), 1.0)
        h_sh = jnp.where(m, pltpu.roll(h, shift, 0), 0.0)
        h = h + a * h_sh
        a = a * a_sh
    return a, h


def _rglru_scan(ua_s, br, bi, lam, c0, vc_s, g_s, hl_s, al_s, po_s, q_s, n, width):
    sub = n // ROWS
    row = lax.broadcasted_iota(jnp.int32, (ROWS, width), 0)
    def rows_of(v, d):
        return jnp.broadcast_to(v[d:d + 1, :], (ROWS, width))

    spl_half = (-0.5 * LRU_C) * _softplus(-lam)
    spl_half = [rows_of(spl_half, d) for d in range(2)]
    br_half = [rows_of(0.5 * br, d) for d in range(2)]
    bi_half = [rows_of(0.5 * bi, d) for d in range(2)]

    def coeffs(d, r0):
        g = g_s[d, pl.ds(r0, ROWS), :]
        vh = vc_s[d, pl.ds(r0, ROWS), :]
        t_r = jnp.tanh(g[:, :width] + br_half[d])
        t_i = jnp.tanh(g[:, width:] + bi_half[d])
        la = spl_half[d] + spl_half[d] * t_r
        a = jnp.exp(la)
        x = jnp.tanh(la) * (-1.0 - a * a)
        mult = jnp.where(x > 0.0, x * lax.rsqrt(x), 0.0)
        return a, mult * (vh + vh * t_i)

    def local(j, carry):
        hf, af, hb, ab = carry
        r0 = pl.multiple_of(j * ROWS, ROWS)
        a, b = coeffs(0, r0)
        hf = a * hf + b
        af = a * af
        hl_s[0, pl.ds(r0, ROWS), :] = hf
        al_s[0, pl.ds(r0, ROWS), :] = af
        r1 = pl.multiple_of((sub - 1 - j) * ROWS, ROWS)
        a, b = coeffs(1, r1)
        hb = a * hb + b
        ab = a * ab
        hl_s[1, pl.ds(r1, ROWS), :] = hb
        al_s[1, pl.ds(r1, ROWS), :] = ab
        return hf, af, hb, ab

    zero = jnp.zeros((ROWS, width), F32)
    one = jnp.ones((ROWS, width), F32)
    hf, af, hb, ab = lax.fori_loop(0, sub, local, (zero, one, zero, one), unroll=4)

    af, hf = _sublane_scan(af, hf, row, reverse=False)
    end_f = hf + af * c0
    c_f = jnp.where(row == 0, c0, pltpu.roll(end_f, 1, 0))
    ab, hb = _sublane_scan(ab, hb, row, reverse=True)
    c_b = jnp.where(row == ROWS - 1, 0.0, pltpu.roll(hb, ROWS - 1, 0))
    c_a = jnp.where(row == ROWS - 1, 1.0, pltpu.roll(ab, ROWS - 1, 0))

    if po_s is not None:
        def fix(j, carry):
            r0 = pl.ds(pl.multiple_of(j * ROWS, ROWS), ROWS)
            a_b = al_s[1, r0, :]
            h = (hl_s[0, r0, :] + al_s[0, r0, :] * c_f) + (hl_s[1, r0, :] + a_b * c_b)
            sg = _silu(ua_s[pl.ds(pl.multiple_of(UA_ROW0 + j * ROWS, ROWS), ROWS), width:])
            po_s[r0, :] = h * sg
            if q_s is not None:
                q_s[r0, :] = (a_b * c_a) * sg
            return carry

        lax.fori_loop(0, sub, fix, 0, unroll=8)

    return end_f[ROWS - 1:ROWS, :], hb[0:1, :], ab[0:1, :]


def _shifted_copy_pieces(vpad_ref, sh_ref):
    n = sh_ref.shape[1]

    def piece(j):
        def run():
            sh_ref[j] = vpad_ref[pl.ds(j, n), :]
        return run

    return [piece(j) for j in range(ROWS)]


def _time_conv_pieces(sh_ref, w_ref, ncol, bases, out_ref):
    def piece(ci, base):
        def run():
            acc = jnp.zeros((CHUNK, ncol), F32)
            for k in range(CONV_K):
                off = SEG_PAD - CONV_PAD + k
                start = base + (off // ROWS) * ROWS
                acc = acc + w_ref[k:k + 1, 0:ncol] * sh_ref[off % ROWS, start:start + CHUNK, :]
            out_ref[ci * CHUNK:(ci + 1) * CHUNK, 0:ncol] = acc
        return run

    return [piece(ci, base) for ci, base in enumerate(bases)]


def _mods_body(act_ref, w_ref, b_ref, o_ref):
    a = _silu(act_ref[...])
    o_ref[0] = jnp.dot(a, w_ref[0], precision=lax.Precision.HIGHEST,
                       preferred_element_type=F32) + b_ref[0]


def _mods_call(act, w_mod, b_mod):
    depth, d_model, d3 = w_mod.shape
    ncol = d_model
    return pl.pallas_call(
        _mods_body,
        grid=(depth, d3 // ncol),
        in_specs=[
            pl.BlockSpec((ROWS, d_model), lambda l, n: (0, 0)),
            pl.BlockSpec((1, d_model, ncol), lambda l, n: (l, 0, n)),
            pl.BlockSpec((1, 1, ncol), lambda l, n: (l, 0, n)),
        ],
        out_specs=pl.BlockSpec((1, ROWS, ncol), lambda l, n: (l, 0, n)),
        out_shape=jax.ShapeDtypeStruct((depth, ROWS, d3), F32),
        compiler_params=pltpu.CompilerParams(
            dimension_semantics=("arbitrary", "arbitrary"), vmem_limit_bytes=VMEM_LIMIT),
        name="mods",
    )(act, w_mod, b_mod.reshape(depth, 1, d3))


def _modulate(x, g, m, d_model):
    return _rms_norm(x, g) * (1.0 + m[:, d_model:2 * d_model]) + m[:, :d_model]


def _ctx_body(ctx_ref, mod_ref, gpre_ref, gpost_ref, win_ref, wo_ref, wg_ref, cw_ref, cb_ref,
              br_ref, bi_ref, lam_ref, dww_ref, dwb_ref, lng_ref, lnb_ref, pm_ref, pmt_ref,
              h0_ref,
              ua_s, vc_s, g_s, hl_s, al_s, po_s, vpad_s, sh_s, y_s,
              *, depth, d_model, width, n):
    xc = ctx_ref[0]
    zeros_pad = jnp.zeros((SEG_PAD, width), F32)
    no_rows = jnp.zeros((LRU_CONV - 1, width), F32)
    zero_state = jnp.zeros((1, width), F32)
    for l in range(depth):
        update = l < depth - 1
        m = mod_ref[l]
        hc = _modulate(xc, gpre_ref[l:l + 1, :], m, d_model).astype(BF16)
        hcp = _dot(pm_ref[...], hc).astype(BF16)
        if update:
            ua_s[UA_ROW0:UA_ROW0 + n, :] = _dot(hcp, win_ref[l, :, :2 * width])
            ub = _dot(hc, win_ref[l, :, 2 * width:])
        else:
            ua_s[UA_ROW0:UA_ROW0 + n, :width] = _dot(hcp, win_ref[l, :, :width])
        _short_convs(ua_s, no_rows, no_rows, cw_ref[l], cb_ref[l], vc_s, n, width)
        _interleave(_gate_pieces(vc_s, wg_ref.at[l], g_s, width), [])
        hf_last, hb_first, _ = _rglru_scan(
            ua_s, br_ref[l], bi_ref[l], lam_ref[l], zero_state, vc_s, g_s, hl_s, al_s,
            po_s if update else None, None, n, width)
        h0_ref[0, 2 * l:2 * l + 1, :] = hf_last
        h0_ref[0, 2 * l + 1:2 * l + 2, :] = hb_first
        if update:
            v = ub[:, :width] * jax.nn.sigmoid(ub[:, width:2 * width])
            vpad_s[0:SEG_PAD, :] = zeros_pad
            vpad_s[SEG_PAD:SEG_PAD + n, :] = v
            vpad_s[SEG_PAD + n:, :] = zeros_pad
            _interleave([], _shifted_copy_pieces(vpad_s, sh_s))
            _interleave([], _time_conv_pieces(
                sh_s, dww_ref.at[l], width, [ci * CHUNK for ci in range(n // CHUNK)], y_s))
            y = y_s[...] + dwb_ref[l:l + 1, :]
            conv = _silu(_layer_norm(y, lng_ref[l:l + 1, :], lnb_ref[l:l + 1, :]))
            mix_a = _dot(pmt_ref[...], po_s[...].astype(BF16)).astype(BF16)
            mix_b = (conv * _silu(ub[:, 2 * width:])).astype(BF16)
            mix = _dot(mix_a, wo_ref[l, :width, :]) + _dot(mix_b, wo_ref[l, width:, :])
            xc = xc + m[:, 2 * d_model:] * _rms_norm(mix, gpost_ref[l:l + 1, :])


def _ctx_call(ctx, mod_c, g_pre, g_post, w_in_bf, w_out_bf, wg, conv_a_w, conv_a_b, b_rgate,
              b_igate, lru_lambda, dw_w, dw_b, ln_g, ln_b):
    bsz, n, d_model = ctx.shape
    depth = w_in_bf.shape[0]
    width = conv_a_b.shape[-1]
    npad = n + 2 * SEG_PAD
    pm, pmt = _perm_matrices(n)

    def full(a):
        nd = a.ndim
        return pl.BlockSpec(a.shape, lambda b, _nd=nd: (0,) * _nd)

    params = (mod_c, g_pre, g_post, w_in_bf, w_out_bf, wg, conv_a_w, conv_a_b, b_rgate, b_igate,
              lru_lambda, dw_w, dw_b, ln_g, ln_b, pm, pmt)
    return pl.pallas_call(
        functools.partial(_ctx_body, depth=depth, d_model=d_model, width=width, n=n),
        grid=(bsz,),
        in_specs=[pl.BlockSpec((1, n, d_model), lambda b: (b, 0, 0))] + [full(a) for a in params],
        out_specs=pl.BlockSpec((1, 2 * depth, width), lambda b: (b, 0, 0)),
        out_shape=jax.ShapeDtypeStruct((bsz, 2 * depth, width), F32),
        scratch_shapes=[
            pltpu.VMEM((n + 2 * UA_ROW0, 2 * width), F32),
            pltpu.VMEM((2, n, width), F32),
            pltpu.VMEM((2, n, 2 * width), F32),
            pltpu.VMEM((2, n, width), F32),
            pltpu.VMEM((2, n, width), F32),
            pltpu.VMEM((n, width), F32),
            pltpu.VMEM((npad, width), F32),
            pltpu.VMEM((ROWS, npad - ROWS, width), F32),
            pltpu.VMEM((n, width), F32),
        ],
        compiler_params=pltpu.CompilerParams(
            dimension_semantics=("arbitrary",), vmem_limit_bytes=VMEM_LIMIT),
        name="context",
    )(ctx, *params)


def _proj_body(x_ref, mod_ref, g_ref, w_ref, hx_ref, v1_ref, v2_ref, *, nt, d_model, width):
    j = pl.program_id(1)
    is_pad = jnp.logical_or(j == 0, j == nt + 1)

    @pl.when(is_pad)
    def _():
        v2_ref[...] = jnp.zeros(v2_ref.shape, v2_ref.dtype)

    @pl.when(jnp.logical_not(is_pad))
    def _():
        half = width // 2
        rows = x_ref.shape[1] // 4
        for c in range(4):
            rs = slice(c * rows, (c + 1) * rows)
            hb = _modulate(x_ref[0, rs, :], g_ref[...], mod_ref[0], d_model).astype(BF16)
            hx_ref[0, rs, :] = hb
            u = _dot(hb, w_ref[...])
            v = u[:, :width] * jax.nn.sigmoid(u[:, width:])
            v1_ref[0, rs, :] = v[:, :half].astype(BF16)
            v2_ref[0, rs, :] = v[:, half:].astype(BF16)


def _proj_call(x, mod_l, g_pre_l, w_b):
    bsz, seq, d_model = x.shape
    width = w_b.shape[1] // 2
    half = width // 2
    tt = TILE_PROJ
    nt = seq // tt

    def tok(b, j):
        return (b, jnp.clip(j - 1, 0, nt - 1), 0)

    return pl.pallas_call(
        functools.partial(_proj_body, nt=nt, d_model=d_model, width=width),
        grid=(bsz, nt + 2),
        in_specs=[
            pl.BlockSpec((1, tt, d_model), tok),
            pl.BlockSpec((1, 1, 3 * d_model), lambda b, j: (b, 0, 0)),
            pl.BlockSpec((1, d_model), lambda b, j: (0, 0)),
            pl.BlockSpec(w_b.shape, lambda b, j: (0, 0)),
        ],
        out_specs=[
            pl.BlockSpec((1, tt, d_model), tok),
            pl.BlockSpec((1, tt, half), tok),
            pl.BlockSpec((1, tt, half), lambda b, j: (b, j, 0)),
        ],
        out_shape=[
            jax.ShapeDtypeStruct((bsz, seq, d_model), BF16),
            jax.ShapeDtypeStruct((bsz, seq, half), BF16),
            jax.ShapeDtypeStruct((bsz, seq + 2 * tt, half), BF16),
        ],
        compiler_params=pltpu.CompilerParams(
            dimension_semantics=("arbitrary", "arbitrary"), vmem_limit_bytes=VMEM_LIMIT),
        name="proj",
    )(x, mod_l, g_pre_l.reshape(1, d_model), w_b)


def _mix_body(hx_ref, hxp_ref, hxn_ref, v1_ref, v2_ref, wa_ref, wg_ref, cw_ref, cb_ref, br_ref,
              bi_ref, lam_ref, dww_ref, dww16_ref, dwb_ref, lng_ref, lnb_ref, pm_ref, h0f_ref,
              zero_ref,
              p_ref, q_ref, mb_ref, ab_ref, hbo_ref,
              hxs_s, ua_s, bg_s, vc_s, g_s, hl_s, al_s, po_s, q_s, carry_s, vpad_s, sh_s, wb_s,
              y_s,
              *, tt, nt, d_model, width, v2_front):
    i = pl.program_id(1)
    half = width // 2
    nseg = tt // GRID_W
    seg_rows = GRID_W + 2 * SEG_PAD
    ext = LRU_CONV - 1
    reach = CONV_PAD * GRID_W
    row_zero = zero_ref[0]

    @pl.when(i == 0)
    def _():
        carry_s[...] = h0f_ref[0]

    def col_slices(total):
        return [slice(c, c + MXU_N) for c in range(0, total, MXU_N)]


    def pad_segment(r):
        def run():
            base = r * seg_rows
            zpad = jnp.zeros((SEG_PAD, half), F32)
            vpad_s[base:base + SEG_PAD, :] = zpad
            vpad_s[base + SEG_PAD:base + SEG_PAD + GRID_W, :] = (
                v1_ref[0, r * GRID_W:(r + 1) * GRID_W, :].astype(F32))
            vpad_s[base + SEG_PAD + GRID_W:base + seg_rows, :] = zpad
        return run

    def conv_chunk(src_fn, w_cols, ci, out_cols):
        def run():
            token = None
            for part in range(CHUNK // CONV_ROWS):
                acc = jnp.zeros((CONV_ROWS, half), F32)
                for k in range(CONV_K):
                    acc = acc + dww_ref[k:k + 1, w_cols] * src_fn(ci, k, part * CONV_ROWS)
                row0 = ci * CHUNK + part * CONV_ROWS
                y_s[row0:row0 + CONV_ROWS, out_cols] = acc
                token = acc[0:ROWS, 0:128]
            return token
        return run

    def shifted(ci, k, r):
        off = SEG_PAD - CONV_PAD + k
        start = ci * seg_rows + (off // ROWS) * ROWS + r
        return sh_s[off % ROWS, start:start + CONV_ROWS, :]

    def spread_taps():
        for k in range(CONV_K):
            wb_s[k] = jnp.broadcast_to(dww16_ref[k:k + 1, half:], (2 * ROWS, half))

    def col_chunk(ci):
        def run():
            win0 = i * tt + (v2_front - reach) + ci * CHUNK
            token = None
            for rt in range(CHUNK // (2 * ROWS)):
                acc = None
                for g0 in range(0, CONV_K, TAP_GROUP):
                    part = None
                    for k in range(g0, min(g0 + TAP_GROUP, CONV_K)):
                        start = pl.multiple_of(win0 + k * GRID_W + rt * 2 * ROWS, 2 * ROWS)
                        term = wb_s[k] * v2_ref[0, pl.ds(start, 2 * ROWS), :]
                        part = term if part is None else part + term
                    part = part.astype(F32)
                    acc = part if acc is None else acc + part
                row0 = ci * CHUNK + rt * 2 * ROWS
                y_s[row0:row0 + 2 * ROWS, half:] = acc
                token = acc[0:ROWS, 0:128]
            return token
        return run

    def permute(cs):
        def run(token):
            hxs_s[0:tt, cs] = _dot(pm_ref[...], hx_ref[0, :, cs]).astype(BF16)
        return run

    def project_a(cs):
        def run(token):
            _mark(hxs_s.at[tt:tt + SPARE_ROWS, 0:128], token)
            lhs = hxs_s[pl.ds(pl.multiple_of(row_zero, SPARE_ROWS), tt), :]
            ua_s[UA_ROW0:UA_ROW0 + tt, cs] = _dot(lhs, wa_ref[:, cs])
        return run

    def project_b(cs):
        def run(token):
            cols = slice(2 * width + cs.start, 2 * width + cs.stop)
            bg_s[:, cs] = _dot(hx_ref[0], wa_ref[:, cols])
        return run

    prep = ([pad_segment(r) for r in range(nseg)] + _shifted_copy_pieces(vpad_s, sh_s)
            + [spread_taps])
    rows_first = len(prep)
    cols_first = rows_first + nseg
    vpu = (prep
           + [conv_chunk(shifted, slice(0, half), ci, slice(0, half)) for ci in range(nseg)]
           + [col_chunk(ci) for ci in range(tt // CHUNK)])
    mxu = ([permute(cs) for cs in col_slices(d_model)]
           + [project_a(cs) for cs in col_slices(2 * width)]
           + [project_b(cs) for cs in col_slices(width)])
    after = [None, None, rows_first, rows_first + 1,
             rows_first + 2, rows_first + 4, rows_first + 5, rows_first + 7,
             cols_first + 1, cols_first + 2]
    _chain(mxu, vpu, after)

    wa_a = wa_ref[:, :width]
    a_prev = jnp.where(i > 0, _dot(hxp_ref[0], wa_a)[HALO - ext:, :], 0.0)
    a_next = jnp.where(i < nt - 1, _dot(hxn_ref[0], wa_a)[:ext, :], 0.0)
    _short_convs(ua_s, a_prev, a_next, cw_ref[...], cb_ref[...], vc_s, tt, width)

    def finish_conv(c, nrows):
        def run():
            rs = slice(c * nrows, (c + 1) * nrows)
            y = y_s[rs, :] + dwb_ref[...]
            conv = _silu(_layer_norm(y, lng_ref[...], lnb_ref[...]))
            out = conv * _silu(bg_s[rs, :])
            mb_ref[0, rs, :] = out.astype(BF16)
            return out[0:ROWS, 0:128]
        return run

    def gate_piece(d, hf):
        def run(token):
            lo = hf * half
            _mark(vc_s.at[d, tt:tt + SPARE_ROWS, lo:lo + 128], token)
            lhs = vc_s[d, pl.ds(pl.multiple_of(row_zero, SPARE_ROWS), tt), lo:lo + half]
            res = _dot(lhs.astype(BF16), wg_ref[d, hf])
            g_s[d, :, lo:lo + half] = res[:, :half]
            g_s[d, :, width + lo:width + lo + half] = res[:, half:]
        return run

    _chain([gate_piece(d, hf) for d in range(2) for hf in range(2)],
           [finish_conv(c, tt // 4) for c in range(4)], [None, 0, 1, 2])

    hf_last, hb_first, a_total = _rglru_scan(
        ua_s, br_ref[...], bi_ref[...], lam_ref[...], carry_s[...], vc_s, g_s, hl_s, al_s,
        po_s, q_s, tt, width)
    carry_s[...] = hf_last
    ab_ref[0, 0] = a_total
    hbo_ref[0, 0] = hb_first
    p_ref[0] = po_s[...].astype(BF16)
    q_ref[0] = q_s[...].astype(BF16)


def _mix_call(hx, v1, v2p, w_a, wg_l, cw, cb, br, bi, lam, dww, dwb, lng, lnb, h0f):
    bsz, seq, d_model = hx.shape
    width = cb.shape[-1]
    half = width // 2
    tt = TILE_MIX
    nt = seq // tt
    hb_per_tile = tt // HALO
    n_halo_blocks = seq // HALO
    v2_front = (v2p.shape[1] - seq) // 2
    nseg = tt // GRID_W
    seg_rows = GRID_W + 2 * SEG_PAD
    reach = CONV_PAD * GRID_W
    pm, _ = _perm_matrices(tt)

    def full(a):
        nd = a.ndim
        return pl.BlockSpec(a.shape, lambda b, i, _nd=nd: (0,) * _nd)

    small = (w_a, wg_l, cw, cb, br, bi, lam, dww, dww.astype(BF16), dwb, lng, lnb, pm)
    tile = lambda b, i: (b, i, 0)
    out_shapes = [jax.ShapeDtypeStruct((bsz, seq, width), BF16)] * 3 + [
        jax.ShapeDtypeStruct((bsz, nt, 1, width), F32)] * 2
    return pl.pallas_call(
        functools.partial(_mix_body, tt=tt, nt=nt, d_model=d_model, width=width,
                          v2_front=v2_front),
        grid=(bsz, nt),
        in_specs=[
            pl.BlockSpec((1, tt, d_model), tile),
            pl.BlockSpec((1, HALO, d_model),
                         lambda b, i: (b, jnp.maximum(i * hb_per_tile - 1, 0), 0)),
            pl.BlockSpec((1, HALO, d_model),
                         lambda b, i: (b, jnp.minimum((i + 1) * hb_per_tile, n_halo_blocks - 1), 0)),
            pl.BlockSpec((1, tt, half), tile),
            pl.BlockSpec((1, v2p.shape[1], half), lambda b, i: (b, 0, 0)),
        ] + [full(a) for a in small] + [
            pl.BlockSpec((1, 1, width), lambda b, i: (b, 0, 0)),
            pl.BlockSpec(memory_space=pltpu.SMEM),
        ],
        out_specs=[pl.BlockSpec((1, tt, width), tile)] * 3 + [
            pl.BlockSpec((1, 1, 1, width), lambda b, i: (b, i, 0, 0))] * 2,
        out_shape=out_shapes,
        scratch_shapes=[
            pltpu.VMEM((tt + SPARE_ROWS, d_model), BF16),
            pltpu.VMEM((tt + 2 * UA_ROW0, 2 * width), F32),
            pltpu.VMEM((tt, width), F32),
            pltpu.VMEM((2, tt + SPARE_ROWS, width), F32),
            pltpu.VMEM((2, tt, 2 * width), F32),
            pltpu.VMEM((2, tt, width), F32),
            pltpu.VMEM((2, tt, width), F32),
            pltpu.VMEM((tt, width), F32),
            pltpu.VMEM((tt, width), F32),
            pltpu.VMEM((1, width), F32),
            pltpu.VMEM((nseg * seg_rows, half), F32),
            pltpu.VMEM((ROWS, nseg * seg_rows - ROWS, half), F32),
            pltpu.VMEM((CONV_K, 2 * ROWS, half), BF16),
            pltpu.VMEM((tt, width), F32),
        ],
        compiler_params=pltpu.CompilerParams(
            dimension_semantics=("arbitrary", "arbitrary"), vmem_limit_bytes=VMEM_LIMIT,
            ),
        name="mixer",
    )(hx, hx, hx, v1, v2p, *small, h0f, jnp.zeros((1,), jnp.int32))


def _out_body(p_ref, q_ref, mb_ref, x_ref, wo_ref, gpost_ref, mod_ref, ab_ref, hb_ref, h0b_ref,
              pmt_ref, o_ref, carry_s, *, d_model, width):
    @pl.when(pl.program_id(1) == 0)
    def _():
        carry_s[...] = h0b_ref[0]

    c = carry_s[...]
    mix_a = (p_ref[0].astype(F32) + q_ref[0].astype(F32) * c).astype(BF16)
    mix_a = _dot(pmt_ref[...], mix_a).astype(BF16)
    mix = _dot(mix_a, wo_ref[:width, :]) + _dot(mb_ref[0], wo_ref[width:, :])
    gate = mod_ref[0][:, 2 * d_model:]
    o_ref[0] = x_ref[0] + gate * _rms_norm(mix, gpost_ref[...])
    carry_s[...] = hb_ref[0, 0] + ab_ref[0, 0] * c


def _out_call(p, q, mb, x, w_o, g_post_l, mod_l, ab, hbo, h0b):
    bsz, seq, d_model = x.shape
    width = p.shape[-1]
    tt = TILE_MIX
    nt = seq // tt
    rev = lambda b, j: (b, nt - 1 - j, 0)
    _, pmt = _perm_matrices(tt)
    return pl.pallas_call(
        functools.partial(_out_body, d_model=d_model, width=width),
        grid=(bsz, nt),
        in_specs=[
            pl.BlockSpec((1, tt, width), rev),
            pl.BlockSpec((1, tt, width), rev),
            pl.BlockSpec((1, tt, width), rev),
            pl.BlockSpec((1, tt, d_model), rev),
            pl.BlockSpec(w_o.shape, lambda b, j: (0, 0)),
            pl.BlockSpec((1, d_model), lambda b, j: (0, 0)),
            pl.BlockSpec((1, 1, 3 * d_model), lambda b, j: (b, 0, 0)),
            pl.BlockSpec((1, 1, 1, width), lambda b, j: (b, nt - 1 - j, 0, 0)),
            pl.BlockSpec((1, 1, 1, width), lambda b, j: (b, nt - 1 - j, 0, 0)),
            pl.BlockSpec((1, 1, width), lambda b, j: (b, 0, 0)),
            pl.BlockSpec((tt, tt), lambda b, j: (0, 0)),
        ],
        out_specs=pl.BlockSpec((1, tt, d_model), rev),
        out_shape=jax.ShapeDtypeStruct((bsz, seq, d_model), F32),
        scratch_shapes=[pltpu.VMEM((1, width), F32)],
        compiler_params=pltpu.CompilerParams(
            dimension_semantics=("arbitrary", "arbitrary"), vmem_limit_bytes=VMEM_LIMIT),
        name="out",
    )(p, q, mb, x, w_o, g_post_l.reshape(1, d_model), mod_l, ab, hbo, h0b, pmt)


def _pack_gate_weights(w_r, w_i):
    depth, ndir, heads, hd, _ = w_r.shape
    hh = heads // 2
    eye = jnp.eye(hh, dtype=w_r.dtype)

    def bd(w):
        w = w.reshape(depth, ndir, 2, hh, hd, hd)
        return jnp.einsum("ldfhij,hg->ldfhigj", w, eye).reshape(depth, ndir, 2, hh * hd, hh * hd)

    return jnp.concatenate([bd(w_r), bd(w_i)], axis=-1).astype(BF16)


def kernel(x, c, ctx, c_ctx, w_mod, b_mod, g_pre, g_post, w_in, conv_a_w, conv_a_b, w_rgate,
           b_rgate, w_igate, b_igate, lru_lambda, dw_w, dw_b, ln_g, ln_b, w_out):
    bsz, seq, d_model = x.shape
    depth = w_mod.shape[0]
    width = conv_a_b.shape[-1]
    assert bsz + 1 <= ROWS and seq % TILE_PROJ == 0 and seq % TILE_MIX == 0
    assert TILE_MIX % GRID_W == 0 and TILE_PROJ >= CONV_PAD * GRID_W
    assert ctx.shape[1] % CHUNK == 0 and width % MXU_N == 0 and d_model % MXU_N == 0

    act = jnp.concatenate(
        [c, c_ctx[None, :], jnp.zeros((ROWS - bsz - 1, d_model), F32)], axis=0)
    mods = _mods_call(act, w_mod, b_mod)

    w_in_bf = w_in.astype(BF16)
    w_out_bf = w_out.astype(BF16)
    wg = _pack_gate_weights(w_rgate, w_igate)

    h0 = _ctx_call(ctx, mods[:, bsz:bsz + 1, :], g_pre, g_post, w_in_bf, w_out_bf, wg, conv_a_w,
                   conv_a_b, b_rgate, b_igate, lru_lambda, dw_w, dw_b, ln_g, ln_b)

    for l in range(depth):
        mod_l = mods[l].reshape(ROWS, 1, 3 * d_model)
        w_b = w_in_bf[l, :, 2 * width:4 * width]
        w_a = jnp.concatenate([w_in_bf[l, :, :2 * width], w_in_bf[l, :, 4 * width:]], axis=1)
        hx, v1, v2p = _proj_call(x, mod_l, g_pre[l], w_b)
        p, q, mb, ab, hbo = _mix_call(
            hx, v1, v2p, w_a, wg[l], conv_a_w[l], conv_a_b[l], b_rgate[l], b_igate[l],
            lru_lambda[l], dw_w[l], dw_b[l:l + 1], ln_g[l:l + 1], ln_b[l:l + 1],
            h0[:, 2 * l:2 * l + 1, :])
        x = _out_call(p, q, mb, x, w_out_bf[l], g_post[l], mod_l, ab, hbo,
                      h0[:, 2 * l + 1:2 * l + 2, :])
    return x
```

```python
import functools

import jax
import jax.numpy as jnp
from jax import lax
from jax.experimental import pallas as pl
from jax.experimental.pallas import tpu as pltpu

F32 = jnp.float32
BF16 = jnp.bfloat16

EPS = 1e-6
LRU_C = 8.0
LRU_CONV = 4
CONV_K = 31
CONV_PAD = CONV_K // 2
GRID_W = 64
HALO = 16
ROWS = 8
CHUNK = 64
CONV_ROWS = 32
SEG_PAD = 16
MXU_N = 256
SPARE_ROWS = 16
TAP_GROUP = 8
ROWS_PER_CHAIN = 256

TILE_PROJ = 1024
TILE_MIX = 512
VMEM_LIMIT = 56 * 1024 * 1024


def _silu(x):
    return x * jax.nn.sigmoid(x)


def _softplus(x):
    return jnp.maximum(x, 0.0) + jnp.log1p(jnp.exp(-jnp.abs(x)))


def _rms_norm(x, g):
    ms = jnp.mean(x * x, axis=-1, keepdims=True)
    return x * lax.rsqrt(ms + EPS) * g


def _layer_norm(x, g, b):
    mu = jnp.mean(x, axis=-1, keepdims=True)
    xc = x - mu
    var = jnp.mean(xc * xc, axis=-1, keepdims=True)
    return xc * lax.rsqrt(var + EPS) * g + b


def _dot(a, b):
    return jnp.dot(a, b, preferred_element_type=F32)


def _interleave(mxu_pieces, vpu_pieces):
    n = len(mxu_pieces)
    per = -(-len(vpu_pieces) // n) if n else 0
    for j, piece in enumerate(mxu_pieces):
        piece()
        for fill in vpu_pieces[j * per:(j + 1) * per]:
            fill()
    for fill in vpu_pieces[n * per:]:
        fill()


def _mark(spare_ref, token):
    if token is None:
        return
    reps = spare_ref.shape[0] // ROWS
    spare_ref[...] = jnp.concatenate([token] * reps, axis=0).astype(spare_ref.dtype)


def _chain(mxu_pieces, vpu_pieces, after):
    tokens = []
    for piece, dep in zip(mxu_pieces, after):
        while dep is not None and len(tokens) <= dep:
            tokens.append(vpu_pieces[len(tokens)]())
        piece(None if dep is None else tokens[dep])
    for fill in vpu_pieces[len(tokens):]:
        fill()


def _perm_matrices(n):
    i = jnp.arange(n)
    src = (n // ROWS) * (i % ROWS) + i // ROWS
    pm = (src[:, None] == i[None, :]).astype(BF16)
    return pm, pm.T


UA_ROW0 = (LRU_CONV - 1) * ROWS


def _short_convs(ua_s, a_prev, a_next, cw, cb, vc_s, n, width):
    sub = n // ROWS
    ext = LRU_CONV - 1
    row = lax.broadcasted_iota(jnp.int32, (ROWS, width), 0)
    cw = 0.5 * cw
    cb = 0.5 * cb

    def blk(p):
        return slice(p * ROWS, (p + 1) * ROWS)

    for k in range(1, ext + 1):
        tail = pltpu.roll(ua_s[blk(ext + sub - k), :width], 1, 0)
        ua_s[blk(ext - k), :width] = jnp.where(row == 0, a_prev[ext - k:ext - k + 1, :], tail)
    for k in range(ext):
        head = pltpu.roll(ua_s[blk(ext + k), :width], ROWS - 1, 0)
        ua_s[blk(ext + sub + k), :width] = jnp.where(row == ROWS - 1, a_next[k:k + 1, :], head)

    def chunk(ci, carry):
        base = ci * CHUNK
        for d in range(2):
            first = 0 if d == 0 else ext
            acc = jnp.broadcast_to(cb[d:d + 1, :], (CHUNK, width))
            for k in range(LRU_CONV):
                start = pl.multiple_of(base + (first + k) * ROWS, ROWS)
                acc = acc + cw[d, k:k + 1, :] * ua_s[pl.ds(start, CHUNK), :width]
            vc_s[d, pl.ds(pl.multiple_of(base, CHUNK), CHUNK), :] = acc
        return carry

    lax.fori_loop(0, n // CHUNK, chunk, 0)


def _gate_pieces(vc_s, wg_ref, g_s, width):
    half = width // 2

    def piece(d, hf):
        def run():
            res = _dot(vc_s[d, :, hf * half:(hf + 1) * half].astype(BF16), wg_ref[d, hf])
            g_s[d, :, hf * half:(hf + 1) * half] = res[:, :half]
            g_s[d, :, width + hf * half:width + (hf + 1) * half] = res[:, half:]
        return run

    return [piece(d, hf) for d in range(2) for hf in range(2)]


def _sublane_scan(a, h, row, reverse):
    for s in (1, 2, 4):
        if reverse:
            m, shift = row < ROWS - s, ROWS - s
        else:
            m, shift = row >= s, s
        a_sh = jnp.where(m, pltpu.roll(a, shift, 0), 1.0)
        h_sh = jnp.where(m, pltpu.roll(h, shift, 0), 0.0)
        h = h + a * h_sh
        a = a * a_sh
    return a, h


def _rglru_scan(ua_s, br, bi, lam, c0, vc_s, g_s, hl_s, al_s, po_s, q_s, n, width):
    sub = n // ROWS
    row = lax.broadcasted_iota(jnp.int32, (ROWS, width), 0)
    def rows_of(v, d):
        return jnp.broadcast_to(v[d:d + 1, :], (ROWS, width))

    spl_half = (-0.5 * LRU_C) * _softplus(-lam)
    spl_half = [rows_of(spl_half, d) for d in range(2)]
    br_half = [rows_of(0.5 * br, d) for d in range(2)]
    bi_half = [rows_of(0.5 * bi, d) for d in range(2)]

    def coeffs(d, r0):
        g = g_s[d, pl.ds(r0, ROWS), :]
        vh = vc_s[d, pl.ds(r0, ROWS), :]
        t_r = jnp.tanh(g[:, :width] + br_half[d])
        t_i = jnp.tanh(g[:, width:] + bi_half[d])
        la = spl_half[d] + spl_half[d] * t_r
        a = jnp.exp(la)
        x = jnp.tanh(la) * (-1.0 - a * a)
        mult = jnp.where(x > 0.0, x * lax.rsqrt(x), 0.0)
        return a, mult * (vh + vh * t_i)

    def local(j, carry):
        hf, af, hb, ab = carry
        r0 = pl.multiple_of(j * ROWS, ROWS)
        a, b = coeffs(0, r0)
        hf = a * hf + b
        af = a * af
        hl_s[0, pl.ds(r0, ROWS), :] = hf
        al_s[0, pl.ds(r0, ROWS), :] = af
        r1 = pl.multiple_of((sub - 1 - j) * ROWS, ROWS)
        a, b = coeffs(1, r1)
        hb = a * hb + b
        ab = a * ab
        hl_s[1, pl.ds(r1, ROWS), :] = hb
        al_s[1, pl.ds(r1, ROWS), :] = ab
        return hf, af, hb, ab

    zero = jnp.zeros((ROWS, width), F32)
    one = jnp.ones((ROWS, width), F32)
    hf, af, hb, ab = lax.fori_loop(0, sub, local, (zero, one, zero, one), unroll=4)

    af, hf = _sublane_scan(af, hf, row, reverse=False)
    end_f = hf + af * c0
    c_f = jnp.where(row == 0, c0, pltpu.roll(end_f, 1, 0))
    ab, hb = _sublane_scan(ab, hb, row, reverse=True)
    c_b = jnp.where(row == ROWS - 1, 0.0, pltpu.roll(hb, ROWS - 1, 0))
    c_a = jnp.where(row == ROWS - 1, 1.0, pltpu.roll(ab, ROWS - 1, 0))

    if po_s is not None:
        def fix(j, carry):
            r0 = pl.ds(pl.multiple_of(j * ROWS, ROWS), ROWS)
            a_b = al_s[1, r0, :]
            h = (hl_s[0, r0, :] + al_s[0, r0, :] * c_f) + (hl_s[1, r0, :] + a_b * c_b)
            sg = _silu(ua_s[pl.ds(pl.multiple_of(UA_ROW0 + j * ROWS, ROWS), ROWS), width:])
            po_s[r0, :] = h * sg
            if q_s is not None:
                q_s[r0, :] = (a_b * c_a) * sg
            return carry

        lax.fori_loop(0, sub, fix, 0, unroll=8)

    return end_f[ROWS - 1:ROWS, :], hb[0:1, :], ab[0:1, :]


def _shifted_copy_pieces(vpad_ref, sh_ref):
    n = sh_ref.shape[1]

    def piece(j):
        def run():
            sh_ref[j] = vpad_ref[pl.ds(j, n), :]
        return run

    return [piece(j) for j in range(ROWS)]


def _time_conv_pieces(sh_ref, w_ref, ncol, bases, out_ref):
    def piece(ci, base):
        def run():
            acc = jnp.zeros((CHUNK, ncol), F32)
            for k in range(CONV_K):
                off = SEG_PAD - CONV_PAD + k
                start = base + (off // ROWS) * ROWS
                acc = acc + w_ref[k:k + 1, 0:ncol] * sh_ref[off % ROWS, start:start + CHUNK, :]
            out_ref[ci * CHUNK:(ci + 1) * CHUNK, 0:ncol] = acc
        return run

    return [piece(ci, base) for ci, base in enumerate(bases)]


def _mods_body(act_ref, w_ref, b_ref, o_ref):
    a = _silu(act_ref[...])
    o_ref[0] = jnp.dot(a, w_ref[0], precision=lax.Precision.HIGHEST,
                       preferred_element_type=F32) + b_ref[0]


def _mods_call(act, w_mod, b_mod):
    depth, d_model, d3 = w_mod.shape
    ncol = d_model
    return pl.pallas_call(
        _mods_body,
        grid=(depth, d3 // ncol),
        in_specs=[
            pl.BlockSpec((ROWS, d_model), lambda l, n: (0, 0)),
            pl.BlockSpec((1, d_model, ncol), lambda l, n: (l, 0, n)),
            pl.BlockSpec((1, 1, ncol), lambda l, n: (l, 0, n)),
        ],
        out_specs=pl.BlockSpec((1, ROWS, ncol), lambda l, n: (l, 0, n)),
        out_shape=jax.ShapeDtypeStruct((depth, ROWS, d3), F32),
        compiler_params=pltpu.CompilerParams(
            dimension_semantics=("arbitrary", "arbitrary"), vmem_limit_bytes=VMEM_LIMIT),
        name="mods",
    )(act, w_mod, b_mod.reshape(depth, 1, d3))


def _modulate(x, g, m, d_model):
    return _rms_norm(x, g) * (1.0 + m[:, d_model:2 * d_model]) + m[:, :d_model]


def _ctx_body(ctx_ref, mod_ref, gpre_ref, gpost_ref, win_ref, wo_ref, wg_ref, cw_ref, cb_ref,
              br_ref, bi_ref, lam_ref, dww_ref, dwb_ref, lng_ref, lnb_ref, pm_ref, pmt_ref,
              h0_ref,
              ua_s, vc_s, g_s, hl_s, al_s, po_s, vpad_s, sh_s, y_s,
              *, depth, d_model, width, n):
    xc = ctx_ref[0]
    zeros_pad = jnp.zeros((SEG_PAD, width), F32)
    no_rows = jnp.zeros((LRU_CONV - 1, width), F32)
    zero_state = jnp.zeros((1, width), F32)
    for l in range(depth):
        update = l < depth - 1
        m = mod_ref[l]
        hc = _modulate(xc, gpre_ref[l:l + 1, :], m, d_model).astype(BF16)
        hcp = _dot(pm_ref[...], hc).astype(BF16)
        if update:
            ua_s[UA_ROW0:UA_ROW0 + n, :] = _dot(hcp, win_ref[l, :, :2 * width])
            ub = _dot(hc, win_ref[l, :, 2 * width:])
        else:
            ua_s[UA_ROW0:UA_ROW0 + n, :width] = _dot(hcp, win_ref[l, :, :width])
        _short_convs(ua_s, no_rows, no_rows, cw_ref[l], cb_ref[l], vc_s, n, width)
        _interleave(_gate_pieces(vc_s, wg_ref.at[l], g_s, width), [])
        hf_last, hb_first, _ = _rglru_scan(
            ua_s, br_ref[l], bi_ref[l], lam_ref[l], zero_state, vc_s, g_s, hl_s, al_s,
            po_s if update else None, None, n, width)
        h0_ref[0, 2 * l:2 * l + 1, :] = hf_last
        h0_ref[0, 2 * l + 1:2 * l + 2, :] = hb_first
        if update:
            v = ub[:, :width] * jax.nn.sigmoid(ub[:, width:2 * width])
            vpad_s[0:SEG_PAD, :] = zeros_pad
            vpad_s[SEG_PAD:SEG_PAD + n, :] = v
            vpad_s[SEG_PAD + n:, :] = zeros_pad
            _interleave([], _shifted_copy_pieces(vpad_s, sh_s))
            _interleave([], _time_conv_pieces(
                sh_s, dww_ref.at[l], width, [ci * CHUNK for ci in range(n // CHUNK)], y_s))
            y = y_s[...] + dwb_ref[l:l + 1, :]
            conv = _silu(_layer_norm(y, lng_ref[l:l + 1, :], lnb_ref[l:l + 1, :]))
            mix_a = _dot(pmt_ref[...], po_s[...].astype(BF16)).astype(BF16)
            mix_b = (conv * _silu(ub[:, 2 * width:])).astype(BF16)
            mix = _dot(mix_a, wo_ref[l, :width, :]) + _dot(mix_b, wo_ref[l, width:, :])
            xc = xc + m[:, 2 * d_model:] * _rms_norm(mix, gpost_ref[l:l + 1, :])


def _ctx_call(ctx, mod_c, g_pre, g_post, w_in_bf, w_out_bf, wg, conv_a_w, conv_a_b, b_rgate,
              b_igate, lru_lambda, dw_w, dw_b, ln_g, ln_b):
    bsz, n, d_model = ctx.shape
    depth = w_in_bf.shape[0]
    width = conv_a_b.shape[-1]
    npad = n + 2 * SEG_PAD
    pm, pmt = _perm_matrices(n)

    def full(a):
        nd = a.ndim
        return pl.BlockSpec(a.shape, lambda b, _nd=nd: (0,) * _nd)

    params = (mod_c, g_pre, g_post, w_in_bf, w_out_bf, wg, conv_a_w, conv_a_b, b_rgate, b_igate,
              lru_lambda, dw_w, dw_b, ln_g, ln_b, pm, pmt)
    return pl.pallas_call(
        functools.partial(_ctx_body, depth=depth, d_model=d_model, width=width, n=n),
        grid=(bsz,),
        in_specs=[pl.BlockSpec((1, n, d_model), lambda b: (b, 0, 0))] + [full(a) for a in params],
        out_specs=pl.BlockSpec((1, 2 * depth, width), lambda b: (b, 0, 0)),
        out_shape=jax.ShapeDtypeStruct((bsz, 2 * depth, width), F32),
        scratch_shapes=[
            pltpu.VMEM((n + 2 * UA_ROW0, 2 * width), F32),
            pltpu.VMEM((2, n, width), F32),
            pltpu.VMEM((2, n, 2 * width), F32),
            pltpu.VMEM((2, n, width), F32),
            pltpu.VMEM((2, n, width), F32),
            pltpu.VMEM((n, width), F32),
            pltpu.VMEM((npad, width), F32),
            pltpu.VMEM((ROWS, npad - ROWS, width), F32),
            pltpu.VMEM((n, width), F32),
        ],
        compiler_params=pltpu.CompilerParams(
            dimension_semantics=("arbitrary",), vmem_limit_bytes=VMEM_LIMIT),
        name="context",
    )(ctx, *params)


def _proj_body(x_ref, mod_ref, g_ref, w_ref, hx_ref, v1_ref, v2_ref, *, nt, d_model, width):
    j = pl.program_id(1)
    is_pad = jnp.logical_or(j == 0, j == nt + 1)

    @pl.when(is_pad)
    def _():
        v2_ref[...] = jnp.zeros(v2_ref.shape, v2_ref.dtype)

    @pl.when(jnp.logical_not(is_pad))
    def _():
        half = width // 2
        rows = ROWS_PER_CHAIN
        for c in range(x_ref.shape[1] // rows):
            rs = slice(c * rows, (c + 1) * rows)
            hb = _modulate(x_ref[0, rs, :], g_ref[...], mod_ref[0], d_model).astype(BF16)
            hx_ref[0, rs, :] = hb
            u = _dot(hb, w_ref[...])
            v = u[:, :width] * jax.nn.sigmoid(u[:, width:])
            v1_ref[0, rs, :] = v[:, :half].astype(BF16)
            v2_ref[0, rs, :] = v[:, half:].astype(BF16)


def _proj_call(x, mod_l, g_pre_l, w_b):
    bsz, seq, d_model = x.shape
    width = w_b.shape[1] // 2
    half = width // 2
    tt = TILE_PROJ
    nt = seq // tt

    def tok(b, j):
        return (b, jnp.clip(j - 1, 0, nt - 1), 0)

    return pl.pallas_call(
        functools.partial(_proj_body, nt=nt, d_model=d_model, width=width),
        grid=(bsz, nt + 2),
        in_specs=[
            pl.BlockSpec((1, tt, d_model), tok),
            pl.BlockSpec((1, 1, 3 * d_model), lambda b, j: (b, 0, 0)),
            pl.BlockSpec((1, d_model), lambda b, j: (0, 0)),
            pl.BlockSpec(w_b.shape, lambda b, j: (0, 0)),
        ],
        out_specs=[
            pl.BlockSpec((1, tt, d_model), tok),
            pl.BlockSpec((1, tt, half), tok),
            pl.BlockSpec((1, tt, half), lambda b, j: (b, j, 0)),
        ],
        out_shape=[
            jax.ShapeDtypeStruct((bsz, seq, d_model), BF16),
            jax.ShapeDtypeStruct((bsz, seq, half), BF16),
            jax.ShapeDtypeStruct((bsz, seq + 2 * tt, half), BF16),
        ],
        compiler_params=pltpu.CompilerParams(
            dimension_semantics=("arbitrary", "arbitrary"), vmem_limit_bytes=VMEM_LIMIT),
        name="proj",
    )(x, mod_l, g_pre_l.reshape(1, d_model), w_b)


def _mix_body(hx_ref, hxp_ref, hxn_ref, v1_ref, v2_ref, wa_ref, wg_ref, cw_ref, cb_ref, br_ref,
              bi_ref, lam_ref, dww_ref, dww16_ref, dwb_ref, lng_ref, lnb_ref, pm_ref, h0f_ref,
              zero_ref,
              p_ref, q_ref, mb_ref, ab_ref, hbo_ref,
              hxs_s, ua_s, bg_s, vc_s, g_s, hl_s, al_s, po_s, q_s, carry_s, vpad_s, sh_s, wb_s,
              y_s,
              *, tt, nt, d_model, width, v2_front):
    i = pl.program_id(1)
    half = width // 2
    nseg = tt // GRID_W
    seg_rows = GRID_W + 2 * SEG_PAD
    ext = LRU_CONV - 1
    reach = CONV_PAD * GRID_W
    row_zero = zero_ref[0]

    @pl.when(i == 0)
    def _():
        carry_s[...] = h0f_ref[0]

    def col_slices(total):
        return [slice(c, c + MXU_N) for c in range(0, total, MXU_N)]


    def pad_segment(r):
        def run():
            base = r * seg_rows
            zpad = jnp.zeros((SEG_PAD, half), F32)
            vpad_s[base:base + SEG_PAD, :] = zpad
            vpad_s[base + SEG_PAD:base + SEG_PAD + GRID_W, :] = (
                v1_ref[0, r * GRID_W:(r + 1) * GRID_W, :].astype(F32))
            vpad_s[base + SEG_PAD + GRID_W:base + seg_rows, :] = zpad
        return run

    def conv_chunk(src_fn, w_cols, ci, out_cols):
        def run():
            token = None
            for part in range(CHUNK // CONV_ROWS):
                acc = jnp.zeros((CONV_ROWS, half), F32)
                for k in range(CONV_K):
                    acc = acc + dww_ref[k:k + 1, w_cols] * src_fn(ci, k, part * CONV_ROWS)
                row0 = ci * CHUNK + part * CONV_ROWS
                y_s[row0:row0 + CONV_ROWS, out_cols] = acc
                token = acc[0:ROWS, 0:128]
            return token
        return run

    def shifted(ci, k, r):
        off = SEG_PAD - CONV_PAD + k
        start = ci * seg_rows + (off // ROWS) * ROWS + r
        return sh_s[off % ROWS, start:start + CONV_ROWS, :]

    def spread_taps():
        for k in range(CONV_K):
            wb_s[k] = jnp.broadcast_to(dww16_ref[k:k + 1, half:], (2 * ROWS, half))

    def col_chunk(ci):
        def run():
            win0 = i * tt + (v2_front - reach) + ci * CHUNK
            token = None
            for rt in range(CHUNK // (2 * ROWS)):
                acc = None
                for g0 in range(0, CONV_K, TAP_GROUP):
                    part = None
                    for k in range(g0, min(g0 + TAP_GROUP, CONV_K)):
                        start = pl.multiple_of(win0 + k * GRID_W + rt * 2 * ROWS, 2 * ROWS)
                        term = wb_s[k] * v2_ref[0, pl.ds(start, 2 * ROWS), :]
                        part = term if part is None else part + term
                    part = part.astype(F32)
                    acc = part if acc is None else acc + part
                row0 = ci * CHUNK + rt * 2 * ROWS
                y_s[row0:row0 + 2 * ROWS, half:] = acc
                token = acc[0:ROWS, 0:128]
            return token
        return run

    def permute(cs):
        def run(token):
            hxs_s[0:tt, cs] = _dot(pm_ref[...], hx_ref[0, :, cs]).astype(BF16)
        return run

    def project_a(cs):
        def run(token):
            _mark(hxs_s.at[tt:tt + SPARE_ROWS, 0:128], token)
            lhs = hxs_s[pl.ds(pl.multiple_of(row_zero, SPARE_ROWS), tt), :]
            ua_s[UA_ROW0:UA_ROW0 + tt, cs] = _dot(lhs, wa_ref[:, cs])
        return run

    def project_b(cs):
        def run(token):
            cols = slice(2 * width + cs.start, 2 * width + cs.stop)
            bg_s[:, cs] = _dot(hx_ref[0], wa_ref[:, cols])
        return run

    prep = ([pad_segment(r) for r in range(nseg)] + _shifted_copy_pieces(vpad_s, sh_s)
            + [spread_taps])
    rows_first = len(prep)
    cols_first = rows_first + nseg
    vpu = (prep
           + [conv_chunk(shifted, slice(0, half), ci, slice(0, half)) for ci in range(nseg)]
           + [col_chunk(ci) for ci in range(tt // CHUNK)])
    mxu = ([permute(cs) for cs in col_slices(d_model)]
           + [project_a(cs) for cs in col_slices(2 * width)]
           + [project_b(cs) for cs in col_slices(width)])
    after = [None, None, rows_first, rows_first + 1,
             rows_first + 2, rows_first + 4, rows_first + 5, rows_first + 7,
             cols_first + 1, cols_first + 2]
    _chain(mxu, vpu, after)

    wa_a = wa_ref[:, :width]
    a_prev = jnp.where(i > 0, _dot(hxp_ref[0], wa_a)[HALO - ext:, :], 0.0)
    a_next = jnp.where(i < nt - 1, _dot(hxn_ref[0], wa_a)[:ext, :], 0.0)
    _short_convs(ua_s, a_prev, a_next, cw_ref[...], cb_ref[...], vc_s, tt, width)

    def finish_conv(c, nrows):
        def run():
            rs = slice(c * nrows, (c + 1) * nrows)
            y = y_s[rs, :] + dwb_ref[...]
            conv = _silu(_layer_norm(y, lng_ref[...], lnb_ref[...]))
            out = conv * _silu(bg_s[rs, :])
            mb_ref[0, rs, :] = out.astype(BF16)
            return out[0:ROWS, 0:128]
        return run

    def gate_piece(d, hf):
        def run(token):
            lo = hf * half
            _mark(vc_s.at[d, tt:tt + SPARE_ROWS, lo:lo + 128], token)
            lhs = vc_s[d, pl.ds(pl.multiple_of(row_zero, SPARE_ROWS), tt), lo:lo + half]
            res = _dot(lhs.astype(BF16), wg_ref[d, hf])
            g_s[d, :, lo:lo + half] = res[:, :half]
            g_s[d, :, width + lo:width + lo + half] = res[:, half:]
        return run

    _chain([gate_piece(d, hf) for d in range(2) for hf in range(2)],
           [finish_conv(c, tt // 4) for c in range(4)], [None, 0, 1, 2])

    hf_last, hb_first, a_total = _rglru_scan(
        ua_s, br_ref[...], bi_ref[...], lam_ref[...], carry_s[...], vc_s, g_s, hl_s, al_s,
        po_s, q_s, tt, width)
    carry_s[...] = hf_last
    ab_ref[0, 0] = a_total
    hbo_ref[0, 0] = hb_first
    p_ref[0] = po_s[...].astype(BF16)
    q_ref[0] = q_s[...].astype(BF16)


def _mix_call(hx, v1, v2p, w_a, wg_l, cw, cb, br, bi, lam, dww, dwb, lng, lnb, h0f):
    bsz, seq, d_model = hx.shape
    width = cb.shape[-1]
    half = width // 2
    tt = TILE_MIX
    nt = seq // tt
    hb_per_tile = tt // HALO
    n_halo_blocks = seq // HALO
    v2_front = (v2p.shape[1] - seq) // 2
    nseg = tt // GRID_W
    seg_rows = GRID_W + 2 * SEG_PAD
    reach = CONV_PAD * GRID_W
    pm, _ = _perm_matrices(tt)

    def full(a):
        nd = a.ndim
        return pl.BlockSpec(a.shape, lambda b, i, _nd=nd: (0,) * _nd)

    small = (w_a, wg_l, cw, cb, br, bi, lam, dww, dww.astype(BF16), dwb, lng, lnb, pm)
    tile = lambda b, i: (b, i, 0)
    out_shapes = [jax.ShapeDtypeStruct((bsz, seq, width), BF16)] * 3 + [
        jax.ShapeDtypeStruct((bsz, nt, 1, width), F32)] * 2
    return pl.pallas_call(
        functools.partial(_mix_body, tt=tt, nt=nt, d_model=d_model, width=width,
                          v2_front=v2_front),
        grid=(bsz, nt),
        in_specs=[
            pl.BlockSpec((1, tt, d_model), tile),
            pl.BlockSpec((1, HALO, d_model),
                         lambda b, i: (b, jnp.maximum(i * hb_per_tile - 1, 0), 0)),
            pl.BlockSpec((1, HALO, d_model),
                         lambda b, i: (b, jnp.minimum((i + 1) * hb_per_tile, n_halo_blocks - 1), 0)),
            pl.BlockSpec((1, tt, half), tile),
            pl.BlockSpec((1, v2p.shape[1], half), lambda b, i: (b, 0, 0)),
        ] + [full(a) for a in small] + [
            pl.BlockSpec((1, 1, width), lambda b, i: (b, 0, 0)),
            pl.BlockSpec(memory_space=pltpu.SMEM),
        ],
        out_specs=[pl.BlockSpec((1, tt, width), tile)] * 3 + [
            pl.BlockSpec((1, 1, 1, width), lambda b, i: (b, i, 0, 0))] * 2,
        out_shape=out_shapes,
        scratch_shapes=[
            pltpu.VMEM((tt + SPARE_ROWS, d_model), BF16),
            pltpu.VMEM((tt + 2 * UA_ROW0, 2 * width), F32),
            pltpu.VMEM((tt, width), F32),
            pltpu.VMEM((2, tt + SPARE_ROWS, width), F32),
            pltpu.VMEM((2, tt, 2 * width), F32),
            pltpu.VMEM((2, tt, width), F32),
            pltpu.VMEM((2, tt, width), F32),
            pltpu.VMEM((tt, width), F32),
            pltpu.VMEM((tt, width), F32),
            pltpu.VMEM((1, width), F32),
            pltpu.VMEM((nseg * seg_rows, half), F32),
            pltpu.VMEM((ROWS, nseg * seg_rows - ROWS, half), F32),
            pltpu.VMEM((CONV_K, 2 * ROWS, half), BF16),
            pltpu.VMEM((tt, width), F32),
        ],
        compiler_params=pltpu.CompilerParams(
            dimension_semantics=("arbitrary", "arbitrary"), vmem_limit_bytes=VMEM_LIMIT,
            ),
        name="mixer",
    )(hx, hx, hx, v1, v2p, *small, h0f, jnp.zeros((1,), jnp.int32))


def _out_body(*refs, d_model, width, fuse_next):
    (p_ref, q_ref, mb_ref, x_ref, wo_ref, gpost_ref, mod_ref, ab_ref, hb_ref, h0b_ref,
     pmt_ref) = refs[:11]
    if fuse_next:
        modn_ref, gpren_ref, wbn_ref, _, o_ref, hx_ref, v1_ref, v2_ref, carry_s = refs[11:]
    else:
        o_ref, carry_s = refs[11:]

    @pl.when(pl.program_id(1) == 0)
    def _():
        carry_s[...] = h0b_ref[0]

    c = carry_s[...]
    mix_a = (p_ref[0].astype(F32) + q_ref[0].astype(F32) * c).astype(BF16)
    gate = mod_ref[0][:, 2 * d_model:]
    mix_t = _dot(pmt_ref[...], mix_a).astype(BF16)
    rows = ROWS_PER_CHAIN
    half = width // 2
    for ch in range(p_ref.shape[1] // rows):
        rs = slice(ch * rows, (ch + 1) * rows)
        mix = _dot(mix_t[rs, :], wo_ref[:width, :]) + _dot(mb_ref[0, rs, :], wo_ref[width:, :])
        x_new = x_ref[0, rs, :] + gate * _rms_norm(mix, gpost_ref[...])
        o_ref[0, rs, :] = x_new
        if fuse_next:
            hb = _modulate(x_new, gpren_ref[...], modn_ref[0], d_model).astype(BF16)
            hx_ref[0, rs, :] = hb
            u = _dot(hb, wbn_ref[...])
            v = u[:, :width] * jax.nn.sigmoid(u[:, width:])
            v1_ref[0, rs, :] = v[:, :half].astype(BF16)
            v2_ref[0, rs, :] = v[:, half:].astype(BF16)
    carry_s[...] = hb_ref[0, 0] + ab_ref[0, 0] * c


def _out_call(p, q, mb, x, w_o, g_post_l, mod_l, ab, hbo, h0b, nxt=None):
    bsz, seq, d_model = x.shape
    width = p.shape[-1]
    half = width // 2
    tt = TILE_MIX
    nt = seq // tt
    pad_blocks = TILE_PROJ // tt
    rev = lambda b, j: (b, nt - 1 - j, 0)
    _, pmt = _perm_matrices(tt)
    operands = [p, q, mb, x, w_o, g_post_l.reshape(1, d_model), mod_l, ab, hbo, h0b, pmt]
    in_specs = [
        pl.BlockSpec((1, tt, width), rev),
        pl.BlockSpec((1, tt, width), rev),
        pl.BlockSpec((1, tt, width), rev),
        pl.BlockSpec((1, tt, d_model), rev),
        pl.BlockSpec(w_o.shape, lambda b, j: (0, 0)),
        pl.BlockSpec((1, d_model), lambda b, j: (0, 0)),
        pl.BlockSpec((1, 1, 3 * d_model), lambda b, j: (b, 0, 0)),
        pl.BlockSpec((1, 1, 1, width), lambda b, j: (b, nt - 1 - j, 0, 0)),
        pl.BlockSpec((1, 1, 1, width), lambda b, j: (b, nt - 1 - j, 0, 0)),
        pl.BlockSpec((1, 1, width), lambda b, j: (b, 0, 0)),
        pl.BlockSpec((tt, tt), lambda b, j: (0, 0)),
    ]
    out_specs = [pl.BlockSpec((1, tt, d_model), rev)]
    out_shape = [jax.ShapeDtypeStruct((bsz, seq, d_model), F32)]
    aliases = {}
    if nxt is not None:
        mod_n, g_pre_n, w_b_n = nxt
        v2_zeros = jnp.zeros((bsz, seq + 2 * TILE_PROJ, half), BF16)
        operands += [mod_n, g_pre_n.reshape(1, d_model), w_b_n, v2_zeros]
        in_specs += [
            pl.BlockSpec((1, 1, 3 * d_model), lambda b, j: (b, 0, 0)),
            pl.BlockSpec((1, d_model), lambda b, j: (0, 0)),
            pl.BlockSpec(w_b_n.shape, lambda b, j: (0, 0)),
            pl.BlockSpec(memory_space=pl.ANY),
        ]
        out_specs += [
            pl.BlockSpec((1, tt, d_model), rev),
            pl.BlockSpec((1, tt, half), rev),
            pl.BlockSpec((1, tt, half), lambda b, j: (b, nt - 1 - j + pad_blocks, 0)),
        ]
        out_shape += [
            jax.ShapeDtypeStruct((bsz, seq, d_model), BF16),
            jax.ShapeDtypeStruct((bsz, seq, half), BF16),
            jax.ShapeDtypeStruct(v2_zeros.shape, BF16),
        ]
        aliases = {len(operands) - 1: 3}
    res = pl.pallas_call(
        functools.partial(_out_body, d_model=d_model, width=width, fuse_next=nxt is not None),
        grid=(bsz, nt),
        in_specs=in_specs,
        out_specs=out_specs,
        out_shape=out_shape,
        input_output_aliases=aliases,
        scratch_shapes=[pltpu.VMEM((1, width), F32)],
        compiler_params=pltpu.CompilerParams(
            dimension_semantics=("arbitrary", "arbitrary"), vmem_limit_bytes=VMEM_LIMIT),
        name="out_proj" if nxt is not None else "out",
    )(*operands)
    return res if nxt is not None else res[0]


def _pack_gate_weights(w_r, w_i):
    depth, ndir, heads, hd, _ = w_r.shape
    hh = heads // 2
    eye = jnp.eye(hh, dtype=w_r.dtype)

    def bd(w):
        w = w.reshape(depth, ndir, 2, hh, hd, hd)
        return jnp.einsum("ldfhij,hg->ldfhigj", w, eye).reshape(depth, ndir, 2, hh * hd, hh * hd)

    return jnp.concatenate([bd(w_r), bd(w_i)], axis=-1).astype(BF16)


def kernel(x, c, ctx, c_ctx, w_mod, b_mod, g_pre, g_post, w_in, conv_a_w, conv_a_b, w_rgate,
           b_rgate, w_igate, b_igate, lru_lambda, dw_w, dw_b, ln_g, ln_b, w_out):
    bsz, seq, d_model = x.shape
    depth = w_mod.shape[0]
    width = conv_a_b.shape[-1]
    assert bsz + 1 <= ROWS and seq % TILE_PROJ == 0 and seq % TILE_MIX == 0
    assert TILE_MIX % GRID_W == 0 and TILE_PROJ >= CONV_PAD * GRID_W
    assert ctx.shape[1] % CHUNK == 0 and width % MXU_N == 0 and d_model % MXU_N == 0

    act = jnp.concatenate(
        [c, c_ctx[None, :], jnp.zeros((ROWS - bsz - 1, d_model), F32)], axis=0)
    mods = _mods_call(act, w_mod, b_mod)

    w_in_bf = w_in.astype(BF16)
    w_out_bf = w_out.astype(BF16)
    wg = _pack_gate_weights(w_rgate, w_igate)

    h0 = _ctx_call(ctx, mods[:, bsz:bsz + 1, :], g_pre, g_post, w_in_bf, w_out_bf, wg, conv_a_w,
                   conv_a_b, b_rgate, b_igate, lru_lambda, dw_w, dw_b, ln_g, ln_b)

    def layer_inputs(l):
        mod_l = mods[l].reshape(ROWS, 1, 3 * d_model)
        return mod_l, g_pre[l], w_in_bf[l, :, 2 * width:4 * width]

    hx, v1, v2p = _proj_call(x, *layer_inputs(0))
    for l in range(depth):
        mod_l = mods[l].reshape(ROWS, 1, 3 * d_model)
        w_a = jnp.concatenate([w_in_bf[l, :, :2 * width], w_in_bf[l, :, 4 * width:]], axis=1)
        p, q, mb, ab, hbo = _mix_call(
            hx, v1, v2p, w_a, wg[l], conv_a_w[l], conv_a_b[l], b_rgate[l], b_igate[l],
            lru_lambda[l], dw_w[l], dw_b[l:l + 1], ln_g[l:l + 1], ln_b[l:l + 1],
            h0[:, 2 * l:2 * l + 1, :])
        nxt = layer_inputs(l + 1) if l + 1 < depth else None
        res = _out_call(p, q, mb, x, w_out_bf[l], g_post[l], mod_l, ab, hbo,
                        h0[:, 2 * l + 1:2 * l + 2, :], nxt)
        if nxt is not None:
            x, hx, v1, v2p = res
        else:
            x = res
    return x
```

```python
import functools

import jax
import jax.numpy as jnp
from jax import lax
from jax.experimental import pallas as pl
from jax.experimental.pallas import tpu as pltpu

F32 = jnp.float32
BF16 = jnp.bfloat16

EPS = 1e-6
LRU_C = 8.0
LRU_CONV = 4
CONV_K = 31
CONV_PAD = CONV_K // 2
GRID_W = 64
HALO = 16
ROWS = 8
CHUNK = 64
CONV_ROWS = 32
SEG_PAD = 16
MXU_N = 256
SPARE_ROWS = 16
TAP_GROUP = 8
ROWS_PER_CHAIN = 256

TILE_PROJ = 1024
TILE_MIX = 512
TILE_OUT = 1024
VMEM_LIMIT = 56 * 1024 * 1024


def _silu(x):
    return x * jax.nn.sigmoid(x)


def _softplus(x):
    return jnp.maximum(x, 0.0) + jnp.log1p(jnp.exp(-jnp.abs(x)))


def _rms_norm(x, g):
    ms = jnp.mean(x * x, axis=-1, keepdims=True)
    return x * lax.rsqrt(ms + EPS) * g


def _layer_norm(x, g, b):
    mu = jnp.mean(x, axis=-1, keepdims=True)
    xc = x - mu
    var = jnp.mean(xc * xc, axis=-1, keepdims=True)
    return xc * lax.rsqrt(var + EPS) * g + b


def _dot(a, b):
    return jnp.dot(a, b, preferred_element_type=F32)


def _interleave(mxu_pieces, vpu_pieces):
    n = len(mxu_pieces)
    per = -(-len(vpu_pieces) // n) if n else 0
    for j, piece in enumerate(mxu_pieces):
        piece()
        for fill in vpu_pieces[j * per:(j + 1) * per]:
            fill()
    for fill in vpu_pieces[n * per:]:
        fill()


def _mark(spare_ref, token):
    if token is None:
        return
    reps = spare_ref.shape[0] // ROWS
    spare_ref[...] = jnp.concatenate([token] * reps, axis=0).astype(spare_ref.dtype)


def _chain(mxu_pieces, vpu_pieces, after):
    tokens = []
    for piece, dep in zip(mxu_pieces, after):
        while dep is not None and len(tokens) <= dep:
            tokens.append(vpu_pieces[len(tokens)]())
        piece(None if dep is None else tokens[dep])
    for fill in vpu_pieces[len(tokens):]:
        fill()


def _perm_matrices(n):
    i = jnp.arange(n)
    src = (n // ROWS) * (i % ROWS) + i // ROWS
    pm = (src[:, None] == i[None, :]).astype(BF16)
    return pm, pm.T


UA_ROW0 = (LRU_CONV - 1) * ROWS


def _short_convs(ua_s, a_prev, a_next, cw, cb, vc_s, n, width):
    sub = n // ROWS
    ext = LRU_CONV - 1
    row = lax.broadcasted_iota(jnp.int32, (ROWS, width), 0)
    cw = 0.5 * cw
    cb = 0.5 * cb

    def blk(p):
        return slice(p * ROWS, (p + 1) * ROWS)

    for k in range(1, ext + 1):
        tail = pltpu.roll(ua_s[blk(ext + sub - k), :width], 1, 0)
        ua_s[blk(ext - k), :width] = jnp.where(row == 0, a_prev[ext - k:ext - k + 1, :], tail)
    for k in range(ext):
        head = pltpu.roll(ua_s[blk(ext + k), :width], ROWS - 1, 0)
        ua_s[blk(ext + sub + k), :width] = jnp.where(row == ROWS - 1, a_next[k:k + 1, :], head)

    def chunk(ci, carry):
        base = ci * CHUNK
        for d in range(2):
            first = 0 if d == 0 else ext
            acc = jnp.broadcast_to(cb[d:d + 1, :], (CHUNK, width))
            for k in range(LRU_CONV):
                start = pl.multiple_of(base + (first + k) * ROWS, ROWS)
                acc = acc + cw[d, k:k + 1, :] * ua_s[pl.ds(start, CHUNK), :width]
            vc_s[d, pl.ds(pl.multiple_of(base, CHUNK), CHUNK), :] = acc
        return carry

    lax.fori_loop(0, n // CHUNK, chunk, 0)


def _gate_pieces(vc_s, wg_ref, g_s, width):
    half = width // 2

    def piece(d, hf):
        def run():
            res = _dot(vc_s[d, :, hf * half:(hf + 1) * half].astype(BF16), wg_ref[d, hf])
            g_s[d, :, hf * half:(hf + 1) * half] = res[:, :half]
            g_s[d, :, width + hf * half:width + (hf + 1) * half] = res[:, half:]
        return run

    return [piece(d, hf) for d in range(2) for hf in range(2)]


def _sublane_scan(a, h, row, reverse):
    for s in (1, 2, 4):
        if reverse:
            m, shift = row < ROWS - s, ROWS - s
        else:
            m, shift = row >= s, s
        a_sh = jnp.where(m, pltpu.roll(a, shift, 0), 1.0)
        h_sh = jnp.where(m, pltpu.roll(h, shift, 0), 0.0)
        h = h + a * h_sh
        a = a * a_sh
    return a, h


def _rglru_scan(ua_s, br, bi, lam, c0, vc_s, g_s, hl_s, al_s, po_s, q_s, n, width):
    sub = n // ROWS
    row = lax.broadcasted_iota(jnp.int32, (ROWS, width), 0)
    def rows_of(v, d):
        return jnp.broadcast_to(v[d:d + 1, :], (ROWS, width))

    spl_half = (-0.5 * LRU_C) * _softplus(-lam)
    spl_half = [rows_of(spl_half, d) for d in range(2)]
    br_half = [rows_of(0.5 * br, d) for d in range(2)]
    bi_half = [rows_of(0.5 * bi, d) for d in range(2)]

    def coeffs(d, r0):
        g = g_s[d, pl.ds(r0, ROWS), :]
        vh = vc_s[d, pl.ds(r0, ROWS), :]
        t_r = jnp.tanh(g[:, :width] + br_half[d])
        t_i = jnp.tanh(g[:, width:] + bi_half[d])
        la = spl_half[d] + spl_half[d] * t_r
        a = jnp.exp(la)
        x = jnp.tanh(la) * (-1.0 - a * a)
        mult = jnp.where(x > 0.0, x * lax.rsqrt(x), 0.0)
        return a, mult * (vh + vh * t_i)

    def local(j, carry):
        hf, af, hb, ab = carry
        r0 = pl.multiple_of(j * ROWS, ROWS)
        a, b = coeffs(0, r0)
        hf = a * hf + b
        af = a * af
        hl_s[0, pl.ds(r0, ROWS), :] = hf
        al_s[0, pl.ds(r0, ROWS), :] = af
        r1 = pl.multiple_of((sub - 1 - j) * ROWS, ROWS)
        a, b = coeffs(1, r1)
        hb = a * hb + b
        ab = a * ab
        hl_s[1, pl.ds(r1, ROWS), :] = hb
        al_s[1, pl.ds(r1, ROWS), :] = ab
        return hf, af, hb, ab

    zero = jnp.zeros((ROWS, width), F32)
    one = jnp.ones((ROWS, width), F32)
    hf, af, hb, ab = lax.fori_loop(0, sub, local, (zero, one, zero, one), unroll=4)

    af, hf = _sublane_scan(af, hf, row, reverse=False)
    end_f = hf + af * c0
    c_f = jnp.where(row == 0, c0, pltpu.roll(end_f, 1, 0))
    ab, hb = _sublane_scan(ab, hb, row, reverse=True)
    c_b = jnp.where(row == ROWS - 1, 0.0, pltpu.roll(hb, ROWS - 1, 0))
    c_a = jnp.where(row == ROWS - 1, 1.0, pltpu.roll(ab, ROWS - 1, 0))

    if po_s is not None:
        def fix(j, carry):
            r0 = pl.ds(pl.multiple_of(j * ROWS, ROWS), ROWS)
            a_b = al_s[1, r0, :]
            h = (hl_s[0, r0, :] + al_s[0, r0, :] * c_f) + (hl_s[1, r0, :] + a_b * c_b)
            sg = _silu(ua_s[pl.ds(pl.multiple_of(UA_ROW0 + j * ROWS, ROWS), ROWS), width:])
            po_s[r0, :] = h * sg
            if q_s is not None:
                q_s[r0, :] = (a_b * c_a) * sg
            return carry

        lax.fori_loop(0, sub, fix, 0, unroll=8)

    return end_f[ROWS - 1:ROWS, :], hb[0:1, :], ab[0:1, :]


def _shifted_copy_pieces(vpad_ref, sh_ref):
    n = sh_ref.shape[1]

    def piece(j):
        def run():
            sh_ref[j] = vpad_ref[pl.ds(j, n), :]
        return run

    return [piece(j) for j in range(ROWS)]


def _time_conv_pieces(sh_ref, w_ref, ncol, bases, out_ref):
    def piece(ci, base):
        def run():
            acc = jnp.zeros((CHUNK, ncol), F32)
            for k in range(CONV_K):
                off = SEG_PAD - CONV_PAD + k
                start = base + (off // ROWS) * ROWS
                acc = acc + w_ref[k:k + 1, 0:ncol] * sh_ref[off % ROWS, start:start + CHUNK, :]
            out_ref[ci * CHUNK:(ci + 1) * CHUNK, 0:ncol] = acc
        return run

    return [piece(ci, base) for ci, base in enumerate(bases)]


def _mods_body(act_ref, w_ref, b_ref, o_ref):
    a = _silu(act_ref[...])
    o_ref[0] = jnp.dot(a, w_ref[0], precision=lax.Precision.HIGHEST,
                       preferred_element_type=F32) + b_ref[0]


def _mods_call(act, w_mod, b_mod):
    depth, d_model, d3 = w_mod.shape
    ncol = d_model
    return pl.pallas_call(
        _mods_body,
        grid=(depth, d3 // ncol),
        in_specs=[
            pl.BlockSpec((ROWS, d_model), lambda l, n: (0, 0)),
            pl.BlockSpec((1, d_model, ncol), lambda l, n: (l, 0, n)),
            pl.BlockSpec((1, 1, ncol), lambda l, n: (l, 0, n)),
        ],
        out_specs=pl.BlockSpec((1, ROWS, ncol), lambda l, n: (l, 0, n)),
        out_shape=jax.ShapeDtypeStruct((depth, ROWS, d3), F32),
        compiler_params=pltpu.CompilerParams(
            dimension_semantics=("arbitrary", "arbitrary"), vmem_limit_bytes=VMEM_LIMIT),
        name="mods",
    )(act, w_mod, b_mod.reshape(depth, 1, d3))


def _modulate(x, g, m, d_model):
    return _rms_norm(x, g) * (1.0 + m[:, d_model:2 * d_model]) + m[:, :d_model]


def _ctx_body(ctx_ref, mod_ref, gpre_ref, gpost_ref, win_ref, wo_ref, wg_ref, cw_ref, cb_ref,
              br_ref, bi_ref, lam_ref, dww_ref, dwb_ref, lng_ref, lnb_ref, pm_ref, pmt_ref,
              h0_ref,
              ua_s, vc_s, g_s, hl_s, al_s, po_s, vpad_s, sh_s, y_s,
              *, depth, d_model, width, n):
    xc = ctx_ref[0]
    zeros_pad = jnp.zeros((SEG_PAD, width), F32)
    no_rows = jnp.zeros((LRU_CONV - 1, width), F32)
    zero_state = jnp.zeros((1, width), F32)
    for l in range(depth):
        update = l < depth - 1
        m = mod_ref[l]
        hc = _modulate(xc, gpre_ref[l:l + 1, :], m, d_model).astype(BF16)
        hcp = _dot(pm_ref[...], hc).astype(BF16)
        if update:
            ua_s[UA_ROW0:UA_ROW0 + n, :] = _dot(hcp, win_ref[l, :, :2 * width])
            ub = _dot(hc, win_ref[l, :, 2 * width:])
        else:
            ua_s[UA_ROW0:UA_ROW0 + n, :width] = _dot(hcp, win_ref[l, :, :width])
        _short_convs(ua_s, no_rows, no_rows, cw_ref[l], cb_ref[l], vc_s, n, width)
        _interleave(_gate_pieces(vc_s, wg_ref.at[l], g_s, width), [])
        hf_last, hb_first, _ = _rglru_scan(
            ua_s, br_ref[l], bi_ref[l], lam_ref[l], zero_state, vc_s, g_s, hl_s, al_s,
            po_s if update else None, None, n, width)
        h0_ref[0, 2 * l:2 * l + 1, :] = hf_last
        h0_ref[0, 2 * l + 1:2 * l + 2, :] = hb_first
        if update:
            v = ub[:, :width] * jax.nn.sigmoid(ub[:, width:2 * width])
            vpad_s[0:SEG_PAD, :] = zeros_pad
            vpad_s[SEG_PAD:SEG_PAD + n, :] = v
            vpad_s[SEG_PAD + n:, :] = zeros_pad
            _interleave([], _shifted_copy_pieces(vpad_s, sh_s))
            _interleave([], _time_conv_pieces(
                sh_s, dww_ref.at[l], width, [ci * CHUNK for ci in range(n // CHUNK)], y_s))
            y = y_s[...] + dwb_ref[l:l + 1, :]
            conv = _silu(_layer_norm(y, lng_ref[l:l + 1, :], lnb_ref[l:l + 1, :]))
            mix_a = _dot(pmt_ref[...], po_s[...].astype(BF16)).astype(BF16)
            mix_b = (conv * _silu(ub[:, 2 * width:])).astype(BF16)
            mix = _dot(mix_a, wo_ref[l, :width, :]) + _dot(mix_b, wo_ref[l, width:, :])
            xc = xc + m[:, 2 * d_model:] * _rms_norm(mix, gpost_ref[l:l + 1, :])


def _ctx_call(ctx, mod_c, g_pre, g_post, w_in_bf, w_out_bf, wg, conv_a_w, conv_a_b, b_rgate,
              b_igate, lru_lambda, dw_w, dw_b, ln_g, ln_b):
    bsz, n, d_model = ctx.shape
    depth = w_in_bf.shape[0]
    width = conv_a_b.shape[-1]
    npad = n + 2 * SEG_PAD
    pm, pmt = _perm_matrices(n)

    def full(a):
        nd = a.ndim
        return pl.BlockSpec(a.shape, lambda b, _nd=nd: (0,) * _nd)

    params = (mod_c, g_pre, g_post, w_in_bf, w_out_bf, wg, conv_a_w, conv_a_b, b_rgate, b_igate,
              lru_lambda, dw_w, dw_b, ln_g, ln_b, pm, pmt)
    return pl.pallas_call(
        functools.partial(_ctx_body, depth=depth, d_model=d_model, width=width, n=n),
        grid=(bsz,),
        in_specs=[pl.BlockSpec((1, n, d_model), lambda b: (b, 0, 0))] + [full(a) for a in params],
        out_specs=pl.BlockSpec((1, 2 * depth, width), lambda b: (b, 0, 0)),
        out_shape=jax.ShapeDtypeStruct((bsz, 2 * depth, width), F32),
        scratch_shapes=[
            pltpu.VMEM((n + 2 * UA_ROW0, 2 * width), F32),
            pltpu.VMEM((2, n, width), F32),
            pltpu.VMEM((2, n, 2 * width), F32),
            pltpu.VMEM((2, n, width), F32),
            pltpu.VMEM((2, n, width), F32),
            pltpu.VMEM((n, width), F32),
            pltpu.VMEM((npad, width), F32),
            pltpu.VMEM((ROWS, npad - ROWS, width), F32),
            pltpu.VMEM((n, width), F32),
        ],
        compiler_params=pltpu.CompilerParams(
            dimension_semantics=("arbitrary",), vmem_limit_bytes=VMEM_LIMIT),
        name="context",
    )(ctx, *params)


def _proj_body(x_ref, mod_ref, g_ref, w_ref, hx_ref, v1_ref, v2_ref, *, nt, d_model, width):
    j = pl.program_id(1)
    is_pad = jnp.logical_or(j == 0, j == nt + 1)

    @pl.when(is_pad)
    def _():
        v2_ref[...] = jnp.zeros(v2_ref.shape, v2_ref.dtype)

    @pl.when(jnp.logical_not(is_pad))
    def _():
        half = width // 2
        rows = ROWS_PER_CHAIN
        for c in range(x_ref.shape[1] // rows):
            rs = slice(c * rows, (c + 1) * rows)
            hb = _modulate(x_ref[0, rs, :], g_ref[...], mod_ref[0], d_model).astype(BF16)
            hx_ref[0, rs, :] = hb
            u = _dot(hb, w_ref[...])
            v = u[:, :width] * jax.nn.sigmoid(u[:, width:])
            v1_ref[0, rs, :] = v[:, :half].astype(BF16)
            v2_ref[0, rs, :] = v[:, half:].astype(BF16)


def _proj_call(x, mod_l, g_pre_l, w_b):
    bsz, seq, d_model = x.shape
    width = w_b.shape[1] // 2
    half = width // 2
    tt = TILE_PROJ
    nt = seq // tt

    def tok(b, j):
        return (b, jnp.clip(j - 1, 0, nt - 1), 0)

    return pl.pallas_call(
        functools.partial(_proj_body, nt=nt, d_model=d_model, width=width),
        grid=(bsz, nt + 2),
        in_specs=[
            pl.BlockSpec((1, tt, d_model), tok),
            pl.BlockSpec((1, 1, 3 * d_model), lambda b, j: (b, 0, 0)),
            pl.BlockSpec((1, d_model), lambda b, j: (0, 0)),
            pl.BlockSpec(w_b.shape, lambda b, j: (0, 0)),
        ],
        out_specs=[
            pl.BlockSpec((1, tt, d_model), tok),
            pl.BlockSpec((1, tt, half), tok),
            pl.BlockSpec((1, tt, half), lambda b, j: (b, j, 0)),
        ],
        out_shape=[
            jax.ShapeDtypeStruct((bsz, seq, d_model), BF16),
            jax.ShapeDtypeStruct((bsz, seq, half), BF16),
            jax.ShapeDtypeStruct((bsz, seq + 2 * tt, half), BF16),
        ],
        compiler_params=pltpu.CompilerParams(
            dimension_semantics=("arbitrary", "arbitrary"), vmem_limit_bytes=VMEM_LIMIT),
        name="proj",
    )(x, mod_l, g_pre_l.reshape(1, d_model), w_b)


def _mix_body(hx_ref, hxp_ref, hxn_ref, v1_ref, v2_ref, wa_ref, wg_ref, cw_ref, cb_ref, br_ref,
              bi_ref, lam_ref, dww_ref, dww16_ref, dwb_ref, lng_ref, lnb_ref, pm_ref, h0f_ref,
              zero_ref,
              p_ref, q_ref, mb_ref, ab_ref, hbo_ref,
              hxs_s, ua_s, bg_s, vc_s, g_s, hl_s, al_s, po_s, q_s, carry_s, vpad_s, sh_s, wb_s,
              y_s,
              *, tt, nt, d_model, width, v2_front):
    i = pl.program_id(1)
    half = width // 2
    nseg = tt // GRID_W
    seg_rows = GRID_W + 2 * SEG_PAD
    ext = LRU_CONV - 1
    reach = CONV_PAD * GRID_W
    row_zero = zero_ref[0]

    @pl.when(i == 0)
    def _():
        carry_s[...] = h0f_ref[0]

    def col_slices(total):
        return [slice(c, c + MXU_N) for c in range(0, total, MXU_N)]


    def pad_segment(r):
        def run():
            base = r * seg_rows
            zpad = jnp.zeros((SEG_PAD, half), F32)
            vpad_s[base:base + SEG_PAD, :] = zpad
            vpad_s[base + SEG_PAD:base + SEG_PAD + GRID_W, :] = (
                v1_ref[0, r * GRID_W:(r + 1) * GRID_W, :].astype(F32))
            vpad_s[base + SEG_PAD + GRID_W:base + seg_rows, :] = zpad
        return run

    def conv_chunk(src_fn, w_cols, ci, out_cols):
        def run():
            token = None
            for part in range(CHUNK // CONV_ROWS):
                acc = jnp.zeros((CONV_ROWS, half), F32)
                for k in range(CONV_K):
                    acc = acc + dww_ref[k:k + 1, w_cols] * src_fn(ci, k, part * CONV_ROWS)
                row0 = ci * CHUNK + part * CONV_ROWS
                y_s[row0:row0 + CONV_ROWS, out_cols] = acc
                token = acc[0:ROWS, 0:128]
            return token
        return run

    def shifted(ci, k, r):
        off = SEG_PAD - CONV_PAD + k
        start = ci * seg_rows + (off // ROWS) * ROWS + r
        return sh_s[off % ROWS, start:start + CONV_ROWS, :]

    def spread_taps():
        for k in range(CONV_K):
            wb_s[k] = jnp.broadcast_to(dww16_ref[k:k + 1, half:], (2 * ROWS, half))

    def col_chunk(ci):
        def run():
            win0 = i * tt + (v2_front - reach) + ci * CHUNK
            token = None
            for rt in range(CHUNK // (2 * ROWS)):
                acc = None
                for g0 in range(0, CONV_K, TAP_GROUP):
                    part = None
                    for k in range(g0, min(g0 + TAP_GROUP, CONV_K)):
                        start = pl.multiple_of(win0 + k * GRID_W + rt * 2 * ROWS, 2 * ROWS)
                        term = wb_s[k] * v2_ref[0, pl.ds(start, 2 * ROWS), :]
                        part = term if part is None else part + term
                    part = part.astype(F32)
                    acc = part if acc is None else acc + part
                row0 = ci * CHUNK + rt * 2 * ROWS
                y_s[row0:row0 + 2 * ROWS, half:] = acc
                token = acc[0:ROWS, 0:128]
            return token
        return run

    def permute(cs):
        def run(token):
            hxs_s[0:tt, cs] = _dot(pm_ref[...], hx_ref[0, :, cs]).astype(BF16)
        return run

    def project_a(cs):
        def run(token):
            _mark(hxs_s.at[tt:tt + SPARE_ROWS, 0:128], token)
            lhs = hxs_s[pl.ds(pl.multiple_of(row_zero, SPARE_ROWS), tt), :]
            ua_s[UA_ROW0:UA_ROW0 + tt, cs] = _dot(lhs, wa_ref[:, cs])
        return run

    def project_b(cs):
        def run(token):
            cols = slice(2 * width + cs.start, 2 * width + cs.stop)
            bg_s[:, cs] = _dot(hx_ref[0], wa_ref[:, cols])
        return run

    prep = ([pad_segment(r) for r in range(nseg)] + _shifted_copy_pieces(vpad_s, sh_s)
            + [spread_taps])
    rows_first = len(prep)
    cols_first = rows_first + nseg
    vpu = (prep
           + [conv_chunk(shifted, slice(0, half), ci, slice(0, half)) for ci in range(nseg)]
           + [col_chunk(ci) for ci in range(tt // CHUNK)])
    mxu = ([permute(cs) for cs in col_slices(d_model)]
           + [project_a(cs) for cs in col_slices(2 * width)]
           + [project_b(cs) for cs in col_slices(width)])
    after = [None, None, rows_first, rows_first + 1,
             rows_first + 2, rows_first + 4, rows_first + 5, rows_first + 7,
             cols_first + 1, cols_first + 2]
    _chain(mxu, vpu, after)

    wa_a = wa_ref[:, :width]
    a_prev = jnp.where(i > 0, _dot(hxp_ref[0], wa_a)[HALO - ext:, :], 0.0)
    a_next = jnp.where(i < nt - 1, _dot(hxn_ref[0], wa_a)[:ext, :], 0.0)
    _short_convs(ua_s, a_prev, a_next, cw_ref[...], cb_ref[...], vc_s, tt, width)

    def finish_conv(c, nrows):
        def run():
            rs = slice(c * nrows, (c + 1) * nrows)
            y = y_s[rs, :] + dwb_ref[...]
            conv = _silu(_layer_norm(y, lng_ref[...], lnb_ref[...]))
            out = conv * _silu(bg_s[rs, :])
            mb_ref[0, rs, :] = out.astype(BF16)
            return out[0:ROWS, 0:128]
        return run

    def gate_piece(d, hf):
        def run(token):
            lo = hf * half
            _mark(vc_s.at[d, tt:tt + SPARE_ROWS, lo:lo + 128], token)
            lhs = vc_s[d, pl.ds(pl.multiple_of(row_zero, SPARE_ROWS), tt), lo:lo + half]
            res = _dot(lhs.astype(BF16), wg_ref[d, hf])
            g_s[d, :, lo:lo + half] = res[:, :half]
            g_s[d, :, width + lo:width + lo + half] = res[:, half:]
        return run

    _chain([gate_piece(d, hf) for d in range(2) for hf in range(2)],
           [finish_conv(c, tt // 4) for c in range(4)], [None, 0, 1, 2])

    hf_last, hb_first, a_total = _rglru_scan(
        ua_s, br_ref[...], bi_ref[...], lam_ref[...], carry_s[...], vc_s, g_s, hl_s, al_s,
        po_s, q_s, tt, width)
    carry_s[...] = hf_last
    ab_ref[0, 0] = a_total
    hbo_ref[0, 0] = hb_first
    p_ref[0] = po_s[...].astype(BF16)
    q_ref[0] = q_s[...].astype(BF16)


def _mix_call(hx, v1, v2p, w_a, wg_l, cw, cb, br, bi, lam, dww, dwb, lng, lnb, h0f):
    bsz, seq, d_model = hx.shape
    width = cb.shape[-1]
    half = width // 2
    tt = TILE_MIX
    nt = seq // tt
    hb_per_tile = tt // HALO
    n_halo_blocks = seq // HALO
    v2_front = (v2p.shape[1] - seq) // 2
    nseg = tt // GRID_W
    seg_rows = GRID_W + 2 * SEG_PAD
    reach = CONV_PAD * GRID_W
    pm, _ = _perm_matrices(tt)

    def full(a):
        nd = a.ndim
        return pl.BlockSpec(a.shape, lambda b, i, _nd=nd: (0,) * _nd)

    small = (w_a, wg_l, cw, cb, br, bi, lam, dww, dww.astype(BF16), dwb, lng, lnb, pm)
    tile = lambda b, i: (b, i, 0)
    out_shapes = [jax.ShapeDtypeStruct((bsz, seq, width), BF16)] * 3 + [
        jax.ShapeDtypeStruct((bsz, nt, 1, width), F32)] * 2
    return pl.pallas_call(
        functools.partial(_mix_body, tt=tt, nt=nt, d_model=d_model, width=width,
                          v2_front=v2_front),
        grid=(bsz, nt),
        in_specs=[
            pl.BlockSpec((1, tt, d_model), tile),
            pl.BlockSpec((1, HALO, d_model),
                         lambda b, i: (b, jnp.maximum(i * hb_per_tile - 1, 0), 0)),
            pl.BlockSpec((1, HALO, d_model),
                         lambda b, i: (b, jnp.minimum((i + 1) * hb_per_tile, n_halo_blocks - 1), 0)),
            pl.BlockSpec((1, tt, half), tile),
            pl.BlockSpec((1, v2p.shape[1], half), lambda b, i: (b, 0, 0)),
        ] + [full(a) for a in small] + [
            pl.BlockSpec((1, 1, width), lambda b, i: (b, 0, 0)),
            pl.BlockSpec(memory_space=pltpu.SMEM),
        ],
        out_specs=[pl.BlockSpec((1, tt, width), tile)] * 3 + [
            pl.BlockSpec((1, 1, 1, width), lambda b, i: (b, i, 0, 0))] * 2,
        out_shape=out_shapes,
        scratch_shapes=[
            pltpu.VMEM((tt + SPARE_ROWS, d_model), BF16),
            pltpu.VMEM((tt + 2 * UA_ROW0, 2 * width), F32),
            pltpu.VMEM((tt, width), F32),
            pltpu.VMEM((2, tt + SPARE_ROWS, width), F32),
            pltpu.VMEM((2, tt, 2 * width), F32),
            pltpu.VMEM((2, tt, width), F32),
            pltpu.VMEM((2, tt, width), F32),
            pltpu.VMEM((tt, width), F32),
            pltpu.VMEM((tt, width), F32),
            pltpu.VMEM((1, width), F32),
            pltpu.VMEM((nseg * seg_rows, half), F32),
            pltpu.VMEM((ROWS, nseg * seg_rows - ROWS, half), F32),
            pltpu.VMEM((CONV_K, 2 * ROWS, half), BF16),
            pltpu.VMEM((tt, width), F32),
        ],
        compiler_params=pltpu.CompilerParams(
            dimension_semantics=("arbitrary", "arbitrary"), vmem_limit_bytes=VMEM_LIMIT,
            ),
        name="mixer",
    )(hx, hx, hx, v1, v2p, *small, h0f, jnp.zeros((1,), jnp.int32))


def _out_body(*refs, d_model, width, fuse_next):
    (p_ref, q_ref, mb_ref, x_ref, wo_ref, gpost_ref, mod_ref, ab_ref, hb_ref, h0b_ref,
     pmt_ref) = refs[:11]
    if fuse_next:
        modn_ref, gpren_ref, wbn_ref, _, o_ref, hx_ref, v1_ref, v2_ref, carry_s = refs[11:]
    else:
        o_ref, carry_s = refs[11:]

    @pl.when(pl.program_id(1) == 0)
    def _():
        carry_s[...] = h0b_ref[0]

    gate = mod_ref[0][:, 2 * d_model:]
    rows = ROWS_PER_CHAIN
    half = width // 2
    tm = pmt_ref.shape[0]
    c = carry_s[...]
    for st in reversed(range(p_ref.shape[1] // tm)):
        t0 = st * tm
        mix_a = (p_ref[0, t0:t0 + tm, :].astype(F32)
                 + q_ref[0, t0:t0 + tm, :].astype(F32) * c).astype(BF16)
        mix_t = _dot(pmt_ref[...], mix_a).astype(BF16)
        c = hb_ref[0, st] + ab_ref[0, st] * c
        for ch in range(tm // rows):
            rs = slice(t0 + ch * rows, t0 + (ch + 1) * rows)
            mix = (_dot(mix_t[ch * rows:(ch + 1) * rows, :], wo_ref[:width, :])
                   + _dot(mb_ref[0, rs, :], wo_ref[width:, :]))
            x_new = x_ref[0, rs, :] + gate * _rms_norm(mix, gpost_ref[...])
            o_ref[0, rs, :] = x_new
            if fuse_next:
                hb = _modulate(x_new, gpren_ref[...], modn_ref[0], d_model).astype(BF16)
                hx_ref[0, rs, :] = hb
                u = _dot(hb, wbn_ref[...])
                v = u[:, :width] * jax.nn.sigmoid(u[:, width:])
                v1_ref[0, rs, :] = v[:, :half].astype(BF16)
                v2_ref[0, rs, :] = v[:, half:].astype(BF16)
    carry_s[...] = c


def _out_call(p, q, mb, x, w_o, g_post_l, mod_l, ab, hbo, h0b, nxt=None):
    bsz, seq, d_model = x.shape
    width = p.shape[-1]
    half = width // 2
    tt = TILE_OUT
    tm = TILE_MIX
    nt = seq // tt
    per = tt // tm
    pad_blocks = TILE_PROJ // tt
    rev = lambda b, j: (b, nt - 1 - j, 0)
    _, pmt = _perm_matrices(tm)
    operands = [p, q, mb, x, w_o, g_post_l.reshape(1, d_model), mod_l, ab, hbo, h0b, pmt]
    in_specs = [
        pl.BlockSpec((1, tt, width), rev),
        pl.BlockSpec((1, tt, width), rev),
        pl.BlockSpec((1, tt, width), rev),
        pl.BlockSpec((1, tt, d_model), rev),
        pl.BlockSpec(w_o.shape, lambda b, j: (0, 0)),
        pl.BlockSpec((1, d_model), lambda b, j: (0, 0)),
        pl.BlockSpec((1, 1, 3 * d_model), lambda b, j: (b, 0, 0)),
        pl.BlockSpec((1, per, 1, width), lambda b, j: (b, nt - 1 - j, 0, 0)),
        pl.BlockSpec((1, per, 1, width), lambda b, j: (b, nt - 1 - j, 0, 0)),
        pl.BlockSpec((1, 1, width), lambda b, j: (b, 0, 0)),
        pl.BlockSpec((tm, tm), lambda b, j: (0, 0)),
    ]
    out_specs = [pl.BlockSpec((1, tt, d_model), rev)]
    out_shape = [jax.ShapeDtypeStruct((bsz, seq, d_model), F32)]
    aliases = {}
    if nxt is not None:
        mod_n, g_pre_n, w_b_n = nxt
        v2_zeros = jnp.zeros((bsz, seq + 2 * TILE_PROJ, half), BF16)
        operands += [mod_n, g_pre_n.reshape(1, d_model), w_b_n, v2_zeros]
        in_specs += [
            pl.BlockSpec((1, 1, 3 * d_model), lambda b, j: (b, 0, 0)),
            pl.BlockSpec((1, d_model), lambda b, j: (0, 0)),
            pl.BlockSpec(w_b_n.shape, lambda b, j: (0, 0)),
            pl.BlockSpec(memory_space=pl.ANY),
        ]
        out_specs += [
            pl.BlockSpec((1, tt, d_model), rev),
            pl.BlockSpec((1, tt, half), rev),
            pl.BlockSpec((1, tt, half), lambda b, j: (b, nt - 1 - j + pad_blocks, 0)),
        ]
        out_shape += [
            jax.ShapeDtypeStruct((bsz, seq, d_model), BF16),
            jax.ShapeDtypeStruct((bsz, seq, half), BF16),
            jax.ShapeDtypeStruct(v2_zeros.shape, BF16),
        ]
        aliases = {len(operands) - 1: 3}
    res = pl.pallas_call(
        functools.partial(_out_body, d_model=d_model, width=width, fuse_next=nxt is not None),
        grid=(bsz, nt),
        in_specs=in_specs,
        out_specs=out_specs,
        out_shape=out_shape,
        input_output_aliases=aliases,
        scratch_shapes=[pltpu.VMEM((1, width), F32)],
        compiler_params=pltpu.CompilerParams(
            dimension_semantics=("arbitrary", "arbitrary"), vmem_limit_bytes=VMEM_LIMIT),
        name="out_proj" if nxt is not None else "out",
    )(*operands)
    return res if nxt is not None else res[0]


def _pack_gate_weights(w_r, w_i):
    depth, ndir, heads, hd, _ = w_r.shape
    hh = heads // 2
    eye = jnp.eye(hh, dtype=w_r.dtype)

    def bd(w):
        w = w.reshape(depth, ndir, 2, hh, hd, hd)
        return jnp.einsum("ldfhij,hg->ldfhigj", w, eye).reshape(depth, ndir, 2, hh * hd, hh * hd)

    return jnp.concatenate([bd(w_r), bd(w_i)], axis=-1).astype(BF16)


def kernel(x, c, ctx, c_ctx, w_mod, b_mod, g_pre, g_post, w_in, conv_a_w, conv_a_b, w_rgate,
           b_rgate, w_igate, b_igate, lru_lambda, dw_w, dw_b, ln_g, ln_b, w_out):
    bsz, seq, d_model = x.shape
    depth = w_mod.shape[0]
    width = conv_a_b.shape[-1]
    assert bsz + 1 <= ROWS and seq % TILE_PROJ == 0 and seq % TILE_MIX == 0
    assert TILE_MIX % GRID_W == 0 and TILE_PROJ >= CONV_PAD * GRID_W
    assert seq % TILE_OUT == 0 and TILE_OUT % TILE_MIX == 0 and TILE_PROJ % TILE_OUT == 0
    assert ctx.shape[1] % CHUNK == 0 and width % MXU_N == 0 and d_model % MXU_N == 0

    act = jnp.concatenate(
        [c, c_ctx[None, :], jnp.zeros((ROWS - bsz - 1, d_model), F32)], axis=0)
    mods = _mods_call(act, w_mod, b_mod)

    w_in_bf = w_in.astype(BF16)
    w_out_bf = w_out.astype(BF16)
    wg = _pack_gate_weights(w_rgate, w_igate)

    h0 = _ctx_call(ctx, mods[:, bsz:bsz + 1, :], g_pre, g_post, w_in_bf, w_out_bf, wg, conv_a_w,
                   conv_a_b, b_rgate, b_igate, lru_lambda, dw_w, dw_b, ln_g, ln_b)

    def layer_inputs(l):
        mod_l = mods[l].reshape(ROWS, 1, 3 * d_model)
        return mod_l, g_pre[l], w_in_bf[l, :, 2 * width:4 * width]

    hx, v1, v2p = _proj_call(x, *layer_inputs(0))
    for l in range(depth):
        mod_l = mods[l].reshape(ROWS, 1, 3 * d_model)
        w_a = jnp.concatenate([w_in_bf[l, :, :2 * width], w_in_bf[l, :, 4 * width:]], axis=1)
        p, q, mb, ab, hbo = _mix_call(
            hx, v1, v2p, w_a, wg[l], conv_a_w[l], conv_a_b[l], b_rgate[l], b_igate[l],
            lru_lambda[l], dw_w[l], dw_b[l:l + 1], ln_g[l:l + 1], ln_b[l:l + 1],
            h0[:, 2 * l:2 * l + 1, :])
        nxt = layer_inputs(l + 1) if l + 1 < depth else None
        res = _out_call(p, q, mb, x, w_out_bf[l], g_post[l], mod_l, ab, hbo,
                        h0[:, 2 * l + 1:2 * l + 2, :], nxt)
        if nxt is not None:
            x, hx, v1, v2p = res
        else:
            x = res
    return x
```

```python
import functools

import jax
import jax.numpy as jnp
import numpy as np
from jax import lax
from jax.experimental import pallas as pl
from jax.experimental.pallas import tpu as pltpu

F32 = jnp.float32
BF16 = jnp.bfloat16

EPS = 1e-6
LRU_C = 8.0
LRU_CONV = 4
CONV_K = 31
CONV_PAD = CONV_K // 2
GRID_W = 64
HALO = 16
ROWS = 8
CHUNK = 64
CONV_ROWS = 32
SEG_PAD = 16
MXU_N = 256
SPARE_ROWS = 16
TAP_GROUP = 8
ROWS_PER_CHAIN = 256

TILE_PROJ = 1024
TILE_MIX = 512
TILE_OUT = 1024
VMEM_LIMIT = 56 * 1024 * 1024


def _silu(x):
    return x * jax.nn.sigmoid(x)


def _softplus(x):
    return jnp.maximum(x, 0.0) + jnp.log1p(jnp.exp(-jnp.abs(x)))


def _rms_norm(x, g):
    ms = jnp.mean(x * x, axis=-1, keepdims=True)
    return x * lax.rsqrt(ms + EPS) * g


def _layer_norm(x, g, b):
    mu = jnp.mean(x, axis=-1, keepdims=True)
    xc = x - mu
    var = jnp.mean(xc * xc, axis=-1, keepdims=True)
    return xc * lax.rsqrt(var + EPS) * g + b


def _dot(a, b):
    return jnp.dot(a, b, preferred_element_type=F32)


def _interleave(mxu_pieces, vpu_pieces):
    n = len(mxu_pieces)
    per = -(-len(vpu_pieces) // n) if n else 0
    for j, piece in enumerate(mxu_pieces):
        piece()
        for fill in vpu_pieces[j * per:(j + 1) * per]:
            fill()
    for fill in vpu_pieces[n * per:]:
        fill()


def _mark(spare_ref, token):
    if token is None:
        return
    reps = spare_ref.shape[0] // ROWS
    spare_ref[...] = jnp.concatenate([token] * reps, axis=0).astype(spare_ref.dtype)


def _chain(mxu_pieces, vpu_pieces, after):
    tokens = []
    for piece, dep in zip(mxu_pieces, after):
        while dep is not None and len(tokens) <= dep:
            tokens.append(vpu_pieces[len(tokens)]())
        piece(None if dep is None else tokens[dep])
    for fill in vpu_pieces[len(tokens):]:
        fill()


def _perm_matrices(n):
    i = np.arange(n)
    src = (n // ROWS) * (i % ROWS) + i // ROWS
    pm = (src[:, None] == i[None, :]).astype(np.float32)
    return jnp.asarray(pm, BF16), jnp.asarray(pm.T, BF16)


UA_ROW0 = (LRU_CONV - 1) * ROWS


def _short_convs(ua_s, a_prev, a_next, cw, cb, vc_s, n, width):
    sub = n // ROWS
    ext = LRU_CONV - 1
    row = lax.broadcasted_iota(jnp.int32, (ROWS, width), 0)
    cw = 0.5 * cw
    cb = 0.5 * cb

    def blk(p):
        return slice(p * ROWS, (p + 1) * ROWS)

    for k in range(1, ext + 1):
        tail = pltpu.roll(ua_s[blk(ext + sub - k), :width], 1, 0)
        ua_s[blk(ext - k), :width] = jnp.where(row == 0, a_prev[ext - k:ext - k + 1, :], tail)
    for k in range(ext):
        head = pltpu.roll(ua_s[blk(ext + k), :width], ROWS - 1, 0)
        ua_s[blk(ext + sub + k), :width] = jnp.where(row == ROWS - 1, a_next[k:k + 1, :], head)

    def chunk(ci, carry):
        base = ci * CHUNK
        for d in range(2):
            first = 0 if d == 0 else ext
            acc = jnp.broadcast_to(cb[d:d + 1, :], (CHUNK, width))
            for k in range(LRU_CONV):
                start = pl.multiple_of(base + (first + k) * ROWS, ROWS)
                acc = acc + cw[d, k:k + 1, :] * ua_s[pl.ds(start, CHUNK), :width]
            vc_s[d, pl.ds(pl.multiple_of(base, CHUNK), CHUNK), :] = acc
        return carry

    lax.fori_loop(0, n // CHUNK, chunk, 0)


def _gate_pieces(vc_s, wg_ref, g_s, width):
    half = width // 2

    def piece(d, hf):
        def run():
            res = _dot(vc_s[d, :, hf * half:(hf + 1) * half].astype(BF16), wg_ref[d, hf])
            g_s[d, :, hf * half:(hf + 1) * half] = res[:, :half]
            g_s[d, :, width + hf * half:width + (hf + 1) * half] = res[:, half:]
        return run

    return [piece(d, hf) for d in range(2) for hf in range(2)]


def _sublane_scan(a, h, row, reverse):
    for s in (1, 2, 4):
        if reverse:
            m, shift = row < ROWS - s, ROWS - s
        else:
            m, shift = row >= s, s
        a_sh = jnp.where(m, pltpu.roll(a, shift, 0), 1.0)
        h_sh = jnp.where(m, pltpu.roll(h, shift, 0), 0.0)
        h = h + a * h_sh
        a = a * a_sh
    return a, h


def _rglru_scan(ua_s, br, bi, lam, c0, vc_s, g_s, hl_s, al_s, po_s, q_s, n, width):
    sub = n // ROWS
    row = lax.broadcasted_iota(jnp.int32, (ROWS, width), 0)
    def rows_of(v, d):
        return jnp.broadcast_to(v[d:d + 1, :], (ROWS, width))

    spl_half = (-0.5 * LRU_C) * _softplus(-lam)
    spl_half = [rows_of(spl_half, d) for d in range(2)]
    br_half = [rows_of(0.5 * br, d) for d in range(2)]
    bi_half = [rows_of(0.5 * bi, d) for d in range(2)]

    def coeffs(d, r0):
        g = g_s[d, pl.ds(r0, ROWS), :]
        vh = vc_s[d, pl.ds(r0, ROWS), :]
        t_r = jnp.tanh(g[:, :width] + br_half[d])
        t_i = jnp.tanh(g[:, width:] + bi_half[d])
        la = spl_half[d] + spl_half[d] * t_r
        a = jnp.exp(la)
        x = jnp.tanh(la) * (-1.0 - a * a)
        mult = jnp.where(x > 0.0, x * lax.rsqrt(x), 0.0)
        return a, mult * (vh + vh * t_i)

    def local(j, carry):
        hf, af, hb, ab = carry
        r0 = pl.multiple_of(j * ROWS, ROWS)
        a, b = coeffs(0, r0)
        hf = a * hf + b
        af = a * af
        hl_s[0, pl.ds(r0, ROWS), :] = hf
        al_s[0, pl.ds(r0, ROWS), :] = af
        r1 = pl.multiple_of((sub - 1 - j) * ROWS, ROWS)
        a, b = coeffs(1, r1)
        hb = a * hb + b
        ab = a * ab
        hl_s[1, pl.ds(r1, ROWS), :] = hb
        al_s[1, pl.ds(r1, ROWS), :] = ab
        return hf, af, hb, ab

    zero = jnp.zeros((ROWS, width), F32)
    one = jnp.ones((ROWS, width), F32)
    hf, af, hb, ab = lax.fori_loop(0, sub, local, (zero, one, zero, one), unroll=4)

    af, hf = _sublane_scan(af, hf, row, reverse=False)
    end_f = hf + af * c0
    c_f = jnp.where(row == 0, c0, pltpu.roll(end_f, 1, 0))
    ab, hb = _sublane_scan(ab, hb, row, reverse=True)
    c_b = jnp.where(row == ROWS - 1, 0.0, pltpu.roll(hb, ROWS - 1, 0))
    c_a = jnp.where(row == ROWS - 1, 1.0, pltpu.roll(ab, ROWS - 1, 0))

    if po_s is not None:
        def fix(j, carry):
            r0 = pl.ds(pl.multiple_of(j * ROWS, ROWS), ROWS)
            a_b = al_s[1, r0, :]
            h = (hl_s[0, r0, :] + al_s[0, r0, :] * c_f) + (hl_s[1, r0, :] + a_b * c_b)
            sg = _silu(ua_s[pl.ds(pl.multiple_of(UA_ROW0 + j * ROWS, ROWS), ROWS), width:])
            po_s[r0, :] = h * sg
            if q_s is not None:
                q_s[r0, :] = (a_b * c_a) * sg
            return carry

        lax.fori_loop(0, sub, fix, 0, unroll=8)

    return end_f[ROWS - 1:ROWS, :], hb[0:1, :], ab[0:1, :]


def _shifted_copy_pieces(vpad_ref, sh_ref):
    n = sh_ref.shape[1]

    def piece(j):
        def run():
            sh_ref[j] = vpad_ref[pl.ds(j, n), :]
        return run

    return [piece(j) for j in range(ROWS)]


def _time_conv_pieces(sh_ref, w_ref, ncol, bases, out_ref):
    def piece(ci, base):
        def run():
            acc = jnp.zeros((CHUNK, ncol), F32)
            for k in range(CONV_K):
                off = SEG_PAD - CONV_PAD + k
                start = base + (off // ROWS) * ROWS
                acc = acc + w_ref[k:k + 1, 0:ncol] * sh_ref[off % ROWS, start:start + CHUNK, :]
            out_ref[ci * CHUNK:(ci + 1) * CHUNK, 0:ncol] = acc
        return run

    return [piece(ci, base) for ci, base in enumerate(bases)]


def _mods_body(act_ref, w_ref, b_ref, o_ref):
    a = _silu(act_ref[...])
    o_ref[0] = jnp.dot(a, w_ref[0], precision=lax.Precision.HIGHEST,
                       preferred_element_type=F32) + b_ref[0]


def _mods_call(act, w_mod, b_mod):
    depth, d_model, d3 = w_mod.shape
    ncol = d_model
    return pl.pallas_call(
        _mods_body,
        grid=(depth, d3 // ncol),
        in_specs=[
            pl.BlockSpec((ROWS, d_model), lambda l, n: (0, 0)),
            pl.BlockSpec((1, d_model, ncol), lambda l, n: (l, 0, n)),
            pl.BlockSpec((1, 1, ncol), lambda l, n: (l, 0, n)),
        ],
        out_specs=pl.BlockSpec((1, ROWS, ncol), lambda l, n: (l, 0, n)),
        out_shape=jax.ShapeDtypeStruct((depth, ROWS, d3), F32),
        compiler_params=pltpu.CompilerParams(
            dimension_semantics=("arbitrary", "arbitrary"), vmem_limit_bytes=VMEM_LIMIT),
        name="mods",
    )(act, w_mod, b_mod.reshape(depth, 1, d3))


def _modulate(x, g, m, d_model):
    return _rms_norm(x, g) * (1.0 + m[:, d_model:2 * d_model]) + m[:, :d_model]


def _ctx_body(ctx_ref, mod_ref, gpre_ref, gpost_ref, win_ref, wo_ref, wg_ref, cw_ref, cb_ref,
              br_ref, bi_ref, lam_ref, dww_ref, dwb_ref, lng_ref, lnb_ref, pm_ref, pmt_ref,
              h0_ref,
              ua_s, vc_s, g_s, hl_s, al_s, po_s, vpad_s, sh_s, y_s,
              *, depth, d_model, width, n):
    xc = ctx_ref[0]
    zeros_pad = jnp.zeros((SEG_PAD, width), F32)
    no_rows = jnp.zeros((LRU_CONV - 1, width), F32)
    zero_state = jnp.zeros((1, width), F32)
    for l in range(depth):
        update = l < depth - 1
        m = mod_ref[l]
        hc = _modulate(xc, gpre_ref[l:l + 1, :], m, d_model).astype(BF16)
        hcp = _dot(pm_ref[...], hc).astype(BF16)
        if update:
            ua_s[UA_ROW0:UA_ROW0 + n, :] = _dot(hcp, win_ref[l, :, :2 * width])
            ub = _dot(hc, win_ref[l, :, 2 * width:])
        else:
            ua_s[UA_ROW0:UA_ROW0 + n, :width] = _dot(hcp, win_ref[l, :, :width])
        _short_convs(ua_s, no_rows, no_rows, cw_ref[l], cb_ref[l], vc_s, n, width)
        _interleave(_gate_pieces(vc_s, wg_ref.at[l], g_s, width), [])
        hf_last, hb_first, _ = _rglru_scan(
            ua_s, br_ref[l], bi_ref[l], lam_ref[l], zero_state, vc_s, g_s, hl_s, al_s,
            po_s if update else None, None, n, width)
        h0_ref[0, 2 * l:2 * l + 1, :] = hf_last
        h0_ref[0, 2 * l + 1:2 * l + 2, :] = hb_first
        if update:
            v = ub[:, :width] * jax.nn.sigmoid(ub[:, width:2 * width])
            vpad_s[0:SEG_PAD, :] = zeros_pad
            vpad_s[SEG_PAD:SEG_PAD + n, :] = v
            vpad_s[SEG_PAD + n:, :] = zeros_pad
            _interleave([], _shifted_copy_pieces(vpad_s, sh_s))
            _interleave([], _time_conv_pieces(
                sh_s, dww_ref.at[l], width, [ci * CHUNK for ci in range(n // CHUNK)], y_s))
            y = y_s[...] + dwb_ref[l:l + 1, :]
            conv = _silu(_layer_norm(y, lng_ref[l:l + 1, :], lnb_ref[l:l + 1, :]))
            mix_a = _dot(pmt_ref[...], po_s[...].astype(BF16)).astype(BF16)
            mix_b = (conv * _silu(ub[:, 2 * width:])).astype(BF16)
            mix = _dot(mix_a, wo_ref[l, :width, :]) + _dot(mix_b, wo_ref[l, width:, :])
            xc = xc + m[:, 2 * d_model:] * _rms_norm(mix, gpost_ref[l:l + 1, :])


def _ctx_call(ctx, mod_c, g_pre, g_post, w_in_bf, w_out_bf, wg, conv_a_w, conv_a_b, b_rgate,
              b_igate, lru_lambda, dw_w, dw_b, ln_g, ln_b):
    bsz, n, d_model = ctx.shape
    depth = w_in_bf.shape[0]
    width = conv_a_b.shape[-1]
    npad = n + 2 * SEG_PAD
    pm, pmt = _perm_matrices(n)

    def full(a):
        nd = a.ndim
        return pl.BlockSpec(a.shape, lambda b, _nd=nd: (0,) * _nd)

    params = (mod_c, g_pre, g_post, w_in_bf, w_out_bf, wg, conv_a_w, conv_a_b, b_rgate, b_igate,
              lru_lambda, dw_w, dw_b, ln_g, ln_b, pm, pmt)
    return pl.pallas_call(
        functools.partial(_ctx_body, depth=depth, d_model=d_model, width=width, n=n),
        grid=(bsz,),
        in_specs=[pl.BlockSpec((1, n, d_model), lambda b: (b, 0, 0))] + [full(a) for a in params],
        out_specs=pl.BlockSpec((1, 2 * depth, width), lambda b: (b, 0, 0)),
        out_shape=jax.ShapeDtypeStruct((bsz, 2 * depth, width), F32),
        scratch_shapes=[
            pltpu.VMEM((n + 2 * UA_ROW0, 2 * width), F32),
            pltpu.VMEM((2, n, width), F32),
            pltpu.VMEM((2, n, 2 * width), F32),
            pltpu.VMEM((2, n, width), F32),
            pltpu.VMEM((2, n, width), F32),
            pltpu.VMEM((n, width), F32),
            pltpu.VMEM((npad, width), F32),
            pltpu.VMEM((ROWS, npad - ROWS, width), F32),
            pltpu.VMEM((n, width), F32),
        ],
        compiler_params=pltpu.CompilerParams(
            dimension_semantics=("arbitrary",), vmem_limit_bytes=VMEM_LIMIT),
        name="context",
    )(ctx, *params)


def _proj_body(x_ref, mod_ref, g_ref, w_ref, hx_ref, v1_ref, v2_ref, *, l, nt, d_model, width):
    b = pl.program_id(0)
    j = pl.program_id(1)
    is_pad = jnp.logical_or(j == 0, j == nt + 1)

    @pl.when(is_pad)
    def _():
        v2_ref[...] = jnp.zeros(v2_ref.shape, v2_ref.dtype)

    @pl.when(jnp.logical_not(is_pad))
    def _():
        half = width // 2
        rows = ROWS_PER_CHAIN
        for c in range(x_ref.shape[1] // rows):
            rs = slice(c * rows, (c + 1) * rows)
            hb = _modulate(x_ref[0, rs, :], g_ref[l:l + 1, :], mod_ref[l, pl.ds(b, 1), :],
                           d_model).astype(BF16)
            hx_ref[0, rs, :] = hb
            u = _dot(hb, w_ref[0])
            v = u[:, :width] * jax.nn.sigmoid(u[:, width:])
            v1_ref[0, rs, :] = v[:, :half].astype(BF16)
            v2_ref[0, rs, :] = v[:, half:].astype(BF16)


def _proj_call(x, mods, g_pre, w_in_bf, l, width):
    bsz, seq, d_model = x.shape
    half = width // 2
    tt = TILE_PROJ
    nt = seq // tt

    def tok(b, j):
        return (b, jnp.clip(j - 1, 0, nt - 1), 0)

    return pl.pallas_call(
        functools.partial(_proj_body, l=l, nt=nt, d_model=d_model, width=width),
        grid=(bsz, nt + 2),
        in_specs=[
            pl.BlockSpec((1, tt, d_model), tok),
            pl.BlockSpec(mods.shape, lambda b, j: (0, 0, 0)),
            pl.BlockSpec(g_pre.shape, lambda b, j: (0, 0)),
            pl.BlockSpec((1, d_model, 2 * width), lambda b, j: (l, 0, 1)),
        ],
        out_specs=[
            pl.BlockSpec((1, tt, d_model), tok),
            pl.BlockSpec((1, tt, half), tok),
            pl.BlockSpec((1, tt, half), lambda b, j: (b, j, 0)),
        ],
        out_shape=[
            jax.ShapeDtypeStruct((bsz, seq, d_model), BF16),
            jax.ShapeDtypeStruct((bsz, seq, half), BF16),
            jax.ShapeDtypeStruct((bsz, seq + 2 * tt, half), BF16),
        ],
        compiler_params=pltpu.CompilerParams(
            dimension_semantics=("arbitrary", "arbitrary"), vmem_limit_bytes=VMEM_LIMIT),
        name="proj",
    )(x, mods, g_pre, w_in_bf)


def _mix_body(hx_ref, hxp_ref, hxn_ref, v1_ref, v2_ref, wa_ref, wbg_ref, wg_ref, cw_ref, cb_ref,
              br_ref, bi_ref, lam_ref, dww_ref, dww16_ref, dwb_ref, lng_ref, lnb_ref, pm_ref,
              h0_ref, zero_ref,
              p_ref, q_ref, mb_ref, ab_ref, hbo_ref,
              hxs_s, ua_s, bg_s, vc_s, g_s, hl_s, al_s, po_s, q_s, carry_s, vpad_s, sh_s, wb_s,
              y_s,
              *, l, tt, nt, d_model, width, v2_front):
    i = pl.program_id(1)
    half = width // 2
    nseg = tt // GRID_W
    seg_rows = GRID_W + 2 * SEG_PAD
    ext = LRU_CONV - 1
    reach = CONV_PAD * GRID_W
    row_zero = zero_ref[0]

    @pl.when(i == 0)
    def _():
        carry_s[...] = h0_ref[pl.program_id(0), 2 * l:2 * l + 1, :]

    def col_slices(total):
        return [slice(c, c + MXU_N) for c in range(0, total, MXU_N)]


    def pad_segment(r):
        def run():
            base = r * seg_rows
            zpad = jnp.zeros((SEG_PAD, half), F32)
            vpad_s[base:base + SEG_PAD, :] = zpad
            vpad_s[base + SEG_PAD:base + SEG_PAD + GRID_W, :] = (
                v1_ref[0, r * GRID_W:(r + 1) * GRID_W, :].astype(F32))
            vpad_s[base + SEG_PAD + GRID_W:base + seg_rows, :] = zpad
        return run

    def conv_chunk(src_fn, w_cols, ci, out_cols):
        def run():
            token = None
            for part in range(CHUNK // CONV_ROWS):
                acc = jnp.zeros((CONV_ROWS, half), F32)
                for k in range(CONV_K):
                    acc = acc + dww_ref[l, k:k + 1, w_cols] * src_fn(ci, k, part * CONV_ROWS)
                row0 = ci * CHUNK + part * CONV_ROWS
                y_s[row0:row0 + CONV_ROWS, out_cols] = acc
                token = acc[0:ROWS, 0:128]
            return token
        return run

    def shifted(ci, k, r):
        off = SEG_PAD - CONV_PAD + k
        start = ci * seg_rows + (off // ROWS) * ROWS + r
        return sh_s[off % ROWS, start:start + CONV_ROWS, :]

    def spread_taps():
        for k in range(CONV_K):
            wb_s[k] = jnp.broadcast_to(dww16_ref[l, k:k + 1, half:], (2 * ROWS, half))

    def col_chunk(ci):
        def run():
            win0 = i * tt + (v2_front - reach) + ci * CHUNK
            token = None
            for rt in range(CHUNK // (2 * ROWS)):
                acc = None
                for g0 in range(0, CONV_K, TAP_GROUP):
                    part = None
                    for k in range(g0, min(g0 + TAP_GROUP, CONV_K)):
                        start = pl.multiple_of(win0 + k * GRID_W + rt * 2 * ROWS, 2 * ROWS)
                        term = wb_s[k] * v2_ref[0, pl.ds(start, 2 * ROWS), :]
                        part = term if part is None else part + term
                    part = part.astype(F32)
                    acc = part if acc is None else acc + part
                row0 = ci * CHUNK + rt * 2 * ROWS
                y_s[row0:row0 + 2 * ROWS, half:] = acc
                token = acc[0:ROWS, 0:128]
            return token
        return run

    def permute(cs):
        def run(token):
            hxs_s[0:tt, cs] = _dot(pm_ref[...], hx_ref[0, :, cs]).astype(BF16)
        return run

    def project_a(cs):
        def run(token):
            _mark(hxs_s.at[tt:tt + SPARE_ROWS, 0:128], token)
            lhs = hxs_s[pl.ds(pl.multiple_of(row_zero, SPARE_ROWS), tt), :]
            ua_s[UA_ROW0:UA_ROW0 + tt, cs] = _dot(lhs, wa_ref[0, :, cs])
        return run

    def project_b(cs):
        def run(token):
            bg_s[:, cs] = _dot(hx_ref[0], wbg_ref[0, :, cs])
        return run

    prep = ([pad_segment(r) for r in range(nseg)] + _shifted_copy_pieces(vpad_s, sh_s)
            + [spread_taps])
    rows_first = len(prep)
    cols_first = rows_first + nseg
    vpu = (prep
           + [conv_chunk(shifted, slice(0, half), ci, slice(0, half)) for ci in range(nseg)]
           + [col_chunk(ci) for ci in range(tt // CHUNK)])
    mxu = ([permute(cs) for cs in col_slices(d_model)]
           + [project_a(cs) for cs in col_slices(2 * width)]
           + [project_b(cs) for cs in col_slices(width)])
    after = [None, None, rows_first, rows_first + 1,
             rows_first + 2, rows_first + 4, rows_first + 5, rows_first + 7,
             cols_first + 1, cols_first + 2]
    _chain(mxu, vpu, after)

    wa_a = wa_ref[0, :, :width]
    a_prev = jnp.where(i > 0, _dot(hxp_ref[0], wa_a)[HALO - ext:, :], 0.0)
    a_next = jnp.where(i < nt - 1, _dot(hxn_ref[0], wa_a)[:ext, :], 0.0)
    _short_convs(ua_s, a_prev, a_next, cw_ref[l], cb_ref[l], vc_s, tt, width)

    def finish_conv(c, nrows):
        def run():
            rs = slice(c * nrows, (c + 1) * nrows)
            y = y_s[rs, :] + dwb_ref[l:l + 1, :]
            conv = _silu(_layer_norm(y, lng_ref[l:l + 1, :], lnb_ref[l:l + 1, :]))
            out = conv * _silu(bg_s[rs, :])
            mb_ref[0, rs, :] = out.astype(BF16)
            return out[0:ROWS, 0:128]
        return run

    def gate_piece(d, hf):
        def run(token):
            lo = hf * half
            _mark(vc_s.at[d, tt:tt + SPARE_ROWS, lo:lo + 128], token)
            lhs = vc_s[d, pl.ds(pl.multiple_of(row_zero, SPARE_ROWS), tt), lo:lo + half]
            res = _dot(lhs.astype(BF16), wg_ref[0, d, hf])
            g_s[d, :, lo:lo + half] = res[:, :half]
            g_s[d, :, width + lo:width + lo + half] = res[:, half:]
        return run

    _chain([gate_piece(d, hf) for d in range(2) for hf in range(2)],
           [finish_conv(c, tt // 4) for c in range(4)], [None, 0, 1, 2])

    hf_last, hb_first, a_total = _rglru_scan(
        ua_s, br_ref[l], bi_ref[l], lam_ref[l], carry_s[...], vc_s, g_s, hl_s, al_s,
        po_s, q_s, tt, width)
    carry_s[...] = hf_last
    ab_ref[0, 0] = a_total
    hbo_ref[0, 0] = hb_first
    p_ref[0] = po_s[...].astype(BF16)
    q_ref[0] = q_s[...].astype(BF16)


def _mix_call(hx, v1, v2p, w_in_bf, wg, cw, cb, br, bi, lam, dww, dww16, dwb, lng, lnb, h0, l):
    bsz, seq, d_model = hx.shape
    width = cb.shape[-1]
    half = width // 2
    tt = TILE_MIX
    nt = seq // tt
    hb_per_tile = tt // HALO
    n_halo_blocks = seq // HALO
    v2_front = (v2p.shape[1] - seq) // 2
    nseg = tt // GRID_W
    seg_rows = GRID_W + 2 * SEG_PAD
    reach = CONV_PAD * GRID_W
    pm, _ = _perm_matrices(tt)

    def full(a):
        nd = a.ndim
        return pl.BlockSpec(a.shape, lambda b, i, _nd=nd: (0,) * _nd)

    small = (cw, cb, br, bi, lam, dww, dww16, dwb, lng, lnb, pm, h0)
    tile = lambda b, i: (b, i, 0)
    out_shapes = [jax.ShapeDtypeStruct((bsz, seq, width), BF16)] * 3 + [
        jax.ShapeDtypeStruct((bsz, nt, 1, width), F32)] * 2
    return pl.pallas_call(
        functools.partial(_mix_body, l=l, tt=tt, nt=nt, d_model=d_model, width=width,
                          v2_front=v2_front),
        grid=(bsz, nt),
        in_specs=[
            pl.BlockSpec((1, tt, d_model), tile),
            pl.BlockSpec((1, HALO, d_model),
                         lambda b, i: (b, jnp.maximum(i * hb_per_tile - 1, 0), 0)),
            pl.BlockSpec((1, HALO, d_model),
                         lambda b, i: (b, jnp.minimum((i + 1) * hb_per_tile, n_halo_blocks - 1), 0)),
            pl.BlockSpec((1, tt, half), tile),
            pl.BlockSpec((1, v2p.shape[1], half), lambda b, i: (b, 0, 0)),
            pl.BlockSpec((1, d_model, 2 * width), lambda b, i: (l, 0, 0)),
            pl.BlockSpec((1, d_model, width), lambda b, i: (l, 0, 4)),
            pl.BlockSpec((1,) + wg.shape[1:], lambda b, i: (l, 0, 0, 0, 0)),
        ] + [full(a) for a in small] + [pl.BlockSpec(memory_space=pltpu.SMEM)],
        out_specs=[pl.BlockSpec((1, tt, width), tile)] * 3 + [
            pl.BlockSpec((1, 1, 1, width), lambda b, i: (b, i, 0, 0))] * 2,
        out_shape=out_shapes,
        scratch_shapes=[
            pltpu.VMEM((tt + SPARE_ROWS, d_model), BF16),
            pltpu.VMEM((tt + 2 * UA_ROW0, 2 * width), F32),
            pltpu.VMEM((tt, width), F32),
            pltpu.VMEM((2, tt + SPARE_ROWS, width), F32),
            pltpu.VMEM((2, tt, 2 * width), F32),
            pltpu.VMEM((2, tt, width), F32),
            pltpu.VMEM((2, tt, width), F32),
            pltpu.VMEM((tt, width), F32),
            pltpu.VMEM((tt, width), F32),
            pltpu.VMEM((1, width), F32),
            pltpu.VMEM((nseg * seg_rows, half), F32),
            pltpu.VMEM((ROWS, nseg * seg_rows - ROWS, half), F32),
            pltpu.VMEM((CONV_K, 2 * ROWS, half), BF16),
            pltpu.VMEM((tt, width), F32),
        ],
        compiler_params=pltpu.CompilerParams(
            dimension_semantics=("arbitrary", "arbitrary"), vmem_limit_bytes=VMEM_LIMIT,
            ),
        name="mixer",
    )(hx, hx, hx, v1, v2p, w_in_bf, w_in_bf, wg, *small, jnp.zeros((1,), jnp.int32))


def _out_body(p_ref, q_ref, mb_ref, x_ref, wo_ref, gpost_ref, mod_ref, ab_ref, hb_ref, h0_ref,
              pmt_ref, *rest, l, d_model, width, fuse_next):
    if fuse_next:
        gpre_ref, wbn_ref, _, o_ref, hx_ref, v1_ref, v2_ref, carry_s = rest
    else:
        o_ref, carry_s = rest
    b = pl.program_id(0)

    @pl.when(pl.program_id(1) == 0)
    def _():
        carry_s[...] = h0_ref[b, 2 * l + 1:2 * l + 2, :]

    gate = mod_ref[l, pl.ds(b, 1), 2 * d_model:]
    rows = ROWS_PER_CHAIN
    half = width // 2
    tm = pmt_ref.shape[0]
    c = carry_s[...]
    for st in reversed(range(p_ref.shape[1] // tm)):
        t0 = st * tm
        mix_a = (p_ref[0, t0:t0 + tm, :].astype(F32)
                 + q_ref[0, t0:t0 + tm, :].astype(F32) * c).astype(BF16)
        mix_t = _dot(pmt_ref[...], mix_a).astype(BF16)
        c = hb_ref[0, st] + ab_ref[0, st] * c
        for ch in range(tm // rows):
            rs = slice(t0 + ch * rows, t0 + (ch + 1) * rows)
            mix = (_dot(mix_t[ch * rows:(ch + 1) * rows, :], wo_ref[0, :width, :])
                   + _dot(mb_ref[0, rs, :], wo_ref[0, width:, :]))
            x_new = x_ref[0, rs, :] + gate * _rms_norm(mix, gpost_ref[l:l + 1, :])
            o_ref[0, rs, :] = x_new
            if fuse_next:
                hb = _modulate(x_new, gpre_ref[l + 1:l + 2, :], mod_ref[l + 1, pl.ds(b, 1), :],
                               d_model).astype(BF16)
                hx_ref[0, rs, :] = hb
                u = _dot(hb, wbn_ref[0])
                v = u[:, :width] * jax.nn.sigmoid(u[:, width:])
                v1_ref[0, rs, :] = v[:, :half].astype(BF16)
                v2_ref[0, rs, :] = v[:, half:].astype(BF16)
    carry_s[...] = c


def _out_call(p, q, mb, x, w_out_bf, g_post, mods, ab, hbo, h0, l, nxt=None):
    bsz, seq, d_model = x.shape
    width = p.shape[-1]
    half = width // 2
    tt = TILE_OUT
    tm = TILE_MIX
    nt = seq // tt
    per = tt // tm
    pad_blocks = TILE_PROJ // tt
    rev = lambda b, j: (b, nt - 1 - j, 0)
    _, pmt = _perm_matrices(tm)

    def full(a):
        nd = a.ndim
        return pl.BlockSpec(a.shape, lambda b, j, _nd=nd: (0,) * _nd)

    operands = [p, q, mb, x, w_out_bf, g_post, mods, ab, hbo, h0, pmt]
    in_specs = [
        pl.BlockSpec((1, tt, width), rev),
        pl.BlockSpec((1, tt, width), rev),
        pl.BlockSpec((1, tt, width), rev),
        pl.BlockSpec((1, tt, d_model), rev),
        pl.BlockSpec((1,) + w_out_bf.shape[1:], lambda b, j: (l, 0, 0)),
        full(g_post),
        full(mods),
        pl.BlockSpec((1, per, 1, width), lambda b, j: (b, nt - 1 - j, 0, 0)),
        pl.BlockSpec((1, per, 1, width), lambda b, j: (b, nt - 1 - j, 0, 0)),
        full(h0),
        full(pmt),
    ]
    out_specs = [pl.BlockSpec((1, tt, d_model), rev)]
    out_shape = [jax.ShapeDtypeStruct((bsz, seq, d_model), F32)]
    aliases = {}
    if nxt is not None:
        g_pre, w_in_bf, v2_buf = nxt
        operands += [g_pre, w_in_bf, v2_buf]
        in_specs += [
            full(g_pre),
            pl.BlockSpec((1, d_model, 2 * width), lambda b, j: (l + 1, 0, 1)),
            pl.BlockSpec(memory_space=pl.ANY),
        ]
        out_specs += [
            pl.BlockSpec((1, tt, d_model), rev),
            pl.BlockSpec((1, tt, half), rev),
            pl.BlockSpec((1, tt, half), lambda b, j: (b, nt - 1 - j + pad_blocks, 0)),
        ]
        out_shape += [
            jax.ShapeDtypeStruct((bsz, seq, d_model), BF16),
            jax.ShapeDtypeStruct((bsz, seq, half), BF16),
            jax.ShapeDtypeStruct(v2_buf.shape, BF16),
        ]
        aliases = {len(operands) - 1: 3}
    res = pl.pallas_call(
        functools.partial(_out_body, l=l, d_model=d_model, width=width,
                          fuse_next=nxt is not None),
        grid=(bsz, nt),
        in_specs=in_specs,
        out_specs=out_specs,
        out_shape=out_shape,
        input_output_aliases=aliases,
        scratch_shapes=[pltpu.VMEM((1, width), F32)],
        compiler_params=pltpu.CompilerParams(
            dimension_semantics=("arbitrary", "arbitrary"), vmem_limit_bytes=VMEM_LIMIT),
        name="out_proj" if nxt is not None else "out",
    )(*operands)
    return res if nxt is not None else res[0]


def _pack_gate_weights(w_r, w_i):
    depth, ndir, heads, hd, _ = w_r.shape
    hh = heads // 2
    eye = jnp.eye(hh, dtype=w_r.dtype)

    def bd(w):
        w = w.reshape(depth, ndir, 2, hh, hd, hd)
        return jnp.einsum("ldfhij,hg->ldfhigj", w, eye).reshape(depth, ndir, 2, hh * hd, hh * hd)

    return jnp.concatenate([bd(w_r), bd(w_i)], axis=-1).astype(BF16)


def kernel(x, c, ctx, c_ctx, w_mod, b_mod, g_pre, g_post, w_in, conv_a_w, conv_a_b, w_rgate,
           b_rgate, w_igate, b_igate, lru_lambda, dw_w, dw_b, ln_g, ln_b, w_out):
    bsz, seq, d_model = x.shape
    depth = w_mod.shape[0]
    width = conv_a_b.shape[-1]
    assert bsz + 1 <= ROWS and seq % TILE_PROJ == 0 and seq % TILE_MIX == 0
    assert TILE_MIX % GRID_W == 0 and TILE_PROJ >= CONV_PAD * GRID_W
    assert seq % TILE_OUT == 0 and TILE_OUT % TILE_MIX == 0 and TILE_PROJ % TILE_OUT == 0
    assert ctx.shape[1] % CHUNK == 0 and width % MXU_N == 0 and d_model % MXU_N == 0

    act = jnp.concatenate(
        [c, c_ctx[None, :], jnp.zeros((ROWS - bsz - 1, d_model), F32)], axis=0)
    mods = _mods_call(act, w_mod, b_mod)

    w_in_bf = w_in.astype(BF16)
    w_out_bf = w_out.astype(BF16)
    wg = _pack_gate_weights(w_rgate, w_igate)

    h0 = _ctx_call(ctx, mods[:, bsz:bsz + 1, :], g_pre, g_post, w_in_bf, w_out_bf, wg, conv_a_w,
                   conv_a_b, b_rgate, b_igate, lru_lambda, dw_w, dw_b, ln_g, ln_b)

    dw_w16 = dw_w.astype(BF16)
    hx, v1, v2p = _proj_call(x, mods, g_pre, w_in_bf, 0, width)
    for l in range(depth):
        p, q, mb, ab, hbo = _mix_call(
            hx, v1, v2p, w_in_bf, wg, conv_a_w, conv_a_b, b_rgate, b_igate, lru_lambda, dw_w,
            dw_w16, dw_b, ln_g, ln_b, h0, l)
        if l + 1 < depth:
            x, hx, v1, v2p = _out_call(p, q, mb, x, w_out_bf, g_post, mods, ab, hbo, h0, l,
                                       nxt=(g_pre, w_in_bf, v2p))
        else:
            x = _out_call(p, q, mb, x, w_out_bf, g_post, mods, ab, hbo, h0, l)
    return x
```

```python
import functools

import jax
import jax.numpy as jnp
import numpy as np
from jax import lax
from jax.experimental import pallas as pl
from jax.experimental.pallas import tpu as pltpu

F32 = jnp.float32
BF16 = jnp.bfloat16

EPS = 1e-6
LRU_C = 8.0
LRU_CONV = 4
CONV_K = 31
CONV_PAD = CONV_K // 2
GRID_W = 64
HALO = 16
ROWS = 8
CHUNK = 64
CONV_ROWS = 32
SEG_PAD = 16
MXU_N = 256
SPARE_ROWS = 16
TAP_GROUP = 8
ROWS_PER_CHAIN = 256

TILE_PROJ = 1024
TILE_MIX = 512
TILE_OUT = 1024
VMEM_LIMIT = 56 * 1024 * 1024


def _silu(x):
    return x * jax.nn.sigmoid(x)


def _softplus(x):
    return jnp.maximum(x, 0.0) + jnp.log1p(jnp.exp(-jnp.abs(x)))


def _rms_norm(x, g):
    ms = jnp.mean(x * x, axis=-1, keepdims=True)
    return x * lax.rsqrt(ms + EPS) * g


def _layer_norm(x, g, b):
    mu = jnp.mean(x, axis=-1, keepdims=True)
    xc = x - mu
    var = jnp.mean(xc * xc, axis=-1, keepdims=True)
    return xc * lax.rsqrt(var + EPS) * g + b


def _dot(a, b):
    return jnp.dot(a, b, preferred_element_type=F32)


def _interleave(mxu_pieces, vpu_pieces):
    n = len(mxu_pieces)
    per = -(-len(vpu_pieces) // n) if n else 0
    for j, piece in enumerate(mxu_pieces):
        piece()
        for fill in vpu_pieces[j * per:(j + 1) * per]:
            fill()
    for fill in vpu_pieces[n * per:]:
        fill()


def _mark(spare_ref, token):
    if token is None:
        return
    reps = spare_ref.shape[0] // ROWS
    spare_ref[...] = jnp.concatenate([token] * reps, axis=0).astype(spare_ref.dtype)


def _release(mxu_pieces, vpu_pieces):
    token = None
    for k in range(max(len(mxu_pieces), len(vpu_pieces))):
        new_token = mxu_pieces[k]() if k < len(mxu_pieces) else token
        if k < len(vpu_pieces):
            vpu_pieces[k](token)
        token = new_token


def _perm_matrices(n):
    i = np.arange(n)
    src = (n // ROWS) * (i % ROWS) + i // ROWS
    pm = (src[:, None] == i[None, :]).astype(np.float32)
    return jnp.asarray(pm, BF16), jnp.asarray(pm.T, BF16)


UA_ROW0 = (LRU_CONV - 1) * ROWS


def _short_convs(ua_s, a_prev, a_next, cw, cb, vc_s, n, width):
    sub = n // ROWS
    ext = LRU_CONV - 1
    row = lax.broadcasted_iota(jnp.int32, (ROWS, width), 0)
    cw = 0.5 * cw
    cb = 0.5 * cb

    def blk(p):
        return slice(p * ROWS, (p + 1) * ROWS)

    for k in range(1, ext + 1):
        tail = pltpu.roll(ua_s[blk(ext + sub - k), :width], 1, 0)
        ua_s[blk(ext - k), :width] = jnp.where(row == 0, a_prev[ext - k:ext - k + 1, :], tail)
    for k in range(ext):
        head = pltpu.roll(ua_s[blk(ext + k), :width], ROWS - 1, 0)
        ua_s[blk(ext + sub + k), :width] = jnp.where(row == ROWS - 1, a_next[k:k + 1, :], head)

    def chunk(ci, carry):
        base = ci * CHUNK
        for d in range(2):
            first = 0 if d == 0 else ext
            acc = jnp.broadcast_to(cb[d:d + 1, :], (CHUNK, width))
            for k in range(LRU_CONV):
                start = pl.multiple_of(base + (first + k) * ROWS, ROWS)
                acc = acc + cw[d, k:k + 1, :] * ua_s[pl.ds(start, CHUNK), :width]
            vc_s[d, pl.ds(pl.multiple_of(base, CHUNK), CHUNK), :] = acc
        return carry

    lax.fori_loop(0, n // CHUNK, chunk, 0)


def _gate_pieces(vc_s, wg_ref, g_s, width):
    half = width // 2

    def piece(d, hf):
        def run():
            res = _dot(vc_s[d, :, hf * half:(hf + 1) * half].astype(BF16), wg_ref[d, hf])
            g_s[d, :, hf * half:(hf + 1) * half] = res[:, :half]
            g_s[d, :, width + hf * half:width + (hf + 1) * half] = res[:, half:]
        return run

    return [piece(d, hf) for d in range(2) for hf in range(2)]


def _sublane_scan(a, h, row, reverse):
    for s in (1, 2, 4):
        if reverse:
            m, shift = row < ROWS - s, ROWS - s
        else:
            m, shift = row >= s, s
        a_sh = jnp.where(m, pltpu.roll(a, shift, 0), 1.0)
        h_sh = jnp.where(m, pltpu.roll(h, shift, 0), 0.0)
        h = h + a * h_sh
        a = a * a_sh
    return a, h


def _rglru_scan(ua_s, br, bi, lam, c0, vc_s, g_s, hl_s, al_s, po_s, q_s, n, width):
    sub = n // ROWS
    row = lax.broadcasted_iota(jnp.int32, (ROWS, width), 0)
    def rows_of(v, d):
        return jnp.broadcast_to(v[d:d + 1, :], (ROWS, width))

    spl_half = (-0.5 * LRU_C) * _softplus(-lam)
    spl_half = [rows_of(spl_half, d) for d in range(2)]
    br_half = [rows_of(0.5 * br, d) for d in range(2)]
    bi_half = [rows_of(0.5 * bi, d) for d in range(2)]

    def coeffs(d, r0):
        g = g_s[d, pl.ds(r0, ROWS), :]
        vh = vc_s[d, pl.ds(r0, ROWS), :]
        t_r = jnp.tanh(g[:, :width] + br_half[d])
        t_i = jnp.tanh(g[:, width:] + bi_half[d])
        la = spl_half[d] + spl_half[d] * t_r
        a = jnp.exp(la)
        x = jnp.tanh(la) * (-1.0 - a * a)
        mult = jnp.where(x > 0.0, x * lax.rsqrt(x), 0.0)
        return a, mult * (vh + vh * t_i)

    def local(j, carry):
        hf, af, hb, ab = carry
        r0 = pl.multiple_of(j * ROWS, ROWS)
        a, b = coeffs(0, r0)
        hf = a * hf + b
        af = a * af
        hl_s[0, pl.ds(r0, ROWS), :] = hf
        al_s[0, pl.ds(r0, ROWS), :] = af
        r1 = pl.multiple_of((sub - 1 - j) * ROWS, ROWS)
        a, b = coeffs(1, r1)
        hb = a * hb + b
        ab = a * ab
        hl_s[1, pl.ds(r1, ROWS), :] = hb
        al_s[1, pl.ds(r1, ROWS), :] = ab
        return hf, af, hb, ab

    zero = jnp.zeros((ROWS, width), F32)
    one = jnp.ones((ROWS, width), F32)
    hf, af, hb, ab = lax.fori_loop(0, sub, local, (zero, one, zero, one), unroll=4)

    af, hf = _sublane_scan(af, hf, row, reverse=False)
    end_f = hf + af * c0
    c_f = jnp.where(row == 0, c0, pltpu.roll(end_f, 1, 0))
    ab, hb = _sublane_scan(ab, hb, row, reverse=True)
    c_b = jnp.where(row == ROWS - 1, 0.0, pltpu.roll(hb, ROWS - 1, 0))
    c_a = jnp.where(row == ROWS - 1, 1.0, pltpu.roll(ab, ROWS - 1, 0))

    if po_s is not None:
        def fix(j, carry):
            r0 = pl.ds(pl.multiple_of(j * ROWS, ROWS), ROWS)
            a_b = al_s[1, r0, :]
            h = (hl_s[0, r0, :] + al_s[0, r0, :] * c_f) + (hl_s[1, r0, :] + a_b * c_b)
            sg = _silu(ua_s[pl.ds(pl.multiple_of(UA_ROW0 + j * ROWS, ROWS), ROWS), width:])
            po_s[r0, :] = h * sg
            if q_s is not None:
                q_s[r0, :] = (a_b * c_a) * sg
            return carry

        lax.fori_loop(0, sub, fix, 0, unroll=8)

    return end_f[ROWS - 1:ROWS, :], hb[0:1, :], ab[0:1, :]


def _shifted_copy_pieces(vpad_ref, sh_ref):
    n = sh_ref.shape[1]

    def piece(j):
        def run():
            sh_ref[j] = vpad_ref[pl.ds(j, n), :]
        return run

    return [piece(j) for j in range(ROWS)]


def _time_conv_pieces(sh_ref, w_ref, ncol, bases, out_ref):
    def piece(ci, base):
        def run():
            acc = jnp.zeros((CHUNK, ncol), F32)
            for k in range(CONV_K):
                off = SEG_PAD - CONV_PAD + k
                start = base + (off // ROWS) * ROWS
                acc = acc + w_ref[k:k + 1, 0:ncol] * sh_ref[off % ROWS, start:start + CHUNK, :]
            out_ref[ci * CHUNK:(ci + 1) * CHUNK, 0:ncol] = acc
        return run

    return [piece(ci, base) for ci, base in enumerate(bases)]


def _mods_body(act_ref, w_ref, b_ref, o_ref):
    a = _silu(act_ref[...])
    o_ref[0] = jnp.dot(a, w_ref[0], precision=lax.Precision.HIGHEST,
                       preferred_element_type=F32) + b_ref[0]


def _mods_call(act, w_mod, b_mod):
    depth, d_model, d3 = w_mod.shape
    ncol = d_model
    return pl.pallas_call(
        _mods_body,
        grid=(depth, d3 // ncol),
        in_specs=[
            pl.BlockSpec((ROWS, d_model), lambda l, n: (0, 0)),
            pl.BlockSpec((1, d_model, ncol), lambda l, n: (l, 0, n)),
            pl.BlockSpec((1, 1, ncol), lambda l, n: (l, 0, n)),
        ],
        out_specs=pl.BlockSpec((1, ROWS, ncol), lambda l, n: (l, 0, n)),
        out_shape=jax.ShapeDtypeStruct((depth, ROWS, d3), F32),
        compiler_params=pltpu.CompilerParams(
            dimension_semantics=("arbitrary", "arbitrary"), vmem_limit_bytes=VMEM_LIMIT),
        name="mods",
    )(act, w_mod, b_mod.reshape(depth, 1, d3))


def _modulate(x, g, m, d_model):
    return _rms_norm(x, g) * (1.0 + m[:, d_model:2 * d_model]) + m[:, :d_model]


def _ctx_body(ctx_ref, mod_ref, gpre_ref, gpost_ref, win_ref, wo_ref, wg_ref, cw_ref, cb_ref,
              br_ref, bi_ref, lam_ref, dww_ref, dwb_ref, lng_ref, lnb_ref, pm_ref, pmt_ref,
              h0_ref,
              ua_s, vc_s, g_s, hl_s, al_s, po_s, vpad_s, sh_s, y_s,
              *, depth, d_model, width, n):
    xc = ctx_ref[0]
    zeros_pad = jnp.zeros((SEG_PAD, width), F32)
    no_rows = jnp.zeros((LRU_CONV - 1, width), F32)
    zero_state = jnp.zeros((1, width), F32)
    for l in range(depth):
        update = l < depth - 1
        m = mod_ref[l]
        hc = _modulate(xc, gpre_ref[l:l + 1, :], m, d_model).astype(BF16)
        hcp = _dot(pm_ref[...], hc).astype(BF16)
        if update:
            ua_s[UA_ROW0:UA_ROW0 + n, :] = _dot(hcp, win_ref[l, :, :2 * width])
            ub = _dot(hc, win_ref[l, :, 2 * width:])
        else:
            ua_s[UA_ROW0:UA_ROW0 + n, :width] = _dot(hcp, win_ref[l, :, :width])
        _short_convs(ua_s, no_rows, no_rows, cw_ref[l], cb_ref[l], vc_s, n, width)
        _interleave(_gate_pieces(vc_s, wg_ref.at[l], g_s, width), [])
        hf_last, hb_first, _ = _rglru_scan(
            ua_s, br_ref[l], bi_ref[l], lam_ref[l], zero_state, vc_s, g_s, hl_s, al_s,
            po_s if update else None, None, n, width)
        h0_ref[0, 2 * l:2 * l + 1, :] = hf_last
        h0_ref[0, 2 * l + 1:2 * l + 2, :] = hb_first
        if update:
            v = ub[:, :width] * jax.nn.sigmoid(ub[:, width:2 * width])
            vpad_s[0:SEG_PAD, :] = zeros_pad
            vpad_s[SEG_PAD:SEG_PAD + n, :] = v
            vpad_s[SEG_PAD + n:, :] = zeros_pad
            _interleave([], _shifted_copy_pieces(vpad_s, sh_s))
            _interleave([], _time_conv_pieces(
                sh_s, dww_ref.at[l], width, [ci * CHUNK for ci in range(n // CHUNK)], y_s))
            y = y_s[...] + dwb_ref[l:l + 1, :]
            conv = _silu(_layer_norm(y, lng_ref[l:l + 1, :], lnb_ref[l:l + 1, :]))
            mix_a = _dot(pmt_ref[...], po_s[...].astype(BF16)).astype(BF16)
            mix_b = (conv * _silu(ub[:, 2 * width:])).astype(BF16)
            mix = _dot(mix_a, wo_ref[l, :width, :]) + _dot(mix_b, wo_ref[l, width:, :])
            xc = xc + m[:, 2 * d_model:] * _rms_norm(mix, gpost_ref[l:l + 1, :])


def _ctx_call(ctx, mod_c, g_pre, g_post, w_in_bf, w_out_bf, wg, conv_a_w, conv_a_b, b_rgate,
              b_igate, lru_lambda, dw_w, dw_b, ln_g, ln_b):
    bsz, n, d_model = ctx.shape
    depth = w_in_bf.shape[0]
    width = conv_a_b.shape[-1]
    npad = n + 2 * SEG_PAD
    pm, pmt = _perm_matrices(n)

    def full(a):
        nd = a.ndim
        return pl.BlockSpec(a.shape, lambda b, _nd=nd: (0,) * _nd)

    params = (mod_c, g_pre, g_post, w_in_bf, w_out_bf, wg, conv_a_w, conv_a_b, b_rgate, b_igate,
              lru_lambda, dw_w, dw_b, ln_g, ln_b, pm, pmt)
    return pl.pallas_call(
        functools.partial(_ctx_body, depth=depth, d_model=d_model, width=width, n=n),
        grid=(bsz,),
        in_specs=[pl.BlockSpec((1, n, d_model), lambda b: (b, 0, 0))] + [full(a) for a in params],
        out_specs=pl.BlockSpec((1, 2 * depth, width), lambda b: (b, 0, 0)),
        out_shape=jax.ShapeDtypeStruct((bsz, 2 * depth, width), F32),
        scratch_shapes=[
            pltpu.VMEM((n + 2 * UA_ROW0, 2 * width), F32),
            pltpu.VMEM((2, n, width), F32),
            pltpu.VMEM((2, n, 2 * width), F32),
            pltpu.VMEM((2, n, width), F32),
            pltpu.VMEM((2, n, width), F32),
            pltpu.VMEM((n, width), F32),
            pltpu.VMEM((npad, width), F32),
            pltpu.VMEM((ROWS, npad - ROWS, width), F32),
            pltpu.VMEM((n, width), F32),
        ],
        compiler_params=pltpu.CompilerParams(
            dimension_semantics=("arbitrary",), vmem_limit_bytes=VMEM_LIMIT),
        name="context",
    )(ctx, *params)


def _proj_body(x_ref, mod_ref, g_ref, w_ref, hx_ref, v1_ref, v2_ref, *, l, nt, d_model, width):
    b = pl.program_id(0)
    j = pl.program_id(1)
    is_pad = jnp.logical_or(j == 0, j == nt + 1)

    @pl.when(is_pad)
    def _():
        v2_ref[...] = jnp.zeros(v2_ref.shape, v2_ref.dtype)

    @pl.when(jnp.logical_not(is_pad))
    def _():
        half = width // 2
        rows = ROWS_PER_CHAIN
        for c in range(x_ref.shape[1] // rows):
            rs = slice(c * rows, (c + 1) * rows)
            hb = _modulate(x_ref[0, rs, :], g_ref[l:l + 1, :], mod_ref[l, pl.ds(b, 1), :],
                           d_model).astype(BF16)
            hx_ref[0, rs, :] = hb
            u = _dot(hb, w_ref[0])
            v = u[:, :width] * jax.nn.sigmoid(u[:, width:])
            v1_ref[0, rs, :] = v[:, :half].astype(BF16)
            v2_ref[0, rs, :] = v[:, half:].astype(BF16)


def _proj_call(x, mods, g_pre, w_in_bf, l, width):
    bsz, seq, d_model = x.shape
    half = width // 2
    tt = TILE_PROJ
    nt = seq // tt

    def tok(b, j):
        return (b, jnp.clip(j - 1, 0, nt - 1), 0)

    return pl.pallas_call(
        functools.partial(_proj_body, l=l, nt=nt, d_model=d_model, width=width),
        grid=(bsz, nt + 2),
        in_specs=[
            pl.BlockSpec((1, tt, d_model), tok),
            pl.BlockSpec(mods.shape, lambda b, j: (0, 0, 0)),
            pl.BlockSpec(g_pre.shape, lambda b, j: (0, 0)),
            pl.BlockSpec((1, d_model, 2 * width), lambda b, j: (l, 0, 1)),
        ],
        out_specs=[
            pl.BlockSpec((1, tt, d_model), tok),
            pl.BlockSpec((1, tt, half), tok),
            pl.BlockSpec((1, tt, half), lambda b, j: (b, j, 0)),
        ],
        out_shape=[
            jax.ShapeDtypeStruct((bsz, seq, d_model), BF16),
            jax.ShapeDtypeStruct((bsz, seq, half), BF16),
            jax.ShapeDtypeStruct((bsz, seq + 2 * tt, half), BF16),
        ],
        compiler_params=pltpu.CompilerParams(
            dimension_semantics=("arbitrary", "arbitrary"), vmem_limit_bytes=VMEM_LIMIT),
        name="proj",
    )(x, mods, g_pre, w_in_bf)


def _mix_body(hx_ref, hxp_ref, hxn_ref, v1_ref, v2_ref, wa_ref, wbg_ref, wg_ref, cw_ref, cb_ref,
              br_ref, bi_ref, lam_ref, dww_ref, dww16_ref, dwb_ref, lng_ref, lnb_ref, pm_ref,
              h0_ref, zero_ref,
              p_ref, q_ref, mb_ref, ab_ref, hbo_ref,
              hxs_s, ua_s, bg_s, vc_s, g_s, hl_s, al_s, po_s, q_s, carry_s, vpad_s, sh_s, wb_s,
              y_s,
              *, l, tt, nt, d_model, width, v2_front):
    i = pl.program_id(1)
    half = width // 2
    nseg = tt // GRID_W
    seg_rows = GRID_W + 2 * SEG_PAD
    ext = LRU_CONV - 1
    reach = CONV_PAD * GRID_W
    row_zero = zero_ref[0]
    sh_rows = nseg * seg_rows - ROWS

    @pl.when(i == 0)
    def _():
        carry_s[...] = h0_ref[pl.program_id(0), 2 * l:2 * l + 1, :]

    def col_slices(total):
        return [slice(c, c + MXU_N) for c in range(0, total, MXU_N)]

    def zero_based(start, align):
        return pl.multiple_of(row_zero + start, align)

    def prepare(token):
        for r in range(nseg):
            base = r * seg_rows
            zpad = jnp.zeros((SEG_PAD, half), F32)
            vpad_s[base:base + SEG_PAD, :] = zpad
            vpad_s[base + SEG_PAD:base + SEG_PAD + GRID_W, :] = (
                v1_ref[0, r * GRID_W:(r + 1) * GRID_W, :].astype(F32))
            vpad_s[base + SEG_PAD + GRID_W:base + seg_rows, :] = zpad
        for j in range(ROWS):
            sh_s[j, 0:sh_rows, :] = vpad_s[pl.ds(j, sh_rows), :]
        for k in range(CONV_K):
            wb_s[k] = jnp.broadcast_to(dww16_ref[l, k:k + 1, half:], (2 * ROWS, half))

    def row_conv(chunks):
        def run(token):
            _mark(sh_s.at[0, sh_rows:sh_rows + ROWS, 0:128], token)
            for ci in chunks:
                for part in range(CHUNK // CONV_ROWS):
                    acc = jnp.zeros((CONV_ROWS, half), F32)
                    for k in range(CONV_K):
                        off = SEG_PAD - CONV_PAD + k
                        start = ci * seg_rows + (off // ROWS) * ROWS + part * CONV_ROWS
                        src = sh_s[off % ROWS, pl.ds(zero_based(start, ROWS), CONV_ROWS), :]
                        acc = acc + dww_ref[l, k:k + 1, 0:half] * src
                    row0 = ci * CHUNK + part * CONV_ROWS
                    y_s[row0:row0 + CONV_ROWS, 0:half] = acc
        return run

    def col_conv(chunks):
        def run(token):
            _mark(wb_s.at[CONV_K, :, 0:128], token)
            for ci in chunks:
                win0 = i * tt + (v2_front - reach) + ci * CHUNK
                for rt in range(CHUNK // (2 * ROWS)):
                    acc = None
                    for g0 in range(0, CONV_K, TAP_GROUP):
                        part = None
                        for k in range(g0, min(g0 + TAP_GROUP, CONV_K)):
                            start = pl.multiple_of(win0 + k * GRID_W + rt * 2 * ROWS, 2 * ROWS)
                            term = wb_s[row_zero + k] * v2_ref[0, pl.ds(start, 2 * ROWS), :]
                            part = term if part is None else part + term
                        part = part.astype(F32)
                        acc = part if acc is None else acc + part
                    row0 = ci * CHUNK + rt * 2 * ROWS
                    y_s[row0:row0 + 2 * ROWS, half:] = acc
        return run

    def permute(slices):
        def run():
            for cs in slices:
                res = _dot(pm_ref[...], hx_ref[0, :, cs])
                hxs_s[0:tt, cs] = res.astype(BF16)
            return res[0:ROWS, 0:128]
        return run

    def project_a(cs):
        def run():
            res = _dot(hxs_s[0:tt, :], wa_ref[0, :, cs])
            ua_s[UA_ROW0:UA_ROW0 + tt, cs] = res
            return res[0:ROWS, 0:128]
        return run

    def project_b(cs):
        def run():
            res = _dot(hx_ref[0], wbg_ref[0, :, cs])
            bg_s[0:tt, cs] = res
            return res[0:ROWS, 0:128]
        return run

    cols_d = col_slices(d_model)
    cols_a = col_slices(2 * width)
    cols_b = col_slices(width)
    quarter = nseg // 4
    row_stages = [row_conv(range(q * quarter, (q + 1) * quarter)) for q in range(4)]
    col_stages = [col_conv(range(q * 2 * quarter, (q + 1) * 2 * quarter)) for q in range(2)]
    _release(
        [permute(cols_d[:2]), permute(cols_d[2:]), project_b(cols_b[0]), project_b(cols_b[1])]
        + [project_a(cs) for cs in cols_a],
        [prepare] + row_stages + col_stages)

    wa_a = wa_ref[0, :, :width]
    a_prev = jnp.where(i > 0, _dot(hxp_ref[0], wa_a)[HALO - ext:, :], 0.0)
    a_next = jnp.where(i < nt - 1, _dot(hxn_ref[0], wa_a)[:ext, :], 0.0)
    _short_convs(ua_s, a_prev, a_next, cw_ref[l], cb_ref[l], vc_s, tt, width)

    def finish_conv(c, nrows):
        def run(token):
            _mark(bg_s.at[tt:tt + ROWS, 0:128], token)
            y = y_s[c * nrows:(c + 1) * nrows, :] + dwb_ref[l:l + 1, :]
            conv = _silu(_layer_norm(y, lng_ref[l:l + 1, :], lnb_ref[l:l + 1, :]))
            out = conv * _silu(bg_s[pl.ds(zero_based(c * nrows, ROWS), nrows), :])
            mb_ref[0, c * nrows:(c + 1) * nrows, :] = out.astype(BF16)
        return run

    def gate_piece(d, hf):
        def run():
            lo = hf * half
            res = _dot(vc_s[d, 0:tt, lo:lo + half].astype(BF16), wg_ref[0, d, hf])
            g_s[d, :, lo:lo + half] = res[:, :half]
            g_s[d, :, width + lo:width + lo + half] = res[:, half:]
            return res[0:ROWS, 0:128]
        return run

    _release([gate_piece(d, hf) for d in range(2) for hf in range(2)],
             [finish_conv(c, tt // 4) for c in range(4)])

    hf_last, hb_first, a_total = _rglru_scan(
        ua_s, br_ref[l], bi_ref[l], lam_ref[l], carry_s[...], vc_s, g_s, hl_s, al_s,
        po_s, q_s, tt, width)
    carry_s[...] = hf_last
    ab_ref[0, 0] = a_total
    hbo_ref[0, 0] = hb_first
    p_ref[0] = po_s[...].astype(BF16)
    q_ref[0] = q_s[...].astype(BF16)


def _mix_call(hx, v1, v2p, w_in_bf, wg, cw, cb, br, bi, lam, dww, dww16, dwb, lng, lnb, h0, l):
    bsz, seq, d_model = hx.shape
    width = cb.shape[-1]
    half = width // 2
    tt = TILE_MIX
    nt = seq // tt
    hb_per_tile = tt // HALO
    n_halo_blocks = seq // HALO
    v2_front = (v2p.shape[1] - seq) // 2
    nseg = tt // GRID_W
    seg_rows = GRID_W + 2 * SEG_PAD
    reach = CONV_PAD * GRID_W
    pm, _ = _perm_matrices(tt)

    def full(a):
        nd = a.ndim
        return pl.BlockSpec(a.shape, lambda b, i, _nd=nd: (0,) * _nd)

    small = (cw, cb, br, bi, lam, dww, dww16, dwb, lng, lnb, pm, h0)
    tile = lambda b, i: (b, i, 0)
    out_shapes = [jax.ShapeDtypeStruct((bsz, seq, width), BF16)] * 3 + [
        jax.ShapeDtypeStruct((bsz, nt, 1, width), F32)] * 2
    return pl.pallas_call(
        functools.partial(_mix_body, l=l, tt=tt, nt=nt, d_model=d_model, width=width,
                          v2_front=v2_front),
        grid=(bsz, nt),
        in_specs=[
            pl.BlockSpec((1, tt, d_model), tile),
            pl.BlockSpec((1, HALO, d_model),
                         lambda b, i: (b, jnp.maximum(i * hb_per_tile - 1, 0), 0)),
            pl.BlockSpec((1, HALO, d_model),
                         lambda b, i: (b, jnp.minimum((i + 1) * hb_per_tile, n_halo_blocks - 1), 0)),
            pl.BlockSpec((1, tt, half), tile),
            pl.BlockSpec((1, v2p.shape[1], half), lambda b, i: (b, 0, 0)),
            pl.BlockSpec((1, d_model, 2 * width), lambda b, i: (l, 0, 0)),
            pl.BlockSpec((1, d_model, width), lambda b, i: (l, 0, 4)),
            pl.BlockSpec((1,) + wg.shape[1:], lambda b, i: (l, 0, 0, 0, 0)),
        ] + [full(a) for a in small] + [pl.BlockSpec(memory_space=pltpu.SMEM)],
        out_specs=[pl.BlockSpec((1, tt, width), tile)] * 3 + [
            pl.BlockSpec((1, 1, 1, width), lambda b, i: (b, i, 0, 0))] * 2,
        out_shape=out_shapes,
        scratch_shapes=[
            pltpu.VMEM((tt + SPARE_ROWS, d_model), BF16),
            pltpu.VMEM((tt + 2 * UA_ROW0, 2 * width), F32),
            pltpu.VMEM((tt + ROWS, width), F32),
            pltpu.VMEM((2, tt + SPARE_ROWS, width), F32),
            pltpu.VMEM((2, tt, 2 * width), F32),
            pltpu.VMEM((2, tt, width), F32),
            pltpu.VMEM((2, tt, width), F32),
            pltpu.VMEM((tt, width), F32),
            pltpu.VMEM((tt, width), F32),
            pltpu.VMEM((1, width), F32),
            pltpu.VMEM((nseg * seg_rows, half), F32),
            pltpu.VMEM((ROWS, nseg * seg_rows, half), F32),
            pltpu.VMEM((CONV_K + 1, 2 * ROWS, half), BF16),
            pltpu.VMEM((tt, width), F32),
        ],
        compiler_params=pltpu.CompilerParams(
            dimension_semantics=("arbitrary", "arbitrary"), vmem_limit_bytes=VMEM_LIMIT,
            ),
        name="mixer",
    )(hx, hx, hx, v1, v2p, w_in_bf, w_in_bf, wg, *small, jnp.zeros((1,), jnp.int32))


def _out_body(p_ref, q_ref, mb_ref, x_ref, wo_ref, gpost_ref, mod_ref, ab_ref, hb_ref, h0_ref,
              pmt_ref, *rest, l, d_model, width, fuse_next):
    if fuse_next:
        gpre_ref, wbn_ref, _, o_ref, hx_ref, v1_ref, v2_ref, carry_s = rest
    else:
        o_ref, carry_s = rest
    b = pl.program_id(0)

    @pl.when(pl.program_id(1) == 0)
    def _():
        carry_s[...] = h0_ref[b, 2 * l + 1:2 * l + 2, :]

    gate = mod_ref[l, pl.ds(b, 1), 2 * d_model:]
    rows = ROWS_PER_CHAIN
    half = width // 2
    tm = pmt_ref.shape[0]
    c = carry_s[...]
    for st in reversed(range(p_ref.shape[1] // tm)):
        t0 = st * tm
        mix_a = (p_ref[0, t0:t0 + tm, :].astype(F32)
                 + q_ref[0, t0:t0 + tm, :].astype(F32) * c).astype(BF16)
        mix_t = _dot(pmt_ref[...], mix_a).astype(BF16)
        c = hb_ref[0, st] + ab_ref[0, st] * c
        for ch in range(tm // rows):
            rs = slice(t0 + ch * rows, t0 + (ch + 1) * rows)
            mix = (_dot(mix_t[ch * rows:(ch + 1) * rows, :], wo_ref[0, :width, :])
                   + _dot(mb_ref[0, rs, :], wo_ref[0, width:, :]))
            x_new = x_ref[0, rs, :] + gate * _rms_norm(mix, gpost_ref[l:l + 1, :])
            o_ref[0, rs, :] = x_new
            if fuse_next:
                hb = _modulate(x_new, gpre_ref[l + 1:l + 2, :], mod_ref[l + 1, pl.ds(b, 1), :],
                               d_model).astype(BF16)
                hx_ref[0, rs, :] = hb
                u = _dot(hb, wbn_ref[0])
                v = u[:, :width] * jax.nn.sigmoid(u[:, width:])
                v1_ref[0, rs, :] = v[:, :half].astype(BF16)
                v2_ref[0, rs, :] = v[:, half:].astype(BF16)
    carry_s[...] = c


def _out_call(p, q, mb, x, w_out_bf, g_post, mods, ab, hbo, h0, l, nxt=None):
    bsz, seq, d_model = x.shape
    width = p.shape[-1]
    half = width // 2
    tt = TILE_OUT
    tm = TILE_MIX
    nt = seq // tt
    per = tt // tm
    pad_blocks = TILE_PROJ // tt
    rev = lambda b, j: (b, nt - 1 - j, 0)
    _, pmt = _perm_matrices(tm)

    def full(a):
        nd = a.ndim
        return pl.BlockSpec(a.shape, lambda b, j, _nd=nd: (0,) * _nd)

    operands = [p, q, mb, x, w_out_bf, g_post, mods, ab, hbo, h0, pmt]
    in_specs = [
        pl.BlockSpec((1, tt, width), rev),
        pl.BlockSpec((1, tt, width), rev),
        pl.BlockSpec((1, tt, width), rev),
        pl.BlockSpec((1, tt, d_model), rev),
        pl.BlockSpec((1,) + w_out_bf.shape[1:], lambda b, j: (l, 0, 0)),
        full(g_post),
        full(mods),
        pl.BlockSpec((1, per, 1, width), lambda b, j: (b, nt - 1 - j, 0, 0)),
        pl.BlockSpec((1, per, 1, width), lambda b, j: (b, nt - 1 - j, 0, 0)),
        full(h0),
        full(pmt),
    ]
    out_specs = [pl.BlockSpec((1, tt, d_model), rev)]
    out_shape = [jax.ShapeDtypeStruct((bsz, seq, d_model), F32)]
    aliases = {}
    if nxt is not None:
        g_pre, w_in_bf, v2_buf = nxt
        operands += [g_pre, w_in_bf, v2_buf]
        in_specs += [
            full(g_pre),
            pl.BlockSpec((1, d_model, 2 * width), lambda b, j: (l + 1, 0, 1)),
            pl.BlockSpec(memory_space=pl.ANY),
        ]
        out_specs += [
            pl.BlockSpec((1, tt, d_model), rev),
            pl.BlockSpec((1, tt, half), rev),
            pl.BlockSpec((1, tt, half), lambda b, j: (b, nt - 1 - j + pad_blocks, 0)),
        ]
        out_shape += [
            jax.ShapeDtypeStruct((bsz, seq, d_model), BF16),
            jax.ShapeDtypeStruct((bsz, seq, half), BF16),
            jax.ShapeDtypeStruct(v2_buf.shape, BF16),
        ]
        aliases = {len(operands) - 1: 3}
    res = pl.pallas_call(
        functools.partial(_out_body, l=l, d_model=d_model, width=width,
                          fuse_next=nxt is not None),
        grid=(bsz, nt),
        in_specs=in_specs,
        out_specs=out_specs,
        out_shape=out_shape,
        input_output_aliases=aliases,
        scratch_shapes=[pltpu.VMEM((1, width), F32)],
        compiler_params=pltpu.CompilerParams(
            dimension_semantics=("arbitrary", "arbitrary"), vmem_limit_bytes=VMEM_LIMIT),
        name="out_proj" if nxt is not None else "out",
    )(*operands)
    return res if nxt is not None else res[0]


def _pack_gate_weights(w_r, w_i):
    depth, ndir, heads, hd, _ = w_r.shape
    hh = heads // 2
    eye = jnp.eye(hh, dtype=w_r.dtype)

    def bd(w):
        w = w.reshape(depth, ndir, 2, hh, hd, hd)
        return jnp.einsum("ldfhij,hg->ldfhigj", w, eye).reshape(depth, ndir, 2, hh * hd, hh * hd)

    return jnp.concatenate([bd(w_r), bd(w_i)], axis=-1).astype(BF16)


def kernel(x, c, ctx, c_ctx, w_mod, b_mod, g_pre, g_post, w_in, conv_a_w, conv_a_b, w_rgate,
           b_rgate, w_igate, b_igate, lru_lambda, dw_w, dw_b, ln_g, ln_b, w_out):
    bsz, seq, d_model = x.shape
    depth = w_mod.shape[0]
    width = conv_a_b.shape[-1]
    assert bsz + 1 <= ROWS and seq % TILE_PROJ == 0 and seq % TILE_MIX == 0
    assert TILE_MIX % GRID_W == 0 and TILE_PROJ >= CONV_PAD * GRID_W
    assert seq % TILE_OUT == 0 and TILE_OUT % TILE_MIX == 0 and TILE_PROJ % TILE_OUT == 0
    assert ctx.shape[1] % CHUNK == 0 and width % MXU_N == 0 and d_model % MXU_N == 0

    act = jnp.concatenate(
        [c, c_ctx[None, :], jnp.zeros((ROWS - bsz - 1, d_model), F32)], axis=0)
    mods = _mods_call(act, w_mod, b_mod)

    w_in_bf = w_in.astype(BF16)
    w_out_bf = w_out.astype(BF16)
    wg = _pack_gate_weights(w_rgate, w_igate)

    h0 = _ctx_call(ctx, mods[:, bsz:bsz + 1, :], g_pre, g_post, w_in_bf, w_out_bf, wg, conv_a_w,
                   conv_a_b, b_rgate, b_igate, lru_lambda, dw_w, dw_b, ln_g, ln_b)

    dw_w16 = dw_w.astype(BF16)
    hx, v1, v2p = _proj_call(x, mods, g_pre, w_in_bf, 0, width)
    for l in range(depth):
        p, q, mb, ab, hbo = _mix_call(
            hx, v1, v2p, w_in_bf, wg, conv_a_w, conv_a_b, b_rgate, b_igate, lru_lambda, dw_w,
            dw_w16, dw_b, ln_g, ln_b, h0, l)
        if l + 1 < depth:
            x, hx, v1, v2p = _out_call(p, q, mb, x, w_out_bf, g_post, mods, ab, hbo, h0, l,
                                       nxt=(g_pre, w_in_bf, v2p))
        else:
            x = _out_call(p, q, mb, x, w_out_bf, g_post, mods, ab, hbo, h0, l)
    return x
```

```python
import functools

import jax
import jax.numpy as jnp
import numpy as np
from jax import lax
from jax.experimental import pallas as pl
from jax.experimental.pallas import tpu as pltpu

F32 = jnp.float32
BF16 = jnp.bfloat16

EPS = 1e-6
LRU_C = 8.0
LRU_CONV = 4
CONV_K = 31
CONV_PAD = CONV_K // 2
GRID_W = 64
HALO = 16
ROWS = 8
CHUNK = 64
CONV_ROWS = 32
SEG_PAD = 16
MXU_N = 256
SPARE_ROWS = 16
TAP_GROUP = 8
ROWS_PER_CHAIN = 256

TILE_PROJ = 1024
TILE_MIX = 512
TILE_OUT = 1024
VMEM_LIMIT = 56 * 1024 * 1024


def _silu(x):
    return x * jax.nn.sigmoid(x)


def _softplus(x):
    return jnp.maximum(x, 0.0) + jnp.log1p(jnp.exp(-jnp.abs(x)))


def _rms_norm(x, g):
    ms = jnp.mean(x * x, axis=-1, keepdims=True)
    return x * lax.rsqrt(ms + EPS) * g


def _layer_norm(x, g, b):
    mu = jnp.mean(x, axis=-1, keepdims=True)
    xc = x - mu
    var = jnp.mean(xc * xc, axis=-1, keepdims=True)
    return xc * lax.rsqrt(var + EPS) * g + b


def _dot(a, b):
    return jnp.dot(a, b, preferred_element_type=F32)


def _interleave(mxu_pieces, vpu_pieces):
    n = len(mxu_pieces)
    per = -(-len(vpu_pieces) // n) if n else 0
    for j, piece in enumerate(mxu_pieces):
        piece()
        for fill in vpu_pieces[j * per:(j + 1) * per]:
            fill()
    for fill in vpu_pieces[n * per:]:
        fill()


def _mark(spare_ref, token):
    if token is None:
        return
    reps = spare_ref.shape[0] // ROWS
    spare_ref[...] = jnp.concatenate([token] * reps, axis=0).astype(spare_ref.dtype)


def _release(mxu_pieces, vpu_pieces):
    token = None
    for k in range(max(len(mxu_pieces), len(vpu_pieces))):
        new_token = mxu_pieces[k]() if k < len(mxu_pieces) else token
        if k < len(vpu_pieces):
            vpu_pieces[k](token)
        token = new_token


def _perm_matrices(n):
    i = np.arange(n)
    src = (n // ROWS) * (i % ROWS) + i // ROWS
    pm = (src[:, None] == i[None, :]).astype(np.float32)
    return jnp.asarray(pm, BF16), jnp.asarray(pm.T, BF16)


UA_ROW0 = (LRU_CONV - 1) * ROWS


def _row_conv_matrices(n):
    cols = n // ROWS
    rows = (cols + 2 * SEG_PAD) * ROWS
    even = np.zeros((rows + ROWS, n), np.float32)
    for i in range(rows):
        q, s = divmod(i, ROWS)
        if SEG_PAD <= q < SEG_PAD + cols:
            even[i, cols * s + q - SEG_PAD] = 1.0
    return jnp.asarray(even[:rows], BF16), jnp.asarray(even[ROWS:], BF16)


def _short_convs(ua_s, a_prev, a_next, cw, cb, vc_s, n, width):
    sub = n // ROWS
    ext = LRU_CONV - 1
    row = lax.broadcasted_iota(jnp.int32, (ROWS, width), 0)
    cw = 0.5 * cw
    cb = 0.5 * cb

    def blk(p):
        return slice(p * ROWS, (p + 1) * ROWS)

    for k in range(1, ext + 1):
        tail = pltpu.roll(ua_s[blk(ext + sub - k), :width], 1, 0)
        ua_s[blk(ext - k), :width] = jnp.where(row == 0, a_prev[ext - k:ext - k + 1, :], tail)
    for k in range(ext):
        head = pltpu.roll(ua_s[blk(ext + k), :width], ROWS - 1, 0)
        ua_s[blk(ext + sub + k), :width] = jnp.where(row == ROWS - 1, a_next[k:k + 1, :], head)

    def chunk(ci, carry):
        base = ci * CHUNK
        for d in range(2):
            first = 0 if d == 0 else ext
            acc = jnp.broadcast_to(cb[d:d + 1, :], (CHUNK, width))
            for k in range(LRU_CONV):
                start = pl.multiple_of(base + (first + k) * ROWS, ROWS)
                acc = acc + cw[d, k:k + 1, :] * ua_s[pl.ds(start, CHUNK), :width]
            vc_s[d, pl.ds(pl.multiple_of(base, CHUNK), CHUNK), :] = acc
        return carry

    lax.fori_loop(0, n // CHUNK, chunk, 0)


def _gate_pieces(vc_s, wg_ref, g_s, width):
    half = width // 2

    def piece(d, hf):
        def run():
            res = _dot(vc_s[d, :, hf * half:(hf + 1) * half].astype(BF16), wg_ref[d, hf])
            g_s[d, :, hf * half:(hf + 1) * half] = res[:, :half]
            g_s[d, :, width + hf * half:width + (hf + 1) * half] = res[:, half:]
        return run

    return [piece(d, hf) for d in range(2) for hf in range(2)]


def _sublane_scan(a, h, row, reverse):
    for s in (1, 2, 4):
        if reverse:
            m, shift = row < ROWS - s, ROWS - s
        else:
            m, shift = row >= s, s
        a_sh = jnp.where(m, pltpu.roll(a, shift, 0), 1.0)
        h_sh = jnp.where(m, pltpu.roll(h, shift, 0), 0.0)
        h = h + a * h_sh
        a = a * a_sh
    return a, h


def _rglru_scan(ua_s, br, bi, lam, c0, vc_s, g_s, hl_s, al_s, po_s, q_s, n, width):
    sub = n // ROWS
    row = lax.broadcasted_iota(jnp.int32, (ROWS, width), 0)
    def rows_of(v, d):
        return jnp.broadcast_to(v[d:d + 1, :], (ROWS, width))

    spl_half = (-0.5 * LRU_C) * _softplus(-lam)
    spl_half = [rows_of(spl_half, d) for d in range(2)]
    br_half = [rows_of(0.5 * br, d) for d in range(2)]
    bi_half = [rows_of(0.5 * bi, d) for d in range(2)]

    def coeffs(d, r0):
        g = g_s[d, pl.ds(r0, ROWS), :]
        vh = vc_s[d, pl.ds(r0, ROWS), :]
        t_r = jnp.tanh(g[:, :width] + br_half[d])
        t_i = jnp.tanh(g[:, width:] + bi_half[d])
        la = spl_half[d] + spl_half[d] * t_r
        a = jnp.exp(la)
        x = jnp.tanh(la) * (-1.0 - a * a)
        mult = jnp.where(x > 0.0, x * lax.rsqrt(x), 0.0)
        return a, mult * (vh + vh * t_i)

    def local(j, carry):
        hf, af, hb, ab = carry
        r0 = pl.multiple_of(j * ROWS, ROWS)
        a, b = coeffs(0, r0)
        hf = a * hf + b
        af = a * af
        hl_s[0, pl.ds(r0, ROWS), :] = hf
        al_s[0, pl.ds(r0, ROWS), :] = af
        r1 = pl.multiple_of((sub - 1 - j) * ROWS, ROWS)
        a, b = coeffs(1, r1)
        hb = a * hb + b
        ab = a * ab
        hl_s[1, pl.ds(r1, ROWS), :] = hb
        al_s[1, pl.ds(r1, ROWS), :] = ab
        return hf, af, hb, ab

    zero = jnp.zeros((ROWS, width), F32)
    one = jnp.ones((ROWS, width), F32)
    hf, af, hb, ab = lax.fori_loop(0, sub, local, (zero, one, zero, one), unroll=4)

    af, hf = _sublane_scan(af, hf, row, reverse=False)
    end_f = hf + af * c0
    c_f = jnp.where(row == 0, c0, pltpu.roll(end_f, 1, 0))
    ab, hb = _sublane_scan(ab, hb, row, reverse=True)
    c_b = jnp.where(row == ROWS - 1, 0.0, pltpu.roll(hb, ROWS - 1, 0))
    c_a = jnp.where(row == ROWS - 1, 1.0, pltpu.roll(ab, ROWS - 1, 0))

    if po_s is not None:
        def fix(j, carry):
            r0 = pl.ds(pl.multiple_of(j * ROWS, ROWS), ROWS)
            a_b = al_s[1, r0, :]
            h = (hl_s[0, r0, :] + al_s[0, r0, :] * c_f) + (hl_s[1, r0, :] + a_b * c_b)
            sg = _silu(ua_s[pl.ds(pl.multiple_of(UA_ROW0 + j * ROWS, ROWS), ROWS), width:])
            po_s[r0, :] = h * sg
            if q_s is not None:
                q_s[r0, :] = (a_b * c_a) * sg
            return carry

        lax.fori_loop(0, sub, fix, 0, unroll=8)

    return end_f[ROWS - 1:ROWS, :], hb[0:1, :], ab[0:1, :]


def _shifted_copy_pieces(vpad_ref, sh_ref):
    n = sh_ref.shape[1]

    def piece(j):
        def run():
            sh_ref[j] = vpad_ref[pl.ds(j, n), :]
        return run

    return [piece(j) for j in range(ROWS)]


def _time_conv_pieces(sh_ref, w_ref, ncol, bases, out_ref):
    def piece(ci, base):
        def run():
            acc = jnp.zeros((CHUNK, ncol), F32)
            for k in range(CONV_K):
                off = SEG_PAD - CONV_PAD + k
                start = base + (off // ROWS) * ROWS
                acc = acc + w_ref[k:k + 1, 0:ncol] * sh_ref[off % ROWS, start:start + CHUNK, :]
            out_ref[ci * CHUNK:(ci + 1) * CHUNK, 0:ncol] = acc
        return run

    return [piece(ci, base) for ci, base in enumerate(bases)]


def _mods_body(act_ref, w_ref, b_ref, o_ref):
    a = _silu(act_ref[...])
    o_ref[0] = jnp.dot(a, w_ref[0], precision=lax.Precision.HIGHEST,
                       preferred_element_type=F32) + b_ref[0]


def _mods_call(act, w_mod, b_mod):
    depth, d_model, d3 = w_mod.shape
    ncol = d_model
    return pl.pallas_call(
        _mods_body,
        grid=(depth, d3 // ncol),
        in_specs=[
            pl.BlockSpec((ROWS, d_model), lambda l, n: (0, 0)),
            pl.BlockSpec((1, d_model, ncol), lambda l, n: (l, 0, n)),
            pl.BlockSpec((1, 1, ncol), lambda l, n: (l, 0, n)),
        ],
        out_specs=pl.BlockSpec((1, ROWS, ncol), lambda l, n: (l, 0, n)),
        out_shape=jax.ShapeDtypeStruct((depth, ROWS, d3), F32),
        compiler_params=pltpu.CompilerParams(
            dimension_semantics=("arbitrary", "arbitrary"), vmem_limit_bytes=VMEM_LIMIT),
        name="mods",
    )(act, w_mod, b_mod.reshape(depth, 1, d3))


def _modulate(x, g, m, d_model):
    return _rms_norm(x, g) * (1.0 + m[:, d_model:2 * d_model]) + m[:, :d_model]


def _ctx_body(ctx_ref, mod_ref, gpre_ref, gpost_ref, win_ref, wo_ref, wg_ref, cw_ref, cb_ref,
              br_ref, bi_ref, lam_ref, dww_ref, dwb_ref, lng_ref, lnb_ref, pm_ref, pmt_ref,
              h0_ref,
              ua_s, vc_s, g_s, hl_s, al_s, po_s, vpad_s, sh_s, y_s,
              *, depth, d_model, width, n):
    xc = ctx_ref[0]
    zeros_pad = jnp.zeros((SEG_PAD, width), F32)
    no_rows = jnp.zeros((LRU_CONV - 1, width), F32)
    zero_state = jnp.zeros((1, width), F32)
    for l in range(depth):
        update = l < depth - 1
        m = mod_ref[l]
        hc = _modulate(xc, gpre_ref[l:l + 1, :], m, d_model).astype(BF16)
        hcp = _dot(pm_ref[...], hc).astype(BF16)
        if update:
            ua_s[UA_ROW0:UA_ROW0 + n, :] = _dot(hcp, win_ref[l, :, :2 * width])
            ub = _dot(hc, win_ref[l, :, 2 * width:])
        else:
            ua_s[UA_ROW0:UA_ROW0 + n, :width] = _dot(hcp, win_ref[l, :, :width])
        _short_convs(ua_s, no_rows, no_rows, cw_ref[l], cb_ref[l], vc_s, n, width)
        _interleave(_gate_pieces(vc_s, wg_ref.at[l], g_s, width), [])
        hf_last, hb_first, _ = _rglru_scan(
            ua_s, br_ref[l], bi_ref[l], lam_ref[l], zero_state, vc_s, g_s, hl_s, al_s,
            po_s if update else None, None, n, width)
        h0_ref[0, 2 * l:2 * l + 1, :] = hf_last
        h0_ref[0, 2 * l + 1:2 * l + 2, :] = hb_first
        if update:
            v = ub[:, :width] * jax.nn.sigmoid(ub[:, width:2 * width])
            vpad_s[0:SEG_PAD, :] = zeros_pad
            vpad_s[SEG_PAD:SEG_PAD + n, :] = v
            vpad_s[SEG_PAD + n:, :] = zeros_pad
            _interleave([], _shifted_copy_pieces(vpad_s, sh_s))
            _interleave([], _time_conv_pieces(
                sh_s, dww_ref.at[l], width, [ci * CHUNK for ci in range(n // CHUNK)], y_s))
            y = y_s[...] + dwb_ref[l:l + 1, :]
            conv = _silu(_layer_norm(y, lng_ref[l:l + 1, :], lnb_ref[l:l + 1, :]))
            mix_a = _dot(pmt_ref[...], po_s[...].astype(BF16)).astype(BF16)
            mix_b = (conv * _silu(ub[:, 2 * width:])).astype(BF16)
            mix = _dot(mix_a, wo_ref[l, :width, :]) + _dot(mix_b, wo_ref[l, width:, :])
            xc = xc + m[:, 2 * d_model:] * _rms_norm(mix, gpost_ref[l:l + 1, :])


def _ctx_call(ctx, mod_c, g_pre, g_post, w_in_bf, w_out_bf, wg, conv_a_w, conv_a_b, b_rgate,
              b_igate, lru_lambda, dw_w, dw_b, ln_g, ln_b):
    bsz, n, d_model = ctx.shape
    depth = w_in_bf.shape[0]
    width = conv_a_b.shape[-1]
    npad = n + 2 * SEG_PAD
    pm, pmt = _perm_matrices(n)

    def full(a):
        nd = a.ndim
        return pl.BlockSpec(a.shape, lambda b, _nd=nd: (0,) * _nd)

    params = (mod_c, g_pre, g_post, w_in_bf, w_out_bf, wg, conv_a_w, conv_a_b, b_rgate, b_igate,
              lru_lambda, dw_w, dw_b, ln_g, ln_b, pm, pmt)
    return pl.pallas_call(
        functools.partial(_ctx_body, depth=depth, d_model=d_model, width=width, n=n),
        grid=(bsz,),
        in_specs=[pl.BlockSpec((1, n, d_model), lambda b: (b, 0, 0))] + [full(a) for a in params],
        out_specs=pl.BlockSpec((1, 2 * depth, width), lambda b: (b, 0, 0)),
        out_shape=jax.ShapeDtypeStruct((bsz, 2 * depth, width), F32),
        scratch_shapes=[
            pltpu.VMEM((n + 2 * UA_ROW0, 2 * width), F32),
            pltpu.VMEM((2, n, width), F32),
            pltpu.VMEM((2, n, 2 * width), F32),
            pltpu.VMEM((2, n, width), F32),
            pltpu.VMEM((2, n, width), F32),
            pltpu.VMEM((n, width), F32),
            pltpu.VMEM((npad, width), F32),
            pltpu.VMEM((ROWS, npad - ROWS, width), F32),
            pltpu.VMEM((n, width), F32),
        ],
        compiler_params=pltpu.CompilerParams(
            dimension_semantics=("arbitrary",), vmem_limit_bytes=VMEM_LIMIT),
        name="context",
    )(ctx, *params)


def _proj_body(x_ref, mod_ref, g_ref, w_ref, hx_ref, v1_ref, v2_ref, *, l, nt, d_model, width):
    b = pl.program_id(0)
    j = pl.program_id(1)
    is_pad = jnp.logical_or(j == 0, j == nt + 1)

    @pl.when(is_pad)
    def _():
        v2_ref[...] = jnp.zeros(v2_ref.shape, v2_ref.dtype)

    @pl.when(jnp.logical_not(is_pad))
    def _():
        half = width // 2
        rows = ROWS_PER_CHAIN
        for c in range(x_ref.shape[1] // rows):
            rs = slice(c * rows, (c + 1) * rows)
            hb = _modulate(x_ref[0, rs, :], g_ref[l:l + 1, :], mod_ref[l, pl.ds(b, 1), :],
                           d_model).astype(BF16)
            hx_ref[0, rs, :] = hb
            u = _dot(hb, w_ref[0])
            v = u[:, :width] * jax.nn.sigmoid(u[:, width:])
            v1_ref[0, rs, :] = v[:, :half].astype(BF16)
            v2_ref[0, rs, :] = v[:, half:].astype(BF16)


def _proj_call(x, mods, g_pre, w_in_bf, l, width):
    bsz, seq, d_model = x.shape
    half = width // 2
    tt = TILE_PROJ
    nt = seq // tt

    def tok(b, j):
        return (b, jnp.clip(j - 1, 0, nt - 1), 0)

    return pl.pallas_call(
        functools.partial(_proj_body, l=l, nt=nt, d_model=d_model, width=width),
        grid=(bsz, nt + 2),
        in_specs=[
            pl.BlockSpec((1, tt, d_model), tok),
            pl.BlockSpec(mods.shape, lambda b, j: (0, 0, 0)),
            pl.BlockSpec(g_pre.shape, lambda b, j: (0, 0)),
            pl.BlockSpec((1, d_model, 2 * width), lambda b, j: (l, 0, 1)),
        ],
        out_specs=[
            pl.BlockSpec((1, tt, d_model), tok),
            pl.BlockSpec((1, tt, half), tok),
            pl.BlockSpec((1, tt, half), lambda b, j: (b, j, 0)),
        ],
        out_shape=[
            jax.ShapeDtypeStruct((bsz, seq, d_model), BF16),
            jax.ShapeDtypeStruct((bsz, seq, half), BF16),
            jax.ShapeDtypeStruct((bsz, seq + 2 * tt, half), BF16),
        ],
        compiler_params=pltpu.CompilerParams(
            dimension_semantics=("arbitrary", "arbitrary"), vmem_limit_bytes=VMEM_LIMIT),
        name="proj",
    )(x, mods, g_pre, w_in_bf)


def _mix_body(hx_ref, hxp_ref, hxn_ref, v1_ref, v2_ref, wa_ref, wbg_ref, wg_ref, cw_ref, cb_ref,
              br_ref, bi_ref, lam_ref, dww16_ref, dwb_ref, lng_ref, lnb_ref, pm_ref, pmt_ref,
              pme_ref, pmo_ref, h0_ref, zero_ref,
              p_ref, q_ref, mb_ref, ab_ref, hbo_ref,
              hxs_s, ua_s, bg_s, vc_s, g_s, hl_s, al_s, po_s, q_s, carry_s, ev_s, od_s, wr_s, wb_s,
              yr_s, y_s,
              *, l, tt, nt, d_model, width, v2_front):
    i = pl.program_id(1)
    half = width // 2
    nseg = tt // GRID_W
    seg_rows = GRID_W + 2 * SEG_PAD
    ext = LRU_CONV - 1
    reach = CONV_PAD * GRID_W
    row_zero = zero_ref[0]

    @pl.when(i == 0)
    def _():
        carry_s[...] = h0_ref[pl.program_id(0), 2 * l:2 * l + 1, :]

    def col_slices(total):
        return [slice(c, c + MXU_N) for c in range(0, total, MXU_N)]

    def zero_based(start, align):
        return pl.multiple_of(row_zero + start, align)

    def prepare(token):
        for k in range(CONV_K):
            wr_s[k] = jnp.broadcast_to(dww16_ref[l, k:k + 1, 0:half], (2 * ROWS, half))
            wb_s[k] = jnp.broadcast_to(dww16_ref[l, k:k + 1, half:], (2 * ROWS, half))

    def spread_rows():
        v1 = v1_ref[0]
        res = _dot(pme_ref[...], v1)
        ev_s[...] = res.astype(BF16)
        od_s[...] = _dot(pmo_ref[...], v1).astype(BF16)
        return res[0:ROWS, 0:128]

    def row_conv(chunks):
        def run(token):
            _mark(wr_s.at[CONV_K, :, 0:128], token)
            for ci in chunks:
                for rt in range(CHUNK // (2 * ROWS)):
                    row0 = ci * CHUNK + rt * 2 * ROWS
                    acc = None
                    for g0 in range(0, CONV_K, TAP_GROUP):
                        part = None
                        for k in range(g0, min(g0 + TAP_GROUP, CONV_K)):
                            shift = SEG_PAD - CONV_PAD + k
                            src = row0 + shift * ROWS
                            if shift % 2 == 0:
                                window = ev_s[src:src + 2 * ROWS, :]
                            else:
                                window = od_s[src - ROWS:src + ROWS, :]
                            term = wr_s[row_zero + k] * window
                            part = term if part is None else part + term
                        part = part.astype(F32)
                        acc = part if acc is None else acc + part
                    yr_s[row0:row0 + 2 * ROWS, :] = acc
        return run

    def gather_rows():
        y_s[:, 0:half] = _dot(pmt_ref[...], yr_s[...].astype(BF16))
        return None

    def col_conv(chunks):
        def run(token):
            _mark(wb_s.at[CONV_K, :, 0:128], token)
            for ci in chunks:
                win0 = i * tt + (v2_front - reach) + ci * CHUNK
                for rt in range(CHUNK // (2 * ROWS)):
                    acc = None
                    for g0 in range(0, CONV_K, TAP_GROUP):
                        part = None
                        for k in range(g0, min(g0 + TAP_GROUP, CONV_K)):
                            start = pl.multiple_of(win0 + k * GRID_W + rt * 2 * ROWS, 2 * ROWS)
                            term = wb_s[row_zero + k] * v2_ref[0, pl.ds(start, 2 * ROWS), :]
                            part = term if part is None else part + term
                        part = part.astype(F32)
                        acc = part if acc is None else acc + part
                    row0 = ci * CHUNK + rt * 2 * ROWS
                    y_s[row0:row0 + 2 * ROWS, half:] = acc
        return run

    def permute(slices):
        def run():
            for cs in slices:
                res = _dot(pm_ref[...], hx_ref[0, :, cs])
                hxs_s[0:tt, cs] = res.astype(BF16)
            return res[0:ROWS, 0:128]
        return run

    def project_a(cs):
        def run():
            res = _dot(hxs_s[0:tt, :], wa_ref[0, :, cs])
            ua_s[UA_ROW0:UA_ROW0 + tt, cs] = res
            return res[0:ROWS, 0:128]
        return run

    def project_b(cs):
        def run():
            res = _dot(hx_ref[0], wbg_ref[0, :, cs])
            bg_s[0:tt, cs] = res
            return res[0:ROWS, 0:128]
        return run

    cols_d = col_slices(d_model)
    cols_a = col_slices(2 * width)
    cols_b = col_slices(width)
    quarter = nseg // 4
    row_stages = [row_conv(range(q * quarter, (q + 1) * quarter)) for q in range(4)]
    col_stages = [col_conv(range(q * 2 * quarter, (q + 1) * 2 * quarter)) for q in range(2)]
    _release(
        [spread_rows, permute(cols_d[:2]), permute(cols_d[2:]), project_b(cols_b[0]),
         project_b(cols_b[1])] + [project_a(cs) for cs in cols_a] + [gather_rows],
        [prepare] + row_stages + col_stages)

    wa_a = wa_ref[0, :, :width]
    a_prev = jnp.where(i > 0, _dot(hxp_ref[0], wa_a)[HALO - ext:, :], 0.0)
    a_next = jnp.where(i < nt - 1, _dot(hxn_ref[0], wa_a)[:ext, :], 0.0)
    _short_convs(ua_s, a_prev, a_next, cw_ref[l], cb_ref[l], vc_s, tt, width)

    def finish_conv(c, nrows):
        def run(token):
            _mark(bg_s.at[tt:tt + ROWS, 0:128], token)
            y = y_s[c * nrows:(c + 1) * nrows, :] + dwb_ref[l:l + 1, :]
            conv = _silu(_layer_norm(y, lng_ref[l:l + 1, :], lnb_ref[l:l + 1, :]))
            out = conv * _silu(bg_s[pl.ds(zero_based(c * nrows, ROWS), nrows), :])
            mb_ref[0, c * nrows:(c + 1) * nrows, :] = out.astype(BF16)
        return run

    def gate_piece(d, hf):
        def run():
            lo = hf * half
            res = _dot(vc_s[d, 0:tt, lo:lo + half].astype(BF16), wg_ref[0, d, hf])
            g_s[d, :, lo:lo + half] = res[:, :half]
            g_s[d, :, width + lo:width + lo + half] = res[:, half:]
            return res[0:ROWS, 0:128]
        return run

    _release([gate_piece(d, hf) for d in range(2) for hf in range(2)],
             [finish_conv(c, tt // 4) for c in range(4)])

    hf_last, hb_first, a_total = _rglru_scan(
        ua_s, br_ref[l], bi_ref[l], lam_ref[l], carry_s[...], vc_s, g_s, hl_s, al_s,
        po_s, q_s, tt, width)
    carry_s[...] = hf_last
    ab_ref[0, 0] = a_total
    hbo_ref[0, 0] = hb_first
    p_ref[0] = po_s[...].astype(BF16)
    q_ref[0] = q_s[...].astype(BF16)


def _mix_call(hx, v1, v2p, w_in_bf, wg, cw, cb, br, bi, lam, dww16, dwb, lng, lnb, h0, l):
    bsz, seq, d_model = hx.shape
    width = cb.shape[-1]
    half = width // 2
    tt = TILE_MIX
    nt = seq // tt
    hb_per_tile = tt // HALO
    n_halo_blocks = seq // HALO
    v2_front = (v2p.shape[1] - seq) // 2
    nseg = tt // GRID_W
    seg_rows = GRID_W + 2 * SEG_PAD
    reach = CONV_PAD * GRID_W
    pm, pmt = _perm_matrices(tt)
    pme, pmo = _row_conv_matrices(tt)

    def full(a):
        nd = a.ndim
        return pl.BlockSpec(a.shape, lambda b, i, _nd=nd: (0,) * _nd)

    small = (cw, cb, br, bi, lam, dww16, dwb, lng, lnb, pm, pmt, pme, pmo, h0)
    tile = lambda b, i: (b, i, 0)
    out_shapes = [jax.ShapeDtypeStruct((bsz, seq, width), BF16)] * 3 + [
        jax.ShapeDtypeStruct((bsz, nt, 1, width), F32)] * 2
    return pl.pallas_call(
        functools.partial(_mix_body, l=l, tt=tt, nt=nt, d_model=d_model, width=width,
                          v2_front=v2_front),
        grid=(bsz, nt),
        in_specs=[
            pl.BlockSpec((1, tt, d_model), tile),
            pl.BlockSpec((1, HALO, d_model),
                         lambda b, i: (b, jnp.maximum(i * hb_per_tile - 1, 0), 0)),
            pl.BlockSpec((1, HALO, d_model),
                         lambda b, i: (b, jnp.minimum((i + 1) * hb_per_tile, n_halo_blocks - 1), 0)),
            pl.BlockSpec((1, tt, half), tile),
            pl.BlockSpec((1, v2p.shape[1], half), lambda b, i: (b, 0, 0)),
            pl.BlockSpec((1, d_model, 2 * width), lambda b, i: (l, 0, 0)),
            pl.BlockSpec((1, d_model, width), lambda b, i: (l, 0, 4)),
            pl.BlockSpec((1,) + wg.shape[1:], lambda b, i: (l, 0, 0, 0, 0)),
        ] + [full(a) for a in small] + [pl.BlockSpec(memory_space=pltpu.SMEM)],
        out_specs=[pl.BlockSpec((1, tt, width), tile)] * 3 + [
            pl.BlockSpec((1, 1, 1, width), lambda b, i: (b, i, 0, 0))] * 2,
        out_shape=out_shapes,
        scratch_shapes=[
            pltpu.VMEM((tt + SPARE_ROWS, d_model), BF16),
            pltpu.VMEM((tt + 2 * UA_ROW0, 2 * width), F32),
            pltpu.VMEM((tt + ROWS, width), F32),
            pltpu.VMEM((2, tt + SPARE_ROWS, width), F32),
            pltpu.VMEM((2, tt, 2 * width), F32),
            pltpu.VMEM((2, tt, width), F32),
            pltpu.VMEM((2, tt, width), F32),
            pltpu.VMEM((tt, width), F32),
            pltpu.VMEM((tt, width), F32),
            pltpu.VMEM((1, width), F32),
            pltpu.VMEM(pme.shape[:1] + (half,), BF16),
            pltpu.VMEM(pme.shape[:1] + (half,), BF16),
            pltpu.VMEM((CONV_K + 1, 2 * ROWS, half), BF16),
            pltpu.VMEM((CONV_K + 1, 2 * ROWS, half), BF16),
            pltpu.VMEM((tt, half), F32),
            pltpu.VMEM((tt, width), F32),
        ],
        compiler_params=pltpu.CompilerParams(
            dimension_semantics=("arbitrary", "arbitrary"), vmem_limit_bytes=VMEM_LIMIT,
            ),
        name="mixer",
    )(hx, hx, hx, v1, v2p, w_in_bf, w_in_bf, wg, *small, jnp.zeros((1,), jnp.int32))


def _out_body(p_ref, q_ref, mb_ref, x_ref, wo_ref, gpost_ref, mod_ref, ab_ref, hb_ref, h0_ref,
              pmt_ref, *rest, l, d_model, width, fuse_next):
    if fuse_next:
        gpre_ref, wbn_ref, _, o_ref, hx_ref, v1_ref, v2_ref, carry_s = rest
    else:
        o_ref, carry_s = rest
    b = pl.program_id(0)

    @pl.when(pl.program_id(1) == 0)
    def _():
        carry_s[...] = h0_ref[b, 2 * l + 1:2 * l + 2, :]

    gate = mod_ref[l, pl.ds(b, 1), 2 * d_model:]
    rows = ROWS_PER_CHAIN
    half = width // 2
    tm = pmt_ref.shape[0]
    c = carry_s[...]
    for st in reversed(range(p_ref.shape[1] // tm)):
        t0 = st * tm
        mix_a = (p_ref[0, t0:t0 + tm, :].astype(F32)
                 + q_ref[0, t0:t0 + tm, :].astype(F32) * c).astype(BF16)
        mix_t = _dot(pmt_ref[...], mix_a).astype(BF16)
        c = hb_ref[0, st] + ab_ref[0, st] * c
        for ch in range(tm // rows):
            rs = slice(t0 + ch * rows, t0 + (ch + 1) * rows)
            mix = (_dot(mix_t[ch * rows:(ch + 1) * rows, :], wo_ref[0, :width, :])
                   + _dot(mb_ref[0, rs, :], wo_ref[0, width:, :]))
            x_new = x_ref[0, rs, :] + gate * _rms_norm(mix, gpost_ref[l:l + 1, :])
            o_ref[0, rs, :] = x_new
            if fuse_next:
                hb = _modulate(x_new, gpre_ref[l + 1:l + 2, :], mod_ref[l + 1, pl.ds(b, 1), :],
                               d_model).astype(BF16)
                hx_ref[0, rs, :] = hb
                u = _dot(hb, wbn_ref[0])
                v = u[:, :width] * jax.nn.sigmoid(u[:, width:])
                v1_ref[0, rs, :] = v[:, :half].astype(BF16)
                v2_ref[0, rs, :] = v[:, half:].astype(BF16)
    carry_s[...] = c


def _out_call(p, q, mb, x, w_out_bf, g_post, mods, ab, hbo, h0, l, nxt=None):
    bsz, seq, d_model = x.shape
    width = p.shape[-1]
    half = width // 2
    tt = TILE_OUT
    tm = TILE_MIX
    nt = seq // tt
    per = tt // tm
    pad_blocks = TILE_PROJ // tt
    rev = lambda b, j: (b, nt - 1 - j, 0)
    _, pmt = _perm_matrices(tm)

    def full(a):
        nd = a.ndim
        return pl.BlockSpec(a.shape, lambda b, j, _nd=nd: (0,) * _nd)

    operands = [p, q, mb, x, w_out_bf, g_post, mods, ab, hbo, h0, pmt]
    in_specs = [
        pl.BlockSpec((1, tt, width), rev),
        pl.BlockSpec((1, tt, width), rev),
        pl.BlockSpec((1, tt, width), rev),
        pl.BlockSpec((1, tt, d_model), rev),
        pl.BlockSpec((1,) + w_out_bf.shape[1:], lambda b, j: (l, 0, 0)),
        full(g_post),
        full(mods),
        pl.BlockSpec((1, per, 1, width), lambda b, j: (b, nt - 1 - j, 0, 0)),
        pl.BlockSpec((1, per, 1, width), lambda b, j: (b, nt - 1 - j, 0, 0)),
        full(h0),
        full(pmt),
    ]
    out_specs = [pl.BlockSpec((1, tt, d_model), rev)]
    out_shape = [jax.ShapeDtypeStruct((bsz, seq, d_model), F32)]
    aliases = {}
    if nxt is not None:
        g_pre, w_in_bf, v2_buf = nxt
        operands += [g_pre, w_in_bf, v2_buf]
        in_specs += [
            full(g_pre),
            pl.BlockSpec((1, d_model, 2 * width), lambda b, j: (l + 1, 0, 1)),
            pl.BlockSpec(memory_space=pl.ANY),
        ]
        out_specs += [
            pl.BlockSpec((1, tt, d_model), rev),
            pl.BlockSpec((1, tt, half), rev),
            pl.BlockSpec((1, tt, half), lambda b, j: (b, nt - 1 - j + pad_blocks, 0)),
        ]
        out_shape += [
            jax.ShapeDtypeStruct((bsz, seq, d_model), BF16),
            jax.ShapeDtypeStruct((bsz, seq, half), BF16),
            jax.ShapeDtypeStruct(v2_buf.shape, BF16),
        ]
        aliases = {len(operands) - 1: 3}
    res = pl.pallas_call(
        functools.partial(_out_body, l=l, d_model=d_model, width=width,
                          fuse_next=nxt is not None),
        grid=(bsz, nt),
        in_specs=in_specs,
        out_specs=out_specs,
        out_shape=out_shape,
        input_output_aliases=aliases,
        scratch_shapes=[pltpu.VMEM((1, width), F32)],
        compiler_params=pltpu.CompilerParams(
            dimension_semantics=("arbitrary", "arbitrary"), vmem_limit_bytes=VMEM_LIMIT),
        name="out_proj" if nxt is not None else "out",
    )(*operands)
    return res if nxt is not None else res[0]


def _pack_gate_weights(w_r, w_i):
    depth, ndir, heads, hd, _ = w_r.shape
    hh = heads // 2
    eye = jnp.eye(hh, dtype=w_r.dtype)

    def bd(w):
        w = w.reshape(depth, ndir, 2, hh, hd, hd)
        return jnp.einsum("ldfhij,hg->ldfhigj", w, eye).reshape(depth, ndir, 2, hh * hd, hh * hd)

    return jnp.concatenate([bd(w_r), bd(w_i)], axis=-1).astype(BF16)


def kernel(x, c, ctx, c_ctx, w_mod, b_mod, g_pre, g_post, w_in, conv_a_w, conv_a_b, w_rgate,
           b_rgate, w_igate, b_igate, lru_lambda, dw_w, dw_b, ln_g, ln_b, w_out):
    bsz, seq, d_model = x.shape
    depth = w_mod.shape[0]
    width = conv_a_b.shape[-1]
    assert bsz + 1 <= ROWS and seq % TILE_PROJ == 0 and seq % TILE_MIX == 0
    assert TILE_MIX % GRID_W == 0 and TILE_PROJ >= CONV_PAD * GRID_W
    assert seq % TILE_OUT == 0 and TILE_OUT % TILE_MIX == 0 and TILE_PROJ % TILE_OUT == 0
    assert ctx.shape[1] % CHUNK == 0 and width % MXU_N == 0 and d_model % MXU_N == 0

    act = jnp.concatenate(
        [c, c_ctx[None, :], jnp.zeros((ROWS - bsz - 1, d_model), F32)], axis=0)
    mods = _mods_call(act, w_mod, b_mod)

    w_in_bf = w_in.astype(BF16)
    w_out_bf = w_out.astype(BF16)
    wg = _pack_gate_weights(w_rgate, w_igate)

    h0 = _ctx_call(ctx, mods[:, bsz:bsz + 1, :], g_pre, g_post, w_in_bf, w_out_bf, wg, conv_a_w,
                   conv_a_b, b_rgate, b_igate, lru_lambda, dw_w, dw_b, ln_g, ln_b)

    dw_w16 = dw_w.astype(BF16)
    hx, v1, v2p = _proj_call(x, mods, g_pre, w_in_bf, 0, width)
    for l in range(depth):
        p, q, mb, ab, hbo = _mix_call(
            hx, v1, v2p, w_in_bf, wg, conv_a_w, conv_a_b, b_rgate, b_igate, lru_lambda, dw_w16,
            dw_b, ln_g, ln_b, h0, l)
        if l + 1 < depth:
            x, hx, v1, v2p = _out_call(p, q, mb, x, w_out_bf, g_post, mods, ab, hbo, h0, l,
                                       nxt=(g_pre, w_in_bf, v2p))
        else:
            x = _out_call(p, q, mb, x, w_out_bf, g_post, mods, ab, hbo, h0, l)
    return x
```

```python
import functools

import jax
import jax.numpy as jnp
import numpy as np
from jax import lax
from jax.experimental import pallas as pl
from jax.experimental.pallas import tpu as pltpu

F32 = jnp.float32
BF16 = jnp.bfloat16

EPS = 1e-6
LRU_C = 8.0
LRU_CONV = 4
CONV_K = 31
CONV_PAD = CONV_K // 2
GRID_W = 64
HALO = 16
ROWS = 8
CHUNK = 64
CONV_ROWS = 32
SEG_PAD = 16
MXU_N = 256
SPARE_ROWS = 16
TAP_GROUP = 8
ROWS_PER_CHAIN = 256

TILE_PROJ = 1024
TILE_MIX = 512
TILE_OUT = 1024
VMEM_LIMIT = 56 * 1024 * 1024


def _silu(x):
    return x * jax.nn.sigmoid(x)


def _softplus(x):
    return jnp.maximum(x, 0.0) + jnp.log1p(jnp.exp(-jnp.abs(x)))


def _rms_norm(x, g):
    ms = jnp.mean(x * x, axis=-1, keepdims=True)
    return x * lax.rsqrt(ms + EPS) * g


def _layer_norm(x, g, b):
    mu = jnp.mean(x, axis=-1, keepdims=True)
    xc = x - mu
    var = jnp.mean(xc * xc, axis=-1, keepdims=True)
    return xc * lax.rsqrt(var + EPS) * g + b


def _dot(a, b):
    return jnp.dot(a, b, preferred_element_type=F32)


def _interleave(mxu_pieces, vpu_pieces):
    n = len(mxu_pieces)
    per = -(-len(vpu_pieces) // n) if n else 0
    for j, piece in enumerate(mxu_pieces):
        piece()
        for fill in vpu_pieces[j * per:(j + 1) * per]:
            fill()
    for fill in vpu_pieces[n * per:]:
        fill()


def _mark(spare_ref, token):
    if token is None:
        return
    reps = spare_ref.shape[0] // ROWS
    spare_ref[...] = jnp.concatenate([token] * reps, axis=0).astype(spare_ref.dtype)


def _release(mxu_pieces, vpu_pieces):
    token = None
    for k in range(max(len(mxu_pieces), len(vpu_pieces))):
        new_token = mxu_pieces[k]() if k < len(mxu_pieces) else token
        if k < len(vpu_pieces):
            vpu_pieces[k](token)
        token = new_token


def _perm_matrices(n):
    i = np.arange(n)
    src = (n // ROWS) * (i % ROWS) + i // ROWS
    pm = (src[:, None] == i[None, :]).astype(np.float32)
    return jnp.asarray(pm, BF16), jnp.asarray(pm.T, BF16)


UA_ROW0 = (LRU_CONV - 1) * ROWS


def _row_conv_matrix(n):
    cols = n // ROWS
    rows = (cols + 2 * SEG_PAD) * ROWS
    mat = np.zeros((rows, n), np.float32)
    for i in range(rows):
        q, s = divmod(i, ROWS)
        if SEG_PAD <= q < SEG_PAD + cols:
            mat[i, cols * s + q - SEG_PAD] = 1.0
    return jnp.asarray(mat, BF16)


def _short_convs(ua_s, a_prev, a_next, cw, cb, vc_s, n, width):
    sub = n // ROWS
    ext = LRU_CONV - 1
    row = lax.broadcasted_iota(jnp.int32, (ROWS, width), 0)
    cw = 0.5 * cw
    cb = 0.5 * cb

    def blk(p):
        return slice(p * ROWS, (p + 1) * ROWS)

    for k in range(1, ext + 1):
        tail = pltpu.roll(ua_s[blk(ext + sub - k), :width], 1, 0)
        ua_s[blk(ext - k), :width] = jnp.where(row == 0, a_prev[ext - k:ext - k + 1, :], tail)
    for k in range(ext):
        head = pltpu.roll(ua_s[blk(ext + k), :width], ROWS - 1, 0)
        ua_s[blk(ext + sub + k), :width] = jnp.where(row == ROWS - 1, a_next[k:k + 1, :], head)

    def chunk(ci, carry):
        base = ci * CHUNK
        for d in range(2):
            first = 0 if d == 0 else ext
            acc = jnp.broadcast_to(cb[d:d + 1, :], (CHUNK, width))
            for k in range(LRU_CONV):
                start = pl.multiple_of(base + (first + k) * ROWS, ROWS)
                acc = acc + cw[d, k:k + 1, :] * ua_s[pl.ds(start, CHUNK), :width]
            vc_s[d, pl.ds(pl.multiple_of(base, CHUNK), CHUNK), :] = acc
        return carry

    lax.fori_loop(0, n // CHUNK, chunk, 0)


def _gate_pieces(vc_s, wg_ref, g_s, width):
    half = width // 2

    def piece(d, hf):
        def run():
            res = _dot(vc_s[d, :, hf * half:(hf + 1) * half].astype(BF16), wg_ref[d, hf])
            g_s[d, :, hf * half:(hf + 1) * half] = res[:, :half]
            g_s[d, :, width + hf * half:width + (hf + 1) * half] = res[:, half:]
        return run

    return [piece(d, hf) for d in range(2) for hf in range(2)]


def _sublane_scan(a, h, row, reverse):
    for s in (1, 2, 4):
        if reverse:
            m, shift = row < ROWS - s, ROWS - s
        else:
            m, shift = row >= s, s
        a_sh = jnp.where(m, pltpu.roll(a, shift, 0), 1.0)
        h_sh = jnp.where(m, pltpu.roll(h, shift, 0), 0.0)
        h = h + a * h_sh
        a = a * a_sh
    return a, h


def _rglru_scan(ua_s, br, bi, lam, c0, vc_s, g_s, hl_s, al_s, po_s, q_s, n, width):
    sub = n // ROWS
    row = lax.broadcasted_iota(jnp.int32, (ROWS, width), 0)
    def rows_of(v, d):
        return jnp.broadcast_to(v[d:d + 1, :], (ROWS, width))

    spl_half = (-0.5 * LRU_C) * _softplus(-lam)
    spl_half = [rows_of(spl_half, d) for d in range(2)]
    br_half = [rows_of(0.5 * br, d) for d in range(2)]
    bi_half = [rows_of(0.5 * bi, d) for d in range(2)]

    def coeffs(d, r0):
        g = g_s[d, pl.ds(r0, ROWS), :]
        vh = vc_s[d, pl.ds(r0, ROWS), :]
        t_r = jnp.tanh(g[:, :width] + br_half[d])
        t_i = jnp.tanh(g[:, width:] + bi_half[d])
        la = spl_half[d] + spl_half[d] * t_r
        a = jnp.exp(la)
        x = jnp.tanh(la) * (-1.0 - a * a)
        mult = jnp.where(x > 0.0, x * lax.rsqrt(x), 0.0)
        return a, mult * (vh + vh * t_i)

    def local(j, carry):
        hf, af, hb, ab = carry
        r0 = pl.multiple_of(j * ROWS, ROWS)
        a, b = coeffs(0, r0)
        hf = a * hf + b
        af = a * af
        hl_s[0, pl.ds(r0, ROWS), :] = hf
        al_s[0, pl.ds(r0, ROWS), :] = af
        r1 = pl.multiple_of((sub - 1 - j) * ROWS, ROWS)
        a, b = coeffs(1, r1)
        hb = a * hb + b
        ab = a * ab
        hl_s[1, pl.ds(r1, ROWS), :] = hb
        al_s[1, pl.ds(r1, ROWS), :] = ab
        return hf, af, hb, ab

    zero = jnp.zeros((ROWS, width), F32)
    one = jnp.ones((ROWS, width), F32)
    hf, af, hb, ab = lax.fori_loop(0, sub, local, (zero, one, zero, one), unroll=4)

    af, hf = _sublane_scan(af, hf, row, reverse=False)
    end_f = hf + af * c0
    c_f = jnp.where(row == 0, c0, pltpu.roll(end_f, 1, 0))
    ab, hb = _sublane_scan(ab, hb, row, reverse=True)
    c_b = jnp.where(row == ROWS - 1, 0.0, pltpu.roll(hb, ROWS - 1, 0))
    c_a = jnp.where(row == ROWS - 1, 1.0, pltpu.roll(ab, ROWS - 1, 0))

    if po_s is not None:
        def fix(j, carry):
            r0 = pl.ds(pl.multiple_of(j * ROWS, ROWS), ROWS)
            a_b = al_s[1, r0, :]
            h = (hl_s[0, r0, :] + al_s[0, r0, :] * c_f) + (hl_s[1, r0, :] + a_b * c_b)
            sg = _silu(ua_s[pl.ds(pl.multiple_of(UA_ROW0 + j * ROWS, ROWS), ROWS), width:])
            po_s[r0, :] = h * sg
            if q_s is not None:
                q_s[r0, :] = (a_b * c_a) * sg
            return carry

        lax.fori_loop(0, sub, fix, 0, unroll=8)

    return end_f[ROWS - 1:ROWS, :], hb[0:1, :], ab[0:1, :]


def _shifted_copy_pieces(vpad_ref, sh_ref):
    n = sh_ref.shape[1]

    def piece(j):
        def run():
            sh_ref[j] = vpad_ref[pl.ds(j, n), :]
        return run

    return [piece(j) for j in range(ROWS)]


def _time_conv_pieces(sh_ref, w_ref, ncol, bases, out_ref):
    def piece(ci, base):
        def run():
            acc = jnp.zeros((CHUNK, ncol), F32)
            for k in range(CONV_K):
                off = SEG_PAD - CONV_PAD + k
                start = base + (off // ROWS) * ROWS
                acc = acc + w_ref[k:k + 1, 0:ncol] * sh_ref[off % ROWS, start:start + CHUNK, :]
            out_ref[ci * CHUNK:(ci + 1) * CHUNK, 0:ncol] = acc
        return run

    return [piece(ci, base) for ci, base in enumerate(bases)]


def _mods_body(act_ref, w_ref, b_ref, o_ref):
    a = _silu(act_ref[...])
    o_ref[0] = jnp.dot(a, w_ref[0], precision=lax.Precision.HIGHEST,
                       preferred_element_type=F32) + b_ref[0]


def _mods_call(act, w_mod, b_mod):
    depth, d_model, d3 = w_mod.shape
    ncol = d_model
    return pl.pallas_call(
        _mods_body,
        grid=(depth, d3 // ncol),
        in_specs=[
            pl.BlockSpec((ROWS, d_model), lambda l, n: (0, 0)),
            pl.BlockSpec((1, d_model, ncol), lambda l, n: (l, 0, n)),
            pl.BlockSpec((1, 1, ncol), lambda l, n: (l, 0, n)),
        ],
        out_specs=pl.BlockSpec((1, ROWS, ncol), lambda l, n: (l, 0, n)),
        out_shape=jax.ShapeDtypeStruct((depth, ROWS, d3), F32),
        compiler_params=pltpu.CompilerParams(
            dimension_semantics=("arbitrary", "arbitrary"), vmem_limit_bytes=VMEM_LIMIT),
        name="mods",
    )(act, w_mod, b_mod.reshape(depth, 1, d3))


def _modulate(x, g, m, d_model):
    return _rms_norm(x, g) * (1.0 + m[:, d_model:2 * d_model]) + m[:, :d_model]


def _ctx_body(ctx_ref, mod_ref, gpre_ref, gpost_ref, win_ref, wo_ref, wg_ref, cw_ref, cb_ref,
              br_ref, bi_ref, lam_ref, dww_ref, dwb_ref, lng_ref, lnb_ref, pm_ref, pmt_ref,
              h0_ref,
              ua_s, vc_s, g_s, hl_s, al_s, po_s, vpad_s, sh_s, y_s,
              *, depth, d_model, width, n):
    xc = ctx_ref[0]
    zeros_pad = jnp.zeros((SEG_PAD, width), F32)
    no_rows = jnp.zeros((LRU_CONV - 1, width), F32)
    zero_state = jnp.zeros((1, width), F32)
    for l in range(depth):
        update = l < depth - 1
        m = mod_ref[l]
        hc = _modulate(xc, gpre_ref[l:l + 1, :], m, d_model).astype(BF16)
        hcp = _dot(pm_ref[...], hc).astype(BF16)
        if update:
            ua_s[UA_ROW0:UA_ROW0 + n, :] = _dot(hcp, win_ref[l, :, :2 * width])
            ub = _dot(hc, win_ref[l, :, 2 * width:])
        else:
            ua_s[UA_ROW0:UA_ROW0 + n, :width] = _dot(hcp, win_ref[l, :, :width])
        _short_convs(ua_s, no_rows, no_rows, cw_ref[l], cb_ref[l], vc_s, n, width)
        _interleave(_gate_pieces(vc_s, wg_ref.at[l], g_s, width), [])
        hf_last, hb_first, _ = _rglru_scan(
            ua_s, br_ref[l], bi_ref[l], lam_ref[l], zero_state, vc_s, g_s, hl_s, al_s,
            po_s if update else None, None, n, width)
        h0_ref[0, 2 * l:2 * l + 1, :] = hf_last
        h0_ref[0, 2 * l + 1:2 * l + 2, :] = hb_first
        if update:
            v = ub[:, :width] * jax.nn.sigmoid(ub[:, width:2 * width])
            vpad_s[0:SEG_PAD, :] = zeros_pad
            vpad_s[SEG_PAD:SEG_PAD + n, :] = v
            vpad_s[SEG_PAD + n:, :] = zeros_pad
            _interleave([], _shifted_copy_pieces(vpad_s, sh_s))
            _interleave([], _time_conv_pieces(
                sh_s, dww_ref.at[l], width, [ci * CHUNK for ci in range(n // CHUNK)], y_s))
            y = y_s[...] + dwb_ref[l:l + 1, :]
            conv = _silu(_layer_norm(y, lng_ref[l:l + 1, :], lnb_ref[l:l + 1, :]))
            mix_a = _dot(pmt_ref[...], po_s[...].astype(BF16)).astype(BF16)
            mix_b = (conv * _silu(ub[:, 2 * width:])).astype(BF16)
            mix = _dot(mix_a, wo_ref[l, :width, :]) + _dot(mix_b, wo_ref[l, width:, :])
            xc = xc + m[:, 2 * d_model:] * _rms_norm(mix, gpost_ref[l:l + 1, :])


def _ctx_call(ctx, mod_c, g_pre, g_post, w_in_bf, w_out_bf, wg, conv_a_w, conv_a_b, b_rgate,
              b_igate, lru_lambda, dw_w, dw_b, ln_g, ln_b):
    bsz, n, d_model = ctx.shape
    depth = w_in_bf.shape[0]
    width = conv_a_b.shape[-1]
    npad = n + 2 * SEG_PAD
    pm, pmt = _perm_matrices(n)

    def full(a):
        nd = a.ndim
        return pl.BlockSpec(a.shape, lambda b, _nd=nd: (0,) * _nd)

    params = (mod_c, g_pre, g_post, w_in_bf, w_out_bf, wg, conv_a_w, conv_a_b, b_rgate, b_igate,
              lru_lambda, dw_w, dw_b, ln_g, ln_b, pm, pmt)
    return pl.pallas_call(
        functools.partial(_ctx_body, depth=depth, d_model=d_model, width=width, n=n),
        grid=(bsz,),
        in_specs=[pl.BlockSpec((1, n, d_model), lambda b: (b, 0, 0))] + [full(a) for a in params],
        out_specs=pl.BlockSpec((1, 2 * depth, width), lambda b: (b, 0, 0)),
        out_shape=jax.ShapeDtypeStruct((bsz, 2 * depth, width), F32),
        scratch_shapes=[
            pltpu.VMEM((n + 2 * UA_ROW0, 2 * width), F32),
            pltpu.VMEM((2, n, width), F32),
            pltpu.VMEM((2, n, 2 * width), F32),
            pltpu.VMEM((2, n, width), F32),
            pltpu.VMEM((2, n, width), F32),
            pltpu.VMEM((n, width), F32),
            pltpu.VMEM((npad, width), F32),
            pltpu.VMEM((ROWS, npad - ROWS, width), F32),
            pltpu.VMEM((n, width), F32),
        ],
        compiler_params=pltpu.CompilerParams(
            dimension_semantics=("arbitrary",), vmem_limit_bytes=VMEM_LIMIT),
        name="context",
    )(ctx, *params)


def _proj_body(x_ref, mod_ref, g_ref, w_ref, hx_ref, v1_ref, v2_ref, *, l, nt, d_model, width):
    b = pl.program_id(0)
    j = pl.program_id(1)
    is_pad = jnp.logical_or(j == 0, j == nt + 1)

    @pl.when(is_pad)
    def _():
        v2_ref[...] = jnp.zeros(v2_ref.shape, v2_ref.dtype)

    @pl.when(jnp.logical_not(is_pad))
    def _():
        half = width // 2
        rows = ROWS_PER_CHAIN
        for c in range(x_ref.shape[1] // rows):
            rs = slice(c * rows, (c + 1) * rows)
            hb = _modulate(x_ref[0, rs, :], g_ref[l:l + 1, :], mod_ref[l, pl.ds(b, 1), :],
                           d_model).astype(BF16)
            hx_ref[0, rs, :] = hb
            u = _dot(hb, w_ref[0])
            v = u[:, :width] * jax.nn.sigmoid(u[:, width:])
            v1_ref[0, rs, :] = v[:, :half].astype(BF16)
            v2_ref[0, rs, :] = v[:, half:].astype(BF16)


def _proj_call(x, mods, g_pre, w_in_bf, l, width):
    bsz, seq, d_model = x.shape
    half = width // 2
    tt = TILE_PROJ
    nt = seq // tt

    def tok(b, j):
        return (b, jnp.clip(j - 1, 0, nt - 1), 0)

    return pl.pallas_call(
        functools.partial(_proj_body, l=l, nt=nt, d_model=d_model, width=width),
        grid=(bsz, nt + 2),
        in_specs=[
            pl.BlockSpec((1, tt, d_model), tok),
            pl.BlockSpec(mods.shape, lambda b, j: (0, 0, 0)),
            pl.BlockSpec(g_pre.shape, lambda b, j: (0, 0)),
            pl.BlockSpec((1, d_model, 2 * width), lambda b, j: (l, 0, 1)),
        ],
        out_specs=[
            pl.BlockSpec((1, tt, d_model), tok),
            pl.BlockSpec((1, tt, half), tok),
            pl.BlockSpec((1, tt, half), lambda b, j: (b, j, 0)),
        ],
        out_shape=[
            jax.ShapeDtypeStruct((bsz, seq, d_model), BF16),
            jax.ShapeDtypeStruct((bsz, seq, half), BF16),
            jax.ShapeDtypeStruct((bsz, seq + 2 * tt, half), BF16),
        ],
        compiler_params=pltpu.CompilerParams(
            dimension_semantics=("arbitrary", "arbitrary"), vmem_limit_bytes=VMEM_LIMIT),
        name="proj",
    )(x, mods, g_pre, w_in_bf)


def _mix_body(hx_ref, hxp_ref, hxn_ref, v1_ref, v2_ref, wa_ref, wbg_ref, wg_ref, cw_ref, cb_ref,
              br_ref, bi_ref, lam_ref, dww16_ref, dwb_ref, lng_ref, lnb_ref, pm_ref, pmt_ref,
              pme_ref, h0_ref, zero_ref,
              p_ref, q_ref, mb_ref, ab_ref, hbo_ref,
              hxs_s, ua_s, bg_s, vc_s, g_s, hl_s, al_s, po_s, q_s, carry_s, ev_s, od_s, wr_s, wb_s,
              yr_s, y_s,
              *, l, tt, nt, d_model, width, v2_front):
    i = pl.program_id(1)
    half = width // 2
    nseg = tt // GRID_W
    seg_rows = GRID_W + 2 * SEG_PAD
    ext = LRU_CONV - 1
    reach = CONV_PAD * GRID_W
    row_zero = zero_ref[0]

    @pl.when(i == 0)
    def _():
        carry_s[...] = h0_ref[pl.program_id(0), 2 * l:2 * l + 1, :]

    def col_slices(total):
        return [slice(c, c + MXU_N) for c in range(0, total, MXU_N)]

    def zero_based(start, align):
        return pl.multiple_of(row_zero + start, align)

    def prepare(token):
        for k in range(CONV_K):
            wr_s[k] = jnp.broadcast_to(dww16_ref[l, k:k + 1, 0:half], (2 * ROWS, half))
            wb_s[k] = jnp.broadcast_to(dww16_ref[l, k:k + 1, half:], (2 * ROWS, half))

    def spread_rows():
        res = _dot(pme_ref[...], v1_ref[0])
        ev_s[...] = res.astype(BF16)
        moved = jnp.concatenate([res[ROWS:, :], jnp.zeros((ROWS, half), F32)], axis=0)
        od_s[...] = moved.astype(BF16)
        return res[0:ROWS, 0:128]

    def row_conv(chunks):
        def run(token):
            _mark(wr_s.at[CONV_K, :, 0:128], token)
            for ci in chunks:
                for rt in range(CHUNK // (2 * ROWS)):
                    row0 = ci * CHUNK + rt * 2 * ROWS
                    acc = None
                    for g0 in range(0, CONV_K, TAP_GROUP):
                        part = None
                        for k in range(g0, min(g0 + TAP_GROUP, CONV_K)):
                            shift = SEG_PAD - CONV_PAD + k
                            src = row0 + shift * ROWS
                            if shift % 2 == 0:
                                window = ev_s[src:src + 2 * ROWS, :]
                            else:
                                window = od_s[src - ROWS:src + ROWS, :]
                            term = wr_s[row_zero + k] * window
                            part = term if part is None else part + term
                        part = part.astype(F32)
                        acc = part if acc is None else acc + part
                    yr_s[row0:row0 + 2 * ROWS, :] = acc
        return run

    def gather_rows():
        y_s[:, 0:half] = _dot(pmt_ref[...], yr_s[...].astype(BF16))
        return None

    def col_conv(chunks):
        def run(token):
            _mark(wb_s.at[CONV_K, :, 0:128], token)
            for ci in chunks:
                win0 = i * tt + (v2_front - reach) + ci * CHUNK
                for rt in range(CHUNK // (2 * ROWS)):
                    acc = None
                    for g0 in range(0, CONV_K, TAP_GROUP):
                        part = None
                        for k in range(g0, min(g0 + TAP_GROUP, CONV_K)):
                            start = pl.multiple_of(win0 + k * GRID_W + rt * 2 * ROWS, 2 * ROWS)
                            term = wb_s[row_zero + k] * v2_ref[0, pl.ds(start, 2 * ROWS), :]
                            part = term if part is None else part + term
                        part = part.astype(F32)
                        acc = part if acc is None else acc + part
                    row0 = ci * CHUNK + rt * 2 * ROWS
                    y_s[row0:row0 + 2 * ROWS, half:] = acc
        return run

    def permute(slices):
        def run():
            for cs in slices:
                res = _dot(pm_ref[...], hx_ref[0, :, cs])
                hxs_s[0:tt, cs] = res.astype(BF16)
            return res[0:ROWS, 0:128]
        return run

    def project_a(cs):
        def run():
            res = _dot(hxs_s[0:tt, :], wa_ref[0, :, cs])
            ua_s[UA_ROW0:UA_ROW0 + tt, cs] = res
            return res[0:ROWS, 0:128]
        return run

    def project_b(cs):
        def run():
            res = _dot(hx_ref[0], wbg_ref[0, :, cs])
            bg_s[0:tt, cs] = res
            return res[0:ROWS, 0:128]
        return run

    cols_d = col_slices(d_model)
    cols_a = col_slices(2 * width)
    cols_b = col_slices(width)
    quarter = nseg // 4
    row_stages = [row_conv(range(q * quarter, (q + 1) * quarter)) for q in range(4)]
    col_stages = [col_conv(range(q * 2 * quarter, (q + 1) * 2 * quarter)) for q in range(2)]
    _release(
        [spread_rows, permute(cols_d[:2]), permute(cols_d[2:]), project_b(cols_b[0]),
         project_b(cols_b[1])] + [project_a(cs) for cs in cols_a] + [gather_rows],
        [prepare] + row_stages + col_stages)

    wa_a = wa_ref[0, :, :width]
    a_prev = jnp.where(i > 0, _dot(hxp_ref[0], wa_a)[HALO - ext:, :], 0.0)
    a_next = jnp.where(i < nt - 1, _dot(hxn_ref[0], wa_a)[:ext, :], 0.0)
    _short_convs(ua_s, a_prev, a_next, cw_ref[l], cb_ref[l], vc_s, tt, width)

    def finish_conv(c, nrows):
        def run(token):
            _mark(bg_s.at[tt:tt + ROWS, 0:128], token)
            y = y_s[c * nrows:(c + 1) * nrows, :] + dwb_ref[l:l + 1, :]
            conv = _silu(_layer_norm(y, lng_ref[l:l + 1, :], lnb_ref[l:l + 1, :]))
            out = conv * _silu(bg_s[pl.ds(zero_based(c * nrows, ROWS), nrows), :])
            mb_ref[0, c * nrows:(c + 1) * nrows, :] = out.astype(BF16)
        return run

    def gate_piece(d, hf):
        def run():
            lo = hf * half
            res = _dot(vc_s[d, 0:tt, lo:lo + half].astype(BF16), wg_ref[0, d, hf])
            g_s[d, :, lo:lo + half] = res[:, :half]
            g_s[d, :, width + lo:width + lo + half] = res[:, half:]
            return res[0:ROWS, 0:128]
        return run

    _release([gate_piece(d, hf) for d in range(2) for hf in range(2)],
             [finish_conv(c, tt // 4) for c in range(4)])

    hf_last, hb_first, a_total = _rglru_scan(
        ua_s, br_ref[l], bi_ref[l], lam_ref[l], carry_s[...], vc_s, g_s, hl_s, al_s,
        po_s, q_s, tt, width)
    carry_s[...] = hf_last
    ab_ref[0, 0] = a_total
    hbo_ref[0, 0] = hb_first
    p_ref[0] = po_s[...].astype(BF16)
    q_ref[0] = q_s[...].astype(BF16)


def _mix_call(hx, v1, v2p, w_in_bf, wg, cw, cb, br, bi, lam, dww16, dwb, lng, lnb, h0, l):
    bsz, seq, d_model = hx.shape
    width = cb.shape[-1]
    half = width // 2
    tt = TILE_MIX
    nt = seq // tt
    hb_per_tile = tt // HALO
    n_halo_blocks = seq // HALO
    v2_front = (v2p.shape[1] - seq) // 2
    nseg = tt // GRID_W
    seg_rows = GRID_W + 2 * SEG_PAD
    reach = CONV_PAD * GRID_W
    pm, pmt = _perm_matrices(tt)
    pme = _row_conv_matrix(tt)

    def full(a):
        nd = a.ndim
        return pl.BlockSpec(a.shape, lambda b, i, _nd=nd: (0,) * _nd)

    small = (cw, cb, br, bi, lam, dww16, dwb, lng, lnb, pm, pmt, pme, h0)
    tile = lambda b, i: (b, i, 0)
    out_shapes = [jax.ShapeDtypeStruct((bsz, seq, width), BF16)] * 3 + [
        jax.ShapeDtypeStruct((bsz, nt, 1, width), F32)] * 2
    return pl.pallas_call(
        functools.partial(_mix_body, l=l, tt=tt, nt=nt, d_model=d_model, width=width,
                          v2_front=v2_front),
        grid=(bsz, nt),
        in_specs=[
            pl.BlockSpec((1, tt, d_model), tile),
            pl.BlockSpec((1, HALO, d_model),
                         lambda b, i: (b, jnp.maximum(i * hb_per_tile - 1, 0), 0)),
            pl.BlockSpec((1, HALO, d_model),
                         lambda b, i: (b, jnp.minimum((i + 1) * hb_per_tile, n_halo_blocks - 1), 0)),
            pl.BlockSpec((1, tt, half), tile),
            pl.BlockSpec((1, v2p.shape[1], half), lambda b, i: (b, 0, 0)),
            pl.BlockSpec((1, d_model, 2 * width), lambda b, i: (l, 0, 0)),
            pl.BlockSpec((1, d_model, width), lambda b, i: (l, 0, 4)),
            pl.BlockSpec((1,) + wg.shape[1:], lambda b, i: (l, 0, 0, 0, 0)),
        ] + [full(a) for a in small] + [pl.BlockSpec(memory_space=pltpu.SMEM)],
        out_specs=[pl.BlockSpec((1, tt, width), tile)] * 3 + [
            pl.BlockSpec((1, 1, 1, width), lambda b, i: (b, i, 0, 0))] * 2,
        out_shape=out_shapes,
        scratch_shapes=[
            pltpu.VMEM((tt + SPARE_ROWS, d_model), BF16),
            pltpu.VMEM((tt + 2 * UA_ROW0, 2 * width), F32),
            pltpu.VMEM((tt + ROWS, width), F32),
            pltpu.VMEM((2, tt + SPARE_ROWS, width), F32),
            pltpu.VMEM((2, tt, 2 * width), F32),
            pltpu.VMEM((2, tt, width), F32),
            pltpu.VMEM((2, tt, width), F32),
            pltpu.VMEM((tt, width), F32),
            pltpu.VMEM((tt, width), F32),
            pltpu.VMEM((1, width), F32),
            pltpu.VMEM(pme.shape[:1] + (half,), BF16),
            pltpu.VMEM(pme.shape[:1] + (half,), BF16),
            pltpu.VMEM((CONV_K + 1, 2 * ROWS, half), BF16),
            pltpu.VMEM((CONV_K + 1, 2 * ROWS, half), BF16),
            pltpu.VMEM((tt, half), F32),
            pltpu.VMEM((tt, width), F32),
        ],
        compiler_params=pltpu.CompilerParams(
            dimension_semantics=("arbitrary", "arbitrary"), vmem_limit_bytes=VMEM_LIMIT,
            ),
        name="mixer",
    )(hx, hx, hx, v1, v2p, w_in_bf, w_in_bf, wg, *small, jnp.zeros((1,), jnp.int32))


def _out_body(p_ref, q_ref, mb_ref, x_ref, wo_ref, gpost_ref, mod_ref, ab_ref, hb_ref, h0_ref,
              pmt_ref, *rest, l, d_model, width, fuse_next):
    if fuse_next:
        gpre_ref, wbn_ref, _, o_ref, hx_ref, v1_ref, v2_ref, carry_s = rest
    else:
        o_ref, carry_s = rest
    b = pl.program_id(0)

    @pl.when(pl.program_id(1) == 0)
    def _():
        carry_s[...] = h0_ref[b, 2 * l + 1:2 * l + 2, :]

    gate = mod_ref[l, pl.ds(b, 1), 2 * d_model:]
    rows = ROWS_PER_CHAIN
    half = width // 2
    tm = pmt_ref.shape[0]
    c = carry_s[...]
    for st in reversed(range(p_ref.shape[1] // tm)):
        t0 = st * tm
        mix_a = (p_ref[0, t0:t0 + tm, :].astype(F32)
                 + q_ref[0, t0:t0 + tm, :].astype(F32) * c).astype(BF16)
        mix_t = _dot(pmt_ref[...], mix_a).astype(BF16)
        c = hb_ref[0, st] + ab_ref[0, st] * c
        for ch in range(tm // rows):
            rs = slice(t0 + ch * rows, t0 + (ch + 1) * rows)
            mix = (_dot(mix_t[ch * rows:(ch + 1) * rows, :], wo_ref[0, :width, :])
                   + _dot(mb_ref[0, rs, :], wo_ref[0, width:, :]))
            x_new = x_ref[0, rs, :] + gate * _rms_norm(mix, gpost_ref[l:l + 1, :])
            o_ref[0, rs, :] = x_new
            if fuse_next:
                hb = _modulate(x_new, gpre_ref[l + 1:l + 2, :], mod_ref[l + 1, pl.ds(b, 1), :],
                               d_model).astype(BF16)
                hx_ref[0, rs, :] = hb
                u = _dot(hb, wbn_ref[0])
                v = u[:, :width] * jax.nn.sigmoid(u[:, width:])
                v1_ref[0, rs, :] = v[:, :half].astype(BF16)
                v2_ref[0, rs, :] = v[:, half:].astype(BF16)
    carry_s[...] = c


def _out_call(p, q, mb, x, w_out_bf, g_post, mods, ab, hbo, h0, l, nxt=None):
    bsz, seq, d_model = x.shape
    width = p.shape[-1]
    half = width // 2
    tt = TILE_OUT
    tm = TILE_MIX
    nt = seq // tt
    per = tt // tm
    pad_blocks = TILE_PROJ // tt
    rev = lambda b, j: (b, nt - 1 - j, 0)
    _, pmt = _perm_matrices(tm)

    def full(a):
        nd = a.ndim
        return pl.BlockSpec(a.shape, lambda b, j, _nd=nd: (0,) * _nd)

    operands = [p, q, mb, x, w_out_bf, g_post, mods, ab, hbo, h0, pmt]
    in_specs = [
        pl.BlockSpec((1, tt, width), rev),
        pl.BlockSpec((1, tt, width), rev),
        pl.BlockSpec((1, tt, width), rev),
        pl.BlockSpec((1, tt, d_model), rev),
        pl.BlockSpec((1,) + w_out_bf.shape[1:], lambda b, j: (l, 0, 0)),
        full(g_post),
        full(mods),
        pl.BlockSpec((1, per, 1, width), lambda b, j: (b, nt - 1 - j, 0, 0)),
        pl.BlockSpec((1, per, 1, width), lambda b, j: (b, nt - 1 - j, 0, 0)),
        full(h0),
        full(pmt),
    ]
    out_specs = [pl.BlockSpec((1, tt, d_model), rev)]
    out_shape = [jax.ShapeDtypeStruct((bsz, seq, d_model), F32)]
    aliases = {}
    if nxt is not None:
        g_pre, w_in_bf, v2_buf = nxt
        operands += [g_pre, w_in_bf, v2_buf]
        in_specs += [
            full(g_pre),
            pl.BlockSpec((1, d_model, 2 * width), lambda b, j: (l + 1, 0, 1)),
            pl.BlockSpec(memory_space=pl.ANY),
        ]
        out_specs += [
            pl.BlockSpec((1, tt, d_model), rev),
            pl.BlockSpec((1, tt, half), rev),
            pl.BlockSpec((1, tt, half), lambda b, j: (b, nt - 1 - j + pad_blocks, 0)),
        ]
        out_shape += [
            jax.ShapeDtypeStruct((bsz, seq, d_model), BF16),
            jax.ShapeDtypeStruct((bsz, seq, half), BF16),
            jax.ShapeDtypeStruct(v2_buf.shape, BF16),
        ]
        aliases = {len(operands) - 1: 3}
    res = pl.pallas_call(
        functools.partial(_out_body, l=l, d_model=d_model, width=width,
                          fuse_next=nxt is not None),
        grid=(bsz, nt),
        in_specs=in_specs,
        out_specs=out_specs,
        out_shape=out_shape,
        input_output_aliases=aliases,
        scratch_shapes=[pltpu.VMEM((1, width), F32)],
        compiler_params=pltpu.CompilerParams(
            dimension_semantics=("arbitrary", "arbitrary"), vmem_limit_bytes=VMEM_LIMIT),
        name="out_proj" if nxt is not None else "out",
    )(*operands)
    return res if nxt is not None else res[0]


def _pack_gate_weights(w_r, w_i):
    depth, ndir, heads, hd, _ = w_r.shape
    hh = heads // 2
    eye = jnp.eye(hh, dtype=w_r.dtype)

    def bd(w):
        w = w.reshape(depth, ndir, 2, hh, hd, hd)
        return jnp.einsum("ldfhij,hg->ldfhigj", w, eye).reshape(depth, ndir, 2, hh * hd, hh * hd)

    return jnp.concatenate([bd(w_r), bd(w_i)], axis=-1).astype(BF16)


def kernel(x, c, ctx, c_ctx, w_mod, b_mod, g_pre, g_post, w_in, conv_a_w, conv_a_b, w_rgate,
           b_rgate, w_igate, b_igate, lru_lambda, dw_w, dw_b, ln_g, ln_b, w_out):
    bsz, seq, d_model = x.shape
    depth = w_mod.shape[0]
    width = conv_a_b.shape[-1]
    assert bsz + 1 <= ROWS and seq % TILE_PROJ == 0 and seq % TILE_MIX == 0
    assert TILE_MIX % GRID_W == 0 and TILE_PROJ >= CONV_PAD * GRID_W
    assert seq % TILE_OUT == 0 and TILE_OUT % TILE_MIX == 0 and TILE_PROJ % TILE_OUT == 0
    assert ctx.shape[1] % CHUNK == 0 and width % MXU_N == 0 and d_model % MXU_N == 0

    act = jnp.concatenate(
        [c, c_ctx[None, :], jnp.zeros((ROWS - bsz - 1, d_model), F32)], axis=0)
    mods = _mods_call(act, w_mod, b_mod)

    w_in_bf = w_in.astype(BF16)
    w_out_bf = w_out.astype(BF16)
    wg = _pack_gate_weights(w_rgate, w_igate)

    h0 = _ctx_call(ctx, mods[:, bsz:bsz + 1, :], g_pre, g_post, w_in_bf, w_out_bf, wg, conv_a_w,
                   conv_a_b, b_rgate, b_igate, lru_lambda, dw_w, dw_b, ln_g, ln_b)

    dw_w16 = dw_w.astype(BF16)
    hx, v1, v2p = _proj_call(x, mods, g_pre, w_in_bf, 0, width)
    for l in range(depth):
        p, q, mb, ab, hbo = _mix_call(
            hx, v1, v2p, w_in_bf, wg, conv_a_w, conv_a_b, b_rgate, b_igate, lru_lambda, dw_w16,
            dw_b, ln_g, ln_b, h0, l)
        if l + 1 < depth:
            x, hx, v1, v2p = _out_call(p, q, mb, x, w_out_bf, g_post, mods, ab, hbo, h0, l,
                                       nxt=(g_pre, w_in_bf, v2p))
        else:
            x = _out_call(p, q, mb, x, w_out_bf, g_post, mods, ab, hbo, h0, l)
    return x
```

```python
import functools

import jax
import jax.numpy as jnp
import numpy as np
from jax import lax
from jax.experimental import pallas as pl
from jax.experimental.pallas import tpu as pltpu

F32 = jnp.float32
BF16 = jnp.bfloat16

EPS = 1e-6
LRU_C = 8.0
LRU_CONV = 4
CONV_K = 31
CONV_PAD = CONV_K // 2
GRID_W = 64
HALO = 16
ROWS = 8
CHUNK = 64
CONV_ROWS = 32
SEG_PAD = 16
MXU_N = 256
SPARE_ROWS = 16
TAP_GROUP = 16
ROWS_PER_CHAIN = 256

TILE_PROJ = 1024
TILE_MIX = 512
TILE_OUT = 1024
VMEM_LIMIT = 56 * 1024 * 1024


def _silu(x):
    return x * jax.nn.sigmoid(x)


def _softplus(x):
    return jnp.maximum(x, 0.0) + jnp.log1p(jnp.exp(-jnp.abs(x)))


def _rms_norm(x, g):
    ms = jnp.mean(x * x, axis=-1, keepdims=True)
    return x * lax.rsqrt(ms + EPS) * g


def _layer_norm(x, g, b):
    mu = jnp.mean(x, axis=-1, keepdims=True)
    xc = x - mu
    var = jnp.mean(xc * xc, axis=-1, keepdims=True)
    return xc * lax.rsqrt(var + EPS) * g + b


def _dot(a, b):
    return jnp.dot(a, b, preferred_element_type=F32)


def _interleave(mxu_pieces, vpu_pieces):
    n = len(mxu_pieces)
    per = -(-len(vpu_pieces) // n) if n else 0
    for j, piece in enumerate(mxu_pieces):
        piece()
        for fill in vpu_pieces[j * per:(j + 1) * per]:
            fill()
    for fill in vpu_pieces[n * per:]:
        fill()


def _mark(spare_ref, token):
    if token is None:
        return
    reps = spare_ref.shape[0] // ROWS
    spare_ref[...] = jnp.concatenate([token] * reps, axis=0).astype(spare_ref.dtype)


def _release(mxu_pieces, vpu_pieces):
    token = None
    for k in range(max(len(mxu_pieces), len(vpu_pieces))):
        new_token = mxu_pieces[k]() if k < len(mxu_pieces) else token
        if k < len(vpu_pieces):
            vpu_pieces[k](token)
        token = new_token


def _perm_matrices(n):
    i = np.arange(n)
    src = (n // ROWS) * (i % ROWS) + i // ROWS
    pm = (src[:, None] == i[None, :]).astype(np.float32)
    return jnp.asarray(pm, BF16), jnp.asarray(pm.T, BF16)


UA_ROW0 = (LRU_CONV - 1) * ROWS


def _row_conv_matrix(n):
    cols = n // ROWS
    rows = (cols + 2 * SEG_PAD) * ROWS
    mat = np.zeros((rows, n), np.float32)
    for i in range(rows):
        q, s = divmod(i, ROWS)
        if SEG_PAD <= q < SEG_PAD + cols:
            mat[i, cols * s + q - SEG_PAD] = 1.0
    return jnp.asarray(mat, BF16)


def _short_convs(ua_s, a_prev, a_next, cw, cb, vc_s, n, width):
    sub = n // ROWS
    ext = LRU_CONV - 1
    row = lax.broadcasted_iota(jnp.int32, (ROWS, width), 0)
    cw = 0.5 * cw
    cb = 0.5 * cb

    def blk(p):
        return slice(p * ROWS, (p + 1) * ROWS)

    for k in range(1, ext + 1):
        tail = pltpu.roll(ua_s[blk(ext + sub - k), :width], 1, 0)
        ua_s[blk(ext - k), :width] = jnp.where(row == 0, a_prev[ext - k:ext - k + 1, :], tail)
    for k in range(ext):
        head = pltpu.roll(ua_s[blk(ext + k), :width], ROWS - 1, 0)
        ua_s[blk(ext + sub + k), :width] = jnp.where(row == ROWS - 1, a_next[k:k + 1, :], head)

    def chunk(ci, carry):
        base = ci * CHUNK
        for d in range(2):
            first = 0 if d == 0 else ext
            acc = jnp.broadcast_to(cb[d:d + 1, :], (CHUNK, width))
            for k in range(LRU_CONV):
                start = pl.multiple_of(base + (first + k) * ROWS, ROWS)
                acc = acc + cw[d, k:k + 1, :] * ua_s[pl.ds(start, CHUNK), :width]
            vc_s[d, pl.ds(pl.multiple_of(base, CHUNK), CHUNK), :] = acc
        return carry

    lax.fori_loop(0, n // CHUNK, chunk, 0)


def _gate_pieces(vc_s, wg_ref, g_s, width):
    half = width // 2

    def piece(d, hf):
        def run():
            res = _dot(vc_s[d, :, hf * half:(hf + 1) * half].astype(BF16), wg_ref[d, hf])
            g_s[d, :, hf * half:(hf + 1) * half] = res[:, :half]
            g_s[d, :, width + hf * half:width + (hf + 1) * half] = res[:, half:]
        return run

    return [piece(d, hf) for d in range(2) for hf in range(2)]


def _sublane_scan(a, h, row, reverse):
    for s in (1, 2, 4):
        if reverse:
            m, shift = row < ROWS - s, ROWS - s
        else:
            m, shift = row >= s, s
        a_sh = jnp.where(m, pltpu.roll(a, shift, 0), 1.0)
        h_sh = jnp.where(m, pltpu.roll(h, shift, 0), 0.0)
        h = h + a * h_sh
        a = a * a_sh
    return a, h


def _rglru_scan(ua_s, br, bi, lam, c0, vc_s, g_s, hl_s, al_s, po_s, q_s, n, width):
    sub = n // ROWS
    row = lax.broadcasted_iota(jnp.int32, (ROWS, width), 0)
    def rows_of(v, d):
        return jnp.broadcast_to(v[d:d + 1, :], (ROWS, width))

    spl_half = (-0.5 * LRU_C) * _softplus(-lam)
    spl_half = [rows_of(spl_half, d) for d in range(2)]
    br_half = [rows_of(0.5 * br, d) for d in range(2)]
    bi_half = [rows_of(0.5 * bi, d) for d in range(2)]

    def coeffs(d, r0):
        g = g_s[d, pl.ds(r0, ROWS), :]
        vh = vc_s[d, pl.ds(r0, ROWS), :]
        t_r = jnp.tanh(g[:, :width] + br_half[d])
        t_i = jnp.tanh(g[:, width:] + bi_half[d])
        la = spl_half[d] + spl_half[d] * t_r
        a = jnp.exp(la)
        x = jnp.tanh(la) * (-1.0 - a * a)
        mult = jnp.where(x > 0.0, x * lax.rsqrt(x), 0.0)
        return a, mult * (vh + vh * t_i)

    def local(j, carry):
        hf, af, hb, ab = carry
        r0 = pl.multiple_of(j * ROWS, ROWS)
        a, b = coeffs(0, r0)
        hf = a * hf + b
        af = a * af
        hl_s[0, pl.ds(r0, ROWS), :] = hf
        al_s[0, pl.ds(r0, ROWS), :] = af
        r1 = pl.multiple_of((sub - 1 - j) * ROWS, ROWS)
        a, b = coeffs(1, r1)
        hb = a * hb + b
        ab = a * ab
        hl_s[1, pl.ds(r1, ROWS), :] = hb
        al_s[1, pl.ds(r1, ROWS), :] = ab
        return hf, af, hb, ab

    zero = jnp.zeros((ROWS, width), F32)
    one = jnp.ones((ROWS, width), F32)
    hf, af, hb, ab = lax.fori_loop(0, sub, local, (zero, one, zero, one), unroll=4)

    af, hf = _sublane_scan(af, hf, row, reverse=False)
    end_f = hf + af * c0
    c_f = jnp.where(row == 0, c0, pltpu.roll(end_f, 1, 0))
    ab, hb = _sublane_scan(ab, hb, row, reverse=True)
    c_b = jnp.where(row == ROWS - 1, 0.0, pltpu.roll(hb, ROWS - 1, 0))
    c_a = jnp.where(row == ROWS - 1, 1.0, pltpu.roll(ab, ROWS - 1, 0))

    if po_s is not None:
        step = 2 * ROWS
        c_f, c_b, c_a = (jnp.concatenate([c, c], axis=0) for c in (c_f, c_b, c_a))

        def fix(j, carry):
            r0 = pl.ds(pl.multiple_of(j * step, step), step)
            a_b = al_s[1, r0, :]
            h = (hl_s[0, r0, :] + al_s[0, r0, :] * c_f) + (hl_s[1, r0, :] + a_b * c_b)
            sg = _silu(ua_s[pl.ds(pl.multiple_of(UA_ROW0 + j * step, ROWS), step), width:])
            po_s[r0, :] = (h * sg).astype(po_s.dtype)
            if q_s is not None:
                q_s[r0, :] = ((a_b * c_a) * sg).astype(q_s.dtype)
            return carry

        lax.fori_loop(0, n // step, fix, 0, unroll=4)

    return end_f[ROWS - 1:ROWS, :], hb[0:1, :], ab[0:1, :]


def _shifted_copy_pieces(vpad_ref, sh_ref):
    n = sh_ref.shape[1]

    def piece(j):
        def run():
            sh_ref[j] = vpad_ref[pl.ds(j, n), :]
        return run

    return [piece(j) for j in range(ROWS)]


def _time_conv_pieces(sh_ref, w_ref, ncol, bases, out_ref):
    def piece(ci, base):
        def run():
            acc = jnp.zeros((CHUNK, ncol), F32)
            for k in range(CONV_K):
                off = SEG_PAD - CONV_PAD + k
                start = base + (off // ROWS) * ROWS
                acc = acc + w_ref[k:k + 1, 0:ncol] * sh_ref[off % ROWS, start:start + CHUNK, :]
            out_ref[ci * CHUNK:(ci + 1) * CHUNK, 0:ncol] = acc
        return run

    return [piece(ci, base) for ci, base in enumerate(bases)]


def _mods_body(act_ref, w_ref, b_ref, o_ref):
    a = _silu(act_ref[...])
    o_ref[0] = jnp.dot(a, w_ref[0], precision=lax.Precision.HIGHEST,
                       preferred_element_type=F32) + b_ref[0]


def _mods_call(act, w_mod, b_mod):
    depth, d_model, d3 = w_mod.shape
    ncol = d_model
    return pl.pallas_call(
        _mods_body,
        grid=(depth, d3 // ncol),
        in_specs=[
            pl.BlockSpec((ROWS, d_model), lambda l, n: (0, 0)),
            pl.BlockSpec((1, d_model, ncol), lambda l, n: (l, 0, n)),
            pl.BlockSpec((1, 1, ncol), lambda l, n: (l, 0, n)),
        ],
        out_specs=pl.BlockSpec((1, ROWS, ncol), lambda l, n: (l, 0, n)),
        out_shape=jax.ShapeDtypeStruct((depth, ROWS, d3), F32),
        compiler_params=pltpu.CompilerParams(
            dimension_semantics=("arbitrary", "arbitrary"), vmem_limit_bytes=VMEM_LIMIT),
        name="mods",
    )(act, w_mod, b_mod.reshape(depth, 1, d3))


def _modulate(x, g, m, d_model):
    return _rms_norm(x, g) * (1.0 + m[:, d_model:2 * d_model]) + m[:, :d_model]


def _ctx_body(ctx_ref, mod_ref, gpre_ref, gpost_ref, win_ref, wo_ref, wg_ref, cw_ref, cb_ref,
              br_ref, bi_ref, lam_ref, dww_ref, dwb_ref, lng_ref, lnb_ref, pm_ref, pmt_ref,
              h0_ref,
              ua_s, vc_s, g_s, hl_s, al_s, po_s, vpad_s, sh_s, y_s,
              *, depth, d_model, width, n):
    xc = ctx_ref[0]
    zeros_pad = jnp.zeros((SEG_PAD, width), F32)
    no_rows = jnp.zeros((LRU_CONV - 1, width), F32)
    zero_state = jnp.zeros((1, width), F32)
    for l in range(depth):
        update = l < depth - 1
        m = mod_ref[l]
        hc = _modulate(xc, gpre_ref[l:l + 1, :], m, d_model).astype(BF16)
        hcp = _dot(pm_ref[...], hc).astype(BF16)
        if update:
            ua_s[UA_ROW0:UA_ROW0 + n, :] = _dot(hcp, win_ref[l, :, :2 * width])
            ub = _dot(hc, win_ref[l, :, 2 * width:])
        else:
            ua_s[UA_ROW0:UA_ROW0 + n, :width] = _dot(hcp, win_ref[l, :, :width])
        _short_convs(ua_s, no_rows, no_rows, cw_ref[l], cb_ref[l], vc_s, n, width)
        _interleave(_gate_pieces(vc_s, wg_ref.at[l], g_s, width), [])
        hf_last, hb_first, _ = _rglru_scan(
            ua_s, br_ref[l], bi_ref[l], lam_ref[l], zero_state, vc_s, g_s, hl_s, al_s,
            po_s if update else None, None, n, width)
        h0_ref[0, 2 * l:2 * l + 1, :] = hf_last
        h0_ref[0, 2 * l + 1:2 * l + 2, :] = hb_first
        if update:
            v = ub[:, :width] * jax.nn.sigmoid(ub[:, width:2 * width])
            vpad_s[0:SEG_PAD, :] = zeros_pad
            vpad_s[SEG_PAD:SEG_PAD + n, :] = v
            vpad_s[SEG_PAD + n:, :] = zeros_pad
            _interleave([], _shifted_copy_pieces(vpad_s, sh_s))
            _interleave([], _time_conv_pieces(
                sh_s, dww_ref.at[l], width, [ci * CHUNK for ci in range(n // CHUNK)], y_s))
            y = y_s[...] + dwb_ref[l:l + 1, :]
            conv = _silu(_layer_norm(y, lng_ref[l:l + 1, :], lnb_ref[l:l + 1, :]))
            mix_a = _dot(pmt_ref[...], po_s[...].astype(BF16)).astype(BF16)
            mix_b = (conv * _silu(ub[:, 2 * width:])).astype(BF16)
            mix = _dot(mix_a, wo_ref[l, :width, :]) + _dot(mix_b, wo_ref[l, width:, :])
            xc = xc + m[:, 2 * d_model:] * _rms_norm(mix, gpost_ref[l:l + 1, :])


def _ctx_call(ctx, mod_c, g_pre, g_post, w_in_bf, w_out_bf, wg, conv_a_w, conv_a_b, b_rgate,
              b_igate, lru_lambda, dw_w, dw_b, ln_g, ln_b):
    bsz, n, d_model = ctx.shape
    depth = w_in_bf.shape[0]
    width = conv_a_b.shape[-1]
    npad = n + 2 * SEG_PAD
    pm, pmt = _perm_matrices(n)

    def full(a):
        nd = a.ndim
        return pl.BlockSpec(a.shape, lambda b, _nd=nd: (0,) * _nd)

    params = (mod_c, g_pre, g_post, w_in_bf, w_out_bf, wg, conv_a_w, conv_a_b, b_rgate, b_igate,
              lru_lambda, dw_w, dw_b, ln_g, ln_b, pm, pmt)
    return pl.pallas_call(
        functools.partial(_ctx_body, depth=depth, d_model=d_model, width=width, n=n),
        grid=(bsz,),
        in_specs=[pl.BlockSpec((1, n, d_model), lambda b: (b, 0, 0))] + [full(a) for a in params],
        out_specs=pl.BlockSpec((1, 2 * depth, width), lambda b: (b, 0, 0)),
        out_shape=jax.ShapeDtypeStruct((bsz, 2 * depth, width), F32),
        scratch_shapes=[
            pltpu.VMEM((n + 2 * UA_ROW0, 2 * width), F32),
            pltpu.VMEM((2, n, width), F32),
            pltpu.VMEM((2, n, 2 * width), F32),
            pltpu.VMEM((2, n, width), F32),
            pltpu.VMEM((2, n, width), F32),
            pltpu.VMEM((n, width), F32),
            pltpu.VMEM((npad, width), F32),
            pltpu.VMEM((ROWS, npad - ROWS, width), F32),
            pltpu.VMEM((n, width), F32),
        ],
        compiler_params=pltpu.CompilerParams(
            dimension_semantics=("arbitrary",), vmem_limit_bytes=VMEM_LIMIT),
        name="context",
    )(ctx, *params)


def _proj_body(x_ref, mod_ref, g_ref, w_ref, hx_ref, v1_ref, v2_ref, *, l, nt, d_model, width):
    b = pl.program_id(0)
    j = pl.program_id(1)
    is_pad = jnp.logical_or(j == 0, j == nt + 1)

    @pl.when(is_pad)
    def _():
        v2_ref[...] = jnp.zeros(v2_ref.shape, v2_ref.dtype)

    @pl.when(jnp.logical_not(is_pad))
    def _():
        half = width // 2
        rows = ROWS_PER_CHAIN
        for c in range(x_ref.shape[1] // rows):
            rs = slice(c * rows, (c + 1) * rows)
            hb = _modulate(x_ref[0, rs, :], g_ref[l:l + 1, :], mod_ref[l, pl.ds(b, 1), :],
                           d_model).astype(BF16)
            hx_ref[0, rs, :] = hb
            u = _dot(hb, w_ref[0])
            v = u[:, :width] * jax.nn.sigmoid(u[:, width:])
            v1_ref[0, rs, :] = v[:, :half].astype(BF16)
            v2_ref[0, rs, :] = v[:, half:].astype(BF16)


def _proj_call(x, mods, g_pre, w_in_bf, l, width):
    bsz, seq, d_model = x.shape
    half = width // 2
    tt = TILE_PROJ
    nt = seq // tt

    def tok(b, j):
        return (b, jnp.clip(j - 1, 0, nt - 1), 0)

    return pl.pallas_call(
        functools.partial(_proj_body, l=l, nt=nt, d_model=d_model, width=width),
        grid=(bsz, nt + 2),
        in_specs=[
            pl.BlockSpec((1, tt, d_model), tok),
            pl.BlockSpec(mods.shape, lambda b, j: (0, 0, 0)),
            pl.BlockSpec(g_pre.shape, lambda b, j: (0, 0)),
            pl.BlockSpec((1, d_model, 2 * width), lambda b, j: (l, 0, 1)),
        ],
        out_specs=[
            pl.BlockSpec((1, tt, d_model), tok),
            pl.BlockSpec((1, tt, half), tok),
            pl.BlockSpec((1, tt, half), lambda b, j: (b, j, 0)),
        ],
        out_shape=[
            jax.ShapeDtypeStruct((bsz, seq, d_model), BF16),
            jax.ShapeDtypeStruct((bsz, seq, half), BF16),
            jax.ShapeDtypeStruct((bsz, seq + 2 * tt, half), BF16),
        ],
        compiler_params=pltpu.CompilerParams(
            dimension_semantics=("arbitrary", "arbitrary"), vmem_limit_bytes=VMEM_LIMIT),
        name="proj",
    )(x, mods, g_pre, w_in_bf)


def _mix_body(hx_ref, hxp_ref, hxn_ref, v1_ref, v2_ref, wa_ref, wbg_ref, wg_ref, cw_ref, cb_ref,
              br_ref, bi_ref, lam_ref, dww16_ref, dwb_ref, lng_ref, lnb_ref, pm_ref, pmt_ref,
              pme_ref, h0_ref, zero_ref,
              p_ref, q_ref, mb_ref, ab_ref, hbo_ref,
              hxs_s, ua_s, bg_s, vc_s, g_s, hl_s, al_s, carry_s, ev_s, od_s, wr_s, wb_s,
              yr_s, y_s,
              *, l, tt, nt, d_model, width, v2_front):
    i = pl.program_id(1)
    half = width // 2
    nseg = tt // GRID_W
    seg_rows = GRID_W + 2 * SEG_PAD
    ext = LRU_CONV - 1
    reach = CONV_PAD * GRID_W
    row_zero = zero_ref[0]

    @pl.when(i == 0)
    def _():
        carry_s[...] = h0_ref[pl.program_id(0), 2 * l:2 * l + 1, :]

    def col_slices(total):
        return [slice(c, c + MXU_N) for c in range(0, total, MXU_N)]

    def zero_based(start, align):
        return pl.multiple_of(row_zero + start, align)

    def prepare(token):
        for k in range(CONV_K):
            wr_s[k] = jnp.broadcast_to(dww16_ref[l, k:k + 1, 0:half], (2 * ROWS, half))
            wb_s[k] = jnp.broadcast_to(dww16_ref[l, k:k + 1, half:], (2 * ROWS, half))

    def spread_rows():
        res = _dot(pme_ref[...], v1_ref[0])
        ev_s[...] = res.astype(BF16)
        moved = jnp.concatenate([res[ROWS:, :], jnp.zeros((ROWS, half), F32)], axis=0)
        od_s[...] = moved.astype(BF16)
        return res[0:ROWS, 0:128]

    def row_conv(chunks):
        def run(token):
            _mark(wr_s.at[CONV_K, :, 0:128], token)
            for ci in chunks:
                for rt in range(CHUNK // (2 * ROWS)):
                    row0 = ci * CHUNK + rt * 2 * ROWS
                    acc = None
                    for g0 in range(0, CONV_K, TAP_GROUP):
                        part = None
                        for k in range(g0, min(g0 + TAP_GROUP, CONV_K)):
                            shift = SEG_PAD - CONV_PAD + k
                            src = row0 + shift * ROWS
                            if shift % 2 == 0:
                                window = ev_s[src:src + 2 * ROWS, :]
                            else:
                                window = od_s[src - ROWS:src + ROWS, :]
                            term = wr_s[row_zero + k] * window
                            part = term if part is None else part + term
                        part = part.astype(F32)
                        acc = part if acc is None else acc + part
                    yr_s[row0:row0 + 2 * ROWS, :] = acc
        return run

    def gather_rows():
        y_s[:, 0:half] = _dot(pmt_ref[...], yr_s[...].astype(BF16))
        return None

    def col_conv(chunks):
        def run(token):
            _mark(wb_s.at[CONV_K, :, 0:128], token)
            for ci in chunks:
                win0 = i * tt + (v2_front - reach) + ci * CHUNK
                for rt in range(CHUNK // (2 * ROWS)):
                    acc = None
                    for g0 in range(0, CONV_K, TAP_GROUP):
                        part = None
                        for k in range(g0, min(g0 + TAP_GROUP, CONV_K)):
                            start = pl.multiple_of(win0 + k * GRID_W + rt * 2 * ROWS, 2 * ROWS)
                            term = wb_s[row_zero + k] * v2_ref[0, pl.ds(start, 2 * ROWS), :]
                            part = term if part is None else part + term
                        part = part.astype(F32)
                        acc = part if acc is None else acc + part
                    row0 = ci * CHUNK + rt * 2 * ROWS
                    y_s[row0:row0 + 2 * ROWS, half:] = acc
        return run

    def permute(slices):
        def run():
            for cs in slices:
                res = _dot(pm_ref[...], hx_ref[0, :, cs])
                hxs_s[0:tt, cs] = res.astype(BF16)
            return res[0:ROWS, 0:128]
        return run

    def project_a(cs):
        def run():
            res = _dot(hxs_s[0:tt, :], wa_ref[0, :, cs])
            ua_s[UA_ROW0:UA_ROW0 + tt, cs] = res
            return res[0:ROWS, 0:128]
        return run

    def project_b(cs):
        def run():
            res = _dot(hx_ref[0], wbg_ref[0, :, cs])
            bg_s[0:tt, cs] = res
            return res[0:ROWS, 0:128]
        return run

    cols_d = col_slices(d_model)
    cols_a = col_slices(2 * width)
    cols_b = col_slices(width)
    quarter = nseg // 4
    row_stages = [row_conv(range(q * quarter, (q + 1) * quarter)) for q in range(4)]
    col_stages = [col_conv(range(q * 2 * quarter, (q + 1) * 2 * quarter)) for q in range(2)]
    _release(
        [spread_rows, permute(cols_d[:2]), permute(cols_d[2:]), project_b(cols_b[0]),
         project_b(cols_b[1])] + [project_a(cs) for cs in cols_a] + [gather_rows],
        [prepare] + row_stages + col_stages)

    wa_a = wa_ref[0, :, :width]
    a_prev = jnp.where(i > 0, _dot(hxp_ref[0], wa_a)[HALO - ext:, :], 0.0)
    a_next = jnp.where(i < nt - 1, _dot(hxn_ref[0], wa_a)[:ext, :], 0.0)
    _short_convs(ua_s, a_prev, a_next, cw_ref[l], cb_ref[l], vc_s, tt, width)

    def finish_conv(c, nrows):
        def run(token):
            _mark(bg_s.at[tt:tt + ROWS, 0:128], token)
            y = y_s[c * nrows:(c + 1) * nrows, :] + dwb_ref[l:l + 1, :]
            conv = _silu(_layer_norm(y, lng_ref[l:l + 1, :], lnb_ref[l:l + 1, :]))
            out = conv * _silu(bg_s[pl.ds(zero_based(c * nrows, ROWS), nrows), :])
            mb_ref[0, c * nrows:(c + 1) * nrows, :] = out.astype(BF16)
        return run

    def gate_piece(d, hf):
        def run():
            lo = hf * half
            res = _dot(vc_s[d, 0:tt, lo:lo + half].astype(BF16), wg_ref[0, d, hf])
            g_s[d, :, lo:lo + half] = res[:, :half]
            g_s[d, :, width + lo:width + lo + half] = res[:, half:]
            return res[0:ROWS, 0:128]
        return run

    _release([gate_piece(d, hf) for d in range(2) for hf in range(2)],
             [finish_conv(c, tt // 4) for c in range(4)])

    hf_last, hb_first, a_total = _rglru_scan(
        ua_s, br_ref[l], bi_ref[l], lam_ref[l], carry_s[...], vc_s, g_s, hl_s, al_s,
        p_ref.at[0], q_ref.at[0], tt, width)
    carry_s[...] = hf_last
    ab_ref[0, 0] = a_total
    hbo_ref[0, 0] = hb_first


def _mix_call(hx, v1, v2p, w_in_bf, wg, cw, cb, br, bi, lam, dww16, dwb, lng, lnb, h0, l):
    bsz, seq, d_model = hx.shape
    width = cb.shape[-1]
    half = width // 2
    tt = TILE_MIX
    nt = seq // tt
    hb_per_tile = tt // HALO
    n_halo_blocks = seq // HALO
    v2_front = (v2p.shape[1] - seq) // 2
    nseg = tt // GRID_W
    seg_rows = GRID_W + 2 * SEG_PAD
    reach = CONV_PAD * GRID_W
    pm, pmt = _perm_matrices(tt)
    pme = _row_conv_matrix(tt)

    def full(a):
        nd = a.ndim
        return pl.BlockSpec(a.shape, lambda b, i, _nd=nd: (0,) * _nd)

    small = (cw, cb, br, bi, lam, dww16, dwb, lng, lnb, pm, pmt, pme, h0)
    tile = lambda b, i: (b, i, 0)
    out_shapes = [jax.ShapeDtypeStruct((bsz, seq, width), BF16)] * 3 + [
        jax.ShapeDtypeStruct((bsz, nt, 1, width), F32)] * 2
    return pl.pallas_call(
        functools.partial(_mix_body, l=l, tt=tt, nt=nt, d_model=d_model, width=width,
                          v2_front=v2_front),
        grid=(bsz, nt),
        in_specs=[
            pl.BlockSpec((1, tt, d_model), tile),
            pl.BlockSpec((1, HALO, d_model),
                         lambda b, i: (b, jnp.maximum(i * hb_per_tile - 1, 0), 0)),
            pl.BlockSpec((1, HALO, d_model),
                         lambda b, i: (b, jnp.minimum((i + 1) * hb_per_tile, n_halo_blocks - 1), 0)),
            pl.BlockSpec((1, tt, half), tile),
            pl.BlockSpec((1, v2p.shape[1], half), lambda b, i: (b, 0, 0)),
            pl.BlockSpec((1, d_model, 2 * width), lambda b, i: (l, 0, 0)),
            pl.BlockSpec((1, d_model, width), lambda b, i: (l, 0, 4)),
            pl.BlockSpec((1,) + wg.shape[1:], lambda b, i: (l, 0, 0, 0, 0)),
        ] + [full(a) for a in small] + [pl.BlockSpec(memory_space=pltpu.SMEM)],
        out_specs=[pl.BlockSpec((1, tt, width), tile)] * 3 + [
            pl.BlockSpec((1, 1, 1, width), lambda b, i: (b, i, 0, 0))] * 2,
        out_shape=out_shapes,
        scratch_shapes=[
            pltpu.VMEM((tt + SPARE_ROWS, d_model), BF16),
            pltpu.VMEM((tt + 2 * UA_ROW0, 2 * width), F32),
            pltpu.VMEM((tt + ROWS, width), F32),
            pltpu.VMEM((2, tt + SPARE_ROWS, width), F32),
            pltpu.VMEM((2, tt, 2 * width), F32),
            pltpu.VMEM((2, tt, width), F32),
            pltpu.VMEM((2, tt, width), F32),
            pltpu.VMEM((1, width), F32),
            pltpu.VMEM(pme.shape[:1] + (half,), BF16),
            pltpu.VMEM(pme.shape[:1] + (half,), BF16),
            pltpu.VMEM((CONV_K + 1, 2 * ROWS, half), BF16),
            pltpu.VMEM((CONV_K + 1, 2 * ROWS, half), BF16),
            pltpu.VMEM((tt, half), F32),
            pltpu.VMEM((tt, width), F32),
        ],
        compiler_params=pltpu.CompilerParams(
            dimension_semantics=("arbitrary", "arbitrary"), vmem_limit_bytes=VMEM_LIMIT,
            ),
        name="mixer",
    )(hx, hx, hx, v1, v2p, w_in_bf, w_in_bf, wg, *small, jnp.zeros((1,), jnp.int32))


def _out_body(p_ref, q_ref, mb_ref, x_ref, wo_ref, gpost_ref, mod_ref, ab_ref, hb_ref, h0_ref,
              pmt_ref, *rest, l, d_model, width, fuse_next):
    if fuse_next:
        gpre_ref, wbn_ref, _, o_ref, hx_ref, v1_ref, v2_ref, carry_s = rest
    else:
        o_ref, carry_s = rest
    b = pl.program_id(0)

    @pl.when(pl.program_id(1) == 0)
    def _():
        carry_s[...] = h0_ref[b, 2 * l + 1:2 * l + 2, :]

    gate = mod_ref[l, pl.ds(b, 1), 2 * d_model:]
    rows = ROWS_PER_CHAIN
    half = width // 2
    tm = pmt_ref.shape[0]
    c = carry_s[...]
    for st in reversed(range(p_ref.shape[1] // tm)):
        t0 = st * tm
        mix_a = (p_ref[0, t0:t0 + tm, :].astype(F32)
                 + q_ref[0, t0:t0 + tm, :].astype(F32) * c).astype(BF16)
        mix_t = _dot(pmt_ref[...], mix_a).astype(BF16)
        c = hb_ref[0, st] + ab_ref[0, st] * c
        for ch in range(tm // rows):
            rs = slice(t0 + ch * rows, t0 + (ch + 1) * rows)
            lhs = jnp.concatenate([mix_t[ch * rows:(ch + 1) * rows, :], mb_ref[0, rs, :]], axis=1)
            mix = _dot(lhs, wo_ref[0])
            x_new = x_ref[0, rs, :] + gate * _rms_norm(mix, gpost_ref[l:l + 1, :])
            o_ref[0, rs, :] = x_new
            if fuse_next:
                hb = _modulate(x_new, gpre_ref[l + 1:l + 2, :], mod_ref[l + 1, pl.ds(b, 1), :],
                               d_model).astype(BF16)
                hx_ref[0, rs, :] = hb
                u = _dot(hb, wbn_ref[0])
                v = u[:, :width] * jax.nn.sigmoid(u[:, width:])
                v1_ref[0, rs, :] = v[:, :half].astype(BF16)
                v2_ref[0, rs, :] = v[:, half:].astype(BF16)
    carry_s[...] = c


def _out_call(p, q, mb, x, w_out_bf, g_post, mods, ab, hbo, h0, l, nxt=None):
    bsz, seq, d_model = x.shape
    width = p.shape[-1]
    half = width // 2
    tt = TILE_OUT
    tm = TILE_MIX
    nt = seq // tt
    per = tt // tm
    pad_blocks = TILE_PROJ // tt
    rev = lambda b, j: (b, nt - 1 - j, 0)
    _, pmt = _perm_matrices(tm)

    def full(a):
        nd = a.ndim
        return pl.BlockSpec(a.shape, lambda b, j, _nd=nd: (0,) * _nd)

    operands = [p, q, mb, x, w_out_bf, g_post, mods, ab, hbo, h0, pmt]
    in_specs = [
        pl.BlockSpec((1, tt, width), rev),
        pl.BlockSpec((1, tt, width), rev),
        pl.BlockSpec((1, tt, width), rev),
        pl.BlockSpec((1, tt, d_model), rev),
        pl.BlockSpec((1,) + w_out_bf.shape[1:], lambda b, j: (l, 0, 0)),
        full(g_post),
        full(mods),
        pl.BlockSpec((1, per, 1, width), lambda b, j: (b, nt - 1 - j, 0, 0)),
        pl.BlockSpec((1, per, 1, width), lambda b, j: (b, nt - 1 - j, 0, 0)),
        full(h0),
        full(pmt),
    ]
    out_specs = [pl.BlockSpec((1, tt, d_model), rev)]
    out_shape = [jax.ShapeDtypeStruct((bsz, seq, d_model), F32)]
    aliases = {}
    if nxt is not None:
        g_pre, w_in_bf, v2_buf = nxt
        operands += [g_pre, w_in_bf, v2_buf]
        in_specs += [
            full(g_pre),
            pl.BlockSpec((1, d_model, 2 * width), lambda b, j: (l + 1, 0, 1)),
            pl.BlockSpec(memory_space=pl.ANY),
        ]
        out_specs += [
            pl.BlockSpec((1, tt, d_model), rev),
            pl.BlockSpec((1, tt, half), rev),
            pl.BlockSpec((1, tt, half), lambda b, j: (b, nt - 1 - j + pad_blocks, 0)),
        ]
        out_shape += [
            jax.ShapeDtypeStruct((bsz, seq, d_model), BF16),
            jax.ShapeDtypeStruct((bsz, seq, half), BF16),
            jax.ShapeDtypeStruct(v2_buf.shape, BF16),
        ]
        aliases = {len(operands) - 1: 3}
    res = pl.pallas_call(
        functools.partial(_out_body, l=l, d_model=d_model, width=width,
                          fuse_next=nxt is not None),
        grid=(bsz, nt),
        in_specs=in_specs,
        out_specs=out_specs,
        out_shape=out_shape,
        input_output_aliases=aliases,
        scratch_shapes=[pltpu.VMEM((1, width), F32)],
        compiler_params=pltpu.CompilerParams(
            dimension_semantics=("arbitrary", "arbitrary"), vmem_limit_bytes=VMEM_LIMIT),
        name="out_proj" if nxt is not None else "out",
    )(*operands)
    return res if nxt is not None else res[0]


def _pack_gate_weights(w_r, w_i):
    depth, ndir, heads, hd, _ = w_r.shape
    hh = heads // 2
    eye = jnp.eye(hh, dtype=w_r.dtype)

    def bd(w):
        w = w.reshape(depth, ndir, 2, hh, hd, hd)
        return jnp.einsum("ldfhij,hg->ldfhigj", w, eye).reshape(depth, ndir, 2, hh * hd, hh * hd)

    return jnp.concatenate([bd(w_r), bd(w_i)], axis=-1).astype(BF16)


def kernel(x, c, ctx, c_ctx, w_mod, b_mod, g_pre, g_post, w_in, conv_a_w, conv_a_b, w_rgate,
           b_rgate, w_igate, b_igate, lru_lambda, dw_w, dw_b, ln_g, ln_b, w_out):
    bsz, seq, d_model = x.shape
    depth = w_mod.shape[0]
    width = conv_a_b.shape[-1]
    assert bsz + 1 <= ROWS and seq % TILE_PROJ == 0 and seq % TILE_MIX == 0
    assert TILE_MIX % GRID_W == 0 and TILE_PROJ >= CONV_PAD * GRID_W
    assert seq % TILE_OUT == 0 and TILE_OUT % TILE_MIX == 0 and TILE_PROJ % TILE_OUT == 0
    assert ctx.shape[1] % CHUNK == 0 and width % MXU_N == 0 and d_model % MXU_N == 0

    act = jnp.concatenate(
        [c, c_ctx[None, :], jnp.zeros((ROWS - bsz - 1, d_model), F32)], axis=0)
    mods = _mods_call(act, w_mod, b_mod)

    w_in_bf = w_in.astype(BF16)
    w_out_bf = w_out.astype(BF16)
    wg = _pack_gate_weights(w_rgate, w_igate)

    h0 = _ctx_call(ctx, mods[:, bsz:bsz + 1, :], g_pre, g_post, w_in_bf, w_out_bf, wg, conv_a_w,
                   conv_a_b, b_rgate, b_igate, lru_lambda, dw_w, dw_b, ln_g, ln_b)

    dw_w16 = dw_w.astype(BF16)
    hx, v1, v2p = _proj_call(x, mods, g_pre, w_in_bf, 0, width)
    for l in range(depth):
        p, q, mb, ab, hbo = _mix_call(
            hx, v1, v2p, w_in_bf, wg, conv_a_w, conv_a_b, b_rgate, b_igate, lru_lambda, dw_w16,
            dw_b, ln_g, ln_b, h0, l)
        if l + 1 < depth:
            x, hx, v1, v2p = _out_call(p, q, mb, x, w_out_bf, g_post, mods, ab, hbo, h0, l,
                                       nxt=(g_pre, w_in_bf, v2p))
        else:
            x = _out_call(p, q, mb, x, w_out_bf, g_post, mods, ab, hbo, h0, l)
    return x
```

```python
import functools

import jax
import jax.numpy as jnp
import numpy as np
from jax import lax
from jax.experimental import pallas as pl
from jax.experimental.pallas import tpu as pltpu

F32 = jnp.float32
BF16 = jnp.bfloat16

EPS = 1e-6
LRU_C = 8.0
LRU_CONV = 4
CONV_K = 31
CONV_PAD = CONV_K // 2
GRID_W = 64
HALO = 16
ROWS = 8
CHUNK = 64
SEG_PAD = 16
MXU_N = 256
TAP_GROUP = 16
ROWS_PER_CHAIN = 512

TILE_PROJ = 1024
TILE_MIX = 512
TILE_OUT = 1024
VMEM_LIMIT = 56 * 1024 * 1024


def _silu(x):
    return x * jax.nn.sigmoid(x)


def _softplus(x):
    return jnp.maximum(x, 0.0) + jnp.log1p(jnp.exp(-jnp.abs(x)))


def _rms_norm(x, g):
    ms = jnp.mean(x * x, axis=-1, keepdims=True)
    return x * lax.rsqrt(ms + EPS) * g


def _layer_norm(x, g, b):
    mu = jnp.mean(x, axis=-1, keepdims=True)
    xc = x - mu
    var = jnp.mean(xc * xc, axis=-1, keepdims=True)
    return xc * lax.rsqrt(var + EPS) * g + b


def _dot(a, b):
    return jnp.dot(a, b, preferred_element_type=F32)


def _mark(spare_ref, token):
    if token is None:
        return
    reps = spare_ref.shape[0] // ROWS
    spare_ref[...] = jnp.concatenate([token] * reps, axis=0).astype(spare_ref.dtype)


def _release(mxu_pieces, vpu_pieces):
    token = None
    for k in range(max(len(mxu_pieces), len(vpu_pieces))):
        new_token = mxu_pieces[k]() if k < len(mxu_pieces) else token
        if k < len(vpu_pieces):
            vpu_pieces[k](token)
        token = new_token


def _perm_matrices(n):
    i = np.arange(n)
    src = (n // ROWS) * (i % ROWS) + i // ROWS
    pm = (src[:, None] == i[None, :]).astype(np.float32)
    return jnp.asarray(pm, BF16), jnp.asarray(pm.T, BF16)


UA_ROW0 = (LRU_CONV - 1) * ROWS


def _row_conv_matrix(n):
    cols = n // ROWS
    rows = (cols + 2 * SEG_PAD) * ROWS
    mat = np.zeros((rows, n), np.float32)
    for i in range(rows):
        q, s = divmod(i, ROWS)
        if SEG_PAD <= q < SEG_PAD + cols:
            mat[i, cols * s + q - SEG_PAD] = 1.0
    return jnp.asarray(mat, BF16)


def _short_convs(ua_s, a_prev, a_next, cw, cb, vc_s, n, width):
    sub = n // ROWS
    ext = LRU_CONV - 1
    row = lax.broadcasted_iota(jnp.int32, (ROWS, width), 0)
    cw = 0.5 * cw
    cb = 0.5 * cb

    def blk(p):
        return slice(p * ROWS, (p + 1) * ROWS)

    for k in range(1, ext + 1):
        tail = pltpu.roll(ua_s[blk(ext + sub - k), :width], 1, 0)
        ua_s[blk(ext - k), :width] = jnp.where(row == 0, a_prev[ext - k:ext - k + 1, :], tail)
    for k in range(ext):
        head = pltpu.roll(ua_s[blk(ext + k), :width], ROWS - 1, 0)
        ua_s[blk(ext + sub + k), :width] = jnp.where(row == ROWS - 1, a_next[k:k + 1, :], head)

    def chunk(ci, carry):
        base = ci * CHUNK
        for d in range(2):
            first = 0 if d == 0 else ext
            acc = jnp.broadcast_to(cb[d:d + 1, :], (CHUNK, width))
            for k in range(LRU_CONV):
                start = pl.multiple_of(base + (first + k) * ROWS, ROWS)
                acc = acc + cw[d, k:k + 1, :] * ua_s[pl.ds(start, CHUNK), :width]
            vc_s[d, pl.ds(pl.multiple_of(base, CHUNK), CHUNK), :] = acc
        return carry

    lax.fori_loop(0, n // CHUNK, chunk, 0)


def _gate_pieces(vc_s, wg_ref, g_s, width):
    half = width // 2

    def piece(d, hf):
        def run():
            res = _dot(vc_s[d, :, hf * half:(hf + 1) * half].astype(BF16), wg_ref[d, hf])
            g_s[d, :, hf * half:(hf + 1) * half] = res[:, :half]
            g_s[d, :, width + hf * half:width + (hf + 1) * half] = res[:, half:]
        return run

    return [piece(d, hf) for d in range(2) for hf in range(2)]


def _sublane_scan(a, h, row, reverse):
    for s in (1, 2, 4):
        if reverse:
            m, shift = row < ROWS - s, ROWS - s
        else:
            m, shift = row >= s, s
        a_sh = jnp.where(m, pltpu.roll(a, shift, 0), 1.0)
        h_sh = jnp.where(m, pltpu.roll(h, shift, 0), 0.0)
        h = h + a * h_sh
        a = a * a_sh
    return a, h


def _rglru_scan(ua_s, br, bi, lam, c0, vc_s, g_s, hl_s, al_s, po_s, q_s, n, width):
    sub = n // ROWS
    row = lax.broadcasted_iota(jnp.int32, (ROWS, width), 0)
    spl_half = (-0.5 * LRU_C) * _softplus(-lam)

    def scan_columns(cols):
        nc = cols.stop - cols.start
        gate_i = slice(width + cols.start, width + cols.stop)

        def rows_of(v, d):
            return jnp.broadcast_to(v[d:d + 1, cols], (ROWS, nc))

        spl_c = [rows_of(spl_half, d) for d in range(2)]
        br_c = [rows_of(0.5 * br, d) for d in range(2)]
        bi_c = [rows_of(0.5 * bi, d) for d in range(2)]

        def coeffs(d, r0):
            vh = vc_s[d, pl.ds(r0, ROWS), cols]
            t_r = jnp.tanh(g_s[d, pl.ds(r0, ROWS), cols] + br_c[d])
            t_i = jnp.tanh(g_s[d, pl.ds(r0, ROWS), gate_i] + bi_c[d])
            la = spl_c[d] + spl_c[d] * t_r
            a = jnp.exp(la)
            x = jnp.tanh(la) * (-1.0 - a * a)
            mult = jnp.where(x > 0.0, x * lax.rsqrt(x), 0.0)
            return a, mult * (vh + vh * t_i)

        def local(j, carry):
            hf, af, hb, ab = carry
            r0 = pl.multiple_of(j * ROWS, ROWS)
            a, b = coeffs(0, r0)
            hf = a * hf + b
            af = a * af
            hl_s[0, pl.ds(r0, ROWS), cols] = hf
            al_s[0, pl.ds(r0, ROWS), cols] = af
            r1 = pl.multiple_of((sub - 1 - j) * ROWS, ROWS)
            a, b = coeffs(1, r1)
            hb = a * hb + b
            ab = a * ab
            hl_s[1, pl.ds(r1, ROWS), cols] = hb
            al_s[1, pl.ds(r1, ROWS), cols] = ab
            return hf, af, hb, ab

        zero = jnp.zeros((ROWS, nc), F32)
        one = jnp.ones((ROWS, nc), F32)
        return lax.fori_loop(0, sub, local, (zero, one, zero, one), unroll=8)

    halves = [scan_columns(slice(c, c + width // 2)) for c in (0, width // 2)]
    hf, af, hb, ab = (jnp.concatenate(parts, axis=1) for parts in zip(*halves))

    af, hf = _sublane_scan(af, hf, row, reverse=False)
    end_f = hf + af * c0
    c_f = jnp.where(row == 0, c0, pltpu.roll(end_f, 1, 0))
    ab, hb = _sublane_scan(ab, hb, row, reverse=True)
    c_b = jnp.where(row == ROWS - 1, 0.0, pltpu.roll(hb, ROWS - 1, 0))
    c_a = jnp.where(row == ROWS - 1, 1.0, pltpu.roll(ab, ROWS - 1, 0))

    if po_s is not None:
        step = 2 * ROWS
        c_f, c_b, c_a = (jnp.concatenate([c, c], axis=0) for c in (c_f, c_b, c_a))

        def fix(j, carry):
            r0 = pl.ds(pl.multiple_of(j * step, step), step)
            a_b = al_s[1, r0, :]
            h = (hl_s[0, r0, :] + al_s[0, r0, :] * c_f) + (hl_s[1, r0, :] + a_b * c_b)
            sg = _silu(ua_s[pl.ds(pl.multiple_of(UA_ROW0 + j * step, ROWS), step), width:])
            po_s[r0, :] = (h * sg).astype(po_s.dtype)
            if q_s is not None:
                q_s[r0, :] = ((a_b * c_a) * sg).astype(q_s.dtype)
            return carry

        lax.fori_loop(0, n // step, fix, 0, unroll=4)

    return end_f[ROWS - 1:ROWS, :], hb[0:1, :], ab[0:1, :]


def _shifted_copy_pieces(vpad_ref, sh_ref):
    n = sh_ref.shape[1]

    def piece(j):
        def run():
            sh_ref[j] = vpad_ref[pl.ds(j, n), :]
        return run

    return [piece(j) for j in range(ROWS)]


def _time_conv_pieces(sh_ref, w_ref, ncol, bases, out_ref):
    def piece(ci, base):
        def run():
            acc = jnp.zeros((CHUNK, ncol), F32)
            for k in range(CONV_K):
                off = SEG_PAD - CONV_PAD + k
                start = base + (off // ROWS) * ROWS
                acc = acc + w_ref[k:k + 1, 0:ncol] * sh_ref[off % ROWS, start:start + CHUNK, :]
            out_ref[ci * CHUNK:(ci + 1) * CHUNK, 0:ncol] = acc
        return run

    return [piece(ci, base) for ci, base in enumerate(bases)]


def _mods_body(act_ref, w_ref, b_ref, o_ref):
    a = _silu(act_ref[...])
    w = w_ref[0]
    a_hi = a.astype(BF16)
    a_lo = (a - a_hi.astype(F32)).astype(BF16)
    w_hi = w.astype(BF16)
    w_lo = (w - w_hi.astype(F32)).astype(BF16)
    o_ref[0] = _dot(a_hi, w_hi) + (_dot(a_hi, w_lo) + _dot(a_lo, w_hi)) + b_ref[0]


def _mods_call(act, w_mod, b_mod):
    depth, d_model, d3 = w_mod.shape
    ncol = d_model
    return pl.pallas_call(
        _mods_body,
        grid=(depth, d3 // ncol),
        in_specs=[
            pl.BlockSpec((ROWS, d_model), lambda l, n: (0, 0)),
            pl.BlockSpec((1, d_model, ncol), lambda l, n: (l, 0, n)),
            pl.BlockSpec((1, 1, ncol), lambda l, n: (l, 0, n)),
        ],
        out_specs=pl.BlockSpec((1, ROWS, ncol), lambda l, n: (l, 0, n)),
        out_shape=jax.ShapeDtypeStruct((depth, ROWS, d3), F32),
        compiler_params=pltpu.CompilerParams(
            dimension_semantics=("arbitrary", "arbitrary"), vmem_limit_bytes=VMEM_LIMIT),
        name="mods",
    )(act, w_mod, b_mod.reshape(depth, 1, d3))


def _modulate(x, g, m, d_model):
    return _rms_norm(x, g) * (1.0 + m[:, d_model:2 * d_model]) + m[:, :d_model]


def _ctx_body(ctx_ref, mod_ref, gpre_ref, gpost_ref, win_ref, wo_ref, wg_ref, cw_ref, cb_ref,
              br_ref, bi_ref, lam_ref, dww_ref, dwb_ref, lng_ref, lnb_ref, pm_ref, pmt_ref,
              h0_ref,
              ua_s, vc_s, g_s, hl_s, al_s, po_s, vpad_s, sh_s, y_s,
              *, depth, d_model, width, n):
    xc = ctx_ref[0]
    zeros_pad = jnp.zeros((SEG_PAD, width), F32)
    no_rows = jnp.zeros((LRU_CONV - 1, width), F32)
    zero_state = jnp.zeros((1, width), F32)
    for l in range(depth):
        update = l < depth - 1
        m = mod_ref[l]
        hc = _modulate(xc, gpre_ref[l:l + 1, :], m, d_model).astype(BF16)
        hcp = _dot(pm_ref[...], hc).astype(BF16)
        if update:
            ua_s[UA_ROW0:UA_ROW0 + n, :] = _dot(hcp, win_ref[l, :, :2 * width])
            ub = _dot(hc, win_ref[l, :, 2 * width:])
        else:
            ua_s[UA_ROW0:UA_ROW0 + n, :width] = _dot(hcp, win_ref[l, :, :width])
        _short_convs(ua_s, no_rows, no_rows, cw_ref[l], cb_ref[l], vc_s, n, width)
        for piece in _gate_pieces(vc_s, wg_ref.at[l], g_s, width):
            piece()
        hf_last, hb_first, _ = _rglru_scan(
            ua_s, br_ref[l], bi_ref[l], lam_ref[l], zero_state, vc_s, g_s, hl_s, al_s,
            po_s if update else None, None, n, width)
        h0_ref[0, 2 * l:2 * l + 1, :] = hf_last
        h0_ref[0, 2 * l + 1:2 * l + 2, :] = hb_first
        if update:
            v = ub[:, :width] * jax.nn.sigmoid(ub[:, width:2 * width])
            vpad_s[0:SEG_PAD, :] = zeros_pad
            vpad_s[SEG_PAD:SEG_PAD + n, :] = v
            vpad_s[SEG_PAD + n:, :] = zeros_pad
            for piece in _shifted_copy_pieces(vpad_s, sh_s) + _time_conv_pieces(
                    sh_s, dww_ref.at[l], width, [ci * CHUNK for ci in range(n // CHUNK)], y_s):
                piece()
            y = y_s[...] + dwb_ref[l:l + 1, :]
            conv = _silu(_layer_norm(y, lng_ref[l:l + 1, :], lnb_ref[l:l + 1, :]))
            mix_a = _dot(pmt_ref[...], po_s[...].astype(BF16)).astype(BF16)
            mix_b = (conv * _silu(ub[:, 2 * width:])).astype(BF16)
            mix = _dot(mix_a, wo_ref[l, :width, :]) + _dot(mix_b, wo_ref[l, width:, :])
            xc = xc + m[:, 2 * d_model:] * _rms_norm(mix, gpost_ref[l:l + 1, :])


def _ctx_call(ctx, mod_c, g_pre, g_post, w_in_bf, w_out_bf, wg, conv_a_w, conv_a_b, b_rgate,
              b_igate, lru_lambda, dw_w, dw_b, ln_g, ln_b):
    bsz, n, d_model = ctx.shape
    depth = w_in_bf.shape[0]
    width = conv_a_b.shape[-1]
    npad = n + 2 * SEG_PAD
    pm, pmt = _perm_matrices(n)

    def full(a):
        nd = a.ndim
        return pl.BlockSpec(a.shape, lambda b, _nd=nd: (0,) * _nd)

    params = (mod_c, g_pre, g_post, w_in_bf, w_out_bf, wg, conv_a_w, conv_a_b, b_rgate, b_igate,
              lru_lambda, dw_w, dw_b, ln_g, ln_b, pm, pmt)
    return pl.pallas_call(
        functools.partial(_ctx_body, depth=depth, d_model=d_model, width=width, n=n),
        grid=(bsz,),
        in_specs=[pl.BlockSpec((1, n, d_model), lambda b: (b, 0, 0))] + [full(a) for a in params],
        out_specs=pl.BlockSpec((1, 2 * depth, width), lambda b: (b, 0, 0)),
        out_shape=jax.ShapeDtypeStruct((bsz, 2 * depth, width), F32),
        scratch_shapes=[
            pltpu.VMEM((n + 2 * UA_ROW0, 2 * width), F32),
            pltpu.VMEM((2, n, width), F32),
            pltpu.VMEM((2, n, 2 * width), F32),
            pltpu.VMEM((2, n, width), F32),
            pltpu.VMEM((2, n, width), F32),
            pltpu.VMEM((n, width), F32),
            pltpu.VMEM((npad, width), F32),
            pltpu.VMEM((ROWS, npad - ROWS, width), F32),
            pltpu.VMEM((n, width), F32),
        ],
        compiler_params=pltpu.CompilerParams(
            dimension_semantics=("arbitrary",), vmem_limit_bytes=VMEM_LIMIT),
        name="context",
    )(ctx, *params)


def _proj_body(x_ref, mod_ref, g_ref, w_ref, hx_ref, v1_ref, v2_ref, *, l, nt, d_model, width):
    b = pl.program_id(0)
    j = pl.program_id(1)
    is_pad = jnp.logical_or(j == 0, j == nt + 1)

    @pl.when(is_pad)
    def _():
        v2_ref[...] = jnp.zeros(v2_ref.shape, v2_ref.dtype)

    @pl.when(jnp.logical_not(is_pad))
    def _():
        half = width // 2
        rows = ROWS_PER_CHAIN
        for c in range(x_ref.shape[1] // rows):
            rs = slice(c * rows, (c + 1) * rows)
            hb = _modulate(x_ref[0, rs, :], g_ref[l:l + 1, :], mod_ref[l, pl.ds(b, 1), :],
                           d_model).astype(BF16)
            hx_ref[0, rs, :] = hb
            u = _dot(hb, w_ref[0])
            v = u[:, :width] * jax.nn.sigmoid(u[:, width:])
            v1_ref[0, rs, :] = v[:, :half].astype(BF16)
            v2_ref[0, rs, :] = v[:, half:].astype(BF16)


def _proj_call(x, mods, g_pre, w_in_bf, l, width):
    bsz, seq, d_model = x.shape
    half = width // 2
    tt = TILE_PROJ
    nt = seq // tt

    def tok(b, j):
        return (b, jnp.clip(j - 1, 0, nt - 1), 0)

    return pl.pallas_call(
        functools.partial(_proj_body, l=l, nt=nt, d_model=d_model, width=width),
        grid=(bsz, nt + 2),
        in_specs=[
            pl.BlockSpec((1, tt, d_model), tok),
            pl.BlockSpec(mods.shape, lambda b, j: (0, 0, 0)),
            pl.BlockSpec(g_pre.shape, lambda b, j: (0, 0)),
            pl.BlockSpec((1, d_model, 2 * width), lambda b, j: (l, 0, 1)),
        ],
        out_specs=[
            pl.BlockSpec((1, tt, d_model), tok),
            pl.BlockSpec((1, tt, half), tok),
            pl.BlockSpec((1, tt, half), lambda b, j: (b, j, 0)),
        ],
        out_shape=[
            jax.ShapeDtypeStruct((bsz, seq, d_model), BF16),
            jax.ShapeDtypeStruct((bsz, seq, half), BF16),
            jax.ShapeDtypeStruct((bsz, seq + 2 * tt, half), BF16),
        ],
        compiler_params=pltpu.CompilerParams(
            dimension_semantics=("arbitrary", "arbitrary"), vmem_limit_bytes=VMEM_LIMIT),
        name="proj",
    )(x, mods, g_pre, w_in_bf)


def _mix_body(hx_ref, hxp_ref, hxn_ref, v1_ref, v2_ref, wa_ref, wbg_ref, wg_ref, cw_ref, cb_ref,
              br_ref, bi_ref, lam_ref, dww16_ref, dwb_ref, lng_ref, lnb_ref, pm_ref, pmt_ref,
              pme_ref, h0_ref, zero_ref,
              p_ref, q_ref, mb_ref, ab_ref, hbo_ref,
              hxs_s, ua_s, bg_s, vc_s, g_s, hl_s, al_s, carry_s, ev_s, od_s, wr_s, wb_s,
              yr_s, y_s,
              *, l, tt, nt, d_model, width, v2_front):
    i = pl.program_id(1)
    half = width // 2
    nseg = tt // GRID_W
    ext = LRU_CONV - 1
    reach = CONV_PAD * GRID_W
    row_zero = zero_ref[0]

    @pl.when(i == 0)
    def _():
        carry_s[...] = h0_ref[pl.program_id(0), 2 * l:2 * l + 1, :]

    def col_slices(total):
        return [slice(c, c + MXU_N) for c in range(0, total, MXU_N)]

    def zero_based(start, align):
        return pl.multiple_of(row_zero + start, align)

    def prepare(token):
        for k in range(CONV_K):
            wr_s[k] = jnp.broadcast_to(dww16_ref[l, k:k + 1, 0:half], (2 * ROWS, half))
            wb_s[k] = jnp.broadcast_to(dww16_ref[l, k:k + 1, half:], (2 * ROWS, half))

    def spread_rows():
        res = _dot(pme_ref[...], v1_ref[0])
        ev_s[...] = res.astype(BF16)
        moved = jnp.concatenate([res[ROWS:, :], jnp.zeros((ROWS, half), F32)], axis=0)
        od_s[...] = moved.astype(BF16)
        return res[0:ROWS, 0:128]

    def row_conv(chunks):
        def run(token):
            _mark(wr_s.at[CONV_K, :, 0:128], token)
            for ci in chunks:
                for rt in range(CHUNK // (2 * ROWS)):
                    row0 = ci * CHUNK + rt * 2 * ROWS
                    acc = None
                    for g0 in range(0, CONV_K, TAP_GROUP):
                        part = None
                        for k in range(g0, min(g0 + TAP_GROUP, CONV_K)):
                            shift = SEG_PAD - CONV_PAD + k
                            src = row0 + shift * ROWS
                            if shift % 2 == 0:
                                window = ev_s[src:src + 2 * ROWS, :]
                            else:
                                window = od_s[src - ROWS:src + ROWS, :]
                            term = wr_s[row_zero + k] * window
                            part = term if part is None else part + term
                        part = part.astype(F32)
                        acc = part if acc is None else acc + part
                    yr_s[row0:row0 + 2 * ROWS, :] = acc
        return run

    def gather_rows():
        y_s[:, 0:half] = _dot(pmt_ref[...], yr_s[...].astype(BF16))
        return None

    def col_conv(chunks):
        def run(token):
            _mark(wb_s.at[CONV_K, :, 0:128], token)
            for ci in chunks:
                win0 = i * tt + (v2_front - reach) + ci * CHUNK
                for rt in range(CHUNK // (2 * ROWS)):
                    acc = None
                    for g0 in range(0, CONV_K, TAP_GROUP):
                        part = None
                        for k in range(g0, min(g0 + TAP_GROUP, CONV_K)):
                            start = pl.multiple_of(win0 + k * GRID_W + rt * 2 * ROWS, 2 * ROWS)
                            term = wb_s[row_zero + k] * v2_ref[0, pl.ds(start, 2 * ROWS), :]
                            part = term if part is None else part + term
                        part = part.astype(F32)
                        acc = part if acc is None else acc + part
                    row0 = ci * CHUNK + rt * 2 * ROWS
                    y_s[row0:row0 + 2 * ROWS, half:] = acc
        return run

    def permute(slices):
        def run():
            for cs in slices:
                res = _dot(pm_ref[...], hx_ref[0, :, cs])
                hxs_s[0:tt, cs] = res.astype(BF16)
            return res[0:ROWS, 0:128]
        return run

    def project_a(cs):
        def run():
            res = _dot(hxs_s[0:tt, :], wa_ref[0, :, cs])
            ua_s[UA_ROW0:UA_ROW0 + tt, cs] = res
            return res[0:ROWS, 0:128]
        return run

    def project_b(cs):
        def run():
            res = _dot(hx_ref[0], wbg_ref[0, :, cs])
            bg_s[0:tt, cs] = res
            return res[0:ROWS, 0:128]
        return run

    cols_d = col_slices(d_model)
    cols_a = col_slices(2 * width)
    cols_b = col_slices(width)
    quarter = nseg // 4
    row_stages = [row_conv(range(q * quarter, (q + 1) * quarter)) for q in range(4)]
    col_stages = [col_conv(range(q * 2 * quarter, (q + 1) * 2 * quarter)) for q in range(2)]
    _release(
        [spread_rows, permute(cols_d[:2]), permute(cols_d[2:]), project_b(cols_b[0]),
         project_b(cols_b[1])] + [project_a(cs) for cs in cols_a] + [gather_rows],
        [prepare] + row_stages + col_stages)

    wa_a = wa_ref[0, :, :width]
    a_prev = jnp.where(i > 0, _dot(hxp_ref[0], wa_a)[HALO - ext:, :], 0.0)
    a_next = jnp.where(i < nt - 1, _dot(hxn_ref[0], wa_a)[:ext, :], 0.0)
    _short_convs(ua_s, a_prev, a_next, cw_ref[l], cb_ref[l], vc_s, tt, width)

    def finish_conv(c, nrows):
        def run(token):
            _mark(bg_s.at[tt:tt + ROWS, 0:128], token)
            y = y_s[c * nrows:(c + 1) * nrows, :] + dwb_ref[l:l + 1, :]
            conv = _silu(_layer_norm(y, lng_ref[l:l + 1, :], lnb_ref[l:l + 1, :]))
            out = conv * _silu(bg_s[pl.ds(zero_based(c * nrows, ROWS), nrows), :])
            mb_ref[0, c * nrows:(c + 1) * nrows, :] = out.astype(BF16)
        return run

    def gate_piece(d, hf):
        def run():
            lo = hf * half
            res = _dot(vc_s[d, 0:tt, lo:lo + half].astype(BF16), wg_ref[0, d, hf])
            g_s[d, :, lo:lo + half] = res[:, :half]
            g_s[d, :, width + lo:width + lo + half] = res[:, half:]
            return res[0:ROWS, 0:128]
        return run

    _release([gate_piece(d, hf) for d in range(2) for hf in range(2)],
             [finish_conv(c, tt // 4) for c in range(4)])

    hf_last, hb_first, a_total = _rglru_scan(
        ua_s, br_ref[l], bi_ref[l], lam_ref[l], carry_s[...], vc_s, g_s, hl_s, al_s,
        p_ref.at[0], q_ref.at[0], tt, width)
    carry_s[...] = hf_last
    ab_ref[0, 0] = a_total
    hbo_ref[0, 0] = hb_first


def _mix_call(hx, v1, v2p, w_in_bf, wg, cw, cb, br, bi, lam, dww16, dwb, lng, lnb, h0, l):
    bsz, seq, d_model = hx.shape
    width = cb.shape[-1]
    half = width // 2
    tt = TILE_MIX
    nt = seq // tt
    hb_per_tile = tt // HALO
    n_halo_blocks = seq // HALO
    v2_front = (v2p.shape[1] - seq) // 2
    pm, pmt = _perm_matrices(tt)
    pme = _row_conv_matrix(tt)

    def full(a):
        nd = a.ndim
        return pl.BlockSpec(a.shape, lambda b, i, _nd=nd: (0,) * _nd)

    small = (cw, cb, br, bi, lam, dww16, dwb, lng, lnb, pm, pmt, pme, h0)
    tile = lambda b, i: (b, i, 0)
    out_shapes = [jax.ShapeDtypeStruct((bsz, seq, width), BF16)] * 3 + [
        jax.ShapeDtypeStruct((bsz, nt, 1, width), F32)] * 2
    return pl.pallas_call(
        functools.partial(_mix_body, l=l, tt=tt, nt=nt, d_model=d_model, width=width,
                          v2_front=v2_front),
        grid=(bsz, nt),
        in_specs=[
            pl.BlockSpec((1, tt, d_model), tile),
            pl.BlockSpec((1, HALO, d_model),
                         lambda b, i: (b, jnp.maximum(i * hb_per_tile - 1, 0), 0)),
            pl.BlockSpec((1, HALO, d_model),
                         lambda b, i: (b, jnp.minimum((i + 1) * hb_per_tile, n_halo_blocks - 1), 0)),
            pl.BlockSpec((1, tt, half), tile),
            pl.BlockSpec((1, v2p.shape[1], half), lambda b, i: (b, 0, 0)),
            pl.BlockSpec((1, d_model, 2 * width), lambda b, i: (l, 0, 0)),
            pl.BlockSpec((1, d_model, width), lambda b, i: (l, 0, 4)),
            pl.BlockSpec((1,) + wg.shape[1:], lambda b, i: (l, 0, 0, 0, 0)),
        ] + [full(a) for a in small] + [pl.BlockSpec(memory_space=pltpu.SMEM)],
        out_specs=[pl.BlockSpec((1, tt, width), tile)] * 3 + [
            pl.BlockSpec((1, 1, 1, width), lambda b, i: (b, i, 0, 0))] * 2,
        out_shape=out_shapes,
        scratch_shapes=[
            pltpu.VMEM((tt, d_model), BF16),
            pltpu.VMEM((tt + 2 * UA_ROW0, 2 * width), F32),
            pltpu.VMEM((tt + ROWS, width), F32),
            pltpu.VMEM((2, tt, width), F32),
            pltpu.VMEM((2, tt, 2 * width), F32),
            pltpu.VMEM((2, tt, width), F32),
            pltpu.VMEM((2, tt, width), F32),
            pltpu.VMEM((1, width), F32),
            pltpu.VMEM(pme.shape[:1] + (half,), BF16),
            pltpu.VMEM(pme.shape[:1] + (half,), BF16),
            pltpu.VMEM((CONV_K + 1, 2 * ROWS, half), BF16),
            pltpu.VMEM((CONV_K + 1, 2 * ROWS, half), BF16),
            pltpu.VMEM((tt, half), F32),
            pltpu.VMEM((tt, width), F32),
        ],
        compiler_params=pltpu.CompilerParams(
            dimension_semantics=("arbitrary", "arbitrary"), vmem_limit_bytes=VMEM_LIMIT,
            ),
        name="mixer",
    )(hx, hx, hx, v1, v2p, w_in_bf, w_in_bf, wg, *small, jnp.zeros((1,), jnp.int32))


def _out_body(p_ref, q_ref, mb_ref, x_ref, wo_ref, gpost_ref, mod_ref, ab_ref, hb_ref, h0_ref,
              pmt_ref, *rest, l, d_model, width, fuse_next):
    if fuse_next:
        gpre_ref, wbn_ref, _, o_ref, hx_ref, v1_ref, v2_ref, carry_s = rest
    else:
        o_ref, carry_s = rest
    b = pl.program_id(0)

    @pl.when(pl.program_id(1) == 0)
    def _():
        carry_s[...] = h0_ref[b, 2 * l + 1:2 * l + 2, :]

    gate = mod_ref[l, pl.ds(b, 1), 2 * d_model:]
    rows = ROWS_PER_CHAIN
    half = width // 2
    tm = pmt_ref.shape[0]
    c = carry_s[...]
    for st in reversed(range(p_ref.shape[1] // tm)):
        t0 = st * tm
        mix_a = (p_ref[0, t0:t0 + tm, :].astype(F32)
                 + q_ref[0, t0:t0 + tm, :].astype(F32) * c).astype(BF16)
        mix_t = _dot(pmt_ref[...], mix_a).astype(BF16)
        c = hb_ref[0, st] + ab_ref[0, st] * c
        for ch in range(tm // rows):
            rs = slice(t0 + ch * rows, t0 + (ch + 1) * rows)
            lhs = jnp.concatenate([mix_t[ch * rows:(ch + 1) * rows, :], mb_ref[0, rs, :]], axis=1)
            mix = _dot(lhs, wo_ref[0])
            x_new = x_ref[0, rs, :] + gate * _rms_norm(mix, gpost_ref[l:l + 1, :])
            o_ref[0, rs, :] = x_new
            if fuse_next:
                hb = _modulate(x_new, gpre_ref[l + 1:l + 2, :], mod_ref[l + 1, pl.ds(b, 1), :],
                               d_model).astype(BF16)
                hx_ref[0, rs, :] = hb
                u = _dot(hb, wbn_ref[0])
                v = u[:, :width] * jax.nn.sigmoid(u[:, width:])
                v1_ref[0, rs, :] = v[:, :half].astype(BF16)
                v2_ref[0, rs, :] = v[:, half:].astype(BF16)
    carry_s[...] = c


def _out_call(p, q, mb, x, w_out_bf, g_post, mods, ab, hbo, h0, l, nxt=None):
    bsz, seq, d_model = x.shape
    width = p.shape[-1]
    half = width // 2
    tt = TILE_OUT
    tm = TILE_MIX
    nt = seq // tt
    per = tt // tm
    pad_blocks = TILE_PROJ // tt
    rev = lambda b, j: (b, nt - 1 - j, 0)
    _, pmt = _perm_matrices(tm)

    def full(a):
        nd = a.ndim
        return pl.BlockSpec(a.shape, lambda b, j, _nd=nd: (0,) * _nd)

    operands = [p, q, mb, x, w_out_bf, g_post, mods, ab, hbo, h0, pmt]
    in_specs = [
        pl.BlockSpec((1, tt, width), rev),
        pl.BlockSpec((1, tt, width), rev),
        pl.BlockSpec((1, tt, width), rev),
        pl.BlockSpec((1, tt, d_model), rev),
        pl.BlockSpec((1,) + w_out_bf.shape[1:], lambda b, j: (l, 0, 0)),
        full(g_post),
        full(mods),
        pl.BlockSpec((1, per, 1, width), lambda b, j: (b, nt - 1 - j, 0, 0)),
        pl.BlockSpec((1, per, 1, width), lambda b, j: (b, nt - 1 - j, 0, 0)),
        full(h0),
        full(pmt),
    ]
    out_specs = [pl.BlockSpec((1, tt, d_model), rev)]
    out_shape = [jax.ShapeDtypeStruct((bsz, seq, d_model), F32)]
    aliases = {}
    if nxt is not None:
        g_pre, w_in_bf, v2_buf = nxt
        operands += [g_pre, w_in_bf, v2_buf]
        in_specs += [
            full(g_pre),
            pl.BlockSpec((1, d_model, 2 * width), lambda b, j: (l + 1, 0, 1)),
            pl.BlockSpec(memory_space=pl.ANY),
        ]
        out_specs += [
            pl.BlockSpec((1, tt, d_model), rev),
            pl.BlockSpec((1, tt, half), rev),
            pl.BlockSpec((1, tt, half), lambda b, j: (b, nt - 1 - j + pad_blocks, 0)),
        ]
        out_shape += [
            jax.ShapeDtypeStruct((bsz, seq, d_model), BF16),
            jax.ShapeDtypeStruct((bsz, seq, half), BF16),
            jax.ShapeDtypeStruct(v2_buf.shape, BF16),
        ]
        aliases = {len(operands) - 1: 3}
    res = pl.pallas_call(
        functools.partial(_out_body, l=l, d_model=d_model, width=width,
                          fuse_next=nxt is not None),
        grid=(bsz, nt),
        in_specs=in_specs,
        out_specs=out_specs,
        out_shape=out_shape,
        input_output_aliases=aliases,
        scratch_shapes=[pltpu.VMEM((1, width), F32)],
        compiler_params=pltpu.CompilerParams(
            dimension_semantics=("arbitrary", "arbitrary"), vmem_limit_bytes=VMEM_LIMIT),
        name="out_proj" if nxt is not None else "out",
    )(*operands)
    return res if nxt is not None else res[0]


def _pack_gate_weights(w_r, w_i):
    depth, ndir, heads, hd, _ = w_r.shape
    hh = heads // 2
    eye = jnp.eye(hh, dtype=w_r.dtype)

    def bd(w):
        w = w.reshape(depth, ndir, 2, hh, hd, hd)
        return jnp.einsum("ldfhij,hg->ldfhigj", w, eye).reshape(depth, ndir, 2, hh * hd, hh * hd)

    return jnp.concatenate([bd(w_r), bd(w_i)], axis=-1).astype(BF16)


def kernel(x, c, ctx, c_ctx, w_mod, b_mod, g_pre, g_post, w_in, conv_a_w, conv_a_b, w_rgate,
           b_rgate, w_igate, b_igate, lru_lambda, dw_w, dw_b, ln_g, ln_b, w_out):
    bsz, seq, d_model = x.shape
    depth = w_mod.shape[0]
    width = conv_a_b.shape[-1]
    assert bsz + 1 <= ROWS and seq % TILE_PROJ == 0 and seq % TILE_MIX == 0
    assert TILE_MIX % GRID_W == 0 and TILE_PROJ >= CONV_PAD * GRID_W
    assert seq % TILE_OUT == 0 and TILE_OUT % TILE_MIX == 0 and TILE_PROJ % TILE_OUT == 0
    assert ctx.shape[1] % CHUNK == 0 and width % MXU_N == 0 and d_model % MXU_N == 0

    act = jnp.concatenate(
        [c, c_ctx[None, :], jnp.zeros((ROWS - bsz - 1, d_model), F32)], axis=0)
    mods = _mods_call(act, w_mod, b_mod)

    w_in_bf = w_in.astype(BF16)
    w_out_bf = w_out.astype(BF16)
    wg = _pack_gate_weights(w_rgate, w_igate)

    h0 = _ctx_call(ctx, mods[:, bsz:bsz + 1, :], g_pre, g_post, w_in_bf, w_out_bf, wg, conv_a_w,
                   conv_a_b, b_rgate, b_igate, lru_lambda, dw_w, dw_b, ln_g, ln_b)

    dw_w16 = dw_w.astype(BF16)
    hx, v1, v2p = _proj_call(x, mods, g_pre, w_in_bf, 0, width)
    for l in range(depth):
        p, q, mb, ab, hbo = _mix_call(
            hx, v1, v2p, w_in_bf, wg, conv_a_w, conv_a_b, b_rgate, b_igate, lru_lambda, dw_w16,
            dw_b, ln_g, ln_b, h0, l)
        if l + 1 < depth:
            x, hx, v1, v2p = _out_call(p, q, mb, x, w_out_bf, g_post, mods, ab, hbo, h0, l,
                                       nxt=(g_pre, w_in_bf, v2p))
        else:
            x = _out_call(p, q, mb, x, w_out_bf, g_post, mods, ab, hbo, h0, l)
    return x
```

```python
import functools

import jax
import jax.numpy as jnp
import numpy as np
from jax import lax
from jax.experimental import pallas as pl
from jax.experimental.pallas import tpu as pltpu

F32 = jnp.float32
BF16 = jnp.bfloat16

EPS = 1e-6
LRU_C = 8.0
LRU_CONV = 4
CONV_K = 31
CONV_PAD = CONV_K // 2
GRID_W = 64
HALO = 16
ROWS = 8
CHUNK = 64
SEG_PAD = 16
MXU_N = 256
TAP_GROUP = 16
ROWS_PER_CHAIN = 512

TILE_PROJ = 1024
TILE_MIX = 512
MIX_TILES_PER_STEP = 2
TILE_OUT = 1024
VMEM_LIMIT = 56 * 1024 * 1024


def _silu(x):
    return x * jax.nn.sigmoid(x)


def _softplus(x):
    return jnp.maximum(x, 0.0) + jnp.log1p(jnp.exp(-jnp.abs(x)))


def _rms_norm(x, g):
    ms = jnp.mean(x * x, axis=-1, keepdims=True)
    return x * lax.rsqrt(ms + EPS) * g


def _layer_norm(x, g, b):
    mu = jnp.mean(x, axis=-1, keepdims=True)
    xc = x - mu
    var = jnp.mean(xc * xc, axis=-1, keepdims=True)
    return xc * lax.rsqrt(var + EPS) * g + b


def _dot(a, b):
    return jnp.dot(a, b, preferred_element_type=F32)


def _mark(spare_ref, token):
    if token is None:
        return
    reps = spare_ref.shape[0] // ROWS
    spare_ref[...] = jnp.concatenate([token] * reps, axis=0).astype(spare_ref.dtype)


def _release(mxu_pieces, vpu_pieces):
    token = None
    for k in range(max(len(mxu_pieces), len(vpu_pieces))):
        new_token = mxu_pieces[k]() if k < len(mxu_pieces) else token
        if k < len(vpu_pieces):
            vpu_pieces[k](token)
        token = new_token


def _perm_matrices(n):
    i = np.arange(n)
    src = (n // ROWS) * (i % ROWS) + i // ROWS
    pm = (src[:, None] == i[None, :]).astype(np.float32)
    return jnp.asarray(pm, BF16), jnp.asarray(pm.T, BF16)


UA_ROW0 = (LRU_CONV - 1) * ROWS


def _row_conv_matrix(n):
    cols = n // ROWS
    rows = (cols + 2 * SEG_PAD) * ROWS
    mat = np.zeros((rows, n), np.float32)
    for i in range(rows):
        q, s = divmod(i, ROWS)
        if SEG_PAD <= q < SEG_PAD + cols:
            mat[i, cols * s + q - SEG_PAD] = 1.0
    return jnp.asarray(mat, BF16)


def _short_convs(ua_s, a_prev, a_next, cw, cb, vc_s, n, width):
    sub = n // ROWS
    ext = LRU_CONV - 1
    row = lax.broadcasted_iota(jnp.int32, (ROWS, width), 0)
    cw = 0.5 * cw
    cb = 0.5 * cb

    def blk(p):
        return slice(p * ROWS, (p + 1) * ROWS)

    for k in range(1, ext + 1):
        tail = pltpu.roll(ua_s[blk(ext + sub - k), :width], 1, 0)
        ua_s[blk(ext - k), :width] = jnp.where(row == 0, a_prev[ext - k:ext - k + 1, :], tail)
    for k in range(ext):
        head = pltpu.roll(ua_s[blk(ext + k), :width], ROWS - 1, 0)
        ua_s[blk(ext + sub + k), :width] = jnp.where(row == ROWS - 1, a_next[k:k + 1, :], head)

    def chunk(ci, carry):
        base = ci * CHUNK
        for d in range(2):
            first = 0 if d == 0 else ext
            acc = jnp.broadcast_to(cb[d:d + 1, :], (CHUNK, width))
            for k in range(LRU_CONV):
                start = pl.multiple_of(base + (first + k) * ROWS, ROWS)
                acc = acc + cw[d, k:k + 1, :] * ua_s[pl.ds(start, CHUNK), :width]
            vc_s[d, pl.ds(pl.multiple_of(base, CHUNK), CHUNK), :] = acc
        return carry

    lax.fori_loop(0, n // CHUNK, chunk, 0)


def _gate_pieces(vc_s, wg_ref, g_s, width):
    half = width // 2

    def piece(d, hf):
        def run():
            res = _dot(vc_s[d, :, hf * half:(hf + 1) * half].astype(BF16), wg_ref[d, hf])
            g_s[d, :, hf * half:(hf + 1) * half] = res[:, :half]
            g_s[d, :, width + hf * half:width + (hf + 1) * half] = res[:, half:]
        return run

    return [piece(d, hf) for d in range(2) for hf in range(2)]


def _sublane_scan(a, h, row, reverse):
    for s in (1, 2, 4):
        if reverse:
            m, shift = row < ROWS - s, ROWS - s
        else:
            m, shift = row >= s, s
        a_sh = jnp.where(m, pltpu.roll(a, shift, 0), 1.0)
        h_sh = jnp.where(m, pltpu.roll(h, shift, 0), 0.0)
        h = h + a * h_sh
        a = a * a_sh
    return a, h


def _rglru_scan(ua_s, br, bi, lam, c0, vc_s, g_s, hl_s, al_s, po_s, q_s, n, width):
    sub = n // ROWS
    row = lax.broadcasted_iota(jnp.int32, (ROWS, width), 0)
    spl_half = (-0.5 * LRU_C) * _softplus(-lam)

    def scan_columns(cols):
        nc = cols.stop - cols.start
        gate_i = slice(width + cols.start, width + cols.stop)

        def rows_of(v, d):
            return jnp.broadcast_to(v[d:d + 1, cols], (ROWS, nc))

        spl_c = [rows_of(spl_half, d) for d in range(2)]
        br_c = [rows_of(0.5 * br, d) for d in range(2)]
        bi_c = [rows_of(0.5 * bi, d) for d in range(2)]

        def coeffs(d, r0):
            vh = vc_s[d, pl.ds(r0, ROWS), cols]
            t_r = jnp.tanh(g_s[d, pl.ds(r0, ROWS), cols] + br_c[d])
            t_i = jnp.tanh(g_s[d, pl.ds(r0, ROWS), gate_i] + bi_c[d])
            la = spl_c[d] + spl_c[d] * t_r
            a = jnp.exp(la)
            x = jnp.tanh(la) * (-1.0 - a * a)
            mult = jnp.where(x > 0.0, x * lax.rsqrt(x), 0.0)
            return a, mult * (vh + vh * t_i)

        def local(j, carry):
            hf, af, hb, ab = carry
            r0 = pl.multiple_of(j * ROWS, ROWS)
            a, b = coeffs(0, r0)
            hf = a * hf + b
            af = a * af
            hl_s[0, pl.ds(r0, ROWS), cols] = hf
            al_s[0, pl.ds(r0, ROWS), cols] = af
            r1 = pl.multiple_of((sub - 1 - j) * ROWS, ROWS)
            a, b = coeffs(1, r1)
            hb = a * hb + b
            ab = a * ab
            hl_s[1, pl.ds(r1, ROWS), cols] = hb
            al_s[1, pl.ds(r1, ROWS), cols] = ab
            return hf, af, hb, ab

        zero = jnp.zeros((ROWS, nc), F32)
        one = jnp.ones((ROWS, nc), F32)
        return lax.fori_loop(0, sub, local, (zero, one, zero, one), unroll=8)

    halves = [scan_columns(slice(c, c + width // 2)) for c in (0, width // 2)]
    hf, af, hb, ab = (jnp.concatenate(parts, axis=1) for parts in zip(*halves))

    af, hf = _sublane_scan(af, hf, row, reverse=False)
    end_f = hf + af * c0
    c_f = jnp.where(row == 0, c0, pltpu.roll(end_f, 1, 0))
    ab, hb = _sublane_scan(ab, hb, row, reverse=True)
    c_b = jnp.where(row == ROWS - 1, 0.0, pltpu.roll(hb, ROWS - 1, 0))
    c_a = jnp.where(row == ROWS - 1, 1.0, pltpu.roll(ab, ROWS - 1, 0))

    if po_s is not None:
        step = 2 * ROWS
        c_f, c_b, c_a = (jnp.concatenate([c, c], axis=0) for c in (c_f, c_b, c_a))

        def fix(j, carry):
            r0 = pl.ds(pl.multiple_of(j * step, step), step)
            a_b = al_s[1, r0, :]
            h = (hl_s[0, r0, :] + al_s[0, r0, :] * c_f) + (hl_s[1, r0, :] + a_b * c_b)
            sg = _silu(ua_s[pl.ds(pl.multiple_of(UA_ROW0 + j * step, ROWS), step), width:])
            po_s[r0, :] = (h * sg).astype(po_s.dtype)
            if q_s is not None:
                q_s[r0, :] = ((a_b * c_a) * sg).astype(q_s.dtype)
            return carry

        lax.fori_loop(0, n // step, fix, 0, unroll=4)

    return end_f[ROWS - 1:ROWS, :], hb[0:1, :], ab[0:1, :]


def _shifted_copy_pieces(vpad_ref, sh_ref):
    n = sh_ref.shape[1]

    def piece(j):
        def run():
            sh_ref[j] = vpad_ref[pl.ds(j, n), :]
        return run

    return [piece(j) for j in range(ROWS)]


def _time_conv_pieces(sh_ref, w_ref, ncol, bases, out_ref):
    def piece(ci, base):
        def run():
            acc = jnp.zeros((CHUNK, ncol), F32)
            for k in range(CONV_K):
                off = SEG_PAD - CONV_PAD + k
                start = base + (off // ROWS) * ROWS
                acc = acc + w_ref[k:k + 1, 0:ncol] * sh_ref[off % ROWS, start:start + CHUNK, :]
            out_ref[ci * CHUNK:(ci + 1) * CHUNK, 0:ncol] = acc
        return run

    return [piece(ci, base) for ci, base in enumerate(bases)]


def _mods_body(act_ref, w_ref, b_ref, o_ref):
    a = _silu(act_ref[...])
    w = w_ref[0]
    a_hi = a.astype(BF16)
    a_lo = (a - a_hi.astype(F32)).astype(BF16)
    w_hi = w.astype(BF16)
    w_lo = (w - w_hi.astype(F32)).astype(BF16)
    o_ref[0] = _dot(a_hi, w_hi) + (_dot(a_hi, w_lo) + _dot(a_lo, w_hi)) + b_ref[0]


def _mods_call(act, w_mod, b_mod):
    depth, d_model, d3 = w_mod.shape
    ncol = d_model
    return pl.pallas_call(
        _mods_body,
        grid=(depth, d3 // ncol),
        in_specs=[
            pl.BlockSpec((ROWS, d_model), lambda l, n: (0, 0)),
            pl.BlockSpec((1, d_model, ncol), lambda l, n: (l, 0, n)),
            pl.BlockSpec((1, 1, ncol), lambda l, n: (l, 0, n)),
        ],
        out_specs=pl.BlockSpec((1, ROWS, ncol), lambda l, n: (l, 0, n)),
        out_shape=jax.ShapeDtypeStruct((depth, ROWS, d3), F32),
        compiler_params=pltpu.CompilerParams(
            dimension_semantics=("arbitrary", "arbitrary"), vmem_limit_bytes=VMEM_LIMIT),
        name="mods",
    )(act, w_mod, b_mod.reshape(depth, 1, d3))


def _modulate(x, g, m, d_model):
    return _rms_norm(x, g) * (1.0 + m[:, d_model:2 * d_model]) + m[:, :d_model]


def _ctx_body(ctx_ref, mod_ref, gpre_ref, gpost_ref, win_ref, wo_ref, wg_ref, cw_ref, cb_ref,
              br_ref, bi_ref, lam_ref, dww_ref, dwb_ref, lng_ref, lnb_ref, pm_ref, pmt_ref,
              h0_ref,
              ua_s, vc_s, g_s, hl_s, al_s, po_s, vpad_s, sh_s, y_s,
              *, depth, d_model, width, n):
    xc = ctx_ref[0]
    zeros_pad = jnp.zeros((SEG_PAD, width), F32)
    no_rows = jnp.zeros((LRU_CONV - 1, width), F32)
    zero_state = jnp.zeros((1, width), F32)
    for l in range(depth):
        update = l < depth - 1
        m = mod_ref[l]
        hc = _modulate(xc, gpre_ref[l:l + 1, :], m, d_model).astype(BF16)
        hcp = _dot(pm_ref[...], hc).astype(BF16)
        if update:
            ua_s[UA_ROW0:UA_ROW0 + n, :] = _dot(hcp, win_ref[l, :, :2 * width])
            ub = _dot(hc, win_ref[l, :, 2 * width:])
        else:
            ua_s[UA_ROW0:UA_ROW0 + n, :width] = _dot(hcp, win_ref[l, :, :width])
        _short_convs(ua_s, no_rows, no_rows, cw_ref[l], cb_ref[l], vc_s, n, width)
        for piece in _gate_pieces(vc_s, wg_ref.at[l], g_s, width):
            piece()
        hf_last, hb_first, _ = _rglru_scan(
            ua_s, br_ref[l], bi_ref[l], lam_ref[l], zero_state, vc_s, g_s, hl_s, al_s,
            po_s if update else None, None, n, width)
        h0_ref[0, 2 * l:2 * l + 1, :] = hf_last
        h0_ref[0, 2 * l + 1:2 * l + 2, :] = hb_first
        if update:
            v = ub[:, :width] * jax.nn.sigmoid(ub[:, width:2 * width])
            vpad_s[0:SEG_PAD, :] = zeros_pad
            vpad_s[SEG_PAD:SEG_PAD + n, :] = v
            vpad_s[SEG_PAD + n:, :] = zeros_pad
            for piece in _shifted_copy_pieces(vpad_s, sh_s) + _time_conv_pieces(
                    sh_s, dww_ref.at[l], width, [ci * CHUNK for ci in range(n // CHUNK)], y_s):
                piece()
            y = y_s[...] + dwb_ref[l:l + 1, :]
            conv = _silu(_layer_norm(y, lng_ref[l:l + 1, :], lnb_ref[l:l + 1, :]))
            mix_a = _dot(pmt_ref[...], po_s[...].astype(BF16)).astype(BF16)
            mix_b = (conv * _silu(ub[:, 2 * width:])).astype(BF16)
            mix = _dot(mix_a, wo_ref[l, :width, :]) + _dot(mix_b, wo_ref[l, width:, :])
            xc = xc + m[:, 2 * d_model:] * _rms_norm(mix, gpost_ref[l:l + 1, :])


def _ctx_call(ctx, mod_c, g_pre, g_post, w_in_bf, w_out_bf, wg, conv_a_w, conv_a_b, b_rgate,
              b_igate, lru_lambda, dw_w, dw_b, ln_g, ln_b):
    bsz, n, d_model = ctx.shape
    depth = w_in_bf.shape[0]
    width = conv_a_b.shape[-1]
    npad = n + 2 * SEG_PAD
    pm, pmt = _perm_matrices(n)

    def full(a):
        nd = a.ndim
        return pl.BlockSpec(a.shape, lambda b, _nd=nd: (0,) * _nd)

    params = (mod_c, g_pre, g_post, w_in_bf, w_out_bf, wg, conv_a_w, conv_a_b, b_rgate, b_igate,
              lru_lambda, dw_w, dw_b, ln_g, ln_b, pm, pmt)
    return pl.pallas_call(
        functools.partial(_ctx_body, depth=depth, d_model=d_model, width=width, n=n),
        grid=(bsz,),
        in_specs=[pl.BlockSpec((1, n, d_model), lambda b: (b, 0, 0))] + [full(a) for a in params],
        out_specs=pl.BlockSpec((1, 2 * depth, width), lambda b: (b, 0, 0)),
        out_shape=jax.ShapeDtypeStruct((bsz, 2 * depth, width), F32),
        scratch_shapes=[
            pltpu.VMEM((n + 2 * UA_ROW0, 2 * width), F32),
            pltpu.VMEM((2, n, width), F32),
            pltpu.VMEM((2, n, 2 * width), F32),
            pltpu.VMEM((2, n, width), F32),
            pltpu.VMEM((2, n, width), F32),
            pltpu.VMEM((n, width), F32),
            pltpu.VMEM((npad, width), F32),
            pltpu.VMEM((ROWS, npad - ROWS, width), F32),
            pltpu.VMEM((n, width), F32),
        ],
        compiler_params=pltpu.CompilerParams(
            dimension_semantics=("arbitrary",), vmem_limit_bytes=VMEM_LIMIT),
        name="context",
    )(ctx, *params)


def _proj_body(x_ref, mod_ref, g_ref, w_ref, hx_ref, v1_ref, v2_ref, *, l, nt, d_model, width):
    b = pl.program_id(0)
    j = pl.program_id(1)
    is_pad = jnp.logical_or(j == 0, j == nt + 1)

    @pl.when(is_pad)
    def _():
        v2_ref[...] = jnp.zeros(v2_ref.shape, v2_ref.dtype)

    @pl.when(jnp.logical_not(is_pad))
    def _():
        half = width // 2
        rows = ROWS_PER_CHAIN
        for c in range(x_ref.shape[1] // rows):
            rs = slice(c * rows, (c + 1) * rows)
            hb = _modulate(x_ref[0, rs, :], g_ref[l:l + 1, :], mod_ref[l, pl.ds(b, 1), :],
                           d_model).astype(BF16)
            hx_ref[0, rs, :] = hb
            u = _dot(hb, w_ref[0])
            v = u[:, :width] * jax.nn.sigmoid(u[:, width:])
            v1_ref[0, rs, :] = v[:, :half].astype(BF16)
            v2_ref[0, rs, :] = v[:, half:].astype(BF16)


def _proj_call(x, mods, g_pre, w_in_bf, l, width):
    bsz, seq, d_model = x.shape
    half = width // 2
    tt = TILE_PROJ
    nt = seq // tt

    def tok(b, j):
        return (b, jnp.clip(j - 1, 0, nt - 1), 0)

    return pl.pallas_call(
        functools.partial(_proj_body, l=l, nt=nt, d_model=d_model, width=width),
        grid=(bsz, nt + 2),
        in_specs=[
            pl.BlockSpec((1, tt, d_model), tok),
            pl.BlockSpec(mods.shape, lambda b, j: (0, 0, 0)),
            pl.BlockSpec(g_pre.shape, lambda b, j: (0, 0)),
            pl.BlockSpec((1, d_model, 2 * width), lambda b, j: (l, 0, 1)),
        ],
        out_specs=[
            pl.BlockSpec((1, tt, d_model), tok),
            pl.BlockSpec((1, tt, half), tok),
            pl.BlockSpec((1, tt, half), lambda b, j: (b, j, 0)),
        ],
        out_shape=[
            jax.ShapeDtypeStruct((bsz, seq, d_model), BF16),
            jax.ShapeDtypeStruct((bsz, seq, half), BF16),
            jax.ShapeDtypeStruct((bsz, seq + 2 * tt, half), BF16),
        ],
        compiler_params=pltpu.CompilerParams(
            dimension_semantics=("arbitrary", "arbitrary"), vmem_limit_bytes=VMEM_LIMIT),
        name="proj",
    )(x, mods, g_pre, w_in_bf)


def _mix_body(*refs, per, **kw):
    for st in range(per):
        _mix_tile(st, *refs, per=per, **kw)


def _mix_tile(st, hx_ref, hxp_ref, hxn_ref, v1_ref, v2_ref, wa_ref, wbg_ref, wg_ref, cw_ref,
              cb_ref, br_ref, bi_ref, lam_ref, dww16_ref, dwb_ref, lng_ref, lnb_ref, pm_ref,
              pmt_ref, pme_ref, h0_ref, zero_ref,
              p_ref, q_ref, mb_ref, ab_ref, hbo_ref,
              hxs_s, ua_s, bg_s, vc_s, g_s, hl_s, al_s, carry_s, ev_s, od_s, wr_s, wb_s,
              yr_s, y_s,
              *, per, l, tt, nt, d_model, width, v2_front):
    i = pl.program_id(1) * per + st
    rows_t = slice(st * tt, (st + 1) * tt)
    half = width // 2
    nseg = tt // GRID_W
    ext = LRU_CONV - 1
    reach = CONV_PAD * GRID_W
    row_zero = zero_ref[0]

    @pl.when(i == 0)
    def _():
        carry_s[...] = h0_ref[pl.program_id(0), 2 * l:2 * l + 1, :]

    def col_slices(total):
        return [slice(c, c + MXU_N) for c in range(0, total, MXU_N)]

    def zero_based(start, align):
        return pl.multiple_of(row_zero + start, align)

    def prepare(token):
        for k in range(CONV_K):
            wr_s[k] = jnp.broadcast_to(dww16_ref[l, k:k + 1, 0:half], (2 * ROWS, half))
            wb_s[k] = jnp.broadcast_to(dww16_ref[l, k:k + 1, half:], (2 * ROWS, half))

    def spread_rows():
        res = _dot(pme_ref[...], v1_ref[0, rows_t, :])
        ev_s[...] = res.astype(BF16)
        moved = jnp.concatenate([res[ROWS:, :], jnp.zeros((ROWS, half), F32)], axis=0)
        od_s[...] = moved.astype(BF16)
        return res[0:ROWS, 0:128]

    def row_conv(chunks):
        def run(token):
            _mark(wr_s.at[CONV_K, :, 0:128], token)
            for ci in chunks:
                for rt in range(CHUNK // (2 * ROWS)):
                    row0 = ci * CHUNK + rt * 2 * ROWS
                    acc = None
                    for g0 in range(0, CONV_K, TAP_GROUP):
                        part = None
                        for k in range(g0, min(g0 + TAP_GROUP, CONV_K)):
                            shift = SEG_PAD - CONV_PAD + k
                            src = row0 + shift * ROWS
                            if shift % 2 == 0:
                                window = ev_s[src:src + 2 * ROWS, :]
                            else:
                                window = od_s[src - ROWS:src + ROWS, :]
                            term = wr_s[row_zero + k] * window
                            part = term if part is None else part + term
                        part = part.astype(F32)
                        acc = part if acc is None else acc + part
                    yr_s[row0:row0 + 2 * ROWS, :] = acc
        return run

    def gather_rows():
        y_s[:, 0:half] = _dot(pmt_ref[...], yr_s[...].astype(BF16))
        return None

    def col_conv(chunks):
        def run(token):
            _mark(wb_s.at[CONV_K, :, 0:128], token)
            for ci in chunks:
                win0 = i * tt + (v2_front - reach) + ci * CHUNK
                for rt in range(CHUNK // (2 * ROWS)):
                    acc = None
                    for g0 in range(0, CONV_K, TAP_GROUP):
                        part = None
                        for k in range(g0, min(g0 + TAP_GROUP, CONV_K)):
                            start = pl.multiple_of(win0 + k * GRID_W + rt * 2 * ROWS, 2 * ROWS)
                            term = wb_s[row_zero + k] * v2_ref[0, pl.ds(start, 2 * ROWS), :]
                            part = term if part is None else part + term
                        part = part.astype(F32)
                        acc = part if acc is None else acc + part
                    row0 = ci * CHUNK + rt * 2 * ROWS
                    y_s[row0:row0 + 2 * ROWS, half:] = acc
        return run

    def permute(slices):
        def run():
            for cs in slices:
                res = _dot(pm_ref[...], hx_ref[0, rows_t, cs])
                hxs_s[0:tt, cs] = res.astype(BF16)
            return res[0:ROWS, 0:128]
        return run

    def project_a(cs):
        def run():
            res = _dot(hxs_s[0:tt, :], wa_ref[0, :, cs])
            ua_s[UA_ROW0:UA_ROW0 + tt, cs] = res
            return res[0:ROWS, 0:128]
        return run

    def project_b(cs):
        def run():
            res = _dot(hx_ref[0, rows_t, :], wbg_ref[0, :, cs])
            bg_s[0:tt, cs] = res
            return res[0:ROWS, 0:128]
        return run

    cols_d = col_slices(d_model)
    cols_a = col_slices(2 * width)
    cols_b = col_slices(width)
    quarter = nseg // 4
    row_stages = [row_conv(range(q * quarter, (q + 1) * quarter)) for q in range(4)]
    col_stages = [col_conv(range(q * 2 * quarter, (q + 1) * 2 * quarter)) for q in range(2)]
    _release(
        [spread_rows, permute(cols_d[:2]), permute(cols_d[2:]), project_b(cols_b[0]),
         project_b(cols_b[1])] + [project_a(cs) for cs in cols_a] + [gather_rows],
        [prepare] + row_stages + col_stages)

    wa_a = wa_ref[0, :, :width]
    before = hxp_ref[0] if st == 0 else hx_ref[0, st * tt - HALO:st * tt, :]
    after = hxn_ref[0] if st == per - 1 else hx_ref[0, (st + 1) * tt:(st + 1) * tt + HALO, :]
    a_prev = jnp.where(i > 0, _dot(before, wa_a)[HALO - ext:, :], 0.0)
    a_next = jnp.where(i < nt - 1, _dot(after, wa_a)[:ext, :], 0.0)
    _short_convs(ua_s, a_prev, a_next, cw_ref[l], cb_ref[l], vc_s, tt, width)

    def finish_conv(c, nrows):
        def run(token):
            _mark(bg_s.at[tt:tt + ROWS, 0:128], token)
            y = y_s[c * nrows:(c + 1) * nrows, :] + dwb_ref[l:l + 1, :]
            conv = _silu(_layer_norm(y, lng_ref[l:l + 1, :], lnb_ref[l:l + 1, :]))
            out = conv * _silu(bg_s[pl.ds(zero_based(c * nrows, ROWS), nrows), :])
            mb_ref[0, st * tt + c * nrows:st * tt + (c + 1) * nrows, :] = out.astype(BF16)
        return run

    def gate_piece(d, hf):
        def run():
            lo = hf * half
            res = _dot(vc_s[d, 0:tt, lo:lo + half].astype(BF16), wg_ref[0, d, hf])
            g_s[d, :, lo:lo + half] = res[:, :half]
            g_s[d, :, width + lo:width + lo + half] = res[:, half:]
            return res[0:ROWS, 0:128]
        return run

    _release([gate_piece(d, hf) for d in range(2) for hf in range(2)],
             [finish_conv(c, tt // 4) for c in range(4)])

    hf_last, hb_first, a_total = _rglru_scan(
        ua_s, br_ref[l], bi_ref[l], lam_ref[l], carry_s[...], vc_s, g_s, hl_s, al_s,
        p_ref.at[0, rows_t], q_ref.at[0, rows_t], tt, width)
    carry_s[...] = hf_last
    ab_ref[0, st] = a_total
    hbo_ref[0, st] = hb_first


def _mix_call(hx, v1, v2p, w_in_bf, wg, cw, cb, br, bi, lam, dww16, dwb, lng, lnb, h0, l):
    bsz, seq, d_model = hx.shape
    width = cb.shape[-1]
    half = width // 2
    tt = TILE_MIX
    per = MIX_TILES_PER_STEP
    nt = seq // tt
    hb_per_step = per * tt // HALO
    n_halo_blocks = seq // HALO
    v2_front = (v2p.shape[1] - seq) // 2
    pm, pmt = _perm_matrices(tt)
    pme = _row_conv_matrix(tt)

    def full(a):
        nd = a.ndim
        return pl.BlockSpec(a.shape, lambda b, i, _nd=nd: (0,) * _nd)

    small = (cw, cb, br, bi, lam, dww16, dwb, lng, lnb, pm, pmt, pme, h0)
    tile = lambda b, i: (b, i, 0)
    out_shapes = [jax.ShapeDtypeStruct((bsz, seq, width), BF16)] * 3 + [
        jax.ShapeDtypeStruct((bsz, nt, 1, width), F32)] * 2
    return pl.pallas_call(
        functools.partial(_mix_body, per=per, l=l, tt=tt, nt=nt, d_model=d_model, width=width,
                          v2_front=v2_front),
        grid=(bsz, nt // per),
        in_specs=[
            pl.BlockSpec((1, per * tt, d_model), tile),
            pl.BlockSpec((1, HALO, d_model),
                         lambda b, i: (b, jnp.maximum(i * hb_per_step - 1, 0), 0)),
            pl.BlockSpec((1, HALO, d_model),
                         lambda b, i: (b, jnp.minimum((i + 1) * hb_per_step, n_halo_blocks - 1), 0)),
            pl.BlockSpec((1, per * tt, half), tile),
            pl.BlockSpec((1, v2p.shape[1], half), lambda b, i: (b, 0, 0)),
            pl.BlockSpec((1, d_model, 2 * width), lambda b, i: (l, 0, 0)),
            pl.BlockSpec((1, d_model, width), lambda b, i: (l, 0, 4)),
            pl.BlockSpec((1,) + wg.shape[1:], lambda b, i: (l, 0, 0, 0, 0)),
        ] + [full(a) for a in small] + [pl.BlockSpec(memory_space=pltpu.SMEM)],
        out_specs=[pl.BlockSpec((1, per * tt, width), tile)] * 3 + [
            pl.BlockSpec((1, per, 1, width), lambda b, i: (b, i, 0, 0))] * 2,
        out_shape=out_shapes,
        scratch_shapes=[
            pltpu.VMEM((tt, d_model), BF16),
            pltpu.VMEM((tt + 2 * UA_ROW0, 2 * width), F32),
            pltpu.VMEM((tt + ROWS, width), F32),
            pltpu.VMEM((2, tt, width), F32),
            pltpu.VMEM((2, tt, 2 * width), F32),
            pltpu.VMEM((2, tt, width), F32),
            pltpu.VMEM((2, tt, width), F32),
            pltpu.VMEM((1, width), F32),
            pltpu.VMEM(pme.shape[:1] + (half,), BF16),
            pltpu.VMEM(pme.shape[:1] + (half,), BF16),
            pltpu.VMEM((CONV_K + 1, 2 * ROWS, half), BF16),
            pltpu.VMEM((CONV_K + 1, 2 * ROWS, half), BF16),
            pltpu.VMEM((tt, half), F32),
            pltpu.VMEM((tt, width), F32),
        ],
        compiler_params=pltpu.CompilerParams(
            dimension_semantics=("arbitrary", "arbitrary"), vmem_limit_bytes=VMEM_LIMIT,
            ),
        name="mixer",
    )(hx, hx, hx, v1, v2p, w_in_bf, w_in_bf, wg, *small, jnp.zeros((1,), jnp.int32))


def _out_body(p_ref, q_ref, mb_ref, x_ref, wo_ref, gpost_ref, mod_ref, ab_ref, hb_ref, h0_ref,
              pmt_ref, *rest, l, d_model, width, fuse_next):
    if fuse_next:
        gpre_ref, wbn_ref, _, o_ref, hx_ref, v1_ref, v2_ref, carry_s = rest
    else:
        o_ref, carry_s = rest
    b = pl.program_id(0)

    @pl.when(pl.program_id(1) == 0)
    def _():
        carry_s[...] = h0_ref[b, 2 * l + 1:2 * l + 2, :]

    gate = mod_ref[l, pl.ds(b, 1), 2 * d_model:]
    rows = ROWS_PER_CHAIN
    half = width // 2
    tm = pmt_ref.shape[0]
    c = carry_s[...]
    for st in reversed(range(p_ref.shape[1] // tm)):
        t0 = st * tm
        mix_a = (p_ref[0, t0:t0 + tm, :].astype(F32)
                 + q_ref[0, t0:t0 + tm, :].astype(F32) * c).astype(BF16)
        mix_t = _dot(pmt_ref[...], mix_a).astype(BF16)
        c = hb_ref[0, st] + ab_ref[0, st] * c
        for ch in range(tm // rows):
            rs = slice(t0 + ch * rows, t0 + (ch + 1) * rows)
            lhs = jnp.concatenate([mix_t[ch * rows:(ch + 1) * rows, :], mb_ref[0, rs, :]], axis=1)
            mix = _dot(lhs, wo_ref[0])
            x_new = x_ref[0, rs, :] + gate * _rms_norm(mix, gpost_ref[l:l + 1, :])
            o_ref[0, rs, :] = x_new
            if fuse_next:
                hb = _modulate(x_new, gpre_ref[l + 1:l + 2, :], mod_ref[l + 1, pl.ds(b, 1), :],
                               d_model).astype(BF16)
                hx_ref[0, rs, :] = hb
                u = _dot(hb, wbn_ref[0])
                v = u[:, :width] * jax.nn.sigmoid(u[:, width:])
                v1_ref[0, rs, :] = v[:, :half].astype(BF16)
                v2_ref[0, rs, :] = v[:, half:].astype(BF16)
    carry_s[...] = c


def _out_call(p, q, mb, x, w_out_bf, g_post, mods, ab, hbo, h0, l, nxt=None):
    bsz, seq, d_model = x.shape
    width = p.shape[-1]
    half = width // 2
    tt = TILE_OUT
    tm = TILE_MIX
    nt = seq // tt
    per = tt // tm
    pad_blocks = TILE_PROJ // tt
    rev = lambda b, j: (b, nt - 1 - j, 0)
    _, pmt = _perm_matrices(tm)

    def full(a):
        nd = a.ndim
        return pl.BlockSpec(a.shape, lambda b, j, _nd=nd: (0,) * _nd)

    operands = [p, q, mb, x, w_out_bf, g_post, mods, ab, hbo, h0, pmt]
    in_specs = [
        pl.BlockSpec((1, tt, width), rev),
        pl.BlockSpec((1, tt, width), rev),
        pl.BlockSpec((1, tt, width), rev),
        pl.BlockSpec((1, tt, d_model), rev),
        pl.BlockSpec((1,) + w_out_bf.shape[1:], lambda b, j: (l, 0, 0)),
        full(g_post),
        full(mods),
        pl.BlockSpec((1, per, 1, width), lambda b, j: (b, nt - 1 - j, 0, 0)),
        pl.BlockSpec((1, per, 1, width), lambda b, j: (b, nt - 1 - j, 0, 0)),
        full(h0),
        full(pmt),
    ]
    out_specs = [pl.BlockSpec((1, tt, d_model), rev)]
    out_shape = [jax.ShapeDtypeStruct((bsz, seq, d_model), F32)]
    aliases = {}
    if nxt is not None:
        g_pre, w_in_bf, v2_buf = nxt
        operands += [g_pre, w_in_bf, v2_buf]
        in_specs += [
            full(g_pre),
            pl.BlockSpec((1, d_model, 2 * width), lambda b, j: (l + 1, 0, 1)),
            pl.BlockSpec(memory_space=pl.ANY),
        ]
        out_specs += [
            pl.BlockSpec((1, tt, d_model), rev),
            pl.BlockSpec((1, tt, half), rev),
            pl.BlockSpec((1, tt, half), lambda b, j: (b, nt - 1 - j + pad_blocks, 0)),
        ]
        out_shape += [
            jax.ShapeDtypeStruct((bsz, seq, d_model), BF16),
            jax.ShapeDtypeStruct((bsz, seq, half), BF16),
            jax.ShapeDtypeStruct(v2_buf.shape, BF16),
        ]
        aliases = {len(operands) - 1: 3}
    res = pl.pallas_call(
        functools.partial(_out_body, l=l, d_model=d_model, width=width,
                          fuse_next=nxt is not None),
        grid=(bsz, nt),
        in_specs=in_specs,
        out_specs=out_specs,
        out_shape=out_shape,
        input_output_aliases=aliases,
        scratch_shapes=[pltpu.VMEM((1, width), F32)],
        compiler_params=pltpu.CompilerParams(
            dimension_semantics=("arbitrary", "arbitrary"), vmem_limit_bytes=VMEM_LIMIT),
        name="out_proj" if nxt is not None else "out",
    )(*operands)
    return res if nxt is not None else res[0]


def _pack_gate_weights(w_r, w_i):
    depth, ndir, heads, hd, _ = w_r.shape
    hh = heads // 2
    eye = jnp.eye(hh, dtype=w_r.dtype)

    def bd(w):
        w = w.reshape(depth, ndir, 2, hh, hd, hd)
        return jnp.einsum("ldfhij,hg->ldfhigj", w, eye).reshape(depth, ndir, 2, hh * hd, hh * hd)

    return jnp.concatenate([bd(w_r), bd(w_i)], axis=-1).astype(BF16)


def kernel(x, c, ctx, c_ctx, w_mod, b_mod, g_pre, g_post, w_in, conv_a_w, conv_a_b, w_rgate,
           b_rgate, w_igate, b_igate, lru_lambda, dw_w, dw_b, ln_g, ln_b, w_out):
    bsz, seq, d_model = x.shape
    depth = w_mod.shape[0]
    width = conv_a_b.shape[-1]
    assert bsz + 1 <= ROWS and seq % TILE_PROJ == 0 and seq % TILE_MIX == 0
    assert TILE_MIX % GRID_W == 0 and TILE_PROJ >= CONV_PAD * GRID_W
    assert seq % TILE_OUT == 0 and TILE_OUT % TILE_MIX == 0 and TILE_PROJ % TILE_OUT == 0
    assert seq % (TILE_MIX * MIX_TILES_PER_STEP) == 0
    assert ctx.shape[1] % CHUNK == 0 and width % MXU_N == 0 and d_model % MXU_N == 0

    act = jnp.concatenate(
        [c, c_ctx[None, :], jnp.zeros((ROWS - bsz - 1, d_model), F32)], axis=0)
    mods = _mods_call(act, w_mod, b_mod)

    w_in_bf = w_in.astype(BF16)
    w_out_bf = w_out.astype(BF16)
    wg = _pack_gate_weights(w_rgate, w_igate)

    h0 = _ctx_call(ctx, mods[:, bsz:bsz + 1, :], g_pre, g_post, w_in_bf, w_out_bf, wg, conv_a_w,
                   conv_a_b, b_rgate, b_igate, lru_lambda, dw_w, dw_b, ln_g, ln_b)

    dw_w16 = dw_w.astype(BF16)
    hx, v1, v2p = _proj_call(x, mods, g_pre, w_in_bf, 0, width)
    for l in range(depth):
        p, q, mb, ab, hbo = _mix_call(
            hx, v1, v2p, w_in_bf, wg, conv_a_w, conv_a_b, b_rgate, b_igate, lru_lambda, dw_w16,
            dw_b, ln_g, ln_b, h0, l)
        if l + 1 < depth:
            x, hx, v1, v2p = _out_call(p, q, mb, x, w_out_bf, g_post, mods, ab, hbo, h0, l,
                                       nxt=(g_pre, w_in_bf, v2p))
        else:
            x = _out_call(p, q, mb, x, w_out_bf, g_post, mods, ab, hbo, h0, l)
    return x
```

```python
import functools

import jax
import jax.numpy as jnp
import numpy as np
from jax import lax
from jax.experimental import pallas as pl
from jax.experimental.pallas import tpu as pltpu

F32 = jnp.float32
BF16 = jnp.bfloat16

EPS = 1e-6
LRU_C = 8.0
LRU_CONV = 4
CONV_K = 31
CONV_PAD = CONV_K // 2
GRID_W = 64
HALO = 16
ROWS = 8
CHUNK = 64
SEG_PAD = 16
MXU_N = 256
TAP_GROUP = 16
ROWS_PER_CHAIN = 512

TILE_PROJ = 1024
TILE_MIX = 512
TILE_OUT = 1024
VMEM_LIMIT = 56 * 1024 * 1024


def _silu(x):
    return x * (0.5 + 0.5 * jnp.tanh(0.5 * x))


def _softplus(x):
    return jnp.maximum(x, 0.0) + jnp.log1p(jnp.exp(-jnp.abs(x)))


def _rms_norm(x, g):
    ms = jnp.mean(x * x, axis=-1, keepdims=True)
    return x * lax.rsqrt(ms + EPS) * g


def _layer_norm(x, g, b):
    mu = jnp.mean(x, axis=-1, keepdims=True)
    xc = x - mu
    var = jnp.mean(xc * xc, axis=-1, keepdims=True)
    return xc * lax.rsqrt(var + EPS) * g + b


def _dot(a, b):
    return jnp.dot(a, b, preferred_element_type=F32)


def _mark(spare_ref, token):
    if token is None:
        return
    reps = spare_ref.shape[0] // ROWS
    spare_ref[...] = jnp.concatenate([token] * reps, axis=0).astype(spare_ref.dtype)


def _release(mxu_pieces, vpu_pieces):
    token = None
    for k in range(max(len(mxu_pieces), len(vpu_pieces))):
        new_token = mxu_pieces[k]() if k < len(mxu_pieces) else token
        if k < len(vpu_pieces):
            vpu_pieces[k](token)
        token = new_token


def _to_sublane_major(x):
    n, cols = x.shape
    return jnp.swapaxes(x.reshape(ROWS, n // ROWS, cols), 0, 1).reshape(n, cols)


def _to_time_order(x):
    n, cols = x.shape
    return jnp.swapaxes(x.reshape(n // ROWS, ROWS, cols), 0, 1).reshape(n, cols)


def _time_order_matrix(n):
    i = np.arange(n)
    src = ROWS * (i % (n // ROWS)) + i // (n // ROWS)
    return jnp.asarray((src[:, None] == i[None, :]).astype(np.float32), BF16)


UA_ROW0 = (LRU_CONV - 1) * ROWS


def _row_conv_matrix(n):
    cols = n // ROWS
    rows = (cols + 2 * SEG_PAD) * ROWS
    mat = np.zeros((rows, n), np.float32)
    for i in range(rows):
        q, s = divmod(i, ROWS)
        if SEG_PAD <= q < SEG_PAD + cols:
            mat[i, cols * s + q - SEG_PAD] = 1.0
    return jnp.asarray(mat, BF16)


def _short_convs(ua_s, a_prev, a_next, cw, cb, vc_s, n, width):
    sub = n // ROWS
    ext = LRU_CONV - 1
    row = lax.broadcasted_iota(jnp.int32, (ROWS, width), 0)
    cw = 0.5 * cw
    cb = 0.5 * cb

    def blk(p):
        return slice(p * ROWS, (p + 1) * ROWS)

    for k in range(1, ext + 1):
        tail = pltpu.roll(ua_s[blk(ext + sub - k), :width], 1, 0)
        ua_s[blk(ext - k), :width] = jnp.where(row == 0, a_prev[ext - k:ext - k + 1, :], tail)
    for k in range(ext):
        head = pltpu.roll(ua_s[blk(ext + k), :width], ROWS - 1, 0)
        ua_s[blk(ext + sub + k), :width] = jnp.where(row == ROWS - 1, a_next[k:k + 1, :], head)

    def chunk(ci, carry):
        base = ci * CHUNK
        for d in range(2):
            first = 0 if d == 0 else ext
            acc = jnp.broadcast_to(cb[d:d + 1, :], (CHUNK, width))
            for k in range(LRU_CONV):
                start = pl.multiple_of(base + (first + k) * ROWS, ROWS)
                acc = acc + cw[d, k:k + 1, :] * ua_s[pl.ds(start, CHUNK), :width]
            vc_s[d, pl.ds(pl.multiple_of(base, CHUNK), CHUNK), :] = acc
        return carry

    lax.fori_loop(0, n // CHUNK, chunk, 0)


def _gate_pieces(vc_s, wg_ref, g_s, width):
    half = width // 2

    def piece(d, hf):
        def run():
            res = _dot(vc_s[d, :, hf * half:(hf + 1) * half].astype(BF16), wg_ref[d, hf])
            g_s[d, :, hf * half:(hf + 1) * half] = res[:, :half]
            g_s[d, :, width + hf * half:width + (hf + 1) * half] = res[:, half:]
        return run

    return [piece(d, hf) for d in range(2) for hf in range(2)]


def _sublane_scan(a, h, row, reverse):
    for s in (1, 2, 4):
        if reverse:
            m, shift = row < ROWS - s, ROWS - s
        else:
            m, shift = row >= s, s
        a_sh = jnp.where(m, pltpu.roll(a, shift, 0), 1.0)
        h_sh = jnp.where(m, pltpu.roll(h, shift, 0), 0.0)
        h = h + a * h_sh
        a = a * a_sh
    return a, h


def _rglru_scan(ua_s, br, bi, lam, c0, vc_s, g_s, hl_s, al_s, po_s, q_s, n, width):
    sub = n // ROWS
    row = lax.broadcasted_iota(jnp.int32, (ROWS, width), 0)
    spl_half = (-0.5 * LRU_C) * _softplus(-lam)

    def scan_columns(cols):
        nc = cols.stop - cols.start
        gate_i = slice(width + cols.start, width + cols.stop)

        def rows_of(v, d):
            return jnp.broadcast_to(v[d:d + 1, cols], (ROWS, nc))

        spl_c = [rows_of(spl_half, d) for d in range(2)]
        br_c = [rows_of(0.5 * br, d) for d in range(2)]
        bi_c = [rows_of(0.5 * bi, d) for d in range(2)]

        def coeffs(d, r0):
            vh = vc_s[d, pl.ds(r0, ROWS), cols]
            t_r = jnp.tanh(g_s[d, pl.ds(r0, ROWS), cols] + br_c[d])
            t_i = jnp.tanh(g_s[d, pl.ds(r0, ROWS), gate_i] + bi_c[d])
            la = spl_c[d] + spl_c[d] * t_r
            a = jnp.exp(la)
            x = jnp.tanh(la) * (-1.0 - a * a)
            mult = jnp.where(x > 0.0, x * lax.rsqrt(x), 0.0)
            return a, mult * (vh + vh * t_i)

        def local(j, carry):
            hf, af, hb, ab = carry
            r0 = pl.multiple_of(j * ROWS, ROWS)
            a, b = coeffs(0, r0)
            hf = a * hf + b
            af = a * af
            hl_s[0, pl.ds(r0, ROWS), cols] = hf
            al_s[0, pl.ds(r0, ROWS), cols] = af
            r1 = pl.multiple_of((sub - 1 - j) * ROWS, ROWS)
            a, b = coeffs(1, r1)
            hb = a * hb + b
            ab = a * ab
            hl_s[1, pl.ds(r1, ROWS), cols] = hb
            al_s[1, pl.ds(r1, ROWS), cols] = ab
            return hf, af, hb, ab

        zero = jnp.zeros((ROWS, nc), F32)
        one = jnp.ones((ROWS, nc), F32)
        return lax.fori_loop(0, sub, local, (zero, one, zero, one), unroll=8)

    halves = [scan_columns(slice(c, c + width // 2)) for c in (0, width // 2)]
    hf, af, hb, ab = (jnp.concatenate(parts, axis=1) for parts in zip(*halves))

    af, hf = _sublane_scan(af, hf, row, reverse=False)
    end_f = hf + af * c0
    c_f = jnp.where(row == 0, c0, pltpu.roll(end_f, 1, 0))
    ab, hb = _sublane_scan(ab, hb, row, reverse=True)
    c_b = jnp.where(row == ROWS - 1, 0.0, pltpu.roll(hb, ROWS - 1, 0))
    c_a = jnp.where(row == ROWS - 1, 1.0, pltpu.roll(ab, ROWS - 1, 0))

    if po_s is not None:
        step = 2 * ROWS
        c_f, c_b, c_a = (jnp.concatenate([c, c], axis=0) for c in (c_f, c_b, c_a))

        def fix(j, carry):
            r0 = pl.ds(pl.multiple_of(j * step, step), step)
            a_b = al_s[1, r0, :]
            h = (hl_s[0, r0, :] + al_s[0, r0, :] * c_f) + (hl_s[1, r0, :] + a_b * c_b)
            sg = _silu(ua_s[pl.ds(pl.multiple_of(UA_ROW0 + j * step, ROWS), step), width:])
            po_s[r0, :] = (h * sg).astype(po_s.dtype)
            if q_s is not None:
                q_s[r0, :] = ((a_b * c_a) * sg).astype(q_s.dtype)
            return carry

        lax.fori_loop(0, n // step, fix, 0, unroll=4)

    return end_f[ROWS - 1:ROWS, :], hb[0:1, :], ab[0:1, :]


def _shifted_copy_pieces(vpad_ref, sh_ref):
    n = sh_ref.shape[1]

    def piece(j):
        def run():
            sh_ref[j] = vpad_ref[pl.ds(j, n), :]
        return run

    return [piece(j) for j in range(ROWS)]


def _time_conv_pieces(sh_ref, w_ref, ncol, bases, out_ref):
    def piece(ci, base):
        def run():
            acc = jnp.zeros((CHUNK, ncol), F32)
            for k in range(CONV_K):
                off = SEG_PAD - CONV_PAD + k
                start = base + (off // ROWS) * ROWS
                acc = acc + w_ref[k:k + 1, 0:ncol] * sh_ref[off % ROWS, start:start + CHUNK, :]
            out_ref[ci * CHUNK:(ci + 1) * CHUNK, 0:ncol] = acc
        return run

    return [piece(ci, base) for ci, base in enumerate(bases)]


def _mods_body(act_ref, w_ref, b_ref, o_ref):
    a = _silu(act_ref[...])
    w = w_ref[0]
    a_hi = a.astype(BF16)
    a_lo = (a - a_hi.astype(F32)).astype(BF16)
    w_hi = w.astype(BF16)
    w_lo = (w - w_hi.astype(F32)).astype(BF16)
    both = _dot(jnp.concatenate([a_hi, a_lo], axis=0), w_hi)
    o_ref[0] = both[:ROWS] + (_dot(a_hi, w_lo) + both[ROWS:]) + b_ref[0]


def _mods_call(act, w_mod, b_mod):
    depth, d_model, d3 = w_mod.shape
    ncol = d_model
    return pl.pallas_call(
        _mods_body,
        grid=(depth, d3 // ncol),
        in_specs=[
            pl.BlockSpec((ROWS, d_model), lambda l, n: (0, 0)),
            pl.BlockSpec((1, d_model, ncol), lambda l, n: (l, 0, n)),
            pl.BlockSpec((1, 1, ncol), lambda l, n: (l, 0, n)),
        ],
        out_specs=pl.BlockSpec((1, ROWS, ncol), lambda l, n: (l, 0, n)),
        out_shape=jax.ShapeDtypeStruct((depth, ROWS, d3), F32),
        compiler_params=pltpu.CompilerParams(
            dimension_semantics=("arbitrary", "arbitrary"), vmem_limit_bytes=VMEM_LIMIT),
        name="mods",
    )(act, w_mod, b_mod.reshape(depth, 1, d3))


def _modulate(x, g, m, d_model):
    return _rms_norm(x, g) * (1.0 + m[:, d_model:2 * d_model]) + m[:, :d_model]


def _ctx_body(ctx_ref, mod_ref, gpre_ref, gpost_ref, win_ref, wo_ref, wg_ref, cw_ref, cb_ref,
              br_ref, bi_ref, lam_ref, dww_ref, dwb_ref, lng_ref, lnb_ref,
              h0_ref,
              ua_s, vc_s, g_s, hl_s, al_s, po_s, vpad_s, sh_s, y_s,
              *, depth, d_model, width, n):
    xc = ctx_ref[0]
    zeros_pad = jnp.zeros((SEG_PAD, width), F32)
    no_rows = jnp.zeros((LRU_CONV - 1, width), F32)
    zero_state = jnp.zeros((1, width), F32)
    for l in range(depth):
        update = l < depth - 1
        m = mod_ref[l]
        hc = _modulate(xc, gpre_ref[l:l + 1, :], m, d_model).astype(BF16)
        hcp = _to_sublane_major(hc)
        if update:
            ua_s[UA_ROW0:UA_ROW0 + n, :] = _dot(hcp, win_ref[l, :, :2 * width])
            ub = _dot(hc, win_ref[l, :, 2 * width:])
        else:
            ua_s[UA_ROW0:UA_ROW0 + n, :width] = _dot(hcp, win_ref[l, :, :width])
        _short_convs(ua_s, no_rows, no_rows, cw_ref[l], cb_ref[l], vc_s, n, width)
        for piece in _gate_pieces(vc_s, wg_ref.at[l], g_s, width):
            piece()
        hf_last, hb_first, _ = _rglru_scan(
            ua_s, br_ref[l], bi_ref[l], lam_ref[l], zero_state, vc_s, g_s, hl_s, al_s,
            po_s if update else None, None, n, width)
        h0_ref[0, 2 * l:2 * l + 1, :] = hf_last
        h0_ref[0, 2 * l + 1:2 * l + 2, :] = hb_first
        if update:
            v = ub[:, :width] * jax.nn.sigmoid(ub[:, width:2 * width])
            vpad_s[0:SEG_PAD, :] = zeros_pad
            vpad_s[SEG_PAD:SEG_PAD + n, :] = v
            vpad_s[SEG_PAD + n:, :] = zeros_pad
            for piece in _shifted_copy_pieces(vpad_s, sh_s) + _time_conv_pieces(
                    sh_s, dww_ref.at[l], width, [ci * CHUNK for ci in range(n // CHUNK)], y_s):
                piece()
            y = y_s[...] + dwb_ref[l:l + 1, :]
            conv = _silu(_layer_norm(y, lng_ref[l:l + 1, :], lnb_ref[l:l + 1, :]))
            mix_a = _to_time_order(po_s[...].astype(BF16))
            mix_b = (conv * _silu(ub[:, 2 * width:])).astype(BF16)
            mix = _dot(mix_a, wo_ref[l, :width, :]) + _dot(mix_b, wo_ref[l, width:, :])
            xc = xc + m[:, 2 * d_model:] * _rms_norm(mix, gpost_ref[l:l + 1, :])


def _ctx_call(ctx, mod_c, g_pre, g_post, w_in_bf, w_out_bf, wg, conv_a_w, conv_a_b, b_rgate,
              b_igate, lru_lambda, dw_w, dw_b, ln_g, ln_b):
    bsz, n, d_model = ctx.shape
    depth = w_in_bf.shape[0]
    width = conv_a_b.shape[-1]
    npad = n + 2 * SEG_PAD
    def full(a):
        nd = a.ndim
        return pl.BlockSpec(a.shape, lambda b, _nd=nd: (0,) * _nd)

    params = (mod_c, g_pre, g_post, w_in_bf, w_out_bf, wg, conv_a_w, conv_a_b, b_rgate, b_igate,
              lru_lambda, dw_w, dw_b, ln_g, ln_b)
    return pl.pallas_call(
        functools.partial(_ctx_body, depth=depth, d_model=d_model, width=width, n=n),
        grid=(bsz,),
        in_specs=[pl.BlockSpec((1, n, d_model), lambda b: (b, 0, 0))] + [full(a) for a in params],
        out_specs=pl.BlockSpec((1, 2 * depth, width), lambda b: (b, 0, 0)),
        out_shape=jax.ShapeDtypeStruct((bsz, 2 * depth, width), F32),
        scratch_shapes=[
            pltpu.VMEM((n + 2 * UA_ROW0, 2 * width), F32),
            pltpu.VMEM((2, n, width), F32),
            pltpu.VMEM((2, n, 2 * width), F32),
            pltpu.VMEM((2, n, width), F32),
            pltpu.VMEM((2, n, width), F32),
            pltpu.VMEM((n, width), F32),
            pltpu.VMEM((npad, width), F32),
            pltpu.VMEM((ROWS, npad - ROWS, width), F32),
            pltpu.VMEM((n, width), F32),
        ],
        compiler_params=pltpu.CompilerParams(
            dimension_semantics=("arbitrary",), vmem_limit_bytes=VMEM_LIMIT),
        name="context",
    )(ctx, *params)


def _proj_body(x_ref, mod_ref, g_ref, w_ref, hx_ref, v1_ref, v2_ref, *, l, nt, d_model, width):
    b = pl.program_id(0)
    j = pl.program_id(1)
    is_pad = jnp.logical_or(j == 0, j == nt + 1)

    @pl.when(is_pad)
    def _():
        v2_ref[...] = jnp.zeros(v2_ref.shape, v2_ref.dtype)

    @pl.when(jnp.logical_not(is_pad))
    def _():
        half = width // 2
        rows = ROWS_PER_CHAIN
        for c in range(x_ref.shape[1] // rows):
            rs = slice(c * rows, (c + 1) * rows)
            hb = _modulate(x_ref[0, rs, :], g_ref[l:l + 1, :], mod_ref[l, pl.ds(b, 1), :],
                           d_model).astype(BF16)
            hx_ref[0, rs, :] = hb
            u = _dot(hb, w_ref[0])
            v = u[:, :width] * jax.nn.sigmoid(u[:, width:])
            v1_ref[0, rs, :] = v[:, :half].astype(BF16)
            v2_ref[0, rs, :] = v[:, half:].astype(BF16)


def _proj_call(x, mods, g_pre, w_in_bf, l, width):
    bsz, seq, d_model = x.shape
    half = width // 2
    tt = TILE_PROJ
    nt = seq // tt

    def tok(b, j):
        return (b, jnp.clip(j - 1, 0, nt - 1), 0)

    return pl.pallas_call(
        functools.partial(_proj_body, l=l, nt=nt, d_model=d_model, width=width),
        grid=(bsz, nt + 2),
        in_specs=[
            pl.BlockSpec((1, tt, d_model), tok),
            pl.BlockSpec(mods.shape, lambda b, j: (0, 0, 0)),
            pl.BlockSpec(g_pre.shape, lambda b, j: (0, 0)),
            pl.BlockSpec((1, d_model, 2 * width), lambda b, j: (l, 0, 1)),
        ],
        out_specs=[
            pl.BlockSpec((1, tt, d_model), tok),
            pl.BlockSpec((1, tt, half), tok),
            pl.BlockSpec((1, tt, half), lambda b, j: (b, j, 0)),
        ],
        out_shape=[
            jax.ShapeDtypeStruct((bsz, seq, d_model), BF16),
            jax.ShapeDtypeStruct((bsz, seq, half), BF16),
            jax.ShapeDtypeStruct((bsz, seq + 2 * tt, half), BF16),
        ],
        compiler_params=pltpu.CompilerParams(
            dimension_semantics=("arbitrary", "arbitrary"), vmem_limit_bytes=VMEM_LIMIT),
        name="proj",
    )(x, mods, g_pre, w_in_bf)


def _mix_body(hx_ref, hxp_ref, hxn_ref, v1_ref, v2_ref, wa_ref, wbg_ref, wg_ref, cw_ref, cb_ref,
              br_ref, bi_ref, lam_ref, dww16_ref, dwb_ref, lng_ref, lnb_ref, pmt_ref, pme_ref,
              h0_ref, zero_ref,
              p_ref, q_ref, mb_ref, ab_ref, hbo_ref,
              hxs_s, ua_s, bg_s, vc_s, g_s, hl_s, al_s, carry_s, ev_s, od_s, wr_s, wb_s,
              yr_s, y_s,
              *, l, tt, nt, d_model, width, v2_front):
    i = pl.program_id(1)
    half = width // 2
    nseg = tt // GRID_W
    ext = LRU_CONV - 1
    reach = CONV_PAD * GRID_W
    row_zero = zero_ref[0]

    @pl.when(i == 0)
    def _():
        carry_s[...] = h0_ref[pl.program_id(0), 2 * l:2 * l + 1, :]

    def col_slices(total):
        return [slice(c, c + MXU_N) for c in range(0, total, MXU_N)]

    def zero_based(start, align):
        return pl.multiple_of(row_zero + start, align)

    def prepare(token):
        for k in range(CONV_K):
            wr_s[k] = jnp.broadcast_to(dww16_ref[l, k:k + 1, 0:half], (2 * ROWS, half))
            wb_s[k] = jnp.broadcast_to(dww16_ref[l, k:k + 1, half:], (2 * ROWS, half))

    def spread_rows():
        res = _dot(pme_ref[...], v1_ref[0])
        ev_s[...] = res.astype(BF16)
        moved = jnp.concatenate([res[ROWS:, :], jnp.zeros((ROWS, half), F32)], axis=0)
        od_s[...] = moved.astype(BF16)
        return res[0:ROWS, 0:128]

    def row_conv(chunks):
        def run(token):
            _mark(wr_s.at[CONV_K, :, 0:128], token)
            for ci in chunks:
                for rt in range(CHUNK // (2 * ROWS)):
                    row0 = ci * CHUNK + rt * 2 * ROWS
                    acc = None
                    for g0 in range(0, CONV_K, TAP_GROUP):
                        part = None
                        for k in range(g0, min(g0 + TAP_GROUP, CONV_K)):
                            shift = SEG_PAD - CONV_PAD + k
                            src = row0 + shift * ROWS
                            if shift % 2 == 0:
                                window = ev_s[src:src + 2 * ROWS, :]
                            else:
                                window = od_s[src - ROWS:src + ROWS, :]
                            term = wr_s[row_zero + k] * window
                            part = term if part is None else part + term
                        part = part.astype(F32)
                        acc = part if acc is None else acc + part
                    yr_s[row0:row0 + 2 * ROWS, :] = acc
        return run

    def gather_rows():
        y_s[:, 0:half] = _dot(pmt_ref[...], yr_s[...].astype(BF16))
        return None

    def col_conv(chunks):
        def run(token):
            _mark(wb_s.at[CONV_K, :, 0:128], token)
            for ci in chunks:
                win0 = i * tt + (v2_front - reach) + ci * CHUNK
                for rt in range(CHUNK // (2 * ROWS)):
                    acc = None
                    for g0 in range(0, CONV_K, TAP_GROUP):
                        part = None
                        for k in range(g0, min(g0 + TAP_GROUP, CONV_K)):
                            start = pl.multiple_of(win0 + k * GRID_W + rt * 2 * ROWS, 2 * ROWS)
                            term = wb_s[row_zero + k] * v2_ref[0, pl.ds(start, 2 * ROWS), :]
                            part = term if part is None else part + term
                        part = part.astype(F32)
                        acc = part if acc is None else acc + part
                    row0 = ci * CHUNK + rt * 2 * ROWS
                    y_s[row0:row0 + 2 * ROWS, half:] = acc
        return run

    def permute(slices):
        def run():
            for cs in slices:
                res = _to_sublane_major(hx_ref[0, :, cs])
                hxs_s[0:tt, cs] = res
            return res[0:ROWS, 0:128].astype(F32)
        return run

    def project_a(cs):
        def run():
            res = _dot(hxs_s[0:tt, :], wa_ref[0, :, cs])
            ua_s[UA_ROW0:UA_ROW0 + tt, cs] = res
            return res[0:ROWS, 0:128]
        return run

    def project_b(cs):
        def run():
            res = _dot(hx_ref[0], wbg_ref[0, :, cs])
            bg_s[0:tt, cs] = res
            return res[0:ROWS, 0:128]
        return run

    cols_d = col_slices(d_model)
    cols_a = col_slices(2 * width)
    cols_b = col_slices(width)
    quarter = nseg // 4
    row_stages = [row_conv(range(q * quarter, (q + 1) * quarter)) for q in range(4)]
    col_stages = [col_conv(range(q * 2 * quarter, (q + 1) * 2 * quarter)) for q in range(2)]
    _release(
        [spread_rows, permute(cols_d[:2]), permute(cols_d[2:]), project_b(cols_b[0]),
         project_b(cols_b[1])] + [project_a(cs) for cs in cols_a] + [gather_rows],
        [prepare] + row_stages + col_stages)

    wa_a = wa_ref[0, :, :width]
    a_prev = jnp.where(i > 0, _dot(hxp_ref[0], wa_a)[HALO - ext:, :], 0.0)
    a_next = jnp.where(i < nt - 1, _dot(hxn_ref[0], wa_a)[:ext, :], 0.0)
    _short_convs(ua_s, a_prev, a_next, cw_ref[l], cb_ref[l], vc_s, tt, width)

    def finish_conv(c, nrows):
        def run(token):
            _mark(bg_s.at[tt:tt + ROWS, 0:128], token)
            y = y_s[c * nrows:(c + 1) * nrows, :] + dwb_ref[l:l + 1, :]
            conv = _silu(_layer_norm(y, lng_ref[l:l + 1, :], lnb_ref[l:l + 1, :]))
            out = conv * _silu(bg_s[pl.ds(zero_based(c * nrows, ROWS), nrows), :])
            mb_ref[0, c * nrows:(c + 1) * nrows, :] = out.astype(BF16)
        return run

    def gate_piece(d, hf):
        def run():
            lo = hf * half
            res = _dot(vc_s[d, 0:tt, lo:lo + half].astype(BF16), wg_ref[0, d, hf])
            g_s[d, :, lo:lo + half] = res[:, :half]
            g_s[d, :, width + lo:width + lo + half] = res[:, half:]
            return res[0:ROWS, 0:128]
        return run

    _release([gate_piece(d, hf) for d in range(2) for hf in range(2)],
             [finish_conv(c, tt // 4) for c in range(4)])

    hf_last, hb_first, a_total = _rglru_scan(
        ua_s, br_ref[l], bi_ref[l], lam_ref[l], carry_s[...], vc_s, g_s, hl_s, al_s,
        p_ref.at[0], q_ref.at[0], tt, width)
    carry_s[...] = hf_last
    ab_ref[0, 0] = a_total
    hbo_ref[0, 0] = hb_first


def _mix_call(hx, v1, v2p, w_in_bf, wg, cw, cb, br, bi, lam, dww16, dwb, lng, lnb, h0, l):
    bsz, seq, d_model = hx.shape
    width = cb.shape[-1]
    half = width // 2
    tt = TILE_MIX
    nt = seq // tt
    hb_per_tile = tt // HALO
    n_halo_blocks = seq // HALO
    v2_front = (v2p.shape[1] - seq) // 2
    pme = _row_conv_matrix(tt)

    def full(a):
        nd = a.ndim
        return pl.BlockSpec(a.shape, lambda b, i, _nd=nd: (0,) * _nd)

    small = (cw, cb, br, bi, lam, dww16, dwb, lng, lnb, _time_order_matrix(tt), pme, h0)
    tile = lambda b, i: (b, i, 0)
    out_shapes = [jax.ShapeDtypeStruct((bsz, seq, width), BF16)] * 3 + [
        jax.ShapeDtypeStruct((bsz, nt, 1, width), F32)] * 2
    return pl.pallas_call(
        functools.partial(_mix_body, l=l, tt=tt, nt=nt, d_model=d_model, width=width,
                          v2_front=v2_front),
        grid=(bsz, nt),
        in_specs=[
            pl.BlockSpec((1, tt, d_model), tile),
            pl.BlockSpec((1, HALO, d_model),
                         lambda b, i: (b, jnp.maximum(i * hb_per_tile - 1, 0), 0)),
            pl.BlockSpec((1, HALO, d_model),
                         lambda b, i: (b, jnp.minimum((i + 1) * hb_per_tile, n_halo_blocks - 1), 0)),
            pl.BlockSpec((1, tt, half), tile),
            pl.BlockSpec((1, v2p.shape[1], half), lambda b, i: (b, 0, 0)),
            pl.BlockSpec((1, d_model, 2 * width), lambda b, i: (l, 0, 0)),
            pl.BlockSpec((1, d_model, width), lambda b, i: (l, 0, 4)),
            pl.BlockSpec((1,) + wg.shape[1:], lambda b, i: (l, 0, 0, 0, 0)),
        ] + [full(a) for a in small] + [pl.BlockSpec(memory_space=pltpu.SMEM)],
        out_specs=[pl.BlockSpec((1, tt, width), tile)] * 3 + [
            pl.BlockSpec((1, 1, 1, width), lambda b, i: (b, i, 0, 0))] * 2,
        out_shape=out_shapes,
        scratch_shapes=[
            pltpu.VMEM((tt, d_model), BF16),
            pltpu.VMEM((tt + 2 * UA_ROW0, 2 * width), F32),
            pltpu.VMEM((tt + ROWS, width), F32),
            pltpu.VMEM((2, tt, width), F32),
            pltpu.VMEM((2, tt, 2 * width), F32),
            pltpu.VMEM((2, tt, width), F32),
            pltpu.VMEM((2, tt, width), F32),
            pltpu.VMEM((1, width), F32),
            pltpu.VMEM(pme.shape[:1] + (half,), BF16),
            pltpu.VMEM(pme.shape[:1] + (half,), BF16),
            pltpu.VMEM((CONV_K + 1, 2 * ROWS, half), BF16),
            pltpu.VMEM((CONV_K + 1, 2 * ROWS, half), BF16),
            pltpu.VMEM((tt, half), F32),
            pltpu.VMEM((tt, width), F32),
        ],
        compiler_params=pltpu.CompilerParams(
            dimension_semantics=("arbitrary", "arbitrary"), vmem_limit_bytes=VMEM_LIMIT,
            ),
        name="mixer",
    )(hx, hx, hx, v1, v2p, w_in_bf, w_in_bf, wg, *small, jnp.zeros((1,), jnp.int32))


def _out_body(p_ref, q_ref, mb_ref, x_ref, wo_ref, gpost_ref, mod_ref, ab_ref, hb_ref, h0_ref,
              *rest, l, d_model, width, fuse_next):
    if fuse_next:
        gpre_ref, wbn_ref, _, o_ref, hx_ref, v1_ref, v2_ref, carry_s = rest
    else:
        o_ref, carry_s = rest
    b = pl.program_id(0)

    @pl.when(pl.program_id(1) == 0)
    def _():
        carry_s[...] = h0_ref[b, 2 * l + 1:2 * l + 2, :]

    gate = mod_ref[l, pl.ds(b, 1), 2 * d_model:]
    rows = ROWS_PER_CHAIN
    half = width // 2
    tm = TILE_MIX
    c = carry_s[...]
    for st in reversed(range(p_ref.shape[1] // tm)):
        t0 = st * tm
        mix_a = (p_ref[0, t0:t0 + tm, :].astype(F32)
                 + q_ref[0, t0:t0 + tm, :].astype(F32) * c).astype(BF16)
        mix_t = _to_time_order(mix_a)
        c = hb_ref[0, st] + ab_ref[0, st] * c
        for ch in range(tm // rows):
            rs = slice(t0 + ch * rows, t0 + (ch + 1) * rows)
            lhs = jnp.concatenate([mix_t[ch * rows:(ch + 1) * rows, :], mb_ref[0, rs, :]], axis=1)
            mix = _dot(lhs, wo_ref[0])
            x_new = x_ref[0, rs, :] + gate * _rms_norm(mix, gpost_ref[l:l + 1, :])
            o_ref[0, rs, :] = x_new
            if fuse_next:
                hb = _modulate(x_new, gpre_ref[l + 1:l + 2, :], mod_ref[l + 1, pl.ds(b, 1), :],
                               d_model).astype(BF16)
                hx_ref[0, rs, :] = hb
                u = _dot(hb, wbn_ref[0])
                v = u[:, :width] * jax.nn.sigmoid(u[:, width:])
                v1_ref[0, rs, :] = v[:, :half].astype(BF16)
                v2_ref[0, rs, :] = v[:, half:].astype(BF16)
    carry_s[...] = c


def _out_call(p, q, mb, x, w_out_bf, g_post, mods, ab, hbo, h0, l, nxt=None):
    bsz, seq, d_model = x.shape
    width = p.shape[-1]
    half = width // 2
    tt = TILE_OUT
    tm = TILE_MIX
    nt = seq // tt
    per = tt // tm
    pad_blocks = TILE_PROJ // tt
    rev = lambda b, j: (b, nt - 1 - j, 0)

    def full(a):
        nd = a.ndim
        return pl.BlockSpec(a.shape, lambda b, j, _nd=nd: (0,) * _nd)

    operands = [p, q, mb, x, w_out_bf, g_post, mods, ab, hbo, h0]
    in_specs = [
        pl.BlockSpec((1, tt, width), rev),
        pl.BlockSpec((1, tt, width), rev),
        pl.BlockSpec((1, tt, width), rev),
        pl.BlockSpec((1, tt, d_model), rev),
        pl.BlockSpec((1,) + w_out_bf.shape[1:], lambda b, j: (l, 0, 0)),
        full(g_post),
        full(mods),
        pl.BlockSpec((1, per, 1, width), lambda b, j: (b, nt - 1 - j, 0, 0)),
        pl.BlockSpec((1, per, 1, width), lambda b, j: (b, nt - 1 - j, 0, 0)),
        full(h0),
    ]
    out_specs = [pl.BlockSpec((1, tt, d_model), rev)]
    out_shape = [jax.ShapeDtypeStruct((bsz, seq, d_model), F32)]
    aliases = {}
    if nxt is not None:
        g_pre, w_in_bf, v2_buf = nxt
        operands += [g_pre, w_in_bf, v2_buf]
        in_specs += [
            full(g_pre),
            pl.BlockSpec((1, d_model, 2 * width), lambda b, j: (l + 1, 0, 1)),
            pl.BlockSpec(memory_space=pl.ANY),
        ]
        out_specs += [
            pl.BlockSpec((1, tt, d_model), rev),
            pl.BlockSpec((1, tt, half), rev),
            pl.BlockSpec((1, tt, half), lambda b, j: (b, nt - 1 - j + pad_blocks, 0)),
        ]
        out_shape += [
            jax.ShapeDtypeStruct((bsz, seq, d_model), BF16),
            jax.ShapeDtypeStruct((bsz, seq, half), BF16),
            jax.ShapeDtypeStruct(v2_buf.shape, BF16),
        ]
        aliases = {len(operands) - 1: 3}
    res = pl.pallas_call(
        functools.partial(_out_body, l=l, d_model=d_model, width=width,
                          fuse_next=nxt is not None),
        grid=(bsz, nt),
        in_specs=in_specs,
        out_specs=out_specs,
        out_shape=out_shape,
        input_output_aliases=aliases,
        scratch_shapes=[pltpu.VMEM((1, width), F32)],
        compiler_params=pltpu.CompilerParams(
            dimension_semantics=("arbitrary", "arbitrary"), vmem_limit_bytes=VMEM_LIMIT),
        name="out_proj" if nxt is not None else "out",
    )(*operands)
    return res if nxt is not None else res[0]


def _pack_gate_weights(w_r, w_i):
    depth, ndir, heads, hd, _ = w_r.shape
    hh = heads // 2
    eye = jnp.eye(hh, dtype=w_r.dtype)

    def bd(w):
        w = w.reshape(depth, ndir, 2, hh, hd, hd)
        return jnp.einsum("ldfhij,hg->ldfhigj", w, eye).reshape(depth, ndir, 2, hh * hd, hh * hd)

    return jnp.concatenate([bd(w_r), bd(w_i)], axis=-1).astype(BF16)


def kernel(x, c, ctx, c_ctx, w_mod, b_mod, g_pre, g_post, w_in, conv_a_w, conv_a_b, w_rgate,
           b_rgate, w_igate, b_igate, lru_lambda, dw_w, dw_b, ln_g, ln_b, w_out):
    bsz, seq, d_model = x.shape
    depth = w_mod.shape[0]
    width = conv_a_b.shape[-1]
    assert bsz + 1 <= ROWS and seq % TILE_PROJ == 0 and seq % TILE_MIX == 0
    assert TILE_MIX % GRID_W == 0 and TILE_PROJ >= CONV_PAD * GRID_W
    assert seq % TILE_OUT == 0 and TILE_OUT % TILE_MIX == 0 and TILE_PROJ % TILE_OUT == 0
    assert ctx.shape[1] % CHUNK == 0 and width % MXU_N == 0 and d_model % MXU_N == 0

    act = jnp.concatenate(
        [c, c_ctx[None, :], jnp.zeros((ROWS - bsz - 1, d_model), F32)], axis=0)
    mods = _mods_call(act, w_mod, b_mod)

    w_in_bf = w_in.astype(BF16)
    w_out_bf = w_out.astype(BF16)
    wg = _pack_gate_weights(w_rgate, w_igate)

    h0 = _ctx_call(ctx, mods[:, bsz:bsz + 1, :], g_pre, g_post, w_in_bf, w_out_bf, wg, conv_a_w,
                   conv_a_b, b_rgate, b_igate, lru_lambda, dw_w, dw_b, ln_g, ln_b)

    dw_w16 = dw_w.astype(BF16)
    hx, v1, v2p = _proj_call(x, mods, g_pre, w_in_bf, 0, width)
    for l in range(depth):
        p, q, mb, ab, hbo = _mix_call(
            hx, v1, v2p, w_in_bf, wg, conv_a_w, conv_a_b, b_rgate, b_igate, lru_lambda, dw_w16,
            dw_b, ln_g, ln_b, h0, l)
        if l + 1 < depth:
            x, hx, v1, v2p = _out_call(p, q, mb, x, w_out_bf, g_post, mods, ab, hbo, h0, l,
                                       nxt=(g_pre, w_in_bf, v2p))
        else:
            x = _out_call(p, q, mb, x, w_out_bf, g_post, mods, ab, hbo, h0, l)
    return x
```

```python
import functools

import jax
import jax.numpy as jnp
import numpy as np
from jax import lax
from jax.experimental import pallas as pl
from jax.experimental.pallas import tpu as pltpu

F32 = jnp.float32
BF16 = jnp.bfloat16

EPS = 1e-6
LRU_C = 8.0
LRU_CONV = 4
CONV_K = 31
CONV_PAD = CONV_K // 2
GRID_W = 64
HALO = 16
ROWS = 8
CHUNK = 64
SEG_PAD = 16
MXU_N = 256
TAP_GROUP = 16
ROWS_PER_CHAIN = 512

TILE_PROJ = 1024
TILE_MIX = 512
TILE_OUT = 1024
VMEM_LIMIT = 56 * 1024 * 1024


def _sigmoid(x):
    return 0.5 + 0.5 * jnp.tanh(0.5 * x)


def _silu(x):
    return x * _sigmoid(x)


def _softplus(x):
    return jnp.maximum(x, 0.0) + jnp.log1p(jnp.exp(-jnp.abs(x)))


def _rms_norm(x, g):
    ms = jnp.mean(x * x, axis=-1, keepdims=True)
    return x * lax.rsqrt(ms + EPS) * g


def _layer_norm(x, g, b):
    mu = jnp.mean(x, axis=-1, keepdims=True)
    xc = x - mu
    var = jnp.mean(xc * xc, axis=-1, keepdims=True)
    return xc * lax.rsqrt(var + EPS) * g + b


def _dot(a, b):
    return jnp.dot(a, b, preferred_element_type=F32)


def _mark(spare_ref, token):
    if token is None:
        return
    reps = spare_ref.shape[0] // ROWS
    spare_ref[...] = jnp.concatenate([token] * reps, axis=0).astype(spare_ref.dtype)


def _release(mxu_pieces, vpu_pieces):
    token = None
    for k in range(max(len(mxu_pieces), len(vpu_pieces))):
        new_token = mxu_pieces[k]() if k < len(mxu_pieces) else token
        if k < len(vpu_pieces):
            vpu_pieces[k](token)
        token = new_token


def _to_sublane_major(x):
    n, cols = x.shape
    return jnp.swapaxes(x.reshape(ROWS, n // ROWS, cols), 0, 1).reshape(n, cols)


def _to_time_order(x):
    n, cols = x.shape
    return jnp.swapaxes(x.reshape(n // ROWS, ROWS, cols), 0, 1).reshape(n, cols)


def _time_order_matrix(n):
    i = np.arange(n)
    src = ROWS * (i % (n // ROWS)) + i // (n // ROWS)
    return jnp.asarray((src[:, None] == i[None, :]).astype(np.float32), BF16)


UA_ROW0 = (LRU_CONV - 1) * ROWS


def _row_conv_matrix(n):
    cols = n // ROWS
    rows = (cols + 2 * SEG_PAD) * ROWS
    mat = np.zeros((rows, n), np.float32)
    for i in range(rows):
        q, s = divmod(i, ROWS)
        if SEG_PAD <= q < SEG_PAD + cols:
            mat[i, cols * s + q - SEG_PAD] = 1.0
    return jnp.asarray(mat, BF16)


def _short_convs(ua_s, a_prev, a_next, cw, cb, vc_s, n, width):
    sub = n // ROWS
    ext = LRU_CONV - 1
    row = lax.broadcasted_iota(jnp.int32, (ROWS, width), 0)
    cw = 0.5 * cw
    cb = 0.5 * cb

    def blk(p):
        return slice(p * ROWS, (p + 1) * ROWS)

    for k in range(1, ext + 1):
        tail = pltpu.roll(ua_s[blk(ext + sub - k), :width], 1, 0)
        ua_s[blk(ext - k), :width] = jnp.where(row == 0, a_prev[ext - k:ext - k + 1, :], tail)
    for k in range(ext):
        head = pltpu.roll(ua_s[blk(ext + k), :width], ROWS - 1, 0)
        ua_s[blk(ext + sub + k), :width] = jnp.where(row == ROWS - 1, a_next[k:k + 1, :], head)

    def chunk(ci, carry):
        base = ci * CHUNK
        for d in range(2):
            first = 0 if d == 0 else ext
            acc = jnp.broadcast_to(cb[d:d + 1, :], (CHUNK, width))
            for k in range(LRU_CONV):
                start = pl.multiple_of(base + (first + k) * ROWS, ROWS)
                acc = acc + cw[d, k:k + 1, :] * ua_s[pl.ds(start, CHUNK), :width]
            vc_s[d, pl.ds(pl.multiple_of(base, CHUNK), CHUNK), :] = acc
        return carry

    lax.fori_loop(0, n // CHUNK, chunk, 0)


def _gate_pieces(vc_s, wg_ref, g_s, width):
    half = width // 2

    def piece(d, hf):
        def run():
            res = _dot(vc_s[d, :, hf * half:(hf + 1) * half].astype(BF16), wg_ref[d, hf])
            g_s[d, :, hf * half:(hf + 1) * half] = res[:, :half]
            g_s[d, :, width + hf * half:width + (hf + 1) * half] = res[:, half:]
        return run

    return [piece(d, hf) for d in range(2) for hf in range(2)]


def _sublane_scan(a, h, row, reverse):
    for s in (1, 2, 4):
        if reverse:
            m, shift = row < ROWS - s, ROWS - s
        else:
            m, shift = row >= s, s
        a_sh = jnp.where(m, pltpu.roll(a, shift, 0), 1.0)
        h_sh = jnp.where(m, pltpu.roll(h, shift, 0), 0.0)
        h = h + a * h_sh
        a = a * a_sh
    return a, h


def _rglru_scan(ua_s, br, bi, lam, c0, vc_s, g_s, hl_s, al_s, po_s, q_s, n, width):
    sub = n // ROWS
    row = lax.broadcasted_iota(jnp.int32, (ROWS, width), 0)
    spl_half = (-0.5 * LRU_C) * _softplus(-lam)

    def scan_columns(cols):
        nc = cols.stop - cols.start
        gate_i = slice(width + cols.start, width + cols.stop)

        def rows_of(v, d):
            return jnp.broadcast_to(v[d:d + 1, cols], (ROWS, nc))

        spl_c = [rows_of(spl_half, d) for d in range(2)]
        br_c = [rows_of(0.5 * br, d) for d in range(2)]
        bi_c = [rows_of(0.5 * bi, d) for d in range(2)]

        def coeffs(d, r0):
            vh = vc_s[d, pl.ds(r0, ROWS), cols]
            t_r = jnp.tanh(g_s[d, pl.ds(r0, ROWS), cols] + br_c[d])
            t_i = jnp.tanh(g_s[d, pl.ds(r0, ROWS), gate_i] + bi_c[d])
            la = spl_c[d] + spl_c[d] * t_r
            a = jnp.exp(la)
            x = jnp.tanh(la) * (-1.0 - a * a)
            mult = jnp.where(x > 0.0, x * lax.rsqrt(x), 0.0)
            return a, mult * (vh + vh * t_i)

        def local(j, carry):
            hf, af, hb, ab = carry
            r0 = pl.multiple_of(j * ROWS, ROWS)
            a, b = coeffs(0, r0)
            hf = a * hf + b
            af = a * af
            hl_s[0, pl.ds(r0, ROWS), cols] = hf
            al_s[0, pl.ds(r0, ROWS), cols] = af
            r1 = pl.multiple_of((sub - 1 - j) * ROWS, ROWS)
            a, b = coeffs(1, r1)
            hb = a * hb + b
            ab = a * ab
            hl_s[1, pl.ds(r1, ROWS), cols] = hb
            al_s[1, pl.ds(r1, ROWS), cols] = ab
            return hf, af, hb, ab

        zero = jnp.zeros((ROWS, nc), F32)
        one = jnp.ones((ROWS, nc), F32)
        return lax.fori_loop(0, sub, local, (zero, one, zero, one), unroll=8)

    halves = [scan_columns(slice(c, c + width // 2)) for c in (0, width // 2)]
    hf, af, hb, ab = (jnp.concatenate(parts, axis=1) for parts in zip(*halves))

    af, hf = _sublane_scan(af, hf, row, reverse=False)
    end_f = hf + af * c0
    c_f = jnp.where(row == 0, c0, pltpu.roll(end_f, 1, 0))
    ab, hb = _sublane_scan(ab, hb, row, reverse=True)
    c_b = jnp.where(row == ROWS - 1, 0.0, pltpu.roll(hb, ROWS - 1, 0))
    c_a = jnp.where(row == ROWS - 1, 1.0, pltpu.roll(ab, ROWS - 1, 0))

    if po_s is not None:
        step = 2 * ROWS
        c_f, c_b, c_a = (jnp.concatenate([c, c], axis=0) for c in (c_f, c_b, c_a))

        def fix(j, carry):
            r0 = pl.ds(pl.multiple_of(j * step, step), step)
            a_b = al_s[1, r0, :]
            h = (hl_s[0, r0, :] + al_s[0, r0, :] * c_f) + (hl_s[1, r0, :] + a_b * c_b)
            sg = _silu(ua_s[pl.ds(pl.multiple_of(UA_ROW0 + j * step, ROWS), step), width:])
            po_s[r0, :] = (h * sg).astype(po_s.dtype)
            if q_s is not None:
                q_s[r0, :] = ((a_b * c_a) * sg).astype(q_s.dtype)
            return carry

        lax.fori_loop(0, n // step, fix, 0, unroll=4)

    return end_f[ROWS - 1:ROWS, :], hb[0:1, :], ab[0:1, :]


def _shifted_copy_pieces(vpad_ref, sh_ref):
    n = sh_ref.shape[1]

    def piece(j):
        def run():
            sh_ref[j] = vpad_ref[pl.ds(j, n), :]
        return run

    return [piece(j) for j in range(ROWS)]


def _time_conv_pieces(sh_ref, w_ref, ncol, bases, out_ref):
    def piece(ci, base):
        def run():
            acc = jnp.zeros((CHUNK, ncol), F32)
            for k in range(CONV_K):
                off = SEG_PAD - CONV_PAD + k
                start = base + (off // ROWS) * ROWS
                acc = acc + w_ref[k:k + 1, 0:ncol] * sh_ref[off % ROWS, start:start + CHUNK, :]
            out_ref[ci * CHUNK:(ci + 1) * CHUNK, 0:ncol] = acc
        return run

    return [piece(ci, base) for ci, base in enumerate(bases)]


def _mods_body(act_ref, w_ref, b_ref, o_ref):
    a = _silu(act_ref[...])
    w = w_ref[0]
    a_hi = a.astype(BF16)
    a_lo = (a - a_hi.astype(F32)).astype(BF16)
    w_hi = w.astype(BF16)
    w_lo = (w - w_hi.astype(F32)).astype(BF16)
    both = _dot(jnp.concatenate([a_hi, a_lo], axis=0), w_hi)
    o_ref[0] = both[:ROWS] + (_dot(a_hi, w_lo) + both[ROWS:]) + b_ref[0]


def _mods_call(act, w_mod, b_mod):
    depth, d_model, d3 = w_mod.shape
    ncol = d_model
    return pl.pallas_call(
        _mods_body,
        grid=(depth, d3 // ncol),
        in_specs=[
            pl.BlockSpec((ROWS, d_model), lambda l, n: (0, 0)),
            pl.BlockSpec((1, d_model, ncol), lambda l, n: (l, 0, n)),
            pl.BlockSpec((1, 1, ncol), lambda l, n: (l, 0, n)),
        ],
        out_specs=pl.BlockSpec((1, ROWS, ncol), lambda l, n: (l, 0, n)),
        out_shape=jax.ShapeDtypeStruct((depth, ROWS, d3), F32),
        compiler_params=pltpu.CompilerParams(
            dimension_semantics=("arbitrary", "arbitrary"), vmem_limit_bytes=VMEM_LIMIT),
        name="mods",
    )(act, w_mod, b_mod.reshape(depth, 1, d3))


def _modulate(x, g, m, d_model):
    return _rms_norm(x, g) * (1.0 + m[:, d_model:2 * d_model]) + m[:, :d_model]


def _ctx_body(ctx_ref, mod_ref, gpre_ref, gpost_ref, win_ref, wlast_ref, wo_ref, wg_ref, cw_ref,
              cb_ref, br_ref, bi_ref, lam_ref, dww_ref, dwb_ref, lng_ref, lnb_ref,
              h0_ref,
              ua_s, vc_s, g_s, hl_s, al_s, po_s, vpad_s, sh_s, y_s,
              *, depth, d_model, width, n):
    xc = ctx_ref[0]
    zeros_pad = jnp.zeros((SEG_PAD, width), F32)
    no_rows = jnp.zeros((LRU_CONV - 1, width), F32)
    zero_state = jnp.zeros((1, width), F32)
    for l in range(depth):
        update = l < depth - 1
        m = mod_ref[l]
        hc = _modulate(xc, gpre_ref[l:l + 1, :], m, d_model).astype(BF16)
        hcp = _to_sublane_major(hc)
        if update:
            ua_s[UA_ROW0:UA_ROW0 + n, :] = _dot(hcp, win_ref[l, :, :2 * width])
            ub = _dot(hc, win_ref[l, :, 2 * width:])
        else:
            ua_s[UA_ROW0:UA_ROW0 + n, :width] = _dot(hcp, wlast_ref[0])
        _short_convs(ua_s, no_rows, no_rows, cw_ref[l], cb_ref[l], vc_s, n, width)
        for piece in _gate_pieces(vc_s, wg_ref.at[l], g_s, width):
            piece()
        hf_last, hb_first, _ = _rglru_scan(
            ua_s, br_ref[l], bi_ref[l], lam_ref[l], zero_state, vc_s, g_s, hl_s, al_s,
            po_s if update else None, None, n, width)
        h0_ref[0, 2 * l:2 * l + 1, :] = hf_last
        h0_ref[0, 2 * l + 1:2 * l + 2, :] = hb_first
        if update:
            v = ub[:, :width] * _sigmoid(ub[:, width:2 * width])
            vpad_s[0:SEG_PAD, :] = zeros_pad
            vpad_s[SEG_PAD:SEG_PAD + n, :] = v
            vpad_s[SEG_PAD + n:, :] = zeros_pad
            for piece in _shifted_copy_pieces(vpad_s, sh_s) + _time_conv_pieces(
                    sh_s, dww_ref.at[l], width, [ci * CHUNK for ci in range(n // CHUNK)], y_s):
                piece()
            y = y_s[...] + dwb_ref[l:l + 1, :]
            conv = _silu(_layer_norm(y, lng_ref[l:l + 1, :], lnb_ref[l:l + 1, :]))
            mix_a = _to_time_order(po_s[...].astype(BF16))
            mix_b = (conv * _silu(ub[:, 2 * width:])).astype(BF16)
            mix = _dot(mix_a, wo_ref[l, :width, :]) + _dot(mix_b, wo_ref[l, width:, :])
            xc = xc + m[:, 2 * d_model:] * _rms_norm(mix, gpost_ref[l:l + 1, :])


def _ctx_call(ctx, mod_c, g_pre, g_post, w_in_bf, w_out_bf, wg, conv_a_w, conv_a_b, b_rgate,
              b_igate, lru_lambda, dw_w, dw_b, ln_g, ln_b):
    bsz, n, d_model = ctx.shape
    depth = w_in_bf.shape[0]
    width = conv_a_b.shape[-1]
    npad = n + 2 * SEG_PAD
    def full(a):
        nd = a.ndim
        return pl.BlockSpec(a.shape, lambda b, _nd=nd: (0,) * _nd)

    def head(a):
        return pl.BlockSpec((depth - 1,) + a.shape[1:], lambda b: (0, 0, 0))

    small = (wg, conv_a_w, conv_a_b, b_rgate, b_igate, lru_lambda, dw_w, dw_b, ln_g, ln_b)
    params = (mod_c, g_pre, g_post, w_in_bf, w_in_bf, w_out_bf) + small
    return pl.pallas_call(
        functools.partial(_ctx_body, depth=depth, d_model=d_model, width=width, n=n),
        grid=(bsz,),
        in_specs=[pl.BlockSpec((1, n, d_model), lambda b: (b, 0, 0)), full(mod_c), full(g_pre),
                  full(g_post), head(w_in_bf),
                  pl.BlockSpec((1, d_model, width), lambda b: (depth - 1, 0, 0)),
                  head(w_out_bf)] + [full(a) for a in small],
        out_specs=pl.BlockSpec((1, 2 * depth, width), lambda b: (b, 0, 0)),
        out_shape=jax.ShapeDtypeStruct((bsz, 2 * depth, width), F32),
        scratch_shapes=[
            pltpu.VMEM((n + 2 * UA_ROW0, 2 * width), F32),
            pltpu.VMEM((2, n, width), F32),
            pltpu.VMEM((2, n, 2 * width), F32),
            pltpu.VMEM((2, n, width), F32),
            pltpu.VMEM((2, n, width), F32),
            pltpu.VMEM((n, width), F32),
            pltpu.VMEM((npad, width), F32),
            pltpu.VMEM((ROWS, npad - ROWS, width), F32),
            pltpu.VMEM((n, width), F32),
        ],
        compiler_params=pltpu.CompilerParams(
            dimension_semantics=("arbitrary",), vmem_limit_bytes=VMEM_LIMIT),
        name="context",
    )(ctx, *params)


def _proj_body(x_ref, mod_ref, g_ref, w_ref, hx_ref, v1_ref, v2_ref, *, l, nt, d_model, width):
    b = pl.program_id(0)
    j = pl.program_id(1)
    is_pad = jnp.logical_or(j == 0, j == nt + 1)

    @pl.when(is_pad)
    def _():
        v2_ref[...] = jnp.zeros(v2_ref.shape, v2_ref.dtype)

    @pl.when(jnp.logical_not(is_pad))
    def _():
        half = width // 2
        rows = ROWS_PER_CHAIN
        for c in range(x_ref.shape[1] // rows):
            rs = slice(c * rows, (c + 1) * rows)
            hb = _modulate(x_ref[0, rs, :], g_ref[l:l + 1, :], mod_ref[l, pl.ds(b, 1), :],
                           d_model).astype(BF16)
            hx_ref[0, rs, :] = hb
            u = _dot(hb, w_ref[0])
            v = u[:, :width] * _sigmoid(u[:, width:])
            v1_ref[0, rs, :] = v[:, :half].astype(BF16)
            v2_ref[0, rs, :] = v[:, half:].astype(BF16)


def _proj_call(x, mods, g_pre, w_in_bf, l, width):
    bsz, seq, d_model = x.shape
    half = width // 2
    tt = TILE_PROJ
    nt = seq // tt

    def tok(b, j):
        return (b, jnp.clip(j - 1, 0, nt - 1), 0)

    return pl.pallas_call(
        functools.partial(_proj_body, l=l, nt=nt, d_model=d_model, width=width),
        grid=(bsz, nt + 2),
        in_specs=[
            pl.BlockSpec((1, tt, d_model), tok),
            pl.BlockSpec(mods.shape, lambda b, j: (0, 0, 0)),
            pl.BlockSpec(g_pre.shape, lambda b, j: (0, 0)),
            pl.BlockSpec((1, d_model, 2 * width), lambda b, j: (l, 0, 1)),
        ],
        out_specs=[
            pl.BlockSpec((1, tt, d_model), tok),
            pl.BlockSpec((1, tt, half), tok),
            pl.BlockSpec((1, tt, half), lambda b, j: (b, j, 0)),
        ],
        out_shape=[
            jax.ShapeDtypeStruct((bsz, seq, d_model), BF16),
            jax.ShapeDtypeStruct((bsz, seq, half), BF16),
            jax.ShapeDtypeStruct((bsz, seq + 2 * tt, half), BF16),
        ],
        compiler_params=pltpu.CompilerParams(
            dimension_semantics=("arbitrary", "arbitrary"), vmem_limit_bytes=VMEM_LIMIT),
        name="proj",
    )(x, mods, g_pre, w_in_bf)


def _mix_body(hx_ref, hxp_ref, hxn_ref, v1_ref, v2_ref, wa_ref, wbg_ref, wg_ref, cw_ref, cb_ref,
              br_ref, bi_ref, lam_ref, dww16_ref, dwb_ref, lng_ref, lnb_ref, pmt_ref, pme_ref,
              h0_ref, zero_ref,
              p_ref, q_ref, mb_ref, ab_ref, hbo_ref,
              hxs_s, ua_s, bg_s, vc_s, g_s, hl_s, al_s, carry_s, ev_s, od_s, wr_s, wb_s,
              yr_s, y_s,
              *, l, tt, nt, d_model, width, v2_front):
    i = pl.program_id(1)
    half = width // 2
    nseg = tt // GRID_W
    ext = LRU_CONV - 1
    reach = CONV_PAD * GRID_W
    row_zero = zero_ref[0]

    @pl.when(i == 0)
    def _():
        carry_s[...] = h0_ref[pl.program_id(0), 2 * l:2 * l + 1, :]

    def col_slices(total):
        return [slice(c, c + MXU_N) for c in range(0, total, MXU_N)]

    def zero_based(start, align):
        return pl.multiple_of(row_zero + start, align)

    def prepare(token):
        for k in range(CONV_K):
            wr_s[k] = jnp.broadcast_to(dww16_ref[l, k:k + 1, 0:half], (2 * ROWS, half))
            wb_s[k] = jnp.broadcast_to(dww16_ref[l, k:k + 1, half:], (2 * ROWS, half))

    def spread_rows():
        res = _dot(pme_ref[...], v1_ref[0])
        ev_s[...] = res.astype(BF16)
        moved = jnp.concatenate([res[ROWS:, :], jnp.zeros((ROWS, half), F32)], axis=0)
        od_s[...] = moved.astype(BF16)
        return res[0:ROWS, 0:128]

    def row_conv(chunks):
        def run(token):
            _mark(wr_s.at[CONV_K, :, 0:128], token)
            for ci in chunks:
                for rt in range(CHUNK // (2 * ROWS)):
                    row0 = ci * CHUNK + rt * 2 * ROWS
                    acc = None
                    for g0 in range(0, CONV_K, TAP_GROUP):
                        part = None
                        for k in range(g0, min(g0 + TAP_GROUP, CONV_K)):
                            shift = SEG_PAD - CONV_PAD + k
                            src = row0 + shift * ROWS
                            if shift % 2 == 0:
                                window = ev_s[src:src + 2 * ROWS, :]
                            else:
                                window = od_s[src - ROWS:src + ROWS, :]
                            term = wr_s[row_zero + k] * window
                            part = term if part is None else part + term
                        part = part.astype(F32)
                        acc = part if acc is None else acc + part
                    yr_s[row0:row0 + 2 * ROWS, :] = acc
        return run

    def gather_rows():
        y_s[:, 0:half] = _dot(pmt_ref[...], yr_s[...].astype(BF16))
        return None

    def col_conv(chunks):
        def run(token):
            _mark(wb_s.at[CONV_K, :, 0:128], token)
            for ci in chunks:
                win0 = i * tt + (v2_front - reach) + ci * CHUNK
                for rt in range(CHUNK // (2 * ROWS)):
                    acc = None
                    for g0 in range(0, CONV_K, TAP_GROUP):
                        part = None
                        for k in range(g0, min(g0 + TAP_GROUP, CONV_K)):
                            start = pl.multiple_of(win0 + k * GRID_W + rt * 2 * ROWS, 2 * ROWS)
                            term = wb_s[row_zero + k] * v2_ref[0, pl.ds(start, 2 * ROWS), :]
                            part = term if part is None else part + term
                        part = part.astype(F32)
                        acc = part if acc is None else acc + part
                    row0 = ci * CHUNK + rt * 2 * ROWS
                    y_s[row0:row0 + 2 * ROWS, half:] = acc
        return run

    def permute(slices):
        def run():
            for cs in slices:
                res = _to_sublane_major(hx_ref[0, :, cs])
                hxs_s[0:tt, cs] = res
            return res[0:ROWS, 0:128].astype(F32)
        return run

    def project_a(cs):
        def run():
            res = _dot(hxs_s[0:tt, :], wa_ref[0, :, cs])
            ua_s[UA_ROW0:UA_ROW0 + tt, cs] = res
            return res[0:ROWS, 0:128]
        return run

    def project_b(cs):
        def run():
            res = _dot(hx_ref[0], wbg_ref[0, :, cs])
            bg_s[0:tt, cs] = res
            return res[0:ROWS, 0:128]
        return run

    cols_d = col_slices(d_model)
    cols_a = col_slices(2 * width)
    cols_b = col_slices(width)
    quarter = nseg // 4
    row_stages = [row_conv(range(q * quarter, (q + 1) * quarter)) for q in range(4)]
    col_stages = [col_conv(range(q * 2 * quarter, (q + 1) * 2 * quarter)) for q in range(2)]
    _release(
        [spread_rows, permute(cols_d[:2]), permute(cols_d[2:]), project_b(cols_b[0]),
         project_b(cols_b[1])] + [project_a(cs) for cs in cols_a] + [gather_rows],
        [prepare] + row_stages + col_stages)

    wa_a = wa_ref[0, :, :width]
    a_prev = jnp.where(i > 0, _dot(hxp_ref[0], wa_a)[HALO - ext:, :], 0.0)
    a_next = jnp.where(i < nt - 1, _dot(hxn_ref[0], wa_a)[:ext, :], 0.0)
    _short_convs(ua_s, a_prev, a_next, cw_ref[l], cb_ref[l], vc_s, tt, width)

    def finish_conv(c, nrows):
        def run(token):
            _mark(bg_s.at[tt:tt + ROWS, 0:128], token)
            y = y_s[c * nrows:(c + 1) * nrows, :] + dwb_ref[l:l + 1, :]
            conv = _silu(_layer_norm(y, lng_ref[l:l + 1, :], lnb_ref[l:l + 1, :]))
            out = conv * _silu(bg_s[pl.ds(zero_based(c * nrows, ROWS), nrows), :])
            mb_ref[0, c * nrows:(c + 1) * nrows, :] = out.astype(BF16)
        return run

    def gate_piece(d, hf):
        def run():
            lo = hf * half
            res = _dot(vc_s[d, 0:tt, lo:lo + half].astype(BF16), wg_ref[0, d, hf])
            g_s[d, :, lo:lo + half] = res[:, :half]
            g_s[d, :, width + lo:width + lo + half] = res[:, half:]
            return res[0:ROWS, 0:128]
        return run

    _release([gate_piece(d, hf) for d in range(2) for hf in range(2)],
             [finish_conv(c, tt // 4) for c in range(4)])

    hf_last, hb_first, a_total = _rglru_scan(
        ua_s, br_ref[l], bi_ref[l], lam_ref[l], carry_s[...], vc_s, g_s, hl_s, al_s,
        p_ref.at[0], q_ref.at[0], tt, width)
    carry_s[...] = hf_last
    ab_ref[0, 0] = a_total
    hbo_ref[0, 0] = hb_first


def _mix_call(hx, v1, v2p, w_in_bf, wg, cw, cb, br, bi, lam, dww16, dwb, lng, lnb, h0, l):
    bsz, seq, d_model = hx.shape
    width = cb.shape[-1]
    half = width // 2
    tt = TILE_MIX
    nt = seq // tt
    hb_per_tile = tt // HALO
    n_halo_blocks = seq // HALO
    v2_front = (v2p.shape[1] - seq) // 2
    pme = _row_conv_matrix(tt)

    def full(a):
        nd = a.ndim
        return pl.BlockSpec(a.shape, lambda b, i, _nd=nd: (0,) * _nd)

    small = (cw, cb, br, bi, lam, dww16, dwb, lng, lnb, _time_order_matrix(tt), pme, h0)
    tile = lambda b, i: (b, i, 0)
    out_shapes = [jax.ShapeDtypeStruct((bsz, seq, width), BF16)] * 3 + [
        jax.ShapeDtypeStruct((bsz, nt, 1, width), F32)] * 2
    return pl.pallas_call(
        functools.partial(_mix_body, l=l, tt=tt, nt=nt, d_model=d_model, width=width,
                          v2_front=v2_front),
        grid=(bsz, nt),
        in_specs=[
            pl.BlockSpec((1, tt, d_model), tile),
            pl.BlockSpec((1, HALO, d_model),
                         lambda b, i: (b, jnp.maximum(i * hb_per_tile - 1, 0), 0)),
            pl.BlockSpec((1, HALO, d_model),
                         lambda b, i: (b, jnp.minimum((i + 1) * hb_per_tile, n_halo_blocks - 1), 0)),
            pl.BlockSpec((1, tt, half), tile),
            pl.BlockSpec((1, v2p.shape[1], half), lambda b, i: (b, 0, 0)),
            pl.BlockSpec((1, d_model, 2 * width), lambda b, i: (l, 0, 0)),
            pl.BlockSpec((1, d_model, width), lambda b, i: (l, 0, 4)),
            pl.BlockSpec((1,) + wg.shape[1:], lambda b, i: (l, 0, 0, 0, 0)),
        ] + [full(a) for a in small] + [pl.BlockSpec(memory_space=pltpu.SMEM)],
        out_specs=[pl.BlockSpec((1, tt, width), tile)] * 3 + [
            pl.BlockSpec((1, 1, 1, width), lambda b, i: (b, i, 0, 0))] * 2,
        out_shape=out_shapes,
        scratch_shapes=[
            pltpu.VMEM((tt, d_model), BF16),
            pltpu.VMEM((tt + 2 * UA_ROW0, 2 * width), F32),
            pltpu.VMEM((tt + ROWS, width), F32),
            pltpu.VMEM((2, tt, width), F32),
            pltpu.VMEM((2, tt, 2 * width), F32),
            pltpu.VMEM((2, tt, width), F32),
            pltpu.VMEM((2, tt, width), F32),
            pltpu.VMEM((1, width), F32),
            pltpu.VMEM(pme.shape[:1] + (half,), BF16),
            pltpu.VMEM(pme.shape[:1] + (half,), BF16),
            pltpu.VMEM((CONV_K + 1, 2 * ROWS, half), BF16),
            pltpu.VMEM((CONV_K + 1, 2 * ROWS, half), BF16),
            pltpu.VMEM((tt, half), F32),
            pltpu.VMEM((tt, width), F32),
        ],
        compiler_params=pltpu.CompilerParams(
            dimension_semantics=("arbitrary", "arbitrary"), vmem_limit_bytes=VMEM_LIMIT,
            ),
        name="mixer",
    )(hx, hx, hx, v1, v2p, w_in_bf, w_in_bf, wg, *small, jnp.zeros((1,), jnp.int32))


def _out_body(p_ref, q_ref, mb_ref, x_ref, wo_ref, gpost_ref, mod_ref, ab_ref, hb_ref, h0_ref,
              *rest, l, d_model, width, fuse_next):
    if fuse_next:
        gpre_ref, wbn_ref, _, o_ref, hx_ref, v1_ref, v2_ref, carry_s = rest
    else:
        o_ref, carry_s = rest
    b = pl.program_id(0)

    @pl.when(pl.program_id(1) == 0)
    def _():
        carry_s[...] = h0_ref[b, 2 * l + 1:2 * l + 2, :]

    gate = mod_ref[l, pl.ds(b, 1), 2 * d_model:]
    rows = ROWS_PER_CHAIN
    half = width // 2
    tm = TILE_MIX
    c = carry_s[...]
    for st in reversed(range(p_ref.shape[1] // tm)):
        t0 = st * tm
        mix_a = (p_ref[0, t0:t0 + tm, :].astype(F32)
                 + q_ref[0, t0:t0 + tm, :].astype(F32) * c).astype(BF16)
        mix_t = _to_time_order(mix_a)
        c = hb_ref[0, st] + ab_ref[0, st] * c
        for ch in range(tm // rows):
            rs = slice(t0 + ch * rows, t0 + (ch + 1) * rows)
            lhs = jnp.concatenate([mix_t[ch * rows:(ch + 1) * rows, :], mb_ref[0, rs, :]], axis=1)
            mix = _dot(lhs, wo_ref[0])
            x_new = x_ref[0, rs, :] + gate * _rms_norm(mix, gpost_ref[l:l + 1, :])
            o_ref[0, rs, :] = x_new
            if fuse_next:
                hb = _modulate(x_new, gpre_ref[l + 1:l + 2, :], mod_ref[l + 1, pl.ds(b, 1), :],
                               d_model).astype(BF16)
                hx_ref[0, rs, :] = hb
                u = _dot(hb, wbn_ref[0])
                v = u[:, :width] * _sigmoid(u[:, width:])
                v1_ref[0, rs, :] = v[:, :half].astype(BF16)
                v2_ref[0, rs, :] = v[:, half:].astype(BF16)
    carry_s[...] = c


def _out_call(p, q, mb, x, w_out_bf, g_post, mods, ab, hbo, h0, l, nxt=None):
    bsz, seq, d_model = x.shape
    width = p.shape[-1]
    half = width // 2
    tt = TILE_OUT
    tm = TILE_MIX
    nt = seq // tt
    per = tt // tm
    pad_blocks = TILE_PROJ // tt
    rev = lambda b, j: (b, nt - 1 - j, 0)

    def full(a):
        nd = a.ndim
        return pl.BlockSpec(a.shape, lambda b, j, _nd=nd: (0,) * _nd)

    operands = [p, q, mb, x, w_out_bf, g_post, mods, ab, hbo, h0]
    in_specs = [
        pl.BlockSpec((1, tt, width), rev),
        pl.BlockSpec((1, tt, width), rev),
        pl.BlockSpec((1, tt, width), rev),
        pl.BlockSpec((1, tt, d_model), rev),
        pl.BlockSpec((1,) + w_out_bf.shape[1:], lambda b, j: (l, 0, 0)),
        full(g_post),
        full(mods),
        pl.BlockSpec((1, per, 1, width), lambda b, j: (b, nt - 1 - j, 0, 0)),
        pl.BlockSpec((1, per, 1, width), lambda b, j: (b, nt - 1 - j, 0, 0)),
        full(h0),
    ]
    out_specs = [pl.BlockSpec((1, tt, d_model), rev)]
    out_shape = [jax.ShapeDtypeStruct((bsz, seq, d_model), F32)]
    aliases = {}
    if nxt is not None:
        g_pre, w_in_bf, v2_buf = nxt
        operands += [g_pre, w_in_bf, v2_buf]
        in_specs += [
            full(g_pre),
            pl.BlockSpec((1, d_model, 2 * width), lambda b, j: (l + 1, 0, 1)),
            pl.BlockSpec(memory_space=pl.ANY),
        ]
        out_specs += [
            pl.BlockSpec((1, tt, d_model), rev),
            pl.BlockSpec((1, tt, half), rev),
            pl.BlockSpec((1, tt, half), lambda b, j: (b, nt - 1 - j + pad_blocks, 0)),
        ]
        out_shape += [
            jax.ShapeDtypeStruct((bsz, seq, d_model), BF16),
            jax.ShapeDtypeStruct((bsz, seq, half), BF16),
            jax.ShapeDtypeStruct(v2_buf.shape, BF16),
        ]
        aliases = {len(operands) - 1: 3}
    res = pl.pallas_call(
        functools.partial(_out_body, l=l, d_model=d_model, width=width,
                          fuse_next=nxt is not None),
        grid=(bsz, nt),
        in_specs=in_specs,
        out_specs=out_specs,
        out_shape=out_shape,
        input_output_aliases=aliases,
        scratch_shapes=[pltpu.VMEM((1, width), F32)],
        compiler_params=pltpu.CompilerParams(
            dimension_semantics=("arbitrary", "arbitrary"), vmem_limit_bytes=VMEM_LIMIT),
        name="out_proj" if nxt is not None else "out",
    )(*operands)
    return res if nxt is not None else res[0]


def _pack_gate_weights(w_r, w_i):
    depth, ndir, heads, hd, _ = w_r.shape
    hh = heads // 2
    eye = jnp.eye(hh, dtype=w_r.dtype)

    def bd(w):
        w = w.reshape(depth, ndir, 2, hh, hd, hd)
        return jnp.einsum("ldfhij,hg->ldfhigj", w, eye).reshape(depth, ndir, 2, hh * hd, hh * hd)

    return jnp.concatenate([bd(w_r), bd(w_i)], axis=-1).astype(BF16)


def kernel(x, c, ctx, c_ctx, w_mod, b_mod, g_pre, g_post, w_in, conv_a_w, conv_a_b, w_rgate,
           b_rgate, w_igate, b_igate, lru_lambda, dw_w, dw_b, ln_g, ln_b, w_out):
    bsz, seq, d_model = x.shape
    depth = w_mod.shape[0]
    width = conv_a_b.shape[-1]
    assert depth >= 2 and bsz + 1 <= ROWS and seq % TILE_PROJ == 0 and seq % TILE_MIX == 0
    assert TILE_MIX % GRID_W == 0 and TILE_PROJ >= CONV_PAD * GRID_W
    assert seq % TILE_OUT == 0 and TILE_OUT % TILE_MIX == 0 and TILE_PROJ % TILE_OUT == 0
    assert ctx.shape[1] % CHUNK == 0 and width % MXU_N == 0 and d_model % MXU_N == 0

    act = jnp.concatenate(
        [c, c_ctx[None, :], jnp.zeros((ROWS - bsz - 1, d_model), F32)], axis=0)
    mods = _mods_call(act, w_mod, b_mod)

    w_in_bf = w_in.astype(BF16)
    w_out_bf = w_out.astype(BF16)
    wg = _pack_gate_weights(w_rgate, w_igate)

    h0 = _ctx_call(ctx, mods[:, bsz:bsz + 1, :], g_pre, g_post, w_in_bf, w_out_bf, wg, conv_a_w,
                   conv_a_b, b_rgate, b_igate, lru_lambda, dw_w, dw_b, ln_g, ln_b)

    dw_w16 = dw_w.astype(BF16)
    hx, v1, v2p = _proj_call(x, mods, g_pre, w_in_bf, 0, width)
    for l in range(depth):
        p, q, mb, ab, hbo = _mix_call(
            hx, v1, v2p, w_in_bf, wg, conv_a_w, conv_a_b, b_rgate, b_igate, lru_lambda, dw_w16,
            dw_b, ln_g, ln_b, h0, l)
        if l + 1 < depth:
            x, hx, v1, v2p = _out_call(p, q, mb, x, w_out_bf, g_post, mods, ab, hbo, h0, l,
                                       nxt=(g_pre, w_in_bf, v2p))
        else:
            x = _out_call(p, q, mb, x, w_out_bf, g_post, mods, ab, hbo, h0, l)
    return x
```

```python
import functools

import jax
import jax.numpy as jnp
import numpy as np
from jax import lax
from jax.experimental import pallas as pl
from jax.experimental.pallas import tpu as pltpu

F32 = jnp.float32
BF16 = jnp.bfloat16

EPS = 1e-6
LRU_C = 8.0
LRU_CONV = 4
CONV_K = 31
CONV_PAD = CONV_K // 2
GRID_W = 64
HALO = 16
ROWS = 8
CHUNK = 64
SEG_PAD = 16
MXU_N = 256
TAP_GROUP = 16
ROWS_PER_CHAIN = 512

TILE_PROJ = 1024
TILE_MIX = 512
TILE_OUT = 1024
VMEM_LIMIT = 56 * 1024 * 1024


def _sigmoid(x):
    return 0.5 + 0.5 * jnp.tanh(0.5 * x)


def _silu(x):
    return x * _sigmoid(x)


def _softplus(x):
    return jnp.maximum(x, 0.0) + jnp.log1p(jnp.exp(-jnp.abs(x)))


def _rms_norm(x, g):
    ms = jnp.mean(x * x, axis=-1, keepdims=True)
    return x * lax.rsqrt(ms + EPS) * g


def _layer_norm(x, g, b):
    mu = jnp.mean(x, axis=-1, keepdims=True)
    xc = x - mu
    var = jnp.mean(xc * xc, axis=-1, keepdims=True)
    return xc * lax.rsqrt(var + EPS) * g + b


def _dot(a, b):
    return jnp.dot(a, b, preferred_element_type=F32)


def _mark(spare_ref, token):
    if token is None:
        return
    reps = spare_ref.shape[0] // ROWS
    spare_ref[...] = jnp.concatenate([token] * reps, axis=0).astype(spare_ref.dtype)


def _release(mxu_pieces, vpu_pieces):
    token = None
    for k in range(max(len(mxu_pieces), len(vpu_pieces))):
        new_token = mxu_pieces[k]() if k < len(mxu_pieces) else token
        if k < len(vpu_pieces):
            vpu_pieces[k](token)
        token = new_token


def _to_sublane_major(x):
    n, cols = x.shape
    return jnp.swapaxes(x.reshape(ROWS, n // ROWS, cols), 0, 1).reshape(n, cols)


def _to_time_order(x):
    n, cols = x.shape
    return jnp.swapaxes(x.reshape(n // ROWS, ROWS, cols), 0, 1).reshape(n, cols)


def _time_order_matrix(n):
    i = np.arange(n)
    src = ROWS * (i % (n // ROWS)) + i // (n // ROWS)
    return jnp.asarray((src[:, None] == i[None, :]).astype(np.float32), BF16)


UA_ROW0 = (LRU_CONV - 1) * ROWS


def _row_conv_matrix(n):
    cols = n // ROWS
    rows = (cols + 2 * SEG_PAD) * ROWS
    mat = np.zeros((rows, n), np.float32)
    for i in range(rows):
        q, s = divmod(i, ROWS)
        if SEG_PAD <= q < SEG_PAD + cols:
            mat[i, cols * s + q - SEG_PAD] = 1.0
    return jnp.asarray(mat, BF16)


def _short_convs(ua_s, a_prev, a_next, cw, cb, vc_s, n, width):
    sub = n // ROWS
    ext = LRU_CONV - 1
    row = lax.broadcasted_iota(jnp.int32, (ROWS, width), 0)
    cw = 0.5 * cw
    cb = 0.5 * cb

    def blk(p):
        return slice(p * ROWS, (p + 1) * ROWS)

    for k in range(1, ext + 1):
        tail = pltpu.roll(ua_s[blk(ext + sub - k), :width], 1, 0)
        ua_s[blk(ext - k), :width] = jnp.where(row == 0, a_prev[ext - k:ext - k + 1, :], tail)
    for k in range(ext):
        head = pltpu.roll(ua_s[blk(ext + k), :width], ROWS - 1, 0)
        ua_s[blk(ext + sub + k), :width] = jnp.where(row == ROWS - 1, a_next[k:k + 1, :], head)

    def chunk(ci, carry):
        base = ci * CHUNK
        for d in range(2):
            first = 0 if d == 0 else ext
            acc = jnp.broadcast_to(cb[d:d + 1, :], (CHUNK, width))
            for k in range(LRU_CONV):
                start = pl.multiple_of(base + (first + k) * ROWS, ROWS)
                acc = acc + cw[d, k:k + 1, :] * ua_s[pl.ds(start, CHUNK), :width]
            vc_s[d, pl.ds(pl.multiple_of(base, CHUNK), CHUNK), :] = acc
        return carry

    lax.fori_loop(0, n // CHUNK, chunk, 0)


def _gate_pieces(vc_s, wg_ref, g_s, width):
    half = width // 2

    def piece(d, hf):
        def run():
            res = _dot(vc_s[d, :, hf * half:(hf + 1) * half].astype(BF16), wg_ref[d, hf])
            g_s[d, :, hf * half:(hf + 1) * half] = res[:, :half]
            g_s[d, :, width + hf * half:width + (hf + 1) * half] = res[:, half:]
        return run

    return [piece(d, hf) for d in range(2) for hf in range(2)]


def _sublane_scan(a, h, row, reverse):
    for s in (1, 2, 4):
        if reverse:
            m, shift = row < ROWS - s, ROWS - s
        else:
            m, shift = row >= s, s
        a_sh = jnp.where(m, pltpu.roll(a, shift, 0), 1.0)
        h_sh = jnp.where(m, pltpu.roll(h, shift, 0), 0.0)
        h = h + a * h_sh
        a = a * a_sh
    return a, h


def _rglru_scan(ua_s, br, bi, lam, c0, vc_s, g_s, hl_s, al_s, po_s, q_s, n, width):
    sub = n // ROWS
    row = lax.broadcasted_iota(jnp.int32, (ROWS, width), 0)
    spl_half = (-0.5 * LRU_C) * _softplus(-lam)

    def scan_columns(cols):
        nc = cols.stop - cols.start
        gate_i = slice(width + cols.start, width + cols.stop)

        def rows_of(v, d):
            return jnp.broadcast_to(v[d:d + 1, cols], (ROWS, nc))

        spl_c = [rows_of(spl_half, d) for d in range(2)]
        br_c = [rows_of(0.5 * br, d) for d in range(2)]
        bi_c = [rows_of(0.5 * bi, d) for d in range(2)]

        def coeffs(d, r0):
            vh = vc_s[d, pl.ds(r0, ROWS), cols]
            t_r = jnp.tanh(g_s[d, pl.ds(r0, ROWS), cols] + br_c[d])
            t_i = jnp.tanh(g_s[d, pl.ds(r0, ROWS), gate_i] + bi_c[d])
            la = spl_c[d] + spl_c[d] * t_r
            a = jnp.exp(la)
            x = jnp.tanh(la) * (-1.0 - a * a)
            mult = jnp.where(x > 0.0, x * lax.rsqrt(x), 0.0)
            return a, mult * (vh + vh * t_i)

        def local(j, carry):
            hf, af, hb, ab = carry
            r0 = pl.multiple_of(j * ROWS, ROWS)
            a, b = coeffs(0, r0)
            hf = a * hf + b
            af = a * af
            hl_s[0, pl.ds(r0, ROWS), cols] = hf
            al_s[0, pl.ds(r0, ROWS), cols] = af
            r1 = pl.multiple_of((sub - 1 - j) * ROWS, ROWS)
            a, b = coeffs(1, r1)
            hb = a * hb + b
            ab = a * ab
            hl_s[1, pl.ds(r1, ROWS), cols] = hb
            al_s[1, pl.ds(r1, ROWS), cols] = ab
            return hf, af, hb, ab

        zero = jnp.zeros((ROWS, nc), F32)
        one = jnp.ones((ROWS, nc), F32)
        return lax.fori_loop(0, sub, local, (zero, one, zero, one), unroll=8)

    halves = [scan_columns(slice(c, c + width // 2)) for c in (0, width // 2)]
    hf, af, hb, ab = (jnp.concatenate(parts, axis=1) for parts in zip(*halves))

    af, hf = _sublane_scan(af, hf, row, reverse=False)
    end_f = hf + af * c0
    c_f = jnp.where(row == 0, c0, pltpu.roll(end_f, 1, 0))
    ab, hb = _sublane_scan(ab, hb, row, reverse=True)
    c_b = jnp.where(row == ROWS - 1, 0.0, pltpu.roll(hb, ROWS - 1, 0))
    c_a = jnp.where(row == ROWS - 1, 1.0, pltpu.roll(ab, ROWS - 1, 0))

    if po_s is not None:
        step = 2 * ROWS
        c_f, c_b, c_a = (jnp.concatenate([c, c], axis=0) for c in (c_f, c_b, c_a))

        def fix(j, carry):
            r0 = pl.ds(pl.multiple_of(j * step, step), step)
            a_b = al_s[1, r0, :]
            h = (hl_s[0, r0, :] + al_s[0, r0, :] * c_f) + (hl_s[1, r0, :] + a_b * c_b)
            sg = _silu(ua_s[pl.ds(pl.multiple_of(UA_ROW0 + j * step, ROWS), step), width:])
            po_s[r0, :] = (h * sg).astype(po_s.dtype)
            if q_s is not None:
                q_s[r0, :] = ((a_b * c_a) * sg).astype(q_s.dtype)
            return carry

        lax.fori_loop(0, n // step, fix, 0, unroll=4)

    return end_f[ROWS - 1:ROWS, :], hb[0:1, :], ab[0:1, :]


def _shifted_copy_pieces(vpad_ref, sh_ref):
    n = sh_ref.shape[1]

    def piece(j):
        def run():
            sh_ref[j] = vpad_ref[pl.ds(j, n), :]
        return run

    return [piece(j) for j in range(ROWS)]


def _time_conv_pieces(sh_ref, w_ref, ncol, bases, out_ref):
    def piece(ci, base):
        def run():
            acc = jnp.zeros((CHUNK, ncol), F32)
            for k in range(CONV_K):
                off = SEG_PAD - CONV_PAD + k
                start = base + (off // ROWS) * ROWS
                acc = acc + w_ref[k:k + 1, 0:ncol] * sh_ref[off % ROWS, start:start + CHUNK, :]
            out_ref[ci * CHUNK:(ci + 1) * CHUNK, 0:ncol] = acc
        return run

    return [piece(ci, base) for ci, base in enumerate(bases)]


def _mods_body(act_ref, w_ref, b_ref, o_ref):
    a = _silu(act_ref[...])
    w = w_ref[0]
    a_hi = a.astype(BF16)
    a_lo = (a - a_hi.astype(F32)).astype(BF16)
    w_hi = w.astype(BF16)
    w_lo = (w - w_hi.astype(F32)).astype(BF16)
    both = _dot(jnp.concatenate([a_hi, a_lo], axis=0), w_hi)
    bias = b_ref[pl.ds(pl.program_id(0), 1), :]
    o_ref[0] = both[:ROWS] + (_dot(a_hi, w_lo) + both[ROWS:]) + bias


def _mods_call(act, w_mod, b_mod):
    depth, d_model, d3 = w_mod.shape
    ncol = d_model
    return pl.pallas_call(
        _mods_body,
        grid=(depth, d3 // ncol),
        in_specs=[
            pl.BlockSpec((ROWS, d_model), lambda l, n: (0, 0)),
            pl.BlockSpec((1, d_model, ncol), lambda l, n: (l, 0, n)),
            pl.BlockSpec((depth, ncol), lambda l, n: (0, n)),
        ],
        out_specs=pl.BlockSpec((1, ROWS, ncol), lambda l, n: (l, 0, n)),
        out_shape=jax.ShapeDtypeStruct((depth, ROWS, d3), F32),
        compiler_params=pltpu.CompilerParams(
            dimension_semantics=("arbitrary", "arbitrary"), vmem_limit_bytes=VMEM_LIMIT),
        name="mods",
    )(act, w_mod, b_mod)


def _modulate(x, g, m, d_model):
    return _rms_norm(x, g) * (1.0 + m[:, d_model:2 * d_model]) + m[:, :d_model]


def _ctx_body(ctx_ref, mod_ref, gpre_ref, gpost_ref, win_ref, wlast_ref, wo_ref, wg_ref, cw_ref,
              cb_ref, br_ref, bi_ref, lam_ref, dww_ref, dwb_ref, lng_ref, lnb_ref,
              h0_ref,
              ua_s, vc_s, g_s, hl_s, al_s, po_s, vpad_s, sh_s, y_s,
              *, depth, d_model, width, n, mod_row):
    xc = ctx_ref[0]
    zeros_pad = jnp.zeros((SEG_PAD, width), F32)
    no_rows = jnp.zeros((LRU_CONV - 1, width), F32)
    zero_state = jnp.zeros((1, width), F32)
    for l in range(depth):
        update = l < depth - 1
        m = mod_ref[l, mod_row:mod_row + 1, :]
        hc = _modulate(xc, gpre_ref[l:l + 1, :], m, d_model).astype(BF16)
        hcp = _to_sublane_major(hc)
        if update:
            ua_s[UA_ROW0:UA_ROW0 + n, :] = _dot(hcp, win_ref[l, :, :2 * width])
            ub = _dot(hc, win_ref[l, :, 2 * width:])
        else:
            ua_s[UA_ROW0:UA_ROW0 + n, :width] = _dot(hcp, wlast_ref[0])
        _short_convs(ua_s, no_rows, no_rows, cw_ref[l], cb_ref[l], vc_s, n, width)
        for piece in _gate_pieces(vc_s, wg_ref.at[l], g_s, width):
            piece()
        hf_last, hb_first, _ = _rglru_scan(
            ua_s, br_ref[l], bi_ref[l], lam_ref[l], zero_state, vc_s, g_s, hl_s, al_s,
            po_s if update else None, None, n, width)
        h0_ref[0, 2 * l:2 * l + 1, :] = hf_last
        h0_ref[0, 2 * l + 1:2 * l + 2, :] = hb_first
        if update:
            v = ub[:, :width] * _sigmoid(ub[:, width:2 * width])
            vpad_s[0:SEG_PAD, :] = zeros_pad
            vpad_s[SEG_PAD:SEG_PAD + n, :] = v
            vpad_s[SEG_PAD + n:, :] = zeros_pad
            for piece in _shifted_copy_pieces(vpad_s, sh_s) + _time_conv_pieces(
                    sh_s, dww_ref.at[l], width, [ci * CHUNK for ci in range(n // CHUNK)], y_s):
                piece()
            y = y_s[...] + dwb_ref[l:l + 1, :]
            conv = _silu(_layer_norm(y, lng_ref[l:l + 1, :], lnb_ref[l:l + 1, :]))
            mix_a = _to_time_order(po_s[...].astype(BF16))
            mix_b = (conv * _silu(ub[:, 2 * width:])).astype(BF16)
            mix = _dot(mix_a, wo_ref[l, :width, :]) + _dot(mix_b, wo_ref[l, width:, :])
            xc = xc + m[:, 2 * d_model:] * _rms_norm(mix, gpost_ref[l:l + 1, :])


def _ctx_call(ctx, mod_c, mod_row, g_pre, g_post, w_in_bf, w_out_bf, wg, conv_a_w, conv_a_b,
              b_rgate, b_igate, lru_lambda, dw_w, dw_b, ln_g, ln_b):
    bsz, n, d_model = ctx.shape
    depth = w_in_bf.shape[0]
    width = conv_a_b.shape[-1]
    npad = n + 2 * SEG_PAD
    def full(a):
        nd = a.ndim
        return pl.BlockSpec(a.shape, lambda b, _nd=nd: (0,) * _nd)

    def head(a):
        return pl.BlockSpec((depth - 1,) + a.shape[1:], lambda b: (0, 0, 0))

    small = (wg, conv_a_w, conv_a_b, b_rgate, b_igate, lru_lambda, dw_w, dw_b, ln_g, ln_b)
    params = (mod_c, g_pre, g_post, w_in_bf, w_in_bf, w_out_bf) + small
    return pl.pallas_call(
        functools.partial(_ctx_body, depth=depth, d_model=d_model, width=width, n=n,
                          mod_row=mod_row),
        grid=(bsz,),
        in_specs=[pl.BlockSpec((1, n, d_model), lambda b: (b, 0, 0)), full(mod_c), full(g_pre),
                  full(g_post), head(w_in_bf),
                  pl.BlockSpec((1, d_model, width), lambda b: (depth - 1, 0, 0)),
                  head(w_out_bf)] + [full(a) for a in small],
        out_specs=pl.BlockSpec((1, 2 * depth, width), lambda b: (b, 0, 0)),
        out_shape=jax.ShapeDtypeStruct((bsz, 2 * depth, width), F32),
        scratch_shapes=[
            pltpu.VMEM((n + 2 * UA_ROW0, 2 * width), F32),
            pltpu.VMEM((2, n, width), F32),
            pltpu.VMEM((2, n, 2 * width), F32),
            pltpu.VMEM((2, n, width), F32),
            pltpu.VMEM((2, n, width), F32),
            pltpu.VMEM((n, width), F32),
            pltpu.VMEM((npad, width), F32),
            pltpu.VMEM((ROWS, npad - ROWS, width), F32),
            pltpu.VMEM((n, width), F32),
        ],
        compiler_params=pltpu.CompilerParams(
            dimension_semantics=("arbitrary",), vmem_limit_bytes=VMEM_LIMIT),
        name="context",
    )(ctx, *params)


def _proj_body(x_ref, mod_ref, g_ref, w_ref, hx_ref, v1_ref, v2_ref, *, l, nt, d_model, width):
    b = pl.program_id(0)
    j = pl.program_id(1)
    is_pad = jnp.logical_or(j == 0, j == nt + 1)

    @pl.when(is_pad)
    def _():
        v2_ref[...] = jnp.zeros(v2_ref.shape, v2_ref.dtype)

    @pl.when(jnp.logical_not(is_pad))
    def _():
        half = width // 2
        rows = ROWS_PER_CHAIN
        for c in range(x_ref.shape[1] // rows):
            rs = slice(c * rows, (c + 1) * rows)
            hb = _modulate(x_ref[0, rs, :], g_ref[l:l + 1, :], mod_ref[l, pl.ds(b, 1), :],
                           d_model).astype(BF16)
            hx_ref[0, rs, :] = hb
            u = _dot(hb, w_ref[0])
            v = u[:, :width] * _sigmoid(u[:, width:])
            v1_ref[0, rs, :] = v[:, :half].astype(BF16)
            v2_ref[0, rs, :] = v[:, half:].astype(BF16)


def _proj_call(x, mods, g_pre, w_in_bf, l, width):
    bsz, seq, d_model = x.shape
    half = width // 2
    tt = TILE_PROJ
    nt = seq // tt

    def tok(b, j):
        return (b, jnp.clip(j - 1, 0, nt - 1), 0)

    return pl.pallas_call(
        functools.partial(_proj_body, l=l, nt=nt, d_model=d_model, width=width),
        grid=(bsz, nt + 2),
        in_specs=[
            pl.BlockSpec((1, tt, d_model), tok),
            pl.BlockSpec(mods.shape, lambda b, j: (0, 0, 0)),
            pl.BlockSpec(g_pre.shape, lambda b, j: (0, 0)),
            pl.BlockSpec((1, d_model, 2 * width), lambda b, j: (l, 0, 1)),
        ],
        out_specs=[
            pl.BlockSpec((1, tt, d_model), tok),
            pl.BlockSpec((1, tt, half), tok),
            pl.BlockSpec((1, tt, half), lambda b, j: (b, j, 0)),
        ],
        out_shape=[
            jax.ShapeDtypeStruct((bsz, seq, d_model), BF16),
            jax.ShapeDtypeStruct((bsz, seq, half), BF16),
            jax.ShapeDtypeStruct((bsz, seq + 2 * tt, half), BF16),
        ],
        compiler_params=pltpu.CompilerParams(
            dimension_semantics=("arbitrary", "arbitrary"), vmem_limit_bytes=VMEM_LIMIT),
        name="proj",
    )(x, mods, g_pre, w_in_bf)


def _mix_body(hx_ref, hxp_ref, hxn_ref, v1_ref, v2_ref, wa_ref, wbg_ref, wg_ref, cw_ref, cb_ref,
              br_ref, bi_ref, lam_ref, dww16_ref, dwb_ref, lng_ref, lnb_ref, pmt_ref, pme_ref,
              h0_ref, zero_ref,
              p_ref, q_ref, mb_ref, ab_ref, hbo_ref,
              hxs_s, ua_s, bg_s, vc_s, g_s, hl_s, al_s, carry_s, ev_s, od_s, wr_s, wb_s,
              yr_s, y_s,
              *, l, tt, nt, d_model, width, v2_front):
    i = pl.program_id(1)
    half = width // 2
    nseg = tt // GRID_W
    ext = LRU_CONV - 1
    reach = CONV_PAD * GRID_W
    row_zero = zero_ref[0]

    @pl.when(i == 0)
    def _():
        carry_s[...] = h0_ref[pl.program_id(0), 2 * l:2 * l + 1, :]

    def col_slices(total):
        return [slice(c, c + MXU_N) for c in range(0, total, MXU_N)]

    def zero_based(start, align):
        return pl.multiple_of(row_zero + start, align)

    def prepare(token):
        for k in range(CONV_K):
            wr_s[k] = jnp.broadcast_to(dww16_ref[l, k:k + 1, 0:half], (2 * ROWS, half))
            wb_s[k] = jnp.broadcast_to(dww16_ref[l, k:k + 1, half:], (2 * ROWS, half))

    def spread_rows():
        res = _dot(pme_ref[...], v1_ref[0])
        ev_s[...] = res.astype(BF16)
        moved = jnp.concatenate([res[ROWS:, :], jnp.zeros((ROWS, half), F32)], axis=0)
        od_s[...] = moved.astype(BF16)
        return res[0:ROWS, 0:128]

    def row_conv(chunks):
        def run(token):
            _mark(wr_s.at[CONV_K, :, 0:128], token)
            for ci in chunks:
                for rt in range(CHUNK // (2 * ROWS)):
                    row0 = ci * CHUNK + rt * 2 * ROWS
                    acc = None
                    for g0 in range(0, CONV_K, TAP_GROUP):
                        part = None
                        for k in range(g0, min(g0 + TAP_GROUP, CONV_K)):
                            shift = SEG_PAD - CONV_PAD + k
                            src = row0 + shift * ROWS
                            if shift % 2 == 0:
                                window = ev_s[src:src + 2 * ROWS, :]
                            else:
                                window = od_s[src - ROWS:src + ROWS, :]
                            term = wr_s[row_zero + k] * window
                            part = term if part is None else part + term
                        part = part.astype(F32)
                        acc = part if acc is None else acc + part
                    yr_s[row0:row0 + 2 * ROWS, :] = acc
        return run

    def gather_rows():
        y_s[:, 0:half] = _dot(pmt_ref[...], yr_s[...].astype(BF16))
        return None

    def col_conv(chunks):
        def run(token):
            _mark(wb_s.at[CONV_K, :, 0:128], token)
            for ci in chunks:
                win0 = i * tt + (v2_front - reach) + ci * CHUNK
                for rt in range(CHUNK // (2 * ROWS)):
                    acc = None
                    for g0 in range(0, CONV_K, TAP_GROUP):
                        part = None
                        for k in range(g0, min(g0 + TAP_GROUP, CONV_K)):
                            start = pl.multiple_of(win0 + k * GRID_W + rt * 2 * ROWS, 2 * ROWS)
                            term = wb_s[row_zero + k] * v2_ref[0, pl.ds(start, 2 * ROWS), :]
                            part = term if part is None else part + term
                        part = part.astype(F32)
                        acc = part if acc is None else acc + part
                    row0 = ci * CHUNK + rt * 2 * ROWS
                    y_s[row0:row0 + 2 * ROWS, half:] = acc
        return run

    def permute(slices):
        def run():
            for cs in slices:
                res = _to_sublane_major(hx_ref[0, :, cs])
                hxs_s[0:tt, cs] = res
            return res[0:ROWS, 0:128].astype(F32)
        return run

    def project_a(cs):
        def run():
            res = _dot(hxs_s[0:tt, :], wa_ref[0, :, cs])
            ua_s[UA_ROW0:UA_ROW0 + tt, cs] = res
            return res[0:ROWS, 0:128]
        return run

    def project_b(cs):
        def run():
            res = _dot(hx_ref[0], wbg_ref[0, :, cs])
            bg_s[0:tt, cs] = res
            return res[0:ROWS, 0:128]
        return run

    cols_d = col_slices(d_model)
    cols_a = col_slices(2 * width)
    cols_b = col_slices(width)
    quarter = nseg // 4
    row_stages = [row_conv(range(q * quarter, (q + 1) * quarter)) for q in range(4)]
    col_stages = [col_conv(range(q * 2 * quarter, (q + 1) * 2 * quarter)) for q in range(2)]
    _release(
        [spread_rows, permute(cols_d[:2]), permute(cols_d[2:]), project_b(cols_b[0]),
         project_b(cols_b[1])] + [project_a(cs) for cs in cols_a] + [gather_rows],
        [prepare] + row_stages + col_stages)

    wa_a = wa_ref[0, :, :width]
    a_prev = jnp.where(i > 0, _dot(hxp_ref[0], wa_a)[HALO - ext:, :], 0.0)
    a_next = jnp.where(i < nt - 1, _dot(hxn_ref[0], wa_a)[:ext, :], 0.0)
    _short_convs(ua_s, a_prev, a_next, cw_ref[l], cb_ref[l], vc_s, tt, width)

    def finish_conv(c, nrows):
        def run(token):
            _mark(bg_s.at[tt:tt + ROWS, 0:128], token)
            y = y_s[c * nrows:(c + 1) * nrows, :] + dwb_ref[l:l + 1, :]
            conv = _silu(_layer_norm(y, lng_ref[l:l + 1, :], lnb_ref[l:l + 1, :]))
            out = conv * _silu(bg_s[pl.ds(zero_based(c * nrows, ROWS), nrows), :])
            mb_ref[0, c * nrows:(c + 1) * nrows, :] = out.astype(BF16)
        return run

    def gate_piece(d, hf):
        def run():
            lo = hf * half
            res = _dot(vc_s[d, 0:tt, lo:lo + half].astype(BF16), wg_ref[0, d, hf])
            g_s[d, :, lo:lo + half] = res[:, :half]
            g_s[d, :, width + lo:width + lo + half] = res[:, half:]
            return res[0:ROWS, 0:128]
        return run

    _release([gate_piece(d, hf) for d in range(2) for hf in range(2)],
             [finish_conv(c, tt // 4) for c in range(4)])

    hf_last, hb_first, a_total = _rglru_scan(
        ua_s, br_ref[l], bi_ref[l], lam_ref[l], carry_s[...], vc_s, g_s, hl_s, al_s,
        p_ref.at[0], q_ref.at[0], tt, width)
    carry_s[...] = hf_last
    ab_ref[0, 0] = a_total
    hbo_ref[0, 0] = hb_first


def _mix_call(hx, v1, v2p, w_in_bf, wg, cw, cb, br, bi, lam, dww16, dwb, lng, lnb, h0, l):
    bsz, seq, d_model = hx.shape
    width = cb.shape[-1]
    half = width // 2
    tt = TILE_MIX
    nt = seq // tt
    hb_per_tile = tt // HALO
    n_halo_blocks = seq // HALO
    v2_front = (v2p.shape[1] - seq) // 2
    pme = _row_conv_matrix(tt)

    def full(a):
        nd = a.ndim
        return pl.BlockSpec(a.shape, lambda b, i, _nd=nd: (0,) * _nd)

    small = (cw, cb, br, bi, lam, dww16, dwb, lng, lnb, _time_order_matrix(tt), pme, h0)
    tile = lambda b, i: (b, i, 0)
    out_shapes = [jax.ShapeDtypeStruct((bsz, seq, width), BF16)] * 3 + [
        jax.ShapeDtypeStruct((bsz, nt, 1, width), F32)] * 2
    return pl.pallas_call(
        functools.partial(_mix_body, l=l, tt=tt, nt=nt, d_model=d_model, width=width,
                          v2_front=v2_front),
        grid=(bsz, nt),
        in_specs=[
            pl.BlockSpec((1, tt, d_model), tile),
            pl.BlockSpec((1, HALO, d_model),
                         lambda b, i: (b, jnp.maximum(i * hb_per_tile - 1, 0), 0)),
            pl.BlockSpec((1, HALO, d_model),
                         lambda b, i: (b, jnp.minimum((i + 1) * hb_per_tile, n_halo_blocks - 1), 0)),
            pl.BlockSpec((1, tt, half), tile),
            pl.BlockSpec((1, v2p.shape[1], half), lambda b, i: (b, 0, 0)),
            pl.BlockSpec((1, d_model, 2 * width), lambda b, i: (l, 0, 0)),
            pl.BlockSpec((1, d_model, width), lambda b, i: (l, 0, 4)),
            pl.BlockSpec((1,) + wg.shape[1:], lambda b, i: (l, 0, 0, 0, 0)),
        ] + [full(a) for a in small] + [pl.BlockSpec(memory_space=pltpu.SMEM)],
        out_specs=[pl.BlockSpec((1, tt, width), tile)] * 3 + [
            pl.BlockSpec((1, 1, 1, width), lambda b, i: (b, i, 0, 0))] * 2,
        out_shape=out_shapes,
        scratch_shapes=[
            pltpu.VMEM((tt, d_model), BF16),
            pltpu.VMEM((tt + 2 * UA_ROW0, 2 * width), F32),
            pltpu.VMEM((tt + ROWS, width), F32),
            pltpu.VMEM((2, tt, width), F32),
            pltpu.VMEM((2, tt, 2 * width), F32),
            pltpu.VMEM((2, tt, width), F32),
            pltpu.VMEM((2, tt, width), F32),
            pltpu.VMEM((1, width), F32),
            pltpu.VMEM(pme.shape[:1] + (half,), BF16),
            pltpu.VMEM(pme.shape[:1] + (half,), BF16),
            pltpu.VMEM((CONV_K + 1, 2 * ROWS, half), BF16),
            pltpu.VMEM((CONV_K + 1, 2 * ROWS, half), BF16),
            pltpu.VMEM((tt, half), F32),
            pltpu.VMEM((tt, width), F32),
        ],
        compiler_params=pltpu.CompilerParams(
            dimension_semantics=("arbitrary", "arbitrary"), vmem_limit_bytes=VMEM_LIMIT,
            ),
        name="mixer",
    )(hx, hx, hx, v1, v2p, w_in_bf, w_in_bf, wg, *small, jnp.zeros((1,), jnp.int32))


def _out_body(p_ref, q_ref, mb_ref, x_ref, wo_ref, gpost_ref, mod_ref, ab_ref, hb_ref, h0_ref,
              *rest, l, d_model, width, fuse_next):
    if fuse_next:
        gpre_ref, wbn_ref, _, o_ref, hx_ref, v1_ref, v2_ref, carry_s = rest
    else:
        o_ref, carry_s = rest
    b = pl.program_id(0)

    @pl.when(pl.program_id(1) == 0)
    def _():
        carry_s[...] = h0_ref[b, 2 * l + 1:2 * l + 2, :]

    gate = mod_ref[l, pl.ds(b, 1), 2 * d_model:]
    rows = ROWS_PER_CHAIN
    half = width // 2
    tm = TILE_MIX
    c = carry_s[...]
    for st in reversed(range(p_ref.shape[1] // tm)):
        t0 = st * tm
        mix_a = (p_ref[0, t0:t0 + tm, :].astype(F32)
                 + q_ref[0, t0:t0 + tm, :].astype(F32) * c).astype(BF16)
        mix_t = _to_time_order(mix_a)
        c = hb_ref[0, st] + ab_ref[0, st] * c
        for ch in range(tm // rows):
            rs = slice(t0 + ch * rows, t0 + (ch + 1) * rows)
            lhs = jnp.concatenate([mix_t[ch * rows:(ch + 1) * rows, :], mb_ref[0, rs, :]], axis=1)
            mix = _dot(lhs, wo_ref[0])
            x_new = x_ref[0, rs, :] + gate * _rms_norm(mix, gpost_ref[l:l + 1, :])
            o_ref[0, rs, :] = x_new
            if fuse_next:
                hb = _modulate(x_new, gpre_ref[l + 1:l + 2, :], mod_ref[l + 1, pl.ds(b, 1), :],
                               d_model).astype(BF16)
                hx_ref[0, rs, :] = hb
                u = _dot(hb, wbn_ref[0])
                v = u[:, :width] * _sigmoid(u[:, width:])
                v1_ref[0, rs, :] = v[:, :half].astype(BF16)
                v2_ref[0, rs, :] = v[:, half:].astype(BF16)
    carry_s[...] = c


def _out_call(p, q, mb, x, w_out_bf, g_post, mods, ab, hbo, h0, l, nxt=None):
    bsz, seq, d_model = x.shape
    width = p.shape[-1]
    half = width // 2
    tt = TILE_OUT
    tm = TILE_MIX
    nt = seq // tt
    per = tt // tm
    pad_blocks = TILE_PROJ // tt
    rev = lambda b, j: (b, nt - 1 - j, 0)

    def full(a):
        nd = a.ndim
        return pl.BlockSpec(a.shape, lambda b, j, _nd=nd: (0,) * _nd)

    operands = [p, q, mb, x, w_out_bf, g_post, mods, ab, hbo, h0]
    in_specs = [
        pl.BlockSpec((1, tt, width), rev),
        pl.BlockSpec((1, tt, width), rev),
        pl.BlockSpec((1, tt, width), rev),
        pl.BlockSpec((1, tt, d_model), rev),
        pl.BlockSpec((1,) + w_out_bf.shape[1:], lambda b, j: (l, 0, 0)),
        full(g_post),
        full(mods),
        pl.BlockSpec((1, per, 1, width), lambda b, j: (b, nt - 1 - j, 0, 0)),
        pl.BlockSpec((1, per, 1, width), lambda b, j: (b, nt - 1 - j, 0, 0)),
        full(h0),
    ]
    out_specs = [pl.BlockSpec((1, tt, d_model), rev)]
    out_shape = [jax.ShapeDtypeStruct((bsz, seq, d_model), F32)]
    aliases = {}
    if nxt is not None:
        g_pre, w_in_bf, v2_buf = nxt
        operands += [g_pre, w_in_bf, v2_buf]
        in_specs += [
            full(g_pre),
            pl.BlockSpec((1, d_model, 2 * width), lambda b, j: (l + 1, 0, 1)),
            pl.BlockSpec(memory_space=pl.ANY),
        ]
        out_specs += [
            pl.BlockSpec((1, tt, d_model), rev),
            pl.BlockSpec((1, tt, half), rev),
            pl.BlockSpec((1, tt, half), lambda b, j: (b, nt - 1 - j + pad_blocks, 0)),
        ]
        out_shape += [
            jax.ShapeDtypeStruct((bsz, seq, d_model), BF16),
            jax.ShapeDtypeStruct((bsz, seq, half), BF16),
            jax.ShapeDtypeStruct(v2_buf.shape, BF16),
        ]
        aliases = {len(operands) - 1: 3}
    res = pl.pallas_call(
        functools.partial(_out_body, l=l, d_model=d_model, width=width,
                          fuse_next=nxt is not None),
        grid=(bsz, nt),
        in_specs=in_specs,
        out_specs=out_specs,
        out_shape=out_shape,
        input_output_aliases=aliases,
        scratch_shapes=[pltpu.VMEM((1, width), F32)],
        compiler_params=pltpu.CompilerParams(
            dimension_semantics=("arbitrary", "arbitrary"), vmem_limit_bytes=VMEM_LIMIT),
        name="out_proj" if nxt is not None else "out",
    )(*operands)
    return res if nxt is not None else res[0]


def _pack_gate_weights(w_r, w_i):
    depth, ndir, heads, hd, _ = w_r.shape
    hh = heads // 2
    eye = jnp.eye(hh, dtype=w_r.dtype)

    def bd(w):
        w = w.reshape(depth, ndir, 2, hh, hd, hd)
        return jnp.einsum("ldfhij,hg->ldfhigj", w, eye).reshape(depth, ndir, 2, hh * hd, hh * hd)

    return jnp.concatenate([bd(w_r), bd(w_i)], axis=-1).astype(BF16)


def kernel(x, c, ctx, c_ctx, w_mod, b_mod, g_pre, g_post, w_in, conv_a_w, conv_a_b, w_rgate,
           b_rgate, w_igate, b_igate, lru_lambda, dw_w, dw_b, ln_g, ln_b, w_out):
    bsz, seq, d_model = x.shape
    depth = w_mod.shape[0]
    width = conv_a_b.shape[-1]
    assert depth >= 2 and bsz + 1 <= ROWS and seq % TILE_PROJ == 0 and seq % TILE_MIX == 0
    assert TILE_MIX % GRID_W == 0 and TILE_PROJ >= CONV_PAD * GRID_W
    assert seq % TILE_OUT == 0 and TILE_OUT % TILE_MIX == 0 and TILE_PROJ % TILE_OUT == 0
    assert ctx.shape[1] % CHUNK == 0 and width % MXU_N == 0 and d_model % MXU_N == 0

    act = jnp.concatenate(
        [c, c_ctx[None, :], jnp.zeros((ROWS - bsz - 1, d_model), F32)], axis=0)
    mods = _mods_call(act, w_mod, b_mod)

    w_in_bf = w_in.astype(BF16)
    w_out_bf = w_out.astype(BF16)
    wg = _pack_gate_weights(w_rgate, w_igate)

    h0 = _ctx_call(ctx, mods, bsz, g_pre, g_post, w_in_bf, w_out_bf, wg, conv_a_w, conv_a_b,
                   b_rgate, b_igate, lru_lambda, dw_w, dw_b, ln_g, ln_b)

    dw_w16 = dw_w.astype(BF16)
    hx, v1, v2p = _proj_call(x, mods, g_pre, w_in_bf, 0, width)
    for l in range(depth):
        p, q, mb, ab, hbo = _mix_call(
            hx, v1, v2p, w_in_bf, wg, conv_a_w, conv_a_b, b_rgate, b_igate, lru_lambda, dw_w16,
            dw_b, ln_g, ln_b, h0, l)
        if l + 1 < depth:
            x, hx, v1, v2p = _out_call(p, q, mb, x, w_out_bf, g_post, mods, ab, hbo, h0, l,
                                       nxt=(g_pre, w_in_bf, v2p))
        else:
            x = _out_call(p, q, mb, x, w_out_bf, g_post, mods, ab, hbo, h0, l)
    return x
```

```python
import functools

import jax
import jax.numpy as jnp
import numpy as np
from jax import lax
from jax.experimental import pallas as pl
from jax.experimental.pallas import tpu as pltpu

F32 = jnp.float32
BF16 = jnp.bfloat16

EPS = 1e-6
LRU_C = 8.0
LRU_CONV = 4
CONV_K = 31
CONV_PAD = CONV_K // 2
GRID_W = 64
HALO = 16
ROWS = 8
CHUNK = 64
SEG_PAD = 16
MXU_N = 256
TAP_GROUP = 16
ROWS_PER_CHAIN = 512

TILE_PROJ = 1024
TILE_MIX = 512
TILE_OUT = 1024
VMEM_LIMIT = 56 * 1024 * 1024


def _sigmoid(x):
    return 0.5 + 0.5 * jnp.tanh(0.5 * x)


def _silu(x):
    return x * _sigmoid(x)


def _softplus(x):
    return jnp.maximum(x, 0.0) + jnp.log1p(jnp.exp(-jnp.abs(x)))


def _rms_norm(x, g):
    ms = jnp.mean(x * x, axis=-1, keepdims=True)
    return x * lax.rsqrt(ms + EPS) * g


def _layer_norm(x, g, b):
    mu = jnp.mean(x, axis=-1, keepdims=True)
    xc = x - mu
    var = jnp.mean(xc * xc, axis=-1, keepdims=True)
    return xc * lax.rsqrt(var + EPS) * g + b


def _dot(a, b):
    return jnp.dot(a, b, preferred_element_type=F32)


def _mark(spare_ref, token):
    if token is None:
        return
    reps = spare_ref.shape[0] // ROWS
    spare_ref[...] = jnp.concatenate([token] * reps, axis=0).astype(spare_ref.dtype)


def _release(mxu_pieces, vpu_pieces):
    token = None
    for k in range(max(len(mxu_pieces), len(vpu_pieces))):
        new_token = mxu_pieces[k]() if k < len(mxu_pieces) else token
        if k < len(vpu_pieces):
            vpu_pieces[k](token)
        token = new_token


def _to_sublane_major(x):
    n, cols = x.shape
    return jnp.swapaxes(x.reshape(ROWS, n // ROWS, cols), 0, 1).reshape(n, cols)


def _to_time_order(x):
    n, cols = x.shape
    return jnp.swapaxes(x.reshape(n // ROWS, ROWS, cols), 0, 1).reshape(n, cols)


def _time_order_matrix(n):
    i = np.arange(n)
    src = ROWS * (i % (n // ROWS)) + i // (n // ROWS)
    return jnp.asarray((src[:, None] == i[None, :]).astype(np.float32), BF16)


UA_ROW0 = (LRU_CONV - 1) * ROWS


def _row_conv_matrix(n):
    cols = n // ROWS
    rows = (cols + 2 * SEG_PAD) * ROWS
    mat = np.zeros((rows, n), np.float32)
    for i in range(rows):
        q, s = divmod(i, ROWS)
        if SEG_PAD <= q < SEG_PAD + cols:
            mat[i, cols * s + q - SEG_PAD] = 1.0
    return jnp.asarray(mat, BF16)


def _short_convs(ua_s, a_prev, a_next, cw, cb, vc_s, n, width):
    sub = n // ROWS
    ext = LRU_CONV - 1
    row = lax.broadcasted_iota(jnp.int32, (ROWS, width), 0)
    cw = 0.5 * cw
    cb = 0.5 * cb

    def blk(p):
        return slice(p * ROWS, (p + 1) * ROWS)

    for k in range(1, ext + 1):
        tail = pltpu.roll(ua_s[blk(ext + sub - k), :width], 1, 0)
        ua_s[blk(ext - k), :width] = jnp.where(row == 0, a_prev[ext - k:ext - k + 1, :], tail)
    for k in range(ext):
        head = pltpu.roll(ua_s[blk(ext + k), :width], ROWS - 1, 0)
        ua_s[blk(ext + sub + k), :width] = jnp.where(row == ROWS - 1, a_next[k:k + 1, :], head)

    def chunk(ci, carry):
        base = ci * CHUNK
        for d in range(2):
            first = 0 if d == 0 else ext
            acc = jnp.broadcast_to(cb[d:d + 1, :], (CHUNK, width))
            for k in range(LRU_CONV):
                start = pl.multiple_of(base + (first + k) * ROWS, ROWS)
                acc = acc + cw[d, k:k + 1, :] * ua_s[pl.ds(start, CHUNK), :width]
            vc_s[d, pl.ds(pl.multiple_of(base, CHUNK), CHUNK), :] = acc
        return carry

    lax.fori_loop(0, n // CHUNK, chunk, 0)


def _gate_pieces(vc_s, wg_ref, g_s, width):
    half = width // 2

    def piece(d, hf):
        def run():
            res = _dot(vc_s[d, :, hf * half:(hf + 1) * half].astype(BF16), wg_ref[d, hf])
            g_s[d, :, hf * half:(hf + 1) * half] = res[:, :half]
            g_s[d, :, width + hf * half:width + (hf + 1) * half] = res[:, half:]
        return run

    return [piece(d, hf) for d in range(2) for hf in range(2)]


def _sublane_scan(a, h, row, reverse):
    for s in (1, 2, 4):
        if reverse:
            m, shift = row < ROWS - s, ROWS - s
        else:
            m, shift = row >= s, s
        a_sh = jnp.where(m, pltpu.roll(a, shift, 0), 1.0)
        h_sh = jnp.where(m, pltpu.roll(h, shift, 0), 0.0)
        h = h + a * h_sh
        a = a * a_sh
    return a, h


def _rglru_scan(ua_s, br, bi, lam, c0, vc_s, g_s, hl_s, al_s, po_s, q_s, n, width):
    sub = n // ROWS
    row = lax.broadcasted_iota(jnp.int32, (ROWS, width), 0)
    spl_half = (-0.5 * LRU_C) * _softplus(-lam)

    def scan_columns(cols):
        nc = cols.stop - cols.start
        gate_i = slice(width + cols.start, width + cols.stop)

        def rows_of(v, d):
            return jnp.broadcast_to(v[d:d + 1, cols], (ROWS, nc))

        spl_c = [rows_of(spl_half, d) for d in range(2)]
        br_c = [rows_of(0.5 * br, d) for d in range(2)]
        bi_c = [rows_of(0.5 * bi, d) for d in range(2)]

        def coeffs(d, r0):
            vh = vc_s[d, pl.ds(r0, ROWS), cols]
            t_r = jnp.tanh(g_s[d, pl.ds(r0, ROWS), cols] + br_c[d])
            t_i = jnp.tanh(g_s[d, pl.ds(r0, ROWS), gate_i] + bi_c[d])
            la = spl_c[d] + spl_c[d] * t_r
            a = jnp.exp(la)
            x = jnp.tanh(la) * (-1.0 - a * a)
            mult = jnp.where(x > 0.0, x * lax.rsqrt(x), 0.0)
            return a, mult * (vh + vh * t_i)

        def local(j, carry):
            hf, af, hb, ab = carry
            r0 = pl.multiple_of(j * ROWS, ROWS)
            a, b = coeffs(0, r0)
            hf = a * hf + b
            af = a * af
            hl_s[0, pl.ds(r0, ROWS), cols] = hf
            al_s[0, pl.ds(r0, ROWS), cols] = af
            r1 = pl.multiple_of((sub - 1 - j) * ROWS, ROWS)
            a, b = coeffs(1, r1)
            hb = a * hb + b
            ab = a * ab
            hl_s[1, pl.ds(r1, ROWS), cols] = hb
            al_s[1, pl.ds(r1, ROWS), cols] = ab
            return hf, af, hb, ab

        zero = jnp.zeros((ROWS, nc), F32)
        one = jnp.ones((ROWS, nc), F32)
        return lax.fori_loop(0, sub, local, (zero, one, zero, one), unroll=True)

    halves = [scan_columns(slice(c, c + width // 2)) for c in (0, width // 2)]
    hf, af, hb, ab = (jnp.concatenate(parts, axis=1) for parts in zip(*halves))

    af, hf = _sublane_scan(af, hf, row, reverse=False)
    end_f = hf + af * c0
    c_f = jnp.where(row == 0, c0, pltpu.roll(end_f, 1, 0))
    ab, hb = _sublane_scan(ab, hb, row, reverse=True)
    c_b = jnp.where(row == ROWS - 1, 0.0, pltpu.roll(hb, ROWS - 1, 0))
    c_a = jnp.where(row == ROWS - 1, 1.0, pltpu.roll(ab, ROWS - 1, 0))

    if po_s is not None:
        step = 2 * ROWS
        c_f, c_b, c_a = (jnp.concatenate([c, c], axis=0) for c in (c_f, c_b, c_a))

        def fix(j, carry):
            r0 = pl.ds(pl.multiple_of(j * step, step), step)
            a_b = al_s[1, r0, :]
            h = (hl_s[0, r0, :] + al_s[0, r0, :] * c_f) + (hl_s[1, r0, :] + a_b * c_b)
            sg = _silu(ua_s[pl.ds(pl.multiple_of(UA_ROW0 + j * step, ROWS), step), width:])
            po_s[r0, :] = (h * sg).astype(po_s.dtype)
            if q_s is not None:
                q_s[r0, :] = ((a_b * c_a) * sg).astype(q_s.dtype)
            return carry

        lax.fori_loop(0, n // step, fix, 0, unroll=4)

    return end_f[ROWS - 1:ROWS, :], hb[0:1, :], ab[0:1, :]


def _shifted_copy_pieces(vpad_ref, sh_ref):
    n = sh_ref.shape[1]

    def piece(j):
        def run():
            sh_ref[j] = vpad_ref[pl.ds(j, n), :]
        return run

    return [piece(j) for j in range(ROWS)]


def _time_conv_pieces(sh_ref, w_ref, ncol, bases, out_ref):
    def piece(ci, base):
        def run():
            acc = jnp.zeros((CHUNK, ncol), F32)
            for k in range(CONV_K):
                off = SEG_PAD - CONV_PAD + k
                start = base + (off // ROWS) * ROWS
                acc = acc + w_ref[k:k + 1, 0:ncol] * sh_ref[off % ROWS, start:start + CHUNK, :]
            out_ref[ci * CHUNK:(ci + 1) * CHUNK, 0:ncol] = acc
        return run

    return [piece(ci, base) for ci, base in enumerate(bases)]


def _mods_body(act_ref, w_ref, b_ref, o_ref):
    a = _silu(act_ref[...])
    w = w_ref[0]
    a_hi = a.astype(BF16)
    a_lo = (a - a_hi.astype(F32)).astype(BF16)
    w_hi = w.astype(BF16)
    w_lo = (w - w_hi.astype(F32)).astype(BF16)
    both = _dot(jnp.concatenate([a_hi, a_lo], axis=0), w_hi)
    bias = b_ref[pl.ds(pl.program_id(0), 1), :]
    o_ref[0] = both[:ROWS] + (_dot(a_hi, w_lo) + both[ROWS:]) + bias


def _mods_call(act, w_mod, b_mod):
    depth, d_model, d3 = w_mod.shape
    ncol = d_model
    return pl.pallas_call(
        _mods_body,
        grid=(depth, d3 // ncol),
        in_specs=[
            pl.BlockSpec((ROWS, d_model), lambda l, n: (0, 0)),
            pl.BlockSpec((1, d_model, ncol), lambda l, n: (l, 0, n)),
            pl.BlockSpec((depth, ncol), lambda l, n: (0, n)),
        ],
        out_specs=pl.BlockSpec((1, ROWS, ncol), lambda l, n: (l, 0, n)),
        out_shape=jax.ShapeDtypeStruct((depth, ROWS, d3), F32),
        compiler_params=pltpu.CompilerParams(
            dimension_semantics=("arbitrary", "arbitrary"), vmem_limit_bytes=VMEM_LIMIT),
        name="mods",
    )(act, w_mod, b_mod)


def _modulate(x, g, m, d_model):
    return _rms_norm(x, g) * (1.0 + m[:, d_model:2 * d_model]) + m[:, :d_model]


def _ctx_body(ctx_ref, mod_ref, gpre_ref, gpost_ref, win_ref, wlast_ref, wo_ref, wg_ref, cw_ref,
              cb_ref, br_ref, bi_ref, lam_ref, dww_ref, dwb_ref, lng_ref, lnb_ref,
              h0_ref,
              ua_s, vc_s, g_s, hl_s, al_s, po_s, vpad_s, sh_s, y_s,
              *, depth, d_model, width, n, mod_row):
    xc = ctx_ref[0]
    zeros_pad = jnp.zeros((SEG_PAD, width), F32)
    no_rows = jnp.zeros((LRU_CONV - 1, width), F32)
    zero_state = jnp.zeros((1, width), F32)
    for l in range(depth):
        update = l < depth - 1
        m = mod_ref[l, mod_row:mod_row + 1, :]
        hc = _modulate(xc, gpre_ref[l:l + 1, :], m, d_model).astype(BF16)
        hcp = _to_sublane_major(hc)
        if update:
            ua_s[UA_ROW0:UA_ROW0 + n, :] = _dot(hcp, win_ref[l, :, :2 * width])
            ub = _dot(hc, win_ref[l, :, 2 * width:])
        else:
            ua_s[UA_ROW0:UA_ROW0 + n, :width] = _dot(hcp, wlast_ref[0])
        _short_convs(ua_s, no_rows, no_rows, cw_ref[l], cb_ref[l], vc_s, n, width)
        for piece in _gate_pieces(vc_s, wg_ref.at[l], g_s, width):
            piece()
        hf_last, hb_first, _ = _rglru_scan(
            ua_s, br_ref[l], bi_ref[l], lam_ref[l], zero_state, vc_s, g_s, hl_s, al_s,
            po_s if update else None, None, n, width)
        h0_ref[0, 2 * l:2 * l + 1, :] = hf_last
        h0_ref[0, 2 * l + 1:2 * l + 2, :] = hb_first
        if update:
            v = ub[:, :width] * _sigmoid(ub[:, width:2 * width])
            vpad_s[0:SEG_PAD, :] = zeros_pad
            vpad_s[SEG_PAD:SEG_PAD + n, :] = v
            vpad_s[SEG_PAD + n:, :] = zeros_pad
            for piece in _shifted_copy_pieces(vpad_s, sh_s) + _time_conv_pieces(
                    sh_s, dww_ref.at[l], width, [ci * CHUNK for ci in range(n // CHUNK)], y_s):
                piece()
            y = y_s[...] + dwb_ref[l:l + 1, :]
            conv = _silu(_layer_norm(y, lng_ref[l:l + 1, :], lnb_ref[l:l + 1, :]))
            mix_a = _to_time_order(po_s[...].astype(BF16))
            mix_b = (conv * _silu(ub[:, 2 * width:])).astype(BF16)
            mix = _dot(mix_a, wo_ref[l, :width, :]) + _dot(mix_b, wo_ref[l, width:, :])
            xc = xc + m[:, 2 * d_model:] * _rms_norm(mix, gpost_ref[l:l + 1, :])


def _ctx_call(ctx, mod_c, mod_row, g_pre, g_post, w_in_bf, w_out_bf, wg, conv_a_w, conv_a_b,
              b_rgate, b_igate, lru_lambda, dw_w, dw_b, ln_g, ln_b):
    bsz, n, d_model = ctx.shape
    depth = w_in_bf.shape[0]
    width = conv_a_b.shape[-1]
    npad = n + 2 * SEG_PAD
    def full(a):
        nd = a.ndim
        return pl.BlockSpec(a.shape, lambda b, _nd=nd: (0,) * _nd)

    def head(a):
        return pl.BlockSpec((depth - 1,) + a.shape[1:], lambda b: (0, 0, 0))

    small = (wg, conv_a_w, conv_a_b, b_rgate, b_igate, lru_lambda, dw_w, dw_b, ln_g, ln_b)
    params = (mod_c, g_pre, g_post, w_in_bf, w_in_bf, w_out_bf) + small
    return pl.pallas_call(
        functools.partial(_ctx_body, depth=depth, d_model=d_model, width=width, n=n,
                          mod_row=mod_row),
        grid=(bsz,),
        in_specs=[pl.BlockSpec((1, n, d_model), lambda b: (b, 0, 0)), full(mod_c), full(g_pre),
                  full(g_post), head(w_in_bf),
                  pl.BlockSpec((1, d_model, width), lambda b: (depth - 1, 0, 0)),
                  head(w_out_bf)] + [full(a) for a in small],
        out_specs=pl.BlockSpec((1, 2 * depth, width), lambda b: (b, 0, 0)),
        out_shape=jax.ShapeDtypeStruct((bsz, 2 * depth, width), F32),
        scratch_shapes=[
            pltpu.VMEM((n + 2 * UA_ROW0, 2 * width), F32),
            pltpu.VMEM((2, n, width), F32),
            pltpu.VMEM((2, n, 2 * width), F32),
            pltpu.VMEM((2, n, width), F32),
            pltpu.VMEM((2, n, width), F32),
            pltpu.VMEM((n, width), F32),
            pltpu.VMEM((npad, width), F32),
            pltpu.VMEM((ROWS, npad - ROWS, width), F32),
            pltpu.VMEM((n, width), F32),
        ],
        compiler_params=pltpu.CompilerParams(
            dimension_semantics=("arbitrary",), vmem_limit_bytes=VMEM_LIMIT),
        name="context",
    )(ctx, *params)


def _proj_body(x_ref, mod_ref, g_ref, w_ref, hx_ref, v1_ref, v2_ref, *, l, nt, d_model, width):
    b = pl.program_id(0)
    j = pl.program_id(1)
    is_pad = jnp.logical_or(j == 0, j == nt + 1)

    @pl.when(is_pad)
    def _():
        v2_ref[...] = jnp.zeros(v2_ref.shape, v2_ref.dtype)

    @pl.when(jnp.logical_not(is_pad))
    def _():
        half = width // 2
        rows = ROWS_PER_CHAIN
        for c in range(x_ref.shape[1] // rows):
            rs = slice(c * rows, (c + 1) * rows)
            hb = _modulate(x_ref[0, rs, :], g_ref[l:l + 1, :], mod_ref[l, pl.ds(b, 1), :],
                           d_model).astype(BF16)
            hx_ref[0, rs, :] = hb
            u = _dot(hb, w_ref[0])
            v = u[:, :width] * _sigmoid(u[:, width:])
            v1_ref[0, rs, :] = v[:, :half].astype(BF16)
            v2_ref[0, rs, :] = v[:, half:].astype(BF16)


def _proj_call(x, mods, g_pre, w_in_bf, l, width):
    bsz, seq, d_model = x.shape
    half = width // 2
    tt = TILE_PROJ
    nt = seq // tt

    def tok(b, j):
        return (b, jnp.clip(j - 1, 0, nt - 1), 0)

    return pl.pallas_call(
        functools.partial(_proj_body, l=l, nt=nt, d_model=d_model, width=width),
        grid=(bsz, nt + 2),
        in_specs=[
            pl.BlockSpec((1, tt, d_model), tok),
            pl.BlockSpec(mods.shape, lambda b, j: (0, 0, 0)),
            pl.BlockSpec(g_pre.shape, lambda b, j: (0, 0)),
            pl.BlockSpec((1, d_model, 2 * width), lambda b, j: (l, 0, 1)),
        ],
        out_specs=[
            pl.BlockSpec((1, tt, d_model), tok),
            pl.BlockSpec((1, tt, half), tok),
            pl.BlockSpec((1, tt, half), lambda b, j: (b, j, 0)),
        ],
        out_shape=[
            jax.ShapeDtypeStruct((bsz, seq, d_model), BF16),
            jax.ShapeDtypeStruct((bsz, seq, half), BF16),
            jax.ShapeDtypeStruct((bsz, seq + 2 * tt, half), BF16),
        ],
        compiler_params=pltpu.CompilerParams(
            dimension_semantics=("arbitrary", "arbitrary"), vmem_limit_bytes=VMEM_LIMIT),
        name="proj",
    )(x, mods, g_pre, w_in_bf)


def _mix_body(hx_ref, hxp_ref, hxn_ref, v1_ref, v2_ref, wa_ref, wbg_ref, wg_ref, cw_ref, cb_ref,
              br_ref, bi_ref, lam_ref, dww16_ref, dwb_ref, lng_ref, lnb_ref, pmt_ref, pme_ref,
              h0_ref, zero_ref,
              p_ref, q_ref, mb_ref, ab_ref, hbo_ref,
              hxs_s, ua_s, bg_s, vc_s, g_s, hl_s, al_s, carry_s, ev_s, od_s, wr_s, wb_s,
              yr_s, y_s,
              *, l, tt, nt, d_model, width, v2_front):
    i = pl.program_id(1)
    half = width // 2
    nseg = tt // GRID_W
    ext = LRU_CONV - 1
    reach = CONV_PAD * GRID_W
    row_zero = zero_ref[0]

    @pl.when(i == 0)
    def _():
        carry_s[...] = h0_ref[pl.program_id(0), 2 * l:2 * l + 1, :]

    def col_slices(total):
        return [slice(c, c + MXU_N) for c in range(0, total, MXU_N)]

    def zero_based(start, align):
        return pl.multiple_of(row_zero + start, align)

    def prepare(token):
        for k in range(CONV_K):
            wr_s[k] = jnp.broadcast_to(dww16_ref[l, k:k + 1, 0:half], (2 * ROWS, half))
            wb_s[k] = jnp.broadcast_to(dww16_ref[l, k:k + 1, half:], (2 * ROWS, half))

    def spread_rows():
        res = _dot(pme_ref[...], v1_ref[0])
        ev_s[...] = res.astype(BF16)
        moved = jnp.concatenate([res[ROWS:, :], jnp.zeros((ROWS, half), F32)], axis=0)
        od_s[...] = moved.astype(BF16)
        return res[0:ROWS, 0:128]

    def row_conv(chunks):
        def run(token):
            _mark(wr_s.at[CONV_K, :, 0:128], token)
            for ci in chunks:
                for rt in range(CHUNK // (2 * ROWS)):
                    row0 = ci * CHUNK + rt * 2 * ROWS
                    acc = None
                    for g0 in range(0, CONV_K, TAP_GROUP):
                        part = None
                        for k in range(g0, min(g0 + TAP_GROUP, CONV_K)):
                            shift = SEG_PAD - CONV_PAD + k
                            src = row0 + shift * ROWS
                            if shift % 2 == 0:
                                window = ev_s[src:src + 2 * ROWS, :]
                            else:
                                window = od_s[src - ROWS:src + ROWS, :]
                            term = wr_s[row_zero + k] * window
                            part = term if part is None else part + term
                        part = part.astype(F32)
                        acc = part if acc is None else acc + part
                    yr_s[row0:row0 + 2 * ROWS, :] = acc
        return run

    def gather_rows():
        y_s[:, 0:half] = _dot(pmt_ref[...], yr_s[...].astype(BF16))
        return None

    def col_conv(chunks):
        def run(token):
            _mark(wb_s.at[CONV_K, :, 0:128], token)
            for ci in chunks:
                win0 = i * tt + (v2_front - reach) + ci * CHUNK
                for rt in range(CHUNK // (2 * ROWS)):
                    acc = None
                    for g0 in range(0, CONV_K, TAP_GROUP):
                        part = None
                        for k in range(g0, min(g0 + TAP_GROUP, CONV_K)):
                            start = pl.multiple_of(win0 + k * GRID_W + rt * 2 * ROWS, 2 * ROWS)
                            term = wb_s[row_zero + k] * v2_ref[0, pl.ds(start, 2 * ROWS), :]
                            part = term if part is None else part + term
                        part = part.astype(F32)
                        acc = part if acc is None else acc + part
                    row0 = ci * CHUNK + rt * 2 * ROWS
                    y_s[row0:row0 + 2 * ROWS, half:] = acc
        return run

    def permute(slices):
        def run():
            for cs in slices:
                res = _to_sublane_major(hx_ref[0, :, cs])
                hxs_s[0:tt, cs] = res
            return res[0:ROWS, 0:128].astype(F32)
        return run

    def project_a(cs):
        def run():
            res = _dot(hxs_s[0:tt, :], wa_ref[0, :, cs])
            ua_s[UA_ROW0:UA_ROW0 + tt, cs] = res
            return res[0:ROWS, 0:128]
        return run

    def project_b(cs):
        def run():
            res = _dot(hx_ref[0], wbg_ref[0, :, cs])
            bg_s[0:tt, cs] = res
            return res[0:ROWS, 0:128]
        return run

    cols_d = col_slices(d_model)
    cols_a = col_slices(2 * width)
    cols_b = col_slices(width)
    quarter = nseg // 4
    row_stages = [row_conv(range(q * quarter, (q + 1) * quarter)) for q in range(4)]
    col_stages = [col_conv(range(q * 2 * quarter, (q + 1) * 2 * quarter)) for q in range(2)]
    _release(
        [spread_rows, permute(cols_d[:2]), permute(cols_d[2:]), project_b(cols_b[0]),
         project_b(cols_b[1])] + [project_a(cs) for cs in cols_a] + [gather_rows],
        [prepare] + row_stages + col_stages)

    wa_a = wa_ref[0, :, :width]
    a_prev = jnp.where(i > 0, _dot(hxp_ref[0], wa_a)[HALO - ext:, :], 0.0)
    a_next = jnp.where(i < nt - 1, _dot(hxn_ref[0], wa_a)[:ext, :], 0.0)
    _short_convs(ua_s, a_prev, a_next, cw_ref[l], cb_ref[l], vc_s, tt, width)

    def finish_conv(c, nrows):
        def run(token):
            _mark(bg_s.at[tt:tt + ROWS, 0:128], token)
            y = y_s[c * nrows:(c + 1) * nrows, :] + dwb_ref[l:l + 1, :]
            conv = _silu(_layer_norm(y, lng_ref[l:l + 1, :], lnb_ref[l:l + 1, :]))
            out = conv * _silu(bg_s[pl.ds(zero_based(c * nrows, ROWS), nrows), :])
            mb_ref[0, c * nrows:(c + 1) * nrows, :] = out.astype(BF16)
        return run

    def gate_piece(d, hf):
        def run():
            lo = hf * half
            res = _dot(vc_s[d, 0:tt, lo:lo + half].astype(BF16), wg_ref[0, d, hf])
            g_s[d, :, lo:lo + half] = res[:, :half]
            g_s[d, :, width + lo:width + lo + half] = res[:, half:]
            return res[0:ROWS, 0:128]
        return run

    _release([gate_piece(d, hf) for d in range(2) for hf in range(2)],
             [finish_conv(c, tt // 4) for c in range(4)])

    hf_last, hb_first, a_total = _rglru_scan(
        ua_s, br_ref[l], bi_ref[l], lam_ref[l], carry_s[...], vc_s, g_s, hl_s, al_s,
        p_ref.at[0], q_ref.at[0], tt, width)
    carry_s[...] = hf_last
    ab_ref[0, 0] = a_total
    hbo_ref[0, 0] = hb_first


def _mix_call(hx, v1, v2p, w_in_bf, wg, cw, cb, br, bi, lam, dww16, dwb, lng, lnb, h0, l):
    bsz, seq, d_model = hx.shape
    width = cb.shape[-1]
    half = width // 2
    tt = TILE_MIX
    nt = seq // tt
    hb_per_tile = tt // HALO
    n_halo_blocks = seq // HALO
    v2_front = (v2p.shape[1] - seq) // 2
    pme = _row_conv_matrix(tt)

    def full(a):
        nd = a.ndim
        return pl.BlockSpec(a.shape, lambda b, i, _nd=nd: (0,) * _nd)

    small = (cw, cb, br, bi, lam, dww16, dwb, lng, lnb, _time_order_matrix(tt), pme, h0)
    tile = lambda b, i: (b, i, 0)
    out_shapes = [jax.ShapeDtypeStruct((bsz, seq, width), BF16)] * 3 + [
        jax.ShapeDtypeStruct((bsz, nt, 1, width), F32)] * 2
    return pl.pallas_call(
        functools.partial(_mix_body, l=l, tt=tt, nt=nt, d_model=d_model, width=width,
                          v2_front=v2_front),
        grid=(bsz, nt),
        in_specs=[
            pl.BlockSpec((1, tt, d_model), tile),
            pl.BlockSpec((1, HALO, d_model),
                         lambda b, i: (b, jnp.maximum(i * hb_per_tile - 1, 0), 0)),
            pl.BlockSpec((1, HALO, d_model),
                         lambda b, i: (b, jnp.minimum((i + 1) * hb_per_tile, n_halo_blocks - 1), 0)),
            pl.BlockSpec((1, tt, half), tile),
            pl.BlockSpec((1, v2p.shape[1], half), lambda b, i: (b, 0, 0)),
            pl.BlockSpec((1, d_model, 2 * width), lambda b, i: (l, 0, 0)),
            pl.BlockSpec((1, d_model, width), lambda b, i: (l, 0, 4)),
            pl.BlockSpec((1,) + wg.shape[1:], lambda b, i: (l, 0, 0, 0, 0)),
        ] + [full(a) for a in small] + [pl.BlockSpec(memory_space=pltpu.SMEM)],
        out_specs=[pl.BlockSpec((1, tt, width), tile)] * 3 + [
            pl.BlockSpec((1, 1, 1, width), lambda b, i: (b, i, 0, 0))] * 2,
        out_shape=out_shapes,
        scratch_shapes=[
            pltpu.VMEM((tt, d_model), BF16),
            pltpu.VMEM((tt + 2 * UA_ROW0, 2 * width), F32),
            pltpu.VMEM((tt + ROWS, width), F32),
            pltpu.VMEM((2, tt, width), F32),
            pltpu.VMEM((2, tt, 2 * width), F32),
            pltpu.VMEM((2, tt, width), F32),
            pltpu.VMEM((2, tt, width), F32),
            pltpu.VMEM((1, width), F32),
            pltpu.VMEM(pme.shape[:1] + (half,), BF16),
            pltpu.VMEM(pme.shape[:1] + (half,), BF16),
            pltpu.VMEM((CONV_K + 1, 2 * ROWS, half), BF16),
            pltpu.VMEM((CONV_K + 1, 2 * ROWS, half), BF16),
            pltpu.VMEM((tt, half), F32),
            pltpu.VMEM((tt, width), F32),
        ],
        compiler_params=pltpu.CompilerParams(
            dimension_semantics=("arbitrary", "arbitrary"), vmem_limit_bytes=VMEM_LIMIT,
            ),
        name="mixer",
    )(hx, hx, hx, v1, v2p, w_in_bf, w_in_bf, wg, *small, jnp.zeros((1,), jnp.int32))


def _out_body(p_ref, q_ref, mb_ref, x_ref, wo_ref, gpost_ref, mod_ref, ab_ref, hb_ref, h0_ref,
              *rest, l, d_model, width, fuse_next):
    if fuse_next:
        gpre_ref, wbn_ref, _, o_ref, hx_ref, v1_ref, v2_ref, carry_s = rest
    else:
        o_ref, carry_s = rest
    b = pl.program_id(0)

    @pl.when(pl.program_id(1) == 0)
    def _():
        carry_s[...] = h0_ref[b, 2 * l + 1:2 * l + 2, :]

    gate = mod_ref[l, pl.ds(b, 1), 2 * d_model:]
    rows = ROWS_PER_CHAIN
    half = width // 2
    tm = TILE_MIX
    c = carry_s[...]
    for st in reversed(range(p_ref.shape[1] // tm)):
        t0 = st * tm
        mix_a = (p_ref[0, t0:t0 + tm, :].astype(F32)
                 + q_ref[0, t0:t0 + tm, :].astype(F32) * c).astype(BF16)
        mix_t = _to_time_order(mix_a)
        c = hb_ref[0, st] + ab_ref[0, st] * c
        for ch in range(tm // rows):
            rs = slice(t0 + ch * rows, t0 + (ch + 1) * rows)
            lhs = jnp.concatenate([mix_t[ch * rows:(ch + 1) * rows, :], mb_ref[0, rs, :]], axis=1)
            mix = _dot(lhs, wo_ref[0])
            x_new = x_ref[0, rs, :] + gate * _rms_norm(mix, gpost_ref[l:l + 1, :])
            o_ref[0, rs, :] = x_new
            if fuse_next:
                hb = _modulate(x_new, gpre_ref[l + 1:l + 2, :], mod_ref[l + 1, pl.ds(b, 1), :],
                               d_model).astype(BF16)
                hx_ref[0, rs, :] = hb
                u = _dot(hb, wbn_ref[0])
                v = u[:, :width] * _sigmoid(u[:, width:])
                v1_ref[0, rs, :] = v[:, :half].astype(BF16)
                v2_ref[0, rs, :] = v[:, half:].astype(BF16)
    carry_s[...] = c


def _out_call(p, q, mb, x, w_out_bf, g_post, mods, ab, hbo, h0, l, nxt=None):
    bsz, seq, d_model = x.shape
    width = p.shape[-1]
    half = width // 2
    tt = TILE_OUT
    tm = TILE_MIX
    nt = seq // tt
    per = tt // tm
    pad_blocks = TILE_PROJ // tt
    rev = lambda b, j: (b, nt - 1 - j, 0)

    def full(a):
        nd = a.ndim
        return pl.BlockSpec(a.shape, lambda b, j, _nd=nd: (0,) * _nd)

    operands = [p, q, mb, x, w_out_bf, g_post, mods, ab, hbo, h0]
    in_specs = [
        pl.BlockSpec((1, tt, width), rev),
        pl.BlockSpec((1, tt, width), rev),
        pl.BlockSpec((1, tt, width), rev),
        pl.BlockSpec((1, tt, d_model), rev),
        pl.BlockSpec((1,) + w_out_bf.shape[1:], lambda b, j: (l, 0, 0)),
        full(g_post),
        full(mods),
        pl.BlockSpec((1, per, 1, width), lambda b, j: (b, nt - 1 - j, 0, 0)),
        pl.BlockSpec((1, per, 1, width), lambda b, j: (b, nt - 1 - j, 0, 0)),
        full(h0),
    ]
    out_specs = [pl.BlockSpec((1, tt, d_model), rev)]
    out_shape = [jax.ShapeDtypeStruct((bsz, seq, d_model), F32)]
    aliases = {}
    if nxt is not None:
        g_pre, w_in_bf, v2_buf = nxt
        operands += [g_pre, w_in_bf, v2_buf]
        in_specs += [
            full(g_pre),
            pl.BlockSpec((1, d_model, 2 * width), lambda b, j: (l + 1, 0, 1)),
            pl.BlockSpec(memory_space=pl.ANY),
        ]
        out_specs += [
            pl.BlockSpec((1, tt, d_model), rev),
            pl.BlockSpec((1, tt, half), rev),
            pl.BlockSpec((1, tt, half), lambda b, j: (b, nt - 1 - j + pad_blocks, 0)),
        ]
        out_shape += [
            jax.ShapeDtypeStruct((bsz, seq, d_model), BF16),
            jax.ShapeDtypeStruct((bsz, seq, half), BF16),
            jax.ShapeDtypeStruct(v2_buf.shape, BF16),
        ]
        aliases = {len(operands) - 1: 3}
    res = pl.pallas_call(
        functools.partial(_out_body, l=l, d_model=d_model, width=width,
                          fuse_next=nxt is not None),
        grid=(bsz, nt),
        in_specs=in_specs,
        out_specs=out_specs,
        out_shape=out_shape,
        input_output_aliases=aliases,
        scratch_shapes=[pltpu.VMEM((1, width), F32)],
        compiler_params=pltpu.CompilerParams(
            dimension_semantics=("arbitrary", "arbitrary"), vmem_limit_bytes=VMEM_LIMIT),
        name="out_proj" if nxt is not None else "out",
    )(*operands)
    return res if nxt is not None else res[0]


def _pack_gate_weights(w_r, w_i):
    depth, ndir, heads, hd, _ = w_r.shape
    hh = heads // 2
    eye = jnp.eye(hh, dtype=w_r.dtype)

    def bd(w):
        w = w.reshape(depth, ndir, 2, hh, hd, hd)
        return jnp.einsum("ldfhij,hg->ldfhigj", w, eye).reshape(depth, ndir, 2, hh * hd, hh * hd)

    return jnp.concatenate([bd(w_r), bd(w_i)], axis=-1).astype(BF16)


def kernel(x, c, ctx, c_ctx, w_mod, b_mod, g_pre, g_post, w_in, conv_a_w, conv_a_b, w_rgate,
           b_rgate, w_igate, b_igate, lru_lambda, dw_w, dw_b, ln_g, ln_b, w_out):
    bsz, seq, d_model = x.shape
    depth = w_mod.shape[0]
    width = conv_a_b.shape[-1]
    assert depth >= 2 and bsz + 1 <= ROWS and seq % TILE_PROJ == 0 and seq % TILE_MIX == 0
    assert TILE_MIX % GRID_W == 0 and TILE_PROJ >= CONV_PAD * GRID_W
    assert seq % TILE_OUT == 0 and TILE_OUT % TILE_MIX == 0 and TILE_PROJ % TILE_OUT == 0
    assert ctx.shape[1] % CHUNK == 0 and width % MXU_N == 0 and d_model % MXU_N == 0

    act = jnp.concatenate(
        [c, c_ctx[None, :], jnp.zeros((ROWS - bsz - 1, d_model), F32)], axis=0)
    mods = _mods_call(act, w_mod, b_mod)

    w_in_bf = w_in.astype(BF16)
    w_out_bf = w_out.astype(BF16)
    wg = _pack_gate_weights(w_rgate, w_igate)

    h0 = _ctx_call(ctx, mods, bsz, g_pre, g_post, w_in_bf, w_out_bf, wg, conv_a_w, conv_a_b,
                   b_rgate, b_igate, lru_lambda, dw_w, dw_b, ln_g, ln_b)

    dw_w16 = dw_w.astype(BF16)
    hx, v1, v2p = _proj_call(x, mods, g_pre, w_in_bf, 0, width)
    for l in range(depth):
        p, q, mb, ab, hbo = _mix_call(
            hx, v1, v2p, w_in_bf, wg, conv_a_w, conv_a_b, b_rgate, b_igate, lru_lambda, dw_w16,
            dw_b, ln_g, ln_b, h0, l)
        if l + 1 < depth:
            x, hx, v1, v2p = _out_call(p, q, mb, x, w_out_bf, g_post, mods, ab, hbo, h0, l,
                                       nxt=(g_pre, w_in_bf, v2p))
        else:
            x = _out_call(p, q, mb, x, w_out_bf, g_post, mods, ab, hbo, h0, l)
    return x
```

```python
import functools

import jax
import jax.numpy as jnp
import numpy as np
from jax import lax
from jax.experimental import pallas as pl
from jax.experimental.pallas import tpu as pltpu

F32 = jnp.float32
BF16 = jnp.bfloat16

EPS = 1e-6
LRU_C = 8.0
LRU_CONV = 4
CONV_K = 31
CONV_PAD = CONV_K // 2
GRID_W = 64
HALO = 16
ROWS = 8
CHUNK = 64
SEG_PAD = 16
MXU_N = 256
TAP_GROUP = 16
ROWS_PER_CHAIN = 512

TILE_PROJ = 1024
TILE_MIX = 512
TILE_OUT = 1024
VMEM_LIMIT = 56 * 1024 * 1024


def _sigmoid(x):
    return 0.5 + 0.5 * jnp.tanh(0.5 * x)


def _silu(x):
    return x * _sigmoid(x)


def _softplus(x):
    return jnp.maximum(x, 0.0) + jnp.log1p(jnp.exp(-jnp.abs(x)))


def _rms_norm(x, g):
    ms = jnp.mean(x * x, axis=-1, keepdims=True)
    return x * lax.rsqrt(ms + EPS) * g


def _layer_norm(x, g, b):
    mu = jnp.mean(x, axis=-1, keepdims=True)
    xc = x - mu
    var = jnp.mean(xc * xc, axis=-1, keepdims=True)
    return xc * lax.rsqrt(var + EPS) * g + b


def _dot(a, b):
    return jnp.dot(a, b, preferred_element_type=F32)


def _mark(spare_ref, token):
    if token is None:
        return
    reps = spare_ref.shape[0] // ROWS
    spare_ref[...] = jnp.concatenate([token] * reps, axis=0).astype(spare_ref.dtype)


def _release(mxu_pieces, vpu_pieces):
    token = None
    for k in range(max(len(mxu_pieces), len(vpu_pieces))):
        new_token = mxu_pieces[k]() if k < len(mxu_pieces) else token
        if k < len(vpu_pieces):
            vpu_pieces[k](token)
        token = new_token


def _to_sublane_major(x):
    n, cols = x.shape
    return jnp.swapaxes(x.reshape(ROWS, n // ROWS, cols), 0, 1).reshape(n, cols)


def _to_time_order(x):
    n, cols = x.shape
    return jnp.swapaxes(x.reshape(n // ROWS, ROWS, cols), 0, 1).reshape(n, cols)


def _time_order_matrix(n):
    i = np.arange(n)
    src = ROWS * (i % (n // ROWS)) + i // (n // ROWS)
    return jnp.asarray((src[:, None] == i[None, :]).astype(np.float32), BF16)


UA_ROW0 = (LRU_CONV - 1) * ROWS


def _row_conv_matrix(n):
    cols = n // ROWS
    rows = (cols + 2 * SEG_PAD) * ROWS
    mat = np.zeros((rows, n), np.float32)
    for i in range(rows):
        q, s = divmod(i, ROWS)
        if SEG_PAD <= q < SEG_PAD + cols:
            mat[i, cols * s + q - SEG_PAD] = 1.0
    return jnp.asarray(mat, BF16)


def _short_convs(ua_s, a_prev, a_next, cw, cb, vc_s, n, width):
    sub = n // ROWS
    ext = LRU_CONV - 1
    row = lax.broadcasted_iota(jnp.int32, (ROWS, width), 0)
    cw = 0.5 * cw
    cb = 0.5 * cb

    def blk(p):
        return slice(p * ROWS, (p + 1) * ROWS)

    for k in range(1, ext + 1):
        tail = pltpu.roll(ua_s[blk(ext + sub - k), :width], 1, 0)
        ua_s[blk(ext - k), :width] = jnp.where(row == 0, a_prev[ext - k:ext - k + 1, :], tail)
    for k in range(ext):
        head = pltpu.roll(ua_s[blk(ext + k), :width], ROWS - 1, 0)
        ua_s[blk(ext + sub + k), :width] = jnp.where(row == ROWS - 1, a_next[k:k + 1, :], head)

    def chunk(ci, carry):
        base = ci * CHUNK
        for d in range(2):
            first = 0 if d == 0 else ext
            acc = jnp.broadcast_to(cb[d:d + 1, :], (CHUNK, width))
            for k in range(LRU_CONV):
                start = pl.multiple_of(base + (first + k) * ROWS, ROWS)
                acc = acc + cw[d, k:k + 1, :] * ua_s[pl.ds(start, CHUNK), :width]
            vc_s[d, pl.ds(pl.multiple_of(base, CHUNK), CHUNK), :] = acc
        return carry

    lax.fori_loop(0, n // CHUNK, chunk, 0, unroll=True)


def _gate_pieces(vc_s, wg_ref, g_s, width):
    half = width // 2

    def piece(d, hf):
        def run():
            res = _dot(vc_s[d, :, hf * half:(hf + 1) * half].astype(BF16), wg_ref[d, hf])
            g_s[d, :, hf * half:(hf + 1) * half] = res[:, :half]
            g_s[d, :, width + hf * half:width + (hf + 1) * half] = res[:, half:]
        return run

    return [piece(d, hf) for d in range(2) for hf in range(2)]


def _sublane_scan(a, h, row, reverse):
    for s in (1, 2, 4):
        if reverse:
            m, shift = row < ROWS - s, ROWS - s
        else:
            m, shift = row >= s, s
        a_sh = jnp.where(m, pltpu.roll(a, shift, 0), 1.0)
        h_sh = jnp.where(m, pltpu.roll(h, shift, 0), 0.0)
        h = h + a * h_sh
        a = a * a_sh
    return a, h


def _rglru_scan(ua_s, br, bi, lam, c0, vc_s, g_s, hl_s, al_s, po_s, q_s, n, width):
    sub = n // ROWS
    row = lax.broadcasted_iota(jnp.int32, (ROWS, width), 0)
    spl_half = (-0.5 * LRU_C) * _softplus(-lam)

    def scan_columns(cols):
        nc = cols.stop - cols.start
        gate_i = slice(width + cols.start, width + cols.stop)

        def rows_of(v, d):
            return jnp.broadcast_to(v[d:d + 1, cols], (ROWS, nc))

        spl_c = [rows_of(spl_half, d) for d in range(2)]
        br_c = [rows_of(0.5 * br, d) for d in range(2)]
        bi_c = [rows_of(0.5 * bi, d) for d in range(2)]

        def coeffs(d, r0):
            vh = vc_s[d, pl.ds(r0, ROWS), cols]
            t_r = jnp.tanh(g_s[d, pl.ds(r0, ROWS), cols] + br_c[d])
            t_i = jnp.tanh(g_s[d, pl.ds(r0, ROWS), gate_i] + bi_c[d])
            la = spl_c[d] + spl_c[d] * t_r
            a = jnp.exp(la)
            x = jnp.tanh(la) * (-1.0 - a * a)
            mult = jnp.where(x > 0.0, x * lax.rsqrt(x), 0.0)
            return a, mult * (vh + vh * t_i)

        def local(j, carry):
            hf, af, hb, ab = carry
            r0 = pl.multiple_of(j * ROWS, ROWS)
            a, b = coeffs(0, r0)
            hf = a * hf + b
            af = a * af
            hl_s[0, pl.ds(r0, ROWS), cols] = hf
            al_s[0, pl.ds(r0, ROWS), cols] = af
            r1 = pl.multiple_of((sub - 1 - j) * ROWS, ROWS)
            a, b = coeffs(1, r1)
            hb = a * hb + b
            ab = a * ab
            hl_s[1, pl.ds(r1, ROWS), cols] = hb
            al_s[1, pl.ds(r1, ROWS), cols] = ab
            return hf, af, hb, ab

        zero = jnp.zeros((ROWS, nc), F32)
        one = jnp.ones((ROWS, nc), F32)
        return lax.fori_loop(0, sub, local, (zero, one, zero, one), unroll=True)

    halves = [scan_columns(slice(c, c + width // 2)) for c in (0, width // 2)]
    hf, af, hb, ab = (jnp.concatenate(parts, axis=1) for parts in zip(*halves))

    af, hf = _sublane_scan(af, hf, row, reverse=False)
    end_f = hf + af * c0
    c_f = jnp.where(row == 0, c0, pltpu.roll(end_f, 1, 0))
    ab, hb = _sublane_scan(ab, hb, row, reverse=True)
    c_b = jnp.where(row == ROWS - 1, 0.0, pltpu.roll(hb, ROWS - 1, 0))
    c_a = jnp.where(row == ROWS - 1, 1.0, pltpu.roll(ab, ROWS - 1, 0))

    if po_s is not None:
        step = 2 * ROWS
        c_f, c_b, c_a = (jnp.concatenate([c, c], axis=0) for c in (c_f, c_b, c_a))

        def fix(j, carry):
            r0 = pl.ds(pl.multiple_of(j * step, step), step)
            a_b = al_s[1, r0, :]
            h = (hl_s[0, r0, :] + al_s[0, r0, :] * c_f) + (hl_s[1, r0, :] + a_b * c_b)
            sg = _silu(ua_s[pl.ds(pl.multiple_of(UA_ROW0 + j * step, ROWS), step), width:])
            po_s[r0, :] = (h * sg).astype(po_s.dtype)
            if q_s is not None:
                q_s[r0, :] = ((a_b * c_a) * sg).astype(q_s.dtype)
            return carry

        lax.fori_loop(0, n // step, fix, 0, unroll=4)

    return end_f[ROWS - 1:ROWS, :], hb[0:1, :], ab[0:1, :]


def _shifted_copy_pieces(vpad_ref, sh_ref):
    n = sh_ref.shape[1]

    def piece(j):
        def run():
            sh_ref[j] = vpad_ref[pl.ds(j, n), :]
        return run

    return [piece(j) for j in range(ROWS)]


def _time_conv_pieces(sh_ref, w_ref, ncol, bases, out_ref):
    def piece(ci, base):
        def run():
            acc = jnp.zeros((CHUNK, ncol), F32)
            for k in range(CONV_K):
                off = SEG_PAD - CONV_PAD + k
                start = base + (off // ROWS) * ROWS
                acc = acc + w_ref[k:k + 1, 0:ncol] * sh_ref[off % ROWS, start:start + CHUNK, :]
            out_ref[ci * CHUNK:(ci + 1) * CHUNK, 0:ncol] = acc
        return run

    return [piece(ci, base) for ci, base in enumerate(bases)]


def _mods_body(act_ref, w_ref, b_ref, o_ref):
    a = _silu(act_ref[...])
    w = w_ref[0]
    a_hi = a.astype(BF16)
    a_lo = (a - a_hi.astype(F32)).astype(BF16)
    w_hi = w.astype(BF16)
    w_lo = (w - w_hi.astype(F32)).astype(BF16)
    both = _dot(jnp.concatenate([a_hi, a_lo], axis=0), w_hi)
    bias = b_ref[pl.ds(pl.program_id(0), 1), :]
    o_ref[0] = both[:ROWS] + (_dot(a_hi, w_lo) + both[ROWS:]) + bias


def _mods_call(act, w_mod, b_mod):
    depth, d_model, d3 = w_mod.shape
    ncol = d_model
    return pl.pallas_call(
        _mods_body,
        grid=(depth, d3 // ncol),
        in_specs=[
            pl.BlockSpec((ROWS, d_model), lambda l, n: (0, 0)),
            pl.BlockSpec((1, d_model, ncol), lambda l, n: (l, 0, n)),
            pl.BlockSpec((depth, ncol), lambda l, n: (0, n)),
        ],
        out_specs=pl.BlockSpec((1, ROWS, ncol), lambda l, n: (l, 0, n)),
        out_shape=jax.ShapeDtypeStruct((depth, ROWS, d3), F32),
        compiler_params=pltpu.CompilerParams(
            dimension_semantics=("arbitrary", "arbitrary"), vmem_limit_bytes=VMEM_LIMIT),
        name="mods",
    )(act, w_mod, b_mod)


def _modulate(x, g, m, d_model):
    return _rms_norm(x, g) * (1.0 + m[:, d_model:2 * d_model]) + m[:, :d_model]


def _ctx_body(ctx_ref, mod_ref, gpre_ref, gpost_ref, win_ref, wlast_ref, wo_ref, wg_ref, cw_ref,
              cb_ref, br_ref, bi_ref, lam_ref, dww_ref, dwb_ref, lng_ref, lnb_ref,
              h0_ref,
              ua_s, vc_s, g_s, hl_s, al_s, po_s, vpad_s, sh_s, y_s,
              *, depth, d_model, width, n, mod_row):
    xc = ctx_ref[0]
    zeros_pad = jnp.zeros((SEG_PAD, width), F32)
    no_rows = jnp.zeros((LRU_CONV - 1, width), F32)
    zero_state = jnp.zeros((1, width), F32)
    for l in range(depth):
        update = l < depth - 1
        m = mod_ref[l, mod_row:mod_row + 1, :]
        hc = _modulate(xc, gpre_ref[l:l + 1, :], m, d_model).astype(BF16)
        hcp = _to_sublane_major(hc)
        if update:
            ua_s[UA_ROW0:UA_ROW0 + n, :] = _dot(hcp, win_ref[l, :, :2 * width])
            ub = _dot(hc, win_ref[l, :, 2 * width:])
        else:
            ua_s[UA_ROW0:UA_ROW0 + n, :width] = _dot(hcp, wlast_ref[0])
        _short_convs(ua_s, no_rows, no_rows, cw_ref[l], cb_ref[l], vc_s, n, width)
        for piece in _gate_pieces(vc_s, wg_ref.at[l], g_s, width):
            piece()
        hf_last, hb_first, _ = _rglru_scan(
            ua_s, br_ref[l], bi_ref[l], lam_ref[l], zero_state, vc_s, g_s, hl_s, al_s,
            po_s if update else None, None, n, width)
        h0_ref[0, 2 * l:2 * l + 1, :] = hf_last
        h0_ref[0, 2 * l + 1:2 * l + 2, :] = hb_first
        if update:
            v = ub[:, :width] * _sigmoid(ub[:, width:2 * width])
            vpad_s[0:SEG_PAD, :] = zeros_pad
            vpad_s[SEG_PAD:SEG_PAD + n, :] = v
            vpad_s[SEG_PAD + n:, :] = zeros_pad
            for piece in _shifted_copy_pieces(vpad_s, sh_s) + _time_conv_pieces(
                    sh_s, dww_ref.at[l], width, [ci * CHUNK for ci in range(n // CHUNK)], y_s):
                piece()
            y = y_s[...] + dwb_ref[l:l + 1, :]
            conv = _silu(_layer_norm(y, lng_ref[l:l + 1, :], lnb_ref[l:l + 1, :]))
            mix_a = _to_time_order(po_s[...].astype(BF16))
            mix_b = (conv * _silu(ub[:, 2 * width:])).astype(BF16)
            mix = _dot(mix_a, wo_ref[l, :width, :]) + _dot(mix_b, wo_ref[l, width:, :])
            xc = xc + m[:, 2 * d_model:] * _rms_norm(mix, gpost_ref[l:l + 1, :])


def _ctx_call(ctx, mod_c, mod_row, g_pre, g_post, w_in_bf, w_out_bf, wg, conv_a_w, conv_a_b,
              b_rgate, b_igate, lru_lambda, dw_w, dw_b, ln_g, ln_b):
    bsz, n, d_model = ctx.shape
    depth = w_in_bf.shape[0]
    width = conv_a_b.shape[-1]
    npad = n + 2 * SEG_PAD
    def full(a):
        nd = a.ndim
        return pl.BlockSpec(a.shape, lambda b, _nd=nd: (0,) * _nd)

    def head(a):
        return pl.BlockSpec((depth - 1,) + a.shape[1:], lambda b: (0, 0, 0))

    small = (wg, conv_a_w, conv_a_b, b_rgate, b_igate, lru_lambda, dw_w, dw_b, ln_g, ln_b)
    params = (mod_c, g_pre, g_post, w_in_bf, w_in_bf, w_out_bf) + small
    return pl.pallas_call(
        functools.partial(_ctx_body, depth=depth, d_model=d_model, width=width, n=n,
                          mod_row=mod_row),
        grid=(bsz,),
        in_specs=[pl.BlockSpec((1, n, d_model), lambda b: (b, 0, 0)), full(mod_c), full(g_pre),
                  full(g_post), head(w_in_bf),
                  pl.BlockSpec((1, d_model, width), lambda b: (depth - 1, 0, 0)),
                  head(w_out_bf)] + [full(a) for a in small],
        out_specs=pl.BlockSpec((1, 2 * depth, width), lambda b: (b, 0, 0)),
        out_shape=jax.ShapeDtypeStruct((bsz, 2 * depth, width), F32),
        scratch_shapes=[
            pltpu.VMEM((n + 2 * UA_ROW0, 2 * width), F32),
            pltpu.VMEM((2, n, width), F32),
            pltpu.VMEM((2, n, 2 * width), F32),
            pltpu.VMEM((2, n, width), F32),
            pltpu.VMEM((2, n, width), F32),
            pltpu.VMEM((n, width), F32),
            pltpu.VMEM((npad, width), F32),
            pltpu.VMEM((ROWS, npad - ROWS, width), F32),
            pltpu.VMEM((n, width), F32),
        ],
        compiler_params=pltpu.CompilerParams(
            dimension_semantics=("arbitrary",), vmem_limit_bytes=VMEM_LIMIT),
        name="context",
    )(ctx, *params)


def _proj_body(x_ref, mod_ref, g_ref, w_ref, hx_ref, v1_ref, v2_ref, *, l, nt, d_model, width):
    b = pl.program_id(0)
    j = pl.program_id(1)
    is_pad = jnp.logical_or(j == 0, j == nt + 1)

    @pl.when(is_pad)
    def _():
        v2_ref[...] = jnp.zeros(v2_ref.shape, v2_ref.dtype)

    @pl.when(jnp.logical_not(is_pad))
    def _():
        half = width // 2
        rows = ROWS_PER_CHAIN
        for c in range(x_ref.shape[1] // rows):
            rs = slice(c * rows, (c + 1) * rows)
            hb = _modulate(x_ref[0, rs, :], g_ref[l:l + 1, :], mod_ref[l, pl.ds(b, 1), :],
                           d_model).astype(BF16)
            hx_ref[0, rs, :] = hb
            u = _dot(hb, w_ref[0])
            v = u[:, :width] * _sigmoid(u[:, width:])
            v1_ref[0, rs, :] = v[:, :half].astype(BF16)
            v2_ref[0, rs, :] = v[:, half:].astype(BF16)


def _proj_call(x, mods, g_pre, w_in_bf, l, width):
    bsz, seq, d_model = x.shape
    half = width // 2
    tt = TILE_PROJ
    nt = seq // tt

    def tok(b, j):
        return (b, jnp.clip(j - 1, 0, nt - 1), 0)

    return pl.pallas_call(
        functools.partial(_proj_body, l=l, nt=nt, d_model=d_model, width=width),
        grid=(bsz, nt + 2),
        in_specs=[
            pl.BlockSpec((1, tt, d_model), tok),
            pl.BlockSpec(mods.shape, lambda b, j: (0, 0, 0)),
            pl.BlockSpec(g_pre.shape, lambda b, j: (0, 0)),
            pl.BlockSpec((1, d_model, 2 * width), lambda b, j: (l, 0, 1)),
        ],
        out_specs=[
            pl.BlockSpec((1, tt, d_model), tok),
            pl.BlockSpec((1, tt, half), tok),
            pl.BlockSpec((1, tt, half), lambda b, j: (b, j, 0)),
        ],
        out_shape=[
            jax.ShapeDtypeStruct((bsz, seq, d_model), BF16),
            jax.ShapeDtypeStruct((bsz, seq, half), BF16),
            jax.ShapeDtypeStruct((bsz, seq + 2 * tt, half), BF16),
        ],
        compiler_params=pltpu.CompilerParams(
            dimension_semantics=("arbitrary", "arbitrary"), vmem_limit_bytes=VMEM_LIMIT),
        name="proj",
    )(x, mods, g_pre, w_in_bf)


def _mix_body(hx_ref, hxp_ref, hxn_ref, v1_ref, v2_ref, wa_ref, wbg_ref, wg_ref, cw_ref, cb_ref,
              br_ref, bi_ref, lam_ref, dww16_ref, dwb_ref, lng_ref, lnb_ref, pmt_ref, pme_ref,
              h0_ref, zero_ref,
              p_ref, q_ref, mb_ref, ab_ref, hbo_ref,
              hxs_s, ua_s, bg_s, vc_s, g_s, hl_s, al_s, carry_s, ev_s, od_s, wr_s, wb_s,
              yr_s, y_s,
              *, l, tt, nt, d_model, width, v2_front):
    i = pl.program_id(1)
    half = width // 2
    nseg = tt // GRID_W
    ext = LRU_CONV - 1
    reach = CONV_PAD * GRID_W
    row_zero = zero_ref[0]

    @pl.when(i == 0)
    def _():
        carry_s[...] = h0_ref[pl.program_id(0), 2 * l:2 * l + 1, :]

    def col_slices(total):
        return [slice(c, c + MXU_N) for c in range(0, total, MXU_N)]

    def zero_based(start, align):
        return pl.multiple_of(row_zero + start, align)

    def prepare(token):
        for k in range(CONV_K):
            wr_s[k] = jnp.broadcast_to(dww16_ref[l, k:k + 1, 0:half], (2 * ROWS, half))
            wb_s[k] = jnp.broadcast_to(dww16_ref[l, k:k + 1, half:], (2 * ROWS, half))

    def spread_rows():
        res = _dot(pme_ref[...], v1_ref[0])
        ev_s[...] = res.astype(BF16)
        moved = jnp.concatenate([res[ROWS:, :], jnp.zeros((ROWS, half), F32)], axis=0)
        od_s[...] = moved.astype(BF16)
        return res[0:ROWS, 0:128]

    def row_conv(chunks):
        def run(token):
            _mark(wr_s.at[CONV_K, :, 0:128], token)
            for ci in chunks:
                for rt in range(CHUNK // (2 * ROWS)):
                    row0 = ci * CHUNK + rt * 2 * ROWS
                    acc = None
                    for g0 in range(0, CONV_K, TAP_GROUP):
                        part = None
                        for k in range(g0, min(g0 + TAP_GROUP, CONV_K)):
                            shift = SEG_PAD - CONV_PAD + k
                            src = row0 + shift * ROWS
                            if shift % 2 == 0:
                                window = ev_s[src:src + 2 * ROWS, :]
                            else:
                                window = od_s[src - ROWS:src + ROWS, :]
                            term = wr_s[row_zero + k] * window
                            part = term if part is None else part + term
                        part = part.astype(F32)
                        acc = part if acc is None else acc + part
                    yr_s[row0:row0 + 2 * ROWS, :] = acc
        return run

    def gather_rows():
        y_s[:, 0:half] = _dot(pmt_ref[...], yr_s[...].astype(BF16))
        return None

    def col_conv(chunks):
        def run(token):
            _mark(wb_s.at[CONV_K, :, 0:128], token)
            for ci in chunks:
                win0 = i * tt + (v2_front - reach) + ci * CHUNK
                for rt in range(CHUNK // (2 * ROWS)):
                    acc = None
                    for g0 in range(0, CONV_K, TAP_GROUP):
                        part = None
                        for k in range(g0, min(g0 + TAP_GROUP, CONV_K)):
                            start = pl.multiple_of(win0 + k * GRID_W + rt * 2 * ROWS, 2 * ROWS)
                            term = wb_s[row_zero + k] * v2_ref[0, pl.ds(start, 2 * ROWS), :]
                            part = term if part is None else part + term
                        part = part.astype(F32)
                        acc = part if acc is None else acc + part
                    row0 = ci * CHUNK + rt * 2 * ROWS
                    y_s[row0:row0 + 2 * ROWS, half:] = acc
        return run

    def permute(slices):
        def run():
            for cs in slices:
                res = _to_sublane_major(hx_ref[0, :, cs])
                hxs_s[0:tt, cs] = res
            return res[0:ROWS, 0:128].astype(F32)
        return run

    def project_a(cs):
        def run():
            res = _dot(hxs_s[0:tt, :], wa_ref[0, :, cs])
            ua_s[UA_ROW0:UA_ROW0 + tt, cs] = res
            return res[0:ROWS, 0:128]
        return run

    def project_b(cs):
        def run():
            res = _dot(hx_ref[0], wbg_ref[0, :, cs])
            bg_s[0:tt, cs] = res
            return res[0:ROWS, 0:128]
        return run

    cols_d = col_slices(d_model)
    cols_a = col_slices(2 * width)
    cols_b = col_slices(width)
    quarter = nseg // 4
    row_stages = [row_conv(range(q * quarter, (q + 1) * quarter)) for q in range(4)]
    col_stages = [col_conv(range(q * 2 * quarter, (q + 1) * 2 * quarter)) for q in range(2)]
    _release(
        [spread_rows, permute(cols_d[:2]), permute(cols_d[2:]), project_b(cols_b[0]),
         project_b(cols_b[1])] + [project_a(cs) for cs in cols_a] + [gather_rows],
        [prepare] + row_stages + col_stages)

    wa_a = wa_ref[0, :, :width]
    a_prev = jnp.where(i > 0, _dot(hxp_ref[0], wa_a)[HALO - ext:, :], 0.0)
    a_next = jnp.where(i < nt - 1, _dot(hxn_ref[0], wa_a)[:ext, :], 0.0)
    _short_convs(ua_s, a_prev, a_next, cw_ref[l], cb_ref[l], vc_s, tt, width)

    def finish_conv(c, nrows):
        def run(token):
            _mark(bg_s.at[tt:tt + ROWS, 0:128], token)
            y = y_s[c * nrows:(c + 1) * nrows, :] + dwb_ref[l:l + 1, :]
            conv = _silu(_layer_norm(y, lng_ref[l:l + 1, :], lnb_ref[l:l + 1, :]))
            out = conv * _silu(bg_s[pl.ds(zero_based(c * nrows, ROWS), nrows), :])
            mb_ref[0, c * nrows:(c + 1) * nrows, :] = out.astype(BF16)
        return run

    def gate_piece(d, hf):
        def run():
            lo = hf * half
            res = _dot(vc_s[d, 0:tt, lo:lo + half].astype(BF16), wg_ref[0, d, hf])
            g_s[d, :, lo:lo + half] = res[:, :half]
            g_s[d, :, width + lo:width + lo + half] = res[:, half:]
            return res[0:ROWS, 0:128]
        return run

    _release([gate_piece(d, hf) for d in range(2) for hf in range(2)],
             [finish_conv(c, tt // 4) for c in range(4)])

    hf_last, hb_first, a_total = _rglru_scan(
        ua_s, br_ref[l], bi_ref[l], lam_ref[l], carry_s[...], vc_s, g_s, hl_s, al_s,
        p_ref.at[0], q_ref.at[0], tt, width)
    carry_s[...] = hf_last
    ab_ref[0, 0] = a_total
    hbo_ref[0, 0] = hb_first


def _mix_call(hx, v1, v2p, w_in_bf, wg, cw, cb, br, bi, lam, dww16, dwb, lng, lnb, h0, l):
    bsz, seq, d_model = hx.shape
    width = cb.shape[-1]
    half = width // 2
    tt = TILE_MIX
    nt = seq // tt
    hb_per_tile = tt // HALO
    n_halo_blocks = seq // HALO
    v2_front = (v2p.shape[1] - seq) // 2
    pme = _row_conv_matrix(tt)

    def full(a):
        nd = a.ndim
        return pl.BlockSpec(a.shape, lambda b, i, _nd=nd: (0,) * _nd)

    small = (cw, cb, br, bi, lam, dww16, dwb, lng, lnb, _time_order_matrix(tt), pme, h0)
    tile = lambda b, i: (b, i, 0)
    out_shapes = [jax.ShapeDtypeStruct((bsz, seq, width), BF16)] * 3 + [
        jax.ShapeDtypeStruct((bsz, nt, 1, width), F32)] * 2
    return pl.pallas_call(
        functools.partial(_mix_body, l=l, tt=tt, nt=nt, d_model=d_model, width=width,
                          v2_front=v2_front),
        grid=(bsz, nt),
        in_specs=[
            pl.BlockSpec((1, tt, d_model), tile),
            pl.BlockSpec((1, HALO, d_model),
                         lambda b, i: (b, jnp.maximum(i * hb_per_tile - 1, 0), 0)),
            pl.BlockSpec((1, HALO, d_model),
                         lambda b, i: (b, jnp.minimum((i + 1) * hb_per_tile, n_halo_blocks - 1), 0)),
            pl.BlockSpec((1, tt, half), tile),
            pl.BlockSpec((1, v2p.shape[1], half), lambda b, i: (b, 0, 0)),
            pl.BlockSpec((1, d_model, 2 * width), lambda b, i: (l, 0, 0)),
            pl.BlockSpec((1, d_model, width), lambda b, i: (l, 0, 4)),
            pl.BlockSpec((1,) + wg.shape[1:], lambda b, i: (l, 0, 0, 0, 0)),
        ] + [full(a) for a in small] + [pl.BlockSpec(memory_space=pltpu.SMEM)],
        out_specs=[pl.BlockSpec((1, tt, width), tile)] * 3 + [
            pl.BlockSpec((1, 1, 1, width), lambda b, i: (b, i, 0, 0))] * 2,
        out_shape=out_shapes,
        scratch_shapes=[
            pltpu.VMEM((tt, d_model), BF16),
            pltpu.VMEM((tt + 2 * UA_ROW0, 2 * width), F32),
            pltpu.VMEM((tt + ROWS, width), F32),
            pltpu.VMEM((2, tt, width), F32),
            pltpu.VMEM((2, tt, 2 * width), F32),
            pltpu.VMEM((2, tt, width), F32),
            pltpu.VMEM((2, tt, width), F32),
            pltpu.VMEM((1, width), F32),
            pltpu.VMEM(pme.shape[:1] + (half,), BF16),
            pltpu.VMEM(pme.shape[:1] + (half,), BF16),
            pltpu.VMEM((CONV_K + 1, 2 * ROWS, half), BF16),
            pltpu.VMEM((CONV_K + 1, 2 * ROWS, half), BF16),
            pltpu.VMEM((tt, half), F32),
            pltpu.VMEM((tt, width), F32),
        ],
        compiler_params=pltpu.CompilerParams(
            dimension_semantics=("arbitrary", "arbitrary"), vmem_limit_bytes=VMEM_LIMIT,
            ),
        name="mixer",
    )(hx, hx, hx, v1, v2p, w_in_bf, w_in_bf, wg, *small, jnp.zeros((1,), jnp.int32))


def _out_body(p_ref, q_ref, mb_ref, x_ref, wo_ref, gpost_ref, mod_ref, ab_ref, hb_ref, h0_ref,
              *rest, l, d_model, width, fuse_next):
    if fuse_next:
        gpre_ref, wbn_ref, _, o_ref, hx_ref, v1_ref, v2_ref, carry_s = rest
    else:
        o_ref, carry_s = rest
    b = pl.program_id(0)

    @pl.when(pl.program_id(1) == 0)
    def _():
        carry_s[...] = h0_ref[b, 2 * l + 1:2 * l + 2, :]

    gate = mod_ref[l, pl.ds(b, 1), 2 * d_model:]
    rows = ROWS_PER_CHAIN
    half = width // 2
    tm = TILE_MIX
    c = carry_s[...]
    for st in reversed(range(p_ref.shape[1] // tm)):
        t0 = st * tm
        mix_a = (p_ref[0, t0:t0 + tm, :].astype(F32)
                 + q_ref[0, t0:t0 + tm, :].astype(F32) * c).astype(BF16)
        mix_t = _to_time_order(mix_a)
        c = hb_ref[0, st] + ab_ref[0, st] * c
        for ch in range(tm // rows):
            rs = slice(t0 + ch * rows, t0 + (ch + 1) * rows)
            lhs = jnp.concatenate([mix_t[ch * rows:(ch + 1) * rows, :], mb_ref[0, rs, :]], axis=1)
            mix = _dot(lhs, wo_ref[0])
            x_new = x_ref[0, rs, :] + gate * _rms_norm(mix, gpost_ref[l:l + 1, :])
            o_ref[0, rs, :] = x_new
            if fuse_next:
                hb = _modulate(x_new, gpre_ref[l + 1:l + 2, :], mod_ref[l + 1, pl.ds(b, 1), :],
                               d_model).astype(BF16)
                hx_ref[0, rs, :] = hb
                u = _dot(hb, wbn_ref[0])
                v = u[:, :width] * _sigmoid(u[:, width:])
                v1_ref[0, rs, :] = v[:, :half].astype(BF16)
                v2_ref[0, rs, :] = v[:, half:].astype(BF16)
    carry_s[...] = c


def _out_call(p, q, mb, x, w_out_bf, g_post, mods, ab, hbo, h0, l, nxt=None):
    bsz, seq, d_model = x.shape
    width = p.shape[-1]
    half = width // 2
    tt = TILE_OUT
    tm = TILE_MIX
    nt = seq // tt
    per = tt // tm
    pad_blocks = TILE_PROJ // tt
    rev = lambda b, j: (b, nt - 1 - j, 0)

    def full(a):
        nd = a.ndim
        return pl.BlockSpec(a.shape, lambda b, j, _nd=nd: (0,) * _nd)

    operands = [p, q, mb, x, w_out_bf, g_post, mods, ab, hbo, h0]
    in_specs = [
        pl.BlockSpec((1, tt, width), rev),
        pl.BlockSpec((1, tt, width), rev),
        pl.BlockSpec((1, tt, width), rev),
        pl.BlockSpec((1, tt, d_model), rev),
        pl.BlockSpec((1,) + w_out_bf.shape[1:], lambda b, j: (l, 0, 0)),
        full(g_post),
        full(mods),
        pl.BlockSpec((1, per, 1, width), lambda b, j: (b, nt - 1 - j, 0, 0)),
        pl.BlockSpec((1, per, 1, width), lambda b, j: (b, nt - 1 - j, 0, 0)),
        full(h0),
    ]
    out_specs = [pl.BlockSpec((1, tt, d_model), rev)]
    out_shape = [jax.ShapeDtypeStruct((bsz, seq, d_model), F32)]
    aliases = {}
    if nxt is not None:
        g_pre, w_in_bf, v2_buf = nxt
        operands += [g_pre, w_in_bf, v2_buf]
        in_specs += [
            full(g_pre),
            pl.BlockSpec((1, d_model, 2 * width), lambda b, j: (l + 1, 0, 1)),
            pl.BlockSpec(memory_space=pl.ANY),
        ]
        out_specs += [
            pl.BlockSpec((1, tt, d_model), rev),
            pl.BlockSpec((1, tt, half), rev),
            pl.BlockSpec((1, tt, half), lambda b, j: (b, nt - 1 - j + pad_blocks, 0)),
        ]
        out_shape += [
            jax.ShapeDtypeStruct((bsz, seq, d_model), BF16),
            jax.ShapeDtypeStruct((bsz, seq, half), BF16),
            jax.ShapeDtypeStruct(v2_buf.shape, BF16),
        ]
        aliases = {len(operands) - 1: 3}
    res = pl.pallas_call(
        functools.partial(_out_body, l=l, d_model=d_model, width=width,
                          fuse_next=nxt is not None),
        grid=(bsz, nt),
        in_specs=in_specs,
        out_specs=out_specs,
        out_shape=out_shape,
        input_output_aliases=aliases,
        scratch_shapes=[pltpu.VMEM((1, width), F32)],
        compiler_params=pltpu.CompilerParams(
            dimension_semantics=("arbitrary", "arbitrary"), vmem_limit_bytes=VMEM_LIMIT),
        name="out_proj" if nxt is not None else "out",
    )(*operands)
    return res if nxt is not None else res[0]


def _pack_gate_weights(w_r, w_i):
    depth, ndir, heads, hd, _ = w_r.shape
    hh = heads // 2
    eye = jnp.eye(hh, dtype=w_r.dtype)

    def bd(w):
        w = w.reshape(depth, ndir, 2, hh, hd, hd)
        return jnp.einsum("ldfhij,hg->ldfhigj", w, eye).reshape(depth, ndir, 2, hh * hd, hh * hd)

    return jnp.concatenate([bd(w_r), bd(w_i)], axis=-1).astype(BF16)


def kernel(x, c, ctx, c_ctx, w_mod, b_mod, g_pre, g_post, w_in, conv_a_w, conv_a_b, w_rgate,
           b_rgate, w_igate, b_igate, lru_lambda, dw_w, dw_b, ln_g, ln_b, w_out):
    bsz, seq, d_model = x.shape
    depth = w_mod.shape[0]
    width = conv_a_b.shape[-1]
    assert depth >= 2 and bsz + 1 <= ROWS and seq % TILE_PROJ == 0 and seq % TILE_MIX == 0
    assert TILE_MIX % GRID_W == 0 and TILE_PROJ >= CONV_PAD * GRID_W
    assert seq % TILE_OUT == 0 and TILE_OUT % TILE_MIX == 0 and TILE_PROJ % TILE_OUT == 0
    assert ctx.shape[1] % CHUNK == 0 and width % MXU_N == 0 and d_model % MXU_N == 0

    act = jnp.concatenate(
        [c, c_ctx[None, :], jnp.zeros((ROWS - bsz - 1, d_model), F32)], axis=0)
    mods = _mods_call(act, w_mod, b_mod)

    w_in_bf = w_in.astype(BF16)
    w_out_bf = w_out.astype(BF16)
    wg = _pack_gate_weights(w_rgate, w_igate)

    h0 = _ctx_call(ctx, mods, bsz, g_pre, g_post, w_in_bf, w_out_bf, wg, conv_a_w, conv_a_b,
                   b_rgate, b_igate, lru_lambda, dw_w, dw_b, ln_g, ln_b)

    dw_w16 = dw_w.astype(BF16)
    hx, v1, v2p = _proj_call(x, mods, g_pre, w_in_bf, 0, width)
    for l in range(depth):
        p, q, mb, ab, hbo = _mix_call(
            hx, v1, v2p, w_in_bf, wg, conv_a_w, conv_a_b, b_rgate, b_igate, lru_lambda, dw_w16,
            dw_b, ln_g, ln_b, h0, l)
        if l + 1 < depth:
            x, hx, v1, v2p = _out_call(p, q, mb, x, w_out_bf, g_post, mods, ab, hbo, h0, l,
                                       nxt=(g_pre, w_in_bf, v2p))
        else:
            x = _out_call(p, q, mb, x, w_out_bf, g_post, mods, ab, hbo, h0, l)
    return x
```

```python
import functools

import jax
import jax.numpy as jnp
import numpy as np
from jax import lax
from jax.experimental import pallas as pl
from jax.experimental.pallas import tpu as pltpu

F32 = jnp.float32
BF16 = jnp.bfloat16

EPS = 1e-6
LRU_C = 8.0
LRU_CONV = 4
CONV_K = 31
CONV_PAD = CONV_K // 2
GRID_W = 64
HALO = 16
ROWS = 8
CHUNK = 64
SEG_PAD = 16
MXU_N = 256
TAP_GROUP = 16
ROWS_PER_CHAIN = 512

TILE_PROJ = 1024
TILE_MIX = 512
TILE_OUT = 1024
VMEM_LIMIT = 56 * 1024 * 1024


def _sigmoid(x):
    return 0.5 + 0.5 * jnp.tanh(0.5 * x)


def _silu(x):
    return x * _sigmoid(x)


def _softplus(x):
    return jnp.maximum(x, 0.0) + jnp.log1p(jnp.exp(-jnp.abs(x)))


def _rms_norm(x, g):
    ms = jnp.mean(x * x, axis=-1, keepdims=True)
    return x * lax.rsqrt(ms + EPS) * g


def _layer_norm(x, g, b):
    mu = jnp.mean(x, axis=-1, keepdims=True)
    xc = x - mu
    var = jnp.mean(xc * xc, axis=-1, keepdims=True)
    return xc * lax.rsqrt(var + EPS) * g + b


def _dot(a, b):
    return jnp.dot(a, b, preferred_element_type=F32)


def _mark(spare_ref, token):
    if token is None:
        return
    reps = spare_ref.shape[0] // ROWS
    spare_ref[...] = jnp.concatenate([token] * reps, axis=0).astype(spare_ref.dtype)


def _release(mxu_pieces, vpu_pieces):
    token = None
    for k in range(max(len(mxu_pieces), len(vpu_pieces))):
        new_token = mxu_pieces[k]() if k < len(mxu_pieces) else token
        if k < len(vpu_pieces):
            vpu_pieces[k](token)
        token = new_token


def _to_sublane_major(x):
    n, cols = x.shape
    return jnp.swapaxes(x.reshape(ROWS, n // ROWS, cols), 0, 1).reshape(n, cols)


def _to_time_order(x):
    n, cols = x.shape
    return jnp.swapaxes(x.reshape(n // ROWS, ROWS, cols), 0, 1).reshape(n, cols)


def _time_order_matrix(n):
    i = np.arange(n)
    src = ROWS * (i % (n // ROWS)) + i // (n // ROWS)
    return jnp.asarray((src[:, None] == i[None, :]).astype(np.float32), BF16)


UA_ROW0 = (LRU_CONV - 1) * ROWS


def _row_conv_matrix(n):
    cols = n // ROWS
    rows = (cols + 2 * SEG_PAD) * ROWS
    mat = np.zeros((rows, n), np.float32)
    for i in range(rows):
        q, s = divmod(i, ROWS)
        if SEG_PAD <= q < SEG_PAD + cols:
            mat[i, cols * s + q - SEG_PAD] = 1.0
    return jnp.asarray(mat, BF16)


def _short_convs(ua_s, a_prev, a_next, cw, cb, vc_s, n, width):
    sub = n // ROWS
    ext = LRU_CONV - 1
    row = lax.broadcasted_iota(jnp.int32, (ROWS, width), 0)
    cw = 0.5 * cw
    cb = 0.5 * cb

    def blk(p):
        return slice(p * ROWS, (p + 1) * ROWS)

    for k in range(1, ext + 1):
        tail = pltpu.roll(ua_s[blk(ext + sub - k), :width], 1, 0)
        ua_s[blk(ext - k), :width] = jnp.where(row == 0, a_prev[ext - k:ext - k + 1, :], tail)
    for k in range(ext):
        head = pltpu.roll(ua_s[blk(ext + k), :width], ROWS - 1, 0)
        ua_s[blk(ext + sub + k), :width] = jnp.where(row == ROWS - 1, a_next[k:k + 1, :], head)

    def chunk(ci, carry):
        base = ci * CHUNK
        for d in range(2):
            first = 0 if d == 0 else ext
            acc = jnp.broadcast_to(cb[d:d + 1, :], (CHUNK, width))
            for k in range(LRU_CONV):
                start = pl.multiple_of(base + (first + k) * ROWS, ROWS)
                acc = acc + cw[d, k:k + 1, :] * ua_s[pl.ds(start, CHUNK), :width]
            vc_s[d, pl.ds(pl.multiple_of(base, CHUNK), CHUNK), :] = acc
        return carry

    lax.fori_loop(0, n // CHUNK, chunk, 0, unroll=True)


def _gate_pieces(vc_s, wg_ref, g_s, width):
    half = width // 2

    def piece(d, hf):
        def run():
            res = _dot(vc_s[d, :, hf * half:(hf + 1) * half].astype(BF16), wg_ref[d, hf])
            g_s[d, :, hf * half:(hf + 1) * half] = res[:, :half]
            g_s[d, :, width + hf * half:width + (hf + 1) * half] = res[:, half:]
        return run

    return [piece(d, hf) for d in range(2) for hf in range(2)]


def _sublane_scan(a, h, row, reverse):
    for s in (1, 2, 4):
        if reverse:
            m, shift = row < ROWS - s, ROWS - s
        else:
            m, shift = row >= s, s
        a_sh = jnp.where(m, pltpu.roll(a, shift, 0), 1.0)
        h_sh = jnp.where(m, pltpu.roll(h, shift, 0), 0.0)
        h = h + a * h_sh
        a = a * a_sh
    return a, h


def _rglru_scan(ua_s, br, bi, lam, c0, vc_s, g_s, hl_s, al_s, po_s, q_s, n, width):
    sub = n // ROWS
    row = lax.broadcasted_iota(jnp.int32, (ROWS, width), 0)
    spl_half = (-0.5 * LRU_C) * _softplus(-lam)

    def scan_columns(cols):
        nc = cols.stop - cols.start
        gate_i = slice(width + cols.start, width + cols.stop)

        def rows_of(v, d):
            return jnp.broadcast_to(v[d:d + 1, cols], (ROWS, nc))

        spl_c = [rows_of(spl_half, d) for d in range(2)]
        br_c = [rows_of(0.5 * br, d) for d in range(2)]
        bi_c = [rows_of(0.5 * bi, d) for d in range(2)]

        def coeffs(d, r0):
            vh = vc_s[d, pl.ds(r0, ROWS), cols]
            t_r = jnp.tanh(g_s[d, pl.ds(r0, ROWS), cols] + br_c[d])
            t_i = jnp.tanh(g_s[d, pl.ds(r0, ROWS), gate_i] + bi_c[d])
            la = spl_c[d] + spl_c[d] * t_r
            a = jnp.exp(la)
            x = jnp.tanh(la) * (-1.0 - a * a)
            mult = jnp.where(x > 0.0, x * lax.rsqrt(x), 0.0)
            return a, mult * (vh + vh * t_i)

        def local(j, carry):
            hf, af, hb, ab = carry
            r0 = pl.multiple_of(j * ROWS, ROWS)
            a, b = coeffs(0, r0)
            hf = a * hf + b
            af = a * af
            hl_s[0, pl.ds(r0, ROWS), cols] = hf
            al_s[0, pl.ds(r0, ROWS), cols] = af
            r1 = pl.multiple_of((sub - 1 - j) * ROWS, ROWS)
            a, b = coeffs(1, r1)
            hb = a * hb + b
            ab = a * ab
            hl_s[1, pl.ds(r1, ROWS), cols] = hb
            al_s[1, pl.ds(r1, ROWS), cols] = ab
            return hf, af, hb, ab

        zero = jnp.zeros((ROWS, nc), F32)
        one = jnp.ones((ROWS, nc), F32)
        return lax.fori_loop(0, sub, local, (zero, one, zero, one), unroll=True)

    halves = [scan_columns(slice(c, c + width // 2)) for c in (0, width // 2)]
    hf, af, hb, ab = (jnp.concatenate(parts, axis=1) for parts in zip(*halves))

    af, hf = _sublane_scan(af, hf, row, reverse=False)
    end_f = hf + af * c0
    c_f = jnp.where(row == 0, c0, pltpu.roll(end_f, 1, 0))
    ab, hb = _sublane_scan(ab, hb, row, reverse=True)
    c_b = jnp.where(row == ROWS - 1, 0.0, pltpu.roll(hb, ROWS - 1, 0))
    c_a = jnp.where(row == ROWS - 1, 1.0, pltpu.roll(ab, ROWS - 1, 0))

    if po_s is not None:
        step = 2 * ROWS
        c_f, c_b, c_a = (jnp.concatenate([c, c], axis=0) for c in (c_f, c_b, c_a))

        def fix(j, carry):
            r0 = pl.ds(pl.multiple_of(j * step, step), step)
            a_b = al_s[1, r0, :]
            h = (hl_s[0, r0, :] + al_s[0, r0, :] * c_f) + (hl_s[1, r0, :] + a_b * c_b)
            sg = _silu(ua_s[pl.ds(pl.multiple_of(UA_ROW0 + j * step, ROWS), step), width:])
            po_s[r0, :] = (h * sg).astype(po_s.dtype)
            if q_s is not None:
                q_s[r0, :] = ((a_b * c_a) * sg).astype(q_s.dtype)
            return carry

        lax.fori_loop(0, n // step, fix, 0, unroll=True)

    return end_f[ROWS - 1:ROWS, :], hb[0:1, :], ab[0:1, :]


def _shifted_copy_pieces(vpad_ref, sh_ref):
    n = sh_ref.shape[1]

    def piece(j):
        def run():
            sh_ref[j] = vpad_ref[pl.ds(j, n), :]
        return run

    return [piece(j) for j in range(ROWS)]


def _time_conv_pieces(sh_ref, w_ref, ncol, bases, out_ref):
    def piece(ci, base):
        def run():
            acc = jnp.zeros((CHUNK, ncol), F32)
            for k in range(CONV_K):
                off = SEG_PAD - CONV_PAD + k
                start = base + (off // ROWS) * ROWS
                acc = acc + w_ref[k:k + 1, 0:ncol] * sh_ref[off % ROWS, start:start + CHUNK, :]
            out_ref[ci * CHUNK:(ci + 1) * CHUNK, 0:ncol] = acc
        return run

    return [piece(ci, base) for ci, base in enumerate(bases)]


def _mods_body(act_ref, w_ref, b_ref, o_ref):
    a = _silu(act_ref[...])
    w = w_ref[0]
    a_hi = a.astype(BF16)
    a_lo = (a - a_hi.astype(F32)).astype(BF16)
    w_hi = w.astype(BF16)
    w_lo = (w - w_hi.astype(F32)).astype(BF16)
    both = _dot(jnp.concatenate([a_hi, a_lo], axis=0), w_hi)
    bias = b_ref[pl.ds(pl.program_id(0), 1), :]
    o_ref[0] = both[:ROWS] + (_dot(a_hi, w_lo) + both[ROWS:]) + bias


def _mods_call(act, w_mod, b_mod):
    depth, d_model, d3 = w_mod.shape
    ncol = d_model
    return pl.pallas_call(
        _mods_body,
        grid=(depth, d3 // ncol),
        in_specs=[
            pl.BlockSpec((ROWS, d_model), lambda l, n: (0, 0)),
            pl.BlockSpec((1, d_model, ncol), lambda l, n: (l, 0, n)),
            pl.BlockSpec((depth, ncol), lambda l, n: (0, n)),
        ],
        out_specs=pl.BlockSpec((1, ROWS, ncol), lambda l, n: (l, 0, n)),
        out_shape=jax.ShapeDtypeStruct((depth, ROWS, d3), F32),
        compiler_params=pltpu.CompilerParams(
            dimension_semantics=("arbitrary", "arbitrary"), vmem_limit_bytes=VMEM_LIMIT),
        name="mods",
    )(act, w_mod, b_mod)


def _modulate(x, g, m, d_model):
    return _rms_norm(x, g) * (1.0 + m[:, d_model:2 * d_model]) + m[:, :d_model]


def _ctx_body(ctx_ref, mod_ref, gpre_ref, gpost_ref, win_ref, wlast_ref, wo_ref, wg_ref, cw_ref,
              cb_ref, br_ref, bi_ref, lam_ref, dww_ref, dwb_ref, lng_ref, lnb_ref,
              h0_ref,
              ua_s, vc_s, g_s, hl_s, al_s, po_s, vpad_s, sh_s, y_s,
              *, depth, d_model, width, n, mod_row):
    xc = ctx_ref[0]
    zeros_pad = jnp.zeros((SEG_PAD, width), F32)
    no_rows = jnp.zeros((LRU_CONV - 1, width), F32)
    zero_state = jnp.zeros((1, width), F32)
    for l in range(depth):
        update = l < depth - 1
        m = mod_ref[l, mod_row:mod_row + 1, :]
        hc = _modulate(xc, gpre_ref[l:l + 1, :], m, d_model).astype(BF16)
        hcp = _to_sublane_major(hc)
        if update:
            ua_s[UA_ROW0:UA_ROW0 + n, :] = _dot(hcp, win_ref[l, :, :2 * width])
            ub = _dot(hc, win_ref[l, :, 2 * width:])
        else:
            ua_s[UA_ROW0:UA_ROW0 + n, :width] = _dot(hcp, wlast_ref[0])
        _short_convs(ua_s, no_rows, no_rows, cw_ref[l], cb_ref[l], vc_s, n, width)
        for piece in _gate_pieces(vc_s, wg_ref.at[l], g_s, width):
            piece()
        hf_last, hb_first, _ = _rglru_scan(
            ua_s, br_ref[l], bi_ref[l], lam_ref[l], zero_state, vc_s, g_s, hl_s, al_s,
            po_s if update else None, None, n, width)
        h0_ref[0, 2 * l:2 * l + 1, :] = hf_last
        h0_ref[0, 2 * l + 1:2 * l + 2, :] = hb_first
        if update:
            v = ub[:, :width] * _sigmoid(ub[:, width:2 * width])
            vpad_s[0:SEG_PAD, :] = zeros_pad
            vpad_s[SEG_PAD:SEG_PAD + n, :] = v
            vpad_s[SEG_PAD + n:, :] = zeros_pad
            for piece in _shifted_copy_pieces(vpad_s, sh_s) + _time_conv_pieces(
                    sh_s, dww_ref.at[l], width, [ci * CHUNK for ci in range(n // CHUNK)], y_s):
                piece()
            y = y_s[...] + dwb_ref[l:l + 1, :]
            conv = _silu(_layer_norm(y, lng_ref[l:l + 1, :], lnb_ref[l:l + 1, :]))
            mix_a = _to_time_order(po_s[...].astype(BF16))
            mix_b = (conv * _silu(ub[:, 2 * width:])).astype(BF16)
            mix = _dot(mix_a, wo_ref[l, :width, :]) + _dot(mix_b, wo_ref[l, width:, :])
            xc = xc + m[:, 2 * d_model:] * _rms_norm(mix, gpost_ref[l:l + 1, :])


def _ctx_call(ctx, mod_c, mod_row, g_pre, g_post, w_in_bf, w_out_bf, wg, conv_a_w, conv_a_b,
              b_rgate, b_igate, lru_lambda, dw_w, dw_b, ln_g, ln_b):
    bsz, n, d_model = ctx.shape
    depth = w_in_bf.shape[0]
    width = conv_a_b.shape[-1]
    npad = n + 2 * SEG_PAD
    def full(a):
        nd = a.ndim
        return pl.BlockSpec(a.shape, lambda b, _nd=nd: (0,) * _nd)

    def head(a):
        return pl.BlockSpec((depth - 1,) + a.shape[1:], lambda b: (0, 0, 0))

    small = (wg, conv_a_w, conv_a_b, b_rgate, b_igate, lru_lambda, dw_w, dw_b, ln_g, ln_b)
    params = (mod_c, g_pre, g_post, w_in_bf, w_in_bf, w_out_bf) + small
    return pl.pallas_call(
        functools.partial(_ctx_body, depth=depth, d_model=d_model, width=width, n=n,
                          mod_row=mod_row),
        grid=(bsz,),
        in_specs=[pl.BlockSpec((1, n, d_model), lambda b: (b, 0, 0)), full(mod_c), full(g_pre),
                  full(g_post), head(w_in_bf),
                  pl.BlockSpec((1, d_model, width), lambda b: (depth - 1, 0, 0)),
                  head(w_out_bf)] + [full(a) for a in small],
        out_specs=pl.BlockSpec((1, 2 * depth, width), lambda b: (b, 0, 0)),
        out_shape=jax.ShapeDtypeStruct((bsz, 2 * depth, width), F32),
        scratch_shapes=[
            pltpu.VMEM((n + 2 * UA_ROW0, 2 * width), F32),
            pltpu.VMEM((2, n, width), F32),
            pltpu.VMEM((2, n, 2 * width), F32),
            pltpu.VMEM((2, n, width), F32),
            pltpu.VMEM((2, n, width), F32),
            pltpu.VMEM((n, width), F32),
            pltpu.VMEM((npad, width), F32),
            pltpu.VMEM((ROWS, npad - ROWS, width), F32),
            pltpu.VMEM((n, width), F32),
        ],
        compiler_params=pltpu.CompilerParams(
            dimension_semantics=("arbitrary",), vmem_limit_bytes=VMEM_LIMIT),
        name="context",
    )(ctx, *params)


def _proj_body(x_ref, mod_ref, g_ref, w_ref, hx_ref, v1_ref, v2_ref, *, l, nt, d_model, width):
    b = pl.program_id(0)
    j = pl.program_id(1)
    is_pad = jnp.logical_or(j == 0, j == nt + 1)

    @pl.when(is_pad)
    def _():
        v2_ref[...] = jnp.zeros(v2_ref.shape, v2_ref.dtype)

    @pl.when(jnp.logical_not(is_pad))
    def _():
        half = width // 2
        rows = ROWS_PER_CHAIN
        for c in range(x_ref.shape[1] // rows):
            rs = slice(c * rows, (c + 1) * rows)
            hb = _modulate(x_ref[0, rs, :], g_ref[l:l + 1, :], mod_ref[l, pl.ds(b, 1), :],
                           d_model).astype(BF16)
            hx_ref[0, rs, :] = hb
            u = _dot(hb, w_ref[0])
            v = u[:, :width] * _sigmoid(u[:, width:])
            v1_ref[0, rs, :] = v[:, :half].astype(BF16)
            v2_ref[0, rs, :] = v[:, half:].astype(BF16)


def _proj_call(x, mods, g_pre, w_in_bf, l, width):
    bsz, seq, d_model = x.shape
    half = width // 2
    tt = TILE_PROJ
    nt = seq // tt

    def tok(b, j):
        return (b, jnp.clip(j - 1, 0, nt - 1), 0)

    return pl.pallas_call(
        functools.partial(_proj_body, l=l, nt=nt, d_model=d_model, width=width),
        grid=(bsz, nt + 2),
        in_specs=[
            pl.BlockSpec((1, tt, d_model), tok),
            pl.BlockSpec(mods.shape, lambda b, j: (0, 0, 0)),
            pl.BlockSpec(g_pre.shape, lambda b, j: (0, 0)),
            pl.BlockSpec((1, d_model, 2 * width), lambda b, j: (l, 0, 1)),
        ],
        out_specs=[
            pl.BlockSpec((1, tt, d_model), tok),
            pl.BlockSpec((1, tt, half), tok),
            pl.BlockSpec((1, tt, half), lambda b, j: (b, j, 0)),
        ],
        out_shape=[
            jax.ShapeDtypeStruct((bsz, seq, d_model), BF16),
            jax.ShapeDtypeStruct((bsz, seq, half), BF16),
            jax.ShapeDtypeStruct((bsz, seq + 2 * tt, half), BF16),
        ],
        compiler_params=pltpu.CompilerParams(
            dimension_semantics=("arbitrary", "arbitrary"), vmem_limit_bytes=VMEM_LIMIT),
        name="proj",
    )(x, mods, g_pre, w_in_bf)


def _mix_body(hx_ref, hxp_ref, hxn_ref, v1_ref, v2_ref, wa_ref, wbg_ref, wg_ref, cw_ref, cb_ref,
              br_ref, bi_ref, lam_ref, dww16_ref, dwb_ref, lng_ref, lnb_ref, pmt_ref, pme_ref,
              h0_ref, zero_ref,
              p_ref, q_ref, mb_ref, ab_ref, hbo_ref,
              hxs_s, ua_s, bg_s, vc_s, g_s, hl_s, al_s, carry_s, ev_s, od_s, wr_s, wb_s,
              yr_s, y_s,
              *, l, tt, nt, d_model, width, v2_front):
    i = pl.program_id(1)
    half = width // 2
    nseg = tt // GRID_W
    ext = LRU_CONV - 1
    reach = CONV_PAD * GRID_W
    row_zero = zero_ref[0]

    @pl.when(i == 0)
    def _():
        carry_s[...] = h0_ref[pl.program_id(0), 2 * l:2 * l + 1, :]

    def col_slices(total):
        return [slice(c, c + MXU_N) for c in range(0, total, MXU_N)]

    def zero_based(start, align):
        return pl.multiple_of(row_zero + start, align)

    def prepare(token):
        for k in range(CONV_K):
            wr_s[k] = jnp.broadcast_to(dww16_ref[l, k:k + 1, 0:half], (2 * ROWS, half))
            wb_s[k] = jnp.broadcast_to(dww16_ref[l, k:k + 1, half:], (2 * ROWS, half))

    def spread_rows():
        res = _dot(pme_ref[...], v1_ref[0])
        ev_s[...] = res.astype(BF16)
        moved = jnp.concatenate([res[ROWS:, :], jnp.zeros((ROWS, half), F32)], axis=0)
        od_s[...] = moved.astype(BF16)
        return res[0:ROWS, 0:128]

    def row_conv(chunks):
        def run(token):
            _mark(wr_s.at[CONV_K, :, 0:128], token)
            for ci in chunks:
                for rt in range(CHUNK // (2 * ROWS)):
                    row0 = ci * CHUNK + rt * 2 * ROWS
                    acc = None
                    for g0 in range(0, CONV_K, TAP_GROUP):
                        part = None
                        for k in range(g0, min(g0 + TAP_GROUP, CONV_K)):
                            shift = SEG_PAD - CONV_PAD + k
                            src = row0 + shift * ROWS
                            if shift % 2 == 0:
                                window = ev_s[src:src + 2 * ROWS, :]
                            else:
                                window = od_s[src - ROWS:src + ROWS, :]
                            term = wr_s[row_zero + k] * window
                            part = term if part is None else part + term
                        part = part.astype(F32)
                        acc = part if acc is None else acc + part
                    yr_s[row0:row0 + 2 * ROWS, :] = acc
        return run

    def gather_rows():
        y_s[:, 0:half] = _dot(pmt_ref[...], yr_s[...].astype(BF16))
        return None

    def col_conv(chunks):
        def run(token):
            _mark(wb_s.at[CONV_K, :, 0:128], token)
            for ci in chunks:
                win0 = i * tt + (v2_front - reach) + ci * CHUNK
                for rt in range(CHUNK // (2 * ROWS)):
                    acc = None
                    for g0 in range(0, CONV_K, TAP_GROUP):
                        part = None
                        for k in range(g0, min(g0 + TAP_GROUP, CONV_K)):
                            start = pl.multiple_of(win0 + k * GRID_W + rt * 2 * ROWS, 2 * ROWS)
                            term = wb_s[row_zero + k] * v2_ref[0, pl.ds(start, 2 * ROWS), :]
                            part = term if part is None else part + term
                        part = part.astype(F32)
                        acc = part if acc is None else acc + part
                    row0 = ci * CHUNK + rt * 2 * ROWS
                    y_s[row0:row0 + 2 * ROWS, half:] = acc
        return run

    def permute(slices):
        def run():
            for cs in slices:
                res = _to_sublane_major(hx_ref[0, :, cs])
                hxs_s[0:tt, cs] = res
            return res[0:ROWS, 0:128].astype(F32)
        return run

    def project_a(cs):
        def run():
            res = _dot(hxs_s[0:tt, :], wa_ref[0, :, cs])
            ua_s[UA_ROW0:UA_ROW0 + tt, cs] = res
            return res[0:ROWS, 0:128]
        return run

    def project_b(cs):
        def run():
            res = _dot(hx_ref[0], wbg_ref[0, :, cs])
            bg_s[0:tt, cs] = res
            return res[0:ROWS, 0:128]
        return run

    cols_d = col_slices(d_model)
    cols_a = col_slices(2 * width)
    cols_b = col_slices(width)
    quarter = nseg // 4
    row_stages = [row_conv(range(q * quarter, (q + 1) * quarter)) for q in range(4)]
    col_stages = [col_conv(range(q * 2 * quarter, (q + 1) * 2 * quarter)) for q in range(2)]
    _release(
        [spread_rows, permute(cols_d[:2]), permute(cols_d[2:]), project_b(cols_b[0]),
         project_b(cols_b[1])] + [project_a(cs) for cs in cols_a] + [gather_rows],
        [prepare] + row_stages + col_stages)

    wa_a = wa_ref[0, :, :width]
    a_prev = jnp.where(i > 0, _dot(hxp_ref[0], wa_a)[HALO - ext:, :], 0.0)
    a_next = jnp.where(i < nt - 1, _dot(hxn_ref[0], wa_a)[:ext, :], 0.0)
    _short_convs(ua_s, a_prev, a_next, cw_ref[l], cb_ref[l], vc_s, tt, width)

    def finish_conv(c, nrows):
        def run(token):
            _mark(bg_s.at[tt:tt + ROWS, 0:128], token)
            y = y_s[c * nrows:(c + 1) * nrows, :] + dwb_ref[l:l + 1, :]
            conv = _silu(_layer_norm(y, lng_ref[l:l + 1, :], lnb_ref[l:l + 1, :]))
            out = conv * _silu(bg_s[pl.ds(zero_based(c * nrows, ROWS), nrows), :])
            mb_ref[0, c * nrows:(c + 1) * nrows, :] = out.astype(BF16)
        return run

    def gate_piece(d, hf):
        def run():
            lo = hf * half
            res = _dot(vc_s[d, 0:tt, lo:lo + half].astype(BF16), wg_ref[0, d, hf])
            g_s[d, :, lo:lo + half] = res[:, :half]
            g_s[d, :, width + lo:width + lo + half] = res[:, half:]
            return res[0:ROWS, 0:128]
        return run

    _release([gate_piece(d, hf) for d in range(2) for hf in range(2)],
             [finish_conv(c, tt // 4) for c in range(4)])

    hf_last, hb_first, a_total = _rglru_scan(
        ua_s, br_ref[l], bi_ref[l], lam_ref[l], carry_s[...], vc_s, g_s, hl_s, al_s,
        p_ref.at[0], q_ref.at[0], tt, width)
    carry_s[...] = hf_last
    ab_ref[0, 0] = a_total
    hbo_ref[0, 0] = hb_first


def _mix_call(hx, v1, v2p, w_in_bf, wg, cw, cb, br, bi, lam, dww16, dwb, lng, lnb, h0, l):
    bsz, seq, d_model = hx.shape
    width = cb.shape[-1]
    half = width // 2
    tt = TILE_MIX
    nt = seq // tt
    hb_per_tile = tt // HALO
    n_halo_blocks = seq // HALO
    v2_front = (v2p.shape[1] - seq) // 2
    pme = _row_conv_matrix(tt)

    def full(a):
        nd = a.ndim
        return pl.BlockSpec(a.shape, lambda b, i, _nd=nd: (0,) * _nd)

    small = (cw, cb, br, bi, lam, dww16, dwb, lng, lnb, _time_order_matrix(tt), pme, h0)
    tile = lambda b, i: (b, i, 0)
    out_shapes = [jax.ShapeDtypeStruct((bsz, seq, width), BF16)] * 3 + [
        jax.ShapeDtypeStruct((bsz, nt, 1, width), F32)] * 2
    return pl.pallas_call(
        functools.partial(_mix_body, l=l, tt=tt, nt=nt, d_model=d_model, width=width,
                          v2_front=v2_front),
        grid=(bsz, nt),
        in_specs=[
            pl.BlockSpec((1, tt, d_model), tile),
            pl.BlockSpec((1, HALO, d_model),
                         lambda b, i: (b, jnp.maximum(i * hb_per_tile - 1, 0), 0)),
            pl.BlockSpec((1, HALO, d_model),
                         lambda b, i: (b, jnp.minimum((i + 1) * hb_per_tile, n_halo_blocks - 1), 0)),
            pl.BlockSpec((1, tt, half), tile),
            pl.BlockSpec((1, v2p.shape[1], half), lambda b, i: (b, 0, 0)),
            pl.BlockSpec((1, d_model, 2 * width), lambda b, i: (l, 0, 0)),
            pl.BlockSpec((1, d_model, width), lambda b, i: (l, 0, 4)),
            pl.BlockSpec((1,) + wg.shape[1:], lambda b, i: (l, 0, 0, 0, 0)),
        ] + [full(a) for a in small] + [pl.BlockSpec(memory_space=pltpu.SMEM)],
        out_specs=[pl.BlockSpec((1, tt, width), tile)] * 3 + [
            pl.BlockSpec((1, 1, 1, width), lambda b, i: (b, i, 0, 0))] * 2,
        out_shape=out_shapes,
        scratch_shapes=[
            pltpu.VMEM((tt, d_model), BF16),
            pltpu.VMEM((tt + 2 * UA_ROW0, 2 * width), F32),
            pltpu.VMEM((tt + ROWS, width), F32),
            pltpu.VMEM((2, tt, width), F32),
            pltpu.VMEM((2, tt, 2 * width), F32),
            pltpu.VMEM((2, tt, width), F32),
            pltpu.VMEM((2, tt, width), F32),
            pltpu.VMEM((1, width), F32),
            pltpu.VMEM(pme.shape[:1] + (half,), BF16),
            pltpu.VMEM(pme.shape[:1] + (half,), BF16),
            pltpu.VMEM((CONV_K + 1, 2 * ROWS, half), BF16),
            pltpu.VMEM((CONV_K + 1, 2 * ROWS, half), BF16),
            pltpu.VMEM((tt, half), F32),
            pltpu.VMEM((tt, width), F32),
        ],
        compiler_params=pltpu.CompilerParams(
            dimension_semantics=("arbitrary", "arbitrary"), vmem_limit_bytes=VMEM_LIMIT,
            ),
        name="mixer",
    )(hx, hx, hx, v1, v2p, w_in_bf, w_in_bf, wg, *small, jnp.zeros((1,), jnp.int32))


def _out_body(p_ref, q_ref, mb_ref, x_ref, wo_ref, gpost_ref, mod_ref, ab_ref, hb_ref, h0_ref,
              *rest, l, d_model, width, fuse_next):
    if fuse_next:
        gpre_ref, wbn_ref, _, o_ref, hx_ref, v1_ref, v2_ref, carry_s = rest
    else:
        o_ref, carry_s = rest
    b = pl.program_id(0)

    @pl.when(pl.program_id(1) == 0)
    def _():
        carry_s[...] = h0_ref[b, 2 * l + 1:2 * l + 2, :]

    gate = mod_ref[l, pl.ds(b, 1), 2 * d_model:]
    rows = ROWS_PER_CHAIN
    half = width // 2
    tm = TILE_MIX
    c = carry_s[...]
    for st in reversed(range(p_ref.shape[1] // tm)):
        t0 = st * tm
        mix_a = (p_ref[0, t0:t0 + tm, :].astype(F32)
                 + q_ref[0, t0:t0 + tm, :].astype(F32) * c).astype(BF16)
        mix_t = _to_time_order(mix_a)
        c = hb_ref[0, st] + ab_ref[0, st] * c
        for ch in range(tm // rows):
            rs = slice(t0 + ch * rows, t0 + (ch + 1) * rows)
            lhs = jnp.concatenate([mix_t[ch * rows:(ch + 1) * rows, :], mb_ref[0, rs, :]], axis=1)
            mix = _dot(lhs, wo_ref[0])
            x_new = x_ref[0, rs, :] + gate * _rms_norm(mix, gpost_ref[l:l + 1, :])
            o_ref[0, rs, :] = x_new
            if fuse_next:
                hb = _modulate(x_new, gpre_ref[l + 1:l + 2, :], mod_ref[l + 1, pl.ds(b, 1), :],
                               d_model).astype(BF16)
                hx_ref[0, rs, :] = hb
                u = _dot(hb, wbn_ref[0])
                v = u[:, :width] * _sigmoid(u[:, width:])
                v1_ref[0, rs, :] = v[:, :half].astype(BF16)
                v2_ref[0, rs, :] = v[:, half:].astype(BF16)
    carry_s[...] = c


def _out_call(p, q, mb, x, w_out_bf, g_post, mods, ab, hbo, h0, l, nxt=None):
    bsz, seq, d_model = x.shape
    width = p.shape[-1]
    half = width // 2
    tt = TILE_OUT
    tm = TILE_MIX
    nt = seq // tt
    per = tt // tm
    pad_blocks = TILE_PROJ // tt
    rev = lambda b, j: (b, nt - 1 - j, 0)

    def full(a):
        nd = a.ndim
        return pl.BlockSpec(a.shape, lambda b, j, _nd=nd: (0,) * _nd)

    operands = [p, q, mb, x, w_out_bf, g_post, mods, ab, hbo, h0]
    in_specs = [
        pl.BlockSpec((1, tt, width), rev),
        pl.BlockSpec((1, tt, width), rev),
        pl.BlockSpec((1, tt, width), rev),
        pl.BlockSpec((1, tt, d_model), rev),
        pl.BlockSpec((1,) + w_out_bf.shape[1:], lambda b, j: (l, 0, 0)),
        full(g_post),
        full(mods),
        pl.BlockSpec((1, per, 1, width), lambda b, j: (b, nt - 1 - j, 0, 0)),
        pl.BlockSpec((1, per, 1, width), lambda b, j: (b, nt - 1 - j, 0, 0)),
        full(h0),
    ]
    out_specs = [pl.BlockSpec((1, tt, d_model), rev)]
    out_shape = [jax.ShapeDtypeStruct((bsz, seq, d_model), F32)]
    aliases = {}
    if nxt is not None:
        g_pre, w_in_bf, v2_buf = nxt
        operands += [g_pre, w_in_bf, v2_buf]
        in_specs += [
            full(g_pre),
            pl.BlockSpec((1, d_model, 2 * width), lambda b, j: (l + 1, 0, 1)),
            pl.BlockSpec(memory_space=pl.ANY),
        ]
        out_specs += [
            pl.BlockSpec((1, tt, d_model), rev),
            pl.BlockSpec((1, tt, half), rev),
            pl.BlockSpec((1, tt, half), lambda b, j: (b, nt - 1 - j + pad_blocks, 0)),
        ]
        out_shape += [
            jax.ShapeDtypeStruct((bsz, seq, d_model), BF16),
            jax.ShapeDtypeStruct((bsz, seq, half), BF16),
            jax.ShapeDtypeStruct(v2_buf.shape, BF16),
        ]
        aliases = {len(operands) - 1: 3}
    res = pl.pallas_call(
        functools.partial(_out_body, l=l, d_model=d_model, width=width,
                          fuse_next=nxt is not None),
        grid=(bsz, nt),
        in_specs=in_specs,
        out_specs=out_specs,
        out_shape=out_shape,
        input_output_aliases=aliases,
        scratch_shapes=[pltpu.VMEM((1, width), F32)],
        compiler_params=pltpu.CompilerParams(
            dimension_semantics=("arbitrary", "arbitrary"), vmem_limit_bytes=VMEM_LIMIT),
        name="out_proj" if nxt is not None else "out",
    )(*operands)
    return res if nxt is not None else res[0]


def _pack_gate_weights(w_r, w_i):
    depth, ndir, heads, hd, _ = w_r.shape
    hh = heads // 2
    eye = jnp.eye(hh, dtype=w_r.dtype)

    def bd(w):
        w = w.reshape(depth, ndir, 2, hh, hd, hd)
        return jnp.einsum("ldfhij,hg->ldfhigj", w, eye).reshape(depth, ndir, 2, hh * hd, hh * hd)

    return jnp.concatenate([bd(w_r), bd(w_i)], axis=-1).astype(BF16)


def kernel(x, c, ctx, c_ctx, w_mod, b_mod, g_pre, g_post, w_in, conv_a_w, conv_a_b, w_rgate,
           b_rgate, w_igate, b_igate, lru_lambda, dw_w, dw_b, ln_g, ln_b, w_out):
    bsz, seq, d_model = x.shape
    depth = w_mod.shape[0]
    width = conv_a_b.shape[-1]
    assert depth >= 2 and bsz + 1 <= ROWS and seq % TILE_PROJ == 0 and seq % TILE_MIX == 0
    assert TILE_MIX % GRID_W == 0 and TILE_PROJ >= CONV_PAD * GRID_W
    assert seq % TILE_OUT == 0 and TILE_OUT % TILE_MIX == 0 and TILE_PROJ % TILE_OUT == 0
    assert ctx.shape[1] % CHUNK == 0 and width % MXU_N == 0 and d_model % MXU_N == 0

    act = jnp.concatenate(
        [c, c_ctx[None, :], jnp.zeros((ROWS - bsz - 1, d_model), F32)], axis=0)
    mods = _mods_call(act, w_mod, b_mod)

    w_in_bf = w_in.astype(BF16)
    w_out_bf = w_out.astype(BF16)
    wg = _pack_gate_weights(w_rgate, w_igate)

    h0 = _ctx_call(ctx, mods, bsz, g_pre, g_post, w_in_bf, w_out_bf, wg, conv_a_w, conv_a_b,
                   b_rgate, b_igate, lru_lambda, dw_w, dw_b, ln_g, ln_b)

    dw_w16 = dw_w.astype(BF16)
    hx, v1, v2p = _proj_call(x, mods, g_pre, w_in_bf, 0, width)
    for l in range(depth):
        p, q, mb, ab, hbo = _mix_call(
            hx, v1, v2p, w_in_bf, wg, conv_a_w, conv_a_b, b_rgate, b_igate, lru_lambda, dw_w16,
            dw_b, ln_g, ln_b, h0, l)
        if l + 1 < depth:
            x, hx, v1, v2p = _out_call(p, q, mb, x, w_out_bf, g_post, mods, ab, hbo, h0, l,
                                       nxt=(g_pre, w_in_bf, v2p))
        else:
            x = _out_call(p, q, mb, x, w_out_bf, g_post, mods, ab, hbo, h0, l)
    return x
```
